```python
import math
import jax, jax.numpy as jnp
from jax import lax
import numpy as np

D_MODEL = 1024
BATCH = 2
SEQ = 8192
DEPTH = 2
DEC_BATCH = 32
DEC_SEQ = 8
PAST_LEN = 16384
PAGE_SIZE = 128

N_AB_LAYERS = (DEPTH + 1) // 2
N_C_LAYERS = DEPTH // 2

H_A = 4
DK_A = 64
DV_A = 128
RET_CHUNK = 128
H_B = 8
DH_B = 64
SWA_PAIRS = ((128, 1), (512, 4), (2048, 16))
SWA_MAX_WINDOW = 2048
S5_GROUP = 16
S5_GROUPS = D_MODEL // S5_GROUP
S5_STATE = 64
D_FF = -(-8 * D_MODEL // (3 * 256)) * 256

AB_WIDTHS = (H_A * DK_A, H_A * DK_A, H_A * DV_A, H_A * DV_A, H_B * DH_B, H_B * DH_B, H_B * DH_B)
AB_IN_WIDTH = sum(AB_WIDTHS)
AB_SPLITS = tuple(int(c) for c in np.cumsum(AB_WIDTHS)[:-1])
AB_MIX_WIDTH = H_A * DV_A + H_B * DH_B

EPS = 1e-6
NEG_INF = -1e30

kernel_name = "retnet_longnet_s5_hybrid_step"


def rms_norm(x, g):
    xf = x.astype(jnp.float32)
    y = xf * lax.rsqrt(jnp.mean(xf * xf, axis=-1, keepdims=True) + EPS)
    return (y * g.astype(jnp.float32)).astype(x.dtype)


def swiglu(h, w_in, w_out):
    gate, up = jnp.split(h @ w_in, 2, axis=-1)
    return (jax.nn.silu(gate) * up) @ w_out


def swa_buffer_len():
    return min(SWA_MAX_WINDOW, PAST_LEN)


def alibi_slopes():
    return jnp.exp2(-8.0 * jnp.arange(1, H_B + 1, dtype=jnp.float32) / H_B)


def retention_log_decay():
    return jnp.log1p(-jnp.exp2(-5.0 - jnp.arange(H_A, dtype=jnp.float32)))


def window_rows(x, n):
    seq = x.shape[1]
    if seq >= n:
        return x[:, seq - n:]
    return jnp.pad(x, ((0, 0), (n - seq, 0), (0, 0), (0, 0)))


def retention_chunk(q, k, v, state, log_g):
    c = q.shape[1]
    idx = jnp.arange(c, dtype=jnp.float32)
    rel = idx[:, None] - idx[None, :]
    decay = jnp.where(rel >= 0, jnp.exp(jnp.maximum(rel, 0.0)[None] * log_g[:, None, None]), 0.0)
    scores = jnp.einsum('bnhd,bmhd->bhnm', q, k) * decay[None]
    inner = jnp.einsum('bhnm,bmhe->bnhe', scores, v)
    q_decay = jnp.exp((idx + 1.0)[:, None] * log_g[None, :])
    cross = jnp.einsum('bnhd,bhde->bnhe', q, state) * q_decay[None, :, :, None]
    k_decay = jnp.exp((c - 1.0 - idx)[:, None] * log_g[None, :])
    new_state = (jnp.exp(c * log_g)[None, :, None, None] * state
                 + jnp.einsum('bmhd,bmhe->bhde', k * k_decay[None, :, :, None], v))
    return inner + cross, new_state


def retention(q, k, v, state, chunk):
    bsz, seq, nh, dk = q.shape
    n_chunks = seq // chunk
    log_g = retention_log_decay()

    def to_chunks(x):
        return jnp.swapaxes(x.astype(jnp.float32).reshape(bsz, n_chunks, chunk, nh, x.shape[-1]), 0, 1)

    def step(st, qkv):
        o, st = retention_chunk(qkv[0], qkv[1], qkv[2], st, log_g)
        return st, o

    state, o = lax.scan(step, state.astype(jnp.float32),
                        (to_chunks(q), to_chunks(k) * dk ** -0.5, to_chunks(v)))
    return jnp.swapaxes(o, 0, 1).reshape(bsz, seq, nh, -1), state


def head_group_norm(o, gain):
    mu = jnp.mean(o, axis=-1, keepdims=True)
    var = jnp.mean(jnp.square(o - mu), axis=-1, keepdims=True)
    y = (o - mu) * lax.rsqrt(var + EPS)
    return y.reshape(o.shape[0], o.shape[1], -1) * gain.astype(jnp.float32)


def dilated_branch_prompt(q, k, v, window, dil, slopes):
    bsz, seq, nh, dh = q.shape
    span = window // dil
    sub = seq // dil
    pad = (-sub) % span

    def blocks(x):
        x = x.reshape(bsz, sub, dil, nh, dh).transpose(0, 2, 1, 3, 4)
        x = jnp.pad(x, ((0, 0), (0, 0), (0, pad), (0, 0), (0, 0)))
        return x.reshape(bsz, dil, -1, span, nh, dh)

    qb, kb, vb = blocks(q), blocks(k), blocks(v)
    nblk = qb.shape[2]

    def with_prev(x):
        prev = jnp.pad(x[:, :, :-1], ((0, 0), (0, 0), (1, 0), (0, 0), (0, 0), (0, 0)))
        return jnp.concatenate([prev, x], axis=3)

    kk, vv = with_prev(kb), with_prev(vb)
    qi = jnp.arange(span)[:, None]
    kj = jnp.arange(2 * span)[None, :]
    dist = span + qi - kj
    key_pos = (jnp.arange(nblk)[:, None, None] - 1) * span + kj[None]
    valid = (dist >= 0) & (dist <= span) & (key_pos >= 0)
    bias = -slopes[:, None, None] * (dil * dist).astype(jnp.float32)[None]
    s = jnp.einsum('brnqhd,brnkhd->brnhqk', qb, kk) * dh ** -0.5 + bias
    s = jnp.where(valid[:, None], s, NEG_INF)
    m = jnp.max(s, axis=-1, keepdims=True)
    p = jnp.exp(s - m)
    den = jnp.sum(p, axis=-1, keepdims=True)
    o = jnp.einsum('brnhqk,brnkhd->brnqhd', p / den, vv)
    lse = (m + jnp.log(den))[..., 0].transpose(0, 1, 2, 4, 3)

    def unblock(x):
        x = x.reshape((bsz, dil, nblk * span) + x.shape[4:])[:, :, :sub]
        x = jnp.swapaxes(x, 1, 2)
        return x.reshape((bsz, seq) + x.shape[3:])

    return unblock(o), unblock(lse)


def dilated_branch_sample(q, k_all, v_all, window, dil, slopes, p0):
    n_new, dh = q.shape[1], q.shape[-1]
    steps = jnp.arange(window // dil + 1)
    pos = PAST_LEN + jnp.arange(n_new)[:, None] - dil * steps[None, :]
    valid = pos >= 0
    idx = jnp.clip(pos - p0, 0, k_all.shape[1] - 1)
    kg = k_all[:, idx]
    vg = v_all[:, idx]
    bias = -slopes[:, None] * (dil * steps).astype(jnp.float32)[None, :]
    s = jnp.einsum('bthd,btkhd->bthk', q, kg) * dh ** -0.5 + bias
    s = jnp.where(valid[:, None, :], s, NEG_INF)
    m = jnp.max(s, axis=-1, keepdims=True)
    p = jnp.exp(s - m)
    den = jnp.sum(p, axis=-1, keepdims=True)
    o = jnp.einsum('bthk,btkhd->bthd', p / den, vg)
    return o, (m + jnp.log(den))[..., 0]


def merge_dilation_branches(branches):
    outs = jnp.stack([o for o, _ in branches])
    lses = jnp.stack([l for _, l in branches])
    wts = jax.nn.softmax(lses, axis=0)
    return jnp.einsum('gbsh,gbshd->bshd', wts, outs)


def ab_project(h, w_in):
    bsz, seq, _ = h.shape
    q_a, k_a, v_a, g_a, q_b, k_b, v_b = jnp.split(h @ w_in, AB_SPLITS, axis=-1)
    hd = lambda t, n: t.reshape(bsz, seq, n, -1)
    return (hd(q_a, H_A), hd(k_a, H_A), hd(v_a, H_A), g_a,
            hd(q_b, H_B), hd(k_b, H_B), hd(v_b, H_B))


def ab_merge(h, o_a, g_a, o_b, gn_gain, w_out):
    bsz, seq, _ = h.shape
    a = jax.nn.silu(g_a.astype(jnp.float32)) * head_group_norm(o_a, gn_gain)
    mixed = jnp.concatenate([a, o_b.reshape(bsz, seq, -1)], axis=-1)
    return mixed.astype(h.dtype) @ w_out


def ab_mixer_prompt(h, w_in, gn_gain, w_out, ret_state0):
    q_a, k_a, v_a, g_a, q_b, k_b, v_b = ab_project(h, w_in)
    o_a, ret_state = retention(q_a, k_a, v_a, ret_state0, RET_CHUNK)
    slopes = alibi_slopes()
    qf, kf, vf = q_b.astype(jnp.float32), k_b.astype(jnp.float32), v_b.astype(jnp.float32)
    o_b = merge_dilation_branches([dilated_branch_prompt(qf, kf, vf, w, d, slopes) for (w, d) in SWA_PAIRS])
    out = ab_merge(h, o_a, g_a, o_b, gn_gain, w_out)
    buf = swa_buffer_len()
    return out, ret_state, window_rows(k_b, buf), window_rows(v_b, buf)


def ab_mixer_sample(h, w_in, gn_gain, w_out, ret_state0, k_past, v_past):
    q_a, k_a, v_a, g_a, q_b, k_b, v_b = ab_project(h, w_in)
    o_a, ret_state = retention(q_a, k_a, v_a, ret_state0, q_a.shape[1])
    k_all = jnp.concatenate([k_past.astype(k_b.dtype), k_b], axis=1)
    v_all = jnp.concatenate([v_past.astype(v_b.dtype), v_b], axis=1)
    buf = k_past.shape[1]
    p0 = PAST_LEN - buf
    slopes = alibi_slopes()
    qf, kf, vf = q_b.astype(jnp.float32), k_all.astype(jnp.float32), v_all.astype(jnp.float32)
    o_b = merge_dilation_branches([dilated_branch_sample(qf, kf, vf, w, d, slopes, p0) for (w, d) in SWA_PAIRS])
    out = ab_merge(h, o_a, g_a, o_b, gn_gain, w_out)
    return out, ret_state, k_all[:, -buf:], v_all[:, -buf:]


def complex_affine_combine(e1, e2):
    a1r, a1i, b1r, b1i = e1
    a2r, a2i, b2r, b2i = e2
    return (a1r * a2r - a1i * a2i, a1r * a2i + a1i * a2r,
            a2r * b1r - a2i * b1i + b2r, a2r * b1i + a2i * b1r + b2i)


def s5_mixer(h, lam_re, lam_im, log_step, b_re, b_im, c_re, c_im, d_skip, w_glu, x0_re, x0_im):
    bsz, seq, _ = h.shape
    f32 = jnp.float32
    lam_re, lam_im = lam_re.astype(f32), lam_im.astype(f32)
    dt = jnp.exp(log_step.astype(f32))[:, None]
    mag = jnp.exp(lam_re * dt)
    ab_re, ab_im = mag * jnp.cos(lam_im * dt), mag * jnp.sin(lam_im * dt)
    den = lam_re * lam_re + lam_im * lam_im
    f_re = ((ab_re - 1.0) * lam_re + ab_im * lam_im) / den
    f_im = (ab_im * lam_re - (ab_re - 1.0) * lam_im) / den
    u = h.astype(f32).reshape(bsz, seq, S5_GROUPS, S5_GROUP)
    bu_re = jnp.einsum('bsgp,gnp->bsgn', u, b_re.astype(f32))
    bu_im = jnp.einsum('bsgp,gnp->bsgn', u, b_im.astype(f32))
    drive_re = f_re * bu_re - f_im * bu_im
    drive_im = f_re * bu_im + f_im * bu_re
    x0_re, x0_im = x0_re.astype(f32), x0_im.astype(f32)
    drive_re = drive_re.at[:, 0].add(ab_re * x0_re - ab_im * x0_im)
    drive_im = drive_im.at[:, 0].add(ab_re * x0_im + ab_im * x0_re)
    a_re = jnp.broadcast_to(ab_re, (1, seq) + ab_re.shape)
    a_im = jnp.broadcast_to(ab_im, (1, seq) + ab_im.shape)
    _, _, xs_re, xs_im = lax.associative_scan(complex_affine_combine, (a_re, a_im, drive_re, drive_im), axis=1)
    y = (jnp.einsum('bsgn,gpn->bsgp', xs_re, c_re.astype(f32))
         - jnp.einsum('bsgn,gpn->bsgp', xs_im, c_im.astype(f32))
         + d_skip.astype(f32).reshape(S5_GROUPS, S5_GROUP) * u).reshape(bsz, seq, D_MODEL)
    z = jax.nn.gelu(y).astype(h.dtype)
    val, gate = jnp.split(z @ w_glu, 2, axis=-1)
    return val * jax.nn.sigmoid(gate), xs_re[:, -1], xs_im[:, -1]


def setup_inputs(seed: int = 0) -> dict:
    key = jax.random.key(seed)
    ks = jax.random.split(key, 24)
    f32 = jnp.float32
    nrm = lambda k, shape, scale: scale * jax.random.normal(k, shape, f32)
    buf = swa_buffer_len()
    n_idx = jnp.arange(S5_STATE, dtype=f32)
    ssm_shape = (N_C_LAYERS, S5_GROUPS, S5_STATE)
    return {
        'x_prompt': nrm(ks[0], (BATCH, SEQ, D_MODEL), 1.0),
        'x_sample': nrm(ks[1], (DEC_BATCH, DEC_SEQ, D_MODEL), 1.0),
        'state_ret': nrm(ks[2], (N_AB_LAYERS, DEC_BATCH, H_A, DK_A, DV_A), 1.0),
        'state_swa_k': nrm(ks[3], (N_AB_LAYERS, DEC_BATCH, buf, H_B, DH_B), 1.0),
        'state_swa_v': nrm(ks[4], (N_AB_LAYERS, DEC_BATCH, buf, H_B, DH_B), 1.0),
        'state_ssm_re': nrm(ks[5], (N_C_LAYERS, DEC_BATCH, S5_GROUPS, S5_STATE), 0.5),
        'state_ssm_im': nrm(ks[6], (N_C_LAYERS, DEC_BATCH, S5_GROUPS, S5_STATE), 0.5),
        'norm_mix': 1.0 + nrm(ks[7], (DEPTH, D_MODEL), 0.02),
        'norm_ffn': 1.0 + nrm(ks[8], (DEPTH, D_MODEL), 0.02),
        'norm_final': 1.0 + nrm(ks[9], (D_MODEL,), 0.02),
        'w_in_ab': nrm(ks[10], (N_AB_LAYERS, D_MODEL, AB_IN_WIDTH), D_MODEL ** -0.5),
        'ret_gn': 1.0 + nrm(ks[11], (N_AB_LAYERS, H_A * DV_A), 0.02),
        'w_out_ab': nrm(ks[12], (N_AB_LAYERS, AB_MIX_WIDTH, D_MODEL), AB_MIX_WIDTH ** -0.5),
        'ssm_lam_re': -0.5 + nrm(ks[13], ssm_shape, 0.01),
        'ssm_lam_im': math.pi * n_idx + nrm(ks[14], ssm_shape, 0.01),
        'ssm_log_step': jax.random.uniform(ks[15], (N_C_LAYERS, S5_GROUPS), f32, math.log(1e-3), math.log(1e-1)),
        'ssm_b_re': nrm(ks[16], (N_C_LAYERS, S5_GROUPS, S5_STATE, S5_GROUP), (2 * S5_GROUP) ** -0.5),
        'ssm_b_im': nrm(ks[17], (N_C_LAYERS, S5_GROUPS, S5_STATE, S5_GROUP), (2 * S5_GROUP) ** -0.5),
        'ssm_c_re': nrm(ks[18], (N_C_LAYERS, S5_GROUPS, S5_GROUP, S5_STATE), S5_STATE ** -0.5),
        'ssm_c_im': nrm(ks[19], (N_C_LAYERS, S5_GROUPS, S5_GROUP, S5_STATE), S5_STATE ** -0.5),
        'ssm_d': nrm(ks[20], (N_C_LAYERS, D_MODEL), 1.0),
        'w_glu': nrm(ks[21], (N_C_LAYERS, D_MODEL, 2 * D_MODEL), D_MODEL ** -0.5),
        'w_ffn_in': nrm(ks[22], (DEPTH, D_MODEL, 2 * D_FF), D_MODEL ** -0.5),
        'w_ffn_out': nrm(ks[23], (DEPTH, D_FF, D_MODEL), D_FF ** -0.5),
    }


def reference(x_prompt, x_sample, state_ret, state_swa_k, state_swa_v, state_ssm_re, state_ssm_im,
              norm_mix, norm_ffn, norm_final, w_in_ab, ret_gn, w_out_ab,
              ssm_lam_re, ssm_lam_im, ssm_log_step, ssm_b_re, ssm_b_im, ssm_c_re, ssm_c_im, ssm_d, w_glu,
              w_ffn_in, w_ffn_out):
    yp, ys = x_prompt, x_sample
    ret_p, ret_s = [], []
    swk_p, swv_p, swk_s, swv_s = [], [], [], []
    sr_p, si_p, sr_s, si_s = [], [], [], []
    for layer in range(DEPTH):
        i = layer // 2
        hp = rms_norm(yp, norm_mix[layer])
        hs = rms_norm(ys, norm_mix[layer])
        if layer % 2 == 0:
            zero_ret = jnp.zeros((yp.shape[0], H_A, DK_A, DV_A), jnp.float32)
            mp, r_p, k_p, v_p = ab_mixer_prompt(hp, w_in_ab[i], ret_gn[i], w_out_ab[i], zero_ret)
            ms, r_s, k_s, v_s = ab_mixer_sample(hs, w_in_ab[i], ret_gn[i], w_out_ab[i],
                                                state_ret[i], state_swa_k[i], state_swa_v[i])
            ret_p.append(r_p); ret_s.append(r_s)
            swk_p.append(k_p); swv_p.append(v_p); swk_s.append(k_s); swv_s.append(v_s)
        else:
            s5w = (ssm_lam_re[i], ssm_lam_im[i], ssm_log_step[i], ssm_b_re[i], ssm_b_im[i],
                   ssm_c_re[i], ssm_c_im[i], ssm_d[i], w_glu[i])
            zero_ssm = jnp.zeros((yp.shape[0], S5_GROUPS, S5_STATE), jnp.float32)
            mp, xr_p, xi_p = s5_mixer(hp, *s5w, zero_ssm, zero_ssm)
            ms, xr_s, xi_s = s5_mixer(hs, *s5w, state_ssm_re[i], state_ssm_im[i])
            sr_p.append(xr_p); si_p.append(xi_p); sr_s.append(xr_s); si_s.append(xi_s)
        yp = yp + mp
        ys = ys + ms
        yp = yp + swiglu(rms_norm(yp, norm_ffn[layer]), w_ffn_in[layer], w_ffn_out[layer])
        ys = ys + swiglu(rms_norm(ys, norm_ffn[layer]), w_ffn_in[layer], w_ffn_out[layer])
    yp = rms_norm(yp, norm_final)
    ys = rms_norm(ys, norm_final)
    return (yp, ys,
            jnp.stack(ret_p), jnp.stack(ret_s),
            jnp.stack(swk_p), jnp.stack(swv_p), jnp.stack(swk_s), jnp.stack(swv_s),
            jnp.stack(sr_p), jnp.stack(si_p), jnp.stack(sr_s), jnp.stack(si_s))
```

```python
import functools
import math

import jax
import jax.numpy as jnp
from jax import lax
from jax.experimental import pallas as pl
from jax.experimental.pallas import tpu as pltpu

F32 = jnp.float32
BF16 = jnp.bfloat16

H_A, DK_A, DV_A = 4, 64, 128
H_B, DH_B = 8, 64
SWA_PAIRS = ((128, 1), (512, 4), (2048, 16))
SPAN = 128
SWA_BUF = 2048
PAST_LEN = 16384
RET_CHUNK = 128
S5_GROUP, S5_STATE = 16, 64
S5_CHUNK = 16
EPS = 1e-6
NEG_INF = -1e30
QA_W, KA_W, VA_W, GA_W = H_A * DK_A, H_A * DK_A, H_A * DV_A, H_A * DV_A
RA_W = QA_W + KA_W + VA_W + GA_W
QKV_W = 3 * H_B * DH_B

LANES = 128
VMEM_LIMIT = 56 * 1024 * 1024


def _params(sem):
    return pltpu.CompilerParams(dimension_semantics=sem, vmem_limit_bytes=VMEM_LIMIT)


def _const_spec(shape):
    nd = len(shape)
    return pl.BlockSpec(shape, lambda *_: (0,) * nd, pipeline_mode=pl.Buffered(1))


def _rms(x, g):
    return x * lax.rsqrt(jnp.mean(x * x, axis=-1, keepdims=True) + EPS) * g


def _dot(a, b):
    return jnp.dot(a, b, preferred_element_type=F32)


def _dot_nt(a, b):
    return lax.dot_general(a, b, (((1,), (1,)), ((), ())), preferred_element_type=F32)


def _dot_tn(a, b):
    return lax.dot_general(a, b, (((0,), (0,)), ((), ())), preferred_element_type=F32)


def _ab_proj_kernel(x_ref, g_ref, w_ref, ra_ref, qkv_ref):
    h = _rms(x_ref[...], g_ref[...]).astype(BF16)
    step = 512
    for n0 in range(0, RA_W, step):
        ra_ref[:, n0:n0 + step] = _dot(h, w_ref[:, n0:n0 + step])
    for n0 in range(0, QKV_W, step):
        qkv_ref[:, n0:n0 + step] = _dot(h, w_ref[:, RA_W + n0:RA_W + n0 + step])


def _ab_proj(x, g, w_bf16, tm):
    t, d = x.shape
    return pl.pallas_call(
        _ab_proj_kernel,
        grid=(t // tm,),
        in_specs=[pl.BlockSpec((tm, d), lambda i: (i, 0)),
                  _const_spec((1, d)),
                  _const_spec((d, RA_W + QKV_W))],
        out_specs=[pl.BlockSpec((tm, RA_W), lambda i: (i, 0)),
                   pl.BlockSpec((tm, QKV_W), lambda i: (i, 0))],
        out_shape=[jax.ShapeDtypeStruct((t, RA_W), F32), jax.ShapeDtypeStruct((t, QKV_W), F32)],
        compiler_params=_params(("arbitrary",)),
        name="ab_proj",
    )(x, g, w_bf16)


def _retention_tables(c_real):
    c = RET_CHUNK
    log_g = jnp.log1p(-jnp.exp2(-5.0 - jnp.arange(H_A, dtype=F32)))
    idx = jnp.arange(c, dtype=F32)
    rel = idx[:, None] - idx[None, :]
    dec = jnp.where(rel >= 0, jnp.exp(jnp.maximum(rel, 0.0)[None] * log_g[:, None, None]), 0.0)
    dec = dec.reshape(H_A * c, c)
    real = (idx < c_real)[:, None]
    qd = jnp.exp((idx + 1.0)[:, None] * log_g[None, :])
    qd = jnp.repeat(qd, DV_A, axis=1)
    kd = jnp.where(real, jnp.exp((c_real - 1.0 - idx)[:, None] * log_g[None, :]), 0.0)
    kd = jnp.repeat(kd, DK_A, axis=1)
    row_h = jnp.arange(H_A * DK_A)[:, None] // DK_A
    col_h = jnp.arange(H_A * DV_A)[None, :] // DV_A
    bd = (row_h == col_h).astype(F32)
    dm = bd * jnp.repeat(jnp.exp(c_real * log_g), DV_A)[None, :]
    return dec, qd, kd, dm, bd


def _retention_kernel(ra_ref, s0_ref, gain_ref, dec_ref, qd_ref, kd_ref, dm_ref, bd_ref,
                      a_ref, sout_ref, sbd_ref, *, c_real, n_chunks):
    c = RET_CHUNK
    j = pl.program_id(1)

    @pl.when(j == 0)
    def _():
        sbd_ref[...] = jnp.zeros_like(sbd_ref)
        for h in range(H_A):
            sbd_ref[h * DK_A:(h + 1) * DK_A, h * DV_A:(h + 1) * DV_A] = s0_ref[0, h]

    lane_q = lax.broadcasted_iota(jnp.int32, (c, QA_W), 1) // DK_A
    for ci in range(n_chunks):
        rows = ra_ref[ci * c_real:(ci + 1) * c_real, :]
        gate = rows[:, QA_W + KA_W + VA_W:]
        if c_real < c:
            rows = jnp.concatenate([rows, jnp.zeros((c - c_real, RA_W), F32)], axis=0)
        q = rows[:, :QA_W]
        k = rows[:, QA_W:QA_W + KA_W] * (DK_A ** -0.5)
        v = rows[:, QA_W + KA_W:QA_W + KA_W + VA_W]
        vb = v.astype(BF16)
        qm = jnp.concatenate([jnp.where(lane_q == h, q, 0.0) for h in range(H_A)], axis=0).astype(BF16)
        s = _dot_nt(qm, k.astype(BF16)) * dec_ref[...]
        sb = s.astype(BF16)
        sbd = sbd_ref[...]
        cross = _dot(q.astype(BF16), sbd.astype(BF16)) * qd_ref[...]
        upd = _dot_tn((k * kd_ref[...]).astype(BF16), vb)
        sbd_ref[...] = sbd * dm_ref[...] + upd * bd_ref[...]
        for h in range(H_A):
            sl = slice(h * DV_A, (h + 1) * DV_A)
            o = _dot(sb[h * c:(h + 1) * c], vb[:, sl]) + cross[:, sl]
            o = o[:c_real]
            mu = jnp.mean(o, axis=-1, keepdims=True)
            var = jnp.mean(jnp.square(o - mu), axis=-1, keepdims=True)
            y = (o - mu) * lax.rsqrt(var + EPS) * gain_ref[:, sl]
            gh = gate[:, sl]
            a_ref[ci * c_real:(ci + 1) * c_real, sl] = (gh * jax.nn.sigmoid(gh) * y).astype(a_ref.dtype)

    @pl.when(j == pl.num_programs(1) - 1)
    def _():
        for h in range(H_A):
            sout_ref[0, h] = sbd_ref[h * DK_A:(h + 1) * DK_A, h * DV_A:(h + 1) * DV_A]


def _retention(ra, state0, gain, *, seq, c_real, n_chunks, out_dtype):
    t = ra.shape[0]
    bsz = t // seq
    rows = c_real * n_chunks
    steps = seq // rows
    tables = _retention_tables(c_real)
    kern = functools.partial(_retention_kernel, c_real=c_real, n_chunks=n_chunks)
    return pl.pallas_call(
        kern,
        grid=(bsz, steps),
        in_specs=[pl.BlockSpec((rows, RA_W), lambda b, j: (b * steps + j, 0)),
                  pl.BlockSpec((1, H_A, DK_A, DV_A), lambda b, j: (b, 0, 0, 0)),
                  _const_spec((1, VA_W))] + [_const_spec(tb.shape) for tb in tables],
        out_specs=[pl.BlockSpec((rows, VA_W), lambda b, j: (b * steps + j, 0)),
                   pl.BlockSpec((1, H_A, DK_A, DV_A), lambda b, j: (b, 0, 0, 0))],
        out_shape=[jax.ShapeDtypeStruct((t, VA_W), out_dtype),
                   jax.ShapeDtypeStruct((bsz, H_A, DK_A, DV_A), F32)],
        scratch_shapes=[pltpu.VMEM((H_A * DK_A, H_A * DV_A), F32)],
        compiler_params=_params(("arbitrary", "arbitrary")),
        name="retention",
    )(ra, state0, gain, *tables)


SWA_ROWS = 2048
N_BRANCH = len(SWA_PAIRS)


def _alibi_slopes():
    return jnp.exp2(-8.0 * jnp.arange(1, H_B + 1, dtype=F32) / H_B)


def _swa_prompt_bias():
    qi = jnp.arange(SPAN)[:, None]
    kj = jnp.arange(2 * SPAN)[None, :]
    dist = SPAN + qi - kj
    band = (dist >= 0) & (dist <= SPAN)
    slopes = _alibi_slopes()
    out = []
    for (_, dil) in SWA_PAIRS:
        pen = -slopes[:, None, None] * (dil * dist).astype(F32)[None]
        normal = jnp.where(band[None], pen, NEG_INF)
        first = jnp.where((band & (kj >= SPAN))[None], pen, NEG_INF)
        out.append(jnp.stack([normal, first], axis=1))
    tab = jnp.stack(out, axis=1)
    tab = tab.reshape(H_B // 2, 2, N_BRANCH, 2, SPAN, 2 * SPAN).transpose(0, 2, 3, 1, 4, 5)
    return tab.reshape(H_B // 2, N_BRANCH, 2, 2 * SPAN, 2 * SPAN)


def _swa_prompt_kernel(q_ref, k_ref, v_ref, bias_ref, o_ref, kbuf, vbuf, acc_ref, l_ref, m_ref):
    j = pl.program_id(2)
    rows = SWA_ROWS

    @pl.when(j == 0)
    def _():
        kbuf[0:rows, :] = jnp.zeros((rows, LANES), F32)
        vbuf[0:rows, :] = jnp.zeros((rows, LANES), F32)

    @pl.when(j > 0)
    def _():
        kbuf[0:rows, :] = kbuf[rows:2 * rows, :]
        vbuf[0:rows, :] = vbuf[rows:2 * rows, :]

    kbuf[rows:2 * rows, :] = k_ref[...]
    vbuf[rows:2 * rows, :] = v_ref[...]

    lane = lax.broadcasted_iota(jnp.int32, (SPAN, LANES), 1)
    head0 = lane < DH_B
    ones = jnp.ones((2 * SPAN, LANES), BF16)
    first_step = (j == 0).astype(jnp.int32)

    for g, (_, dil) in enumerate(SWA_PAIRS):
        blocks = rows // (SPAN * dil)

        def unit(u, carry, g=g, dil=dil):
            wb = u // dil
            r = u % dil
            q_start = wb * (SPAN * dil) + r
            k_start = rows + (wb - 1) * (SPAN * dil) + r
            q = q_ref[pl.ds(q_start, SPAN, stride=dil), :] * (DH_B ** -0.5)
            qm = jnp.concatenate([jnp.where(head0, q, 0.0), jnp.where(head0, 0.0, q)], axis=0).astype(BF16)
            kk = kbuf[pl.ds(k_start, 2 * SPAN, stride=dil), :].astype(BF16)
            vv = vbuf[pl.ds(k_start, 2 * SPAN, stride=dil), :].astype(BF16)
            variant = jnp.where(wb == 0, first_step, 0)
            s = _dot_nt(qm, kk) + bias_ref[0, g, variant]
            m = jnp.max(s, axis=-1, keepdims=True)
            p = jnp.exp(s - m).astype(BF16)
            res = _dot(p, jnp.concatenate([vv, ones], axis=1))
            acc = jnp.where(head0, res[:SPAN, :LANES], res[SPAN:, :LANES])
            den = jnp.where(head0, res[:SPAN, LANES:], res[SPAN:, LANES:])
            mm = jnp.where(head0, jnp.broadcast_to(m[:SPAN], (SPAN, LANES)),
                           jnp.broadcast_to(m[SPAN:], (SPAN, LANES)))
            acc_ref[g, pl.ds(q_start, SPAN, stride=dil), :] = acc
            l_ref[g, pl.ds(q_start, SPAN, stride=dil), :] = den
            m_ref[g, pl.ds(q_start, SPAN, stride=dil), :] = mm
            return carry

        lax.fori_loop(0, blocks * dil, unit, 0)

    tile = 256
    for r0 in range(0, rows, tile):
        sl = slice(r0, r0 + tile)
        ms = [m_ref[g, sl, :] for g in range(N_BRANCH)]
        mx = functools.reduce(jnp.maximum, ms)
        ws = [jnp.exp(mg - mx) for mg in ms]
        num = sum(w * acc_ref[g, sl, :] for g, w in enumerate(ws))
        den = sum(w * l_ref[g, sl, :] for g, w in enumerate(ws))
        o_ref[sl, :] = (num / den).astype(o_ref.dtype)


def _swa_prompt(qkv, bsz, seq):
    t = qkv.shape[0]
    steps = seq // SWA_ROWS
    npair = H_B // 2
    bias = _swa_prompt_bias()
    blk = (SWA_ROWS, LANES)
    return pl.pallas_call(
        _swa_prompt_kernel,
        grid=(bsz, npair, steps),
        in_specs=[pl.BlockSpec(blk, lambda b, hp, j: (b * steps + j, hp)),
                  pl.BlockSpec(blk, lambda b, hp, j: (b * steps + j, npair + hp)),
                  pl.BlockSpec(blk, lambda b, hp, j: (b * steps + j, 2 * npair + hp)),
                  pl.BlockSpec((1, N_BRANCH, 2, 2 * SPAN, 2 * SPAN), lambda b, hp, j: (hp, 0, 0, 0, 0))],
        out_specs=pl.BlockSpec(blk, lambda b, hp, j: (b * steps + j, hp)),
        out_shape=jax.ShapeDtypeStruct((t, H_B * DH_B), BF16),
        scratch_shapes=[pltpu.VMEM((2 * SWA_ROWS, LANES), F32), pltpu.VMEM((2 * SWA_ROWS, LANES), F32),
                        pltpu.VMEM((N_BRANCH, SWA_ROWS, LANES), F32),
                        pltpu.VMEM((N_BRANCH, SWA_ROWS, LANES), F32),
                        pltpu.VMEM((N_BRANCH, SWA_ROWS, LANES), F32)],
        compiler_params=_params(("arbitrary", "arbitrary", "arbitrary")),
        name="swa_prompt",
    )(qkv, qkv, qkv, bias)


def _swa_sample_bias(n_new):
    t = jnp.arange(n_new)[:, None]
    slopes = _alibi_slopes()

    def table(j):
        dist = SWA_BUF + t - j
        out = []
        for (window, dil) in SWA_PAIRS:
            valid = (dist >= 0) & (dist <= window) & (dist % dil == 0) & (PAST_LEN + t - dist >= 0)
            pen = -slopes[:, None, None] * dist.astype(F32)[None]
            out.append(jnp.where(valid[None], pen, NEG_INF).reshape(H_B * n_new, -1))
        return jnp.stack(out)

    bias_a = table(jnp.arange(SWA_BUF)[None, :])
    jb = jnp.arange(LANES)[None, :]
    bias_b = jnp.where(jb < n_new, table(SWA_BUF + jb), NEG_INF)
    return bias_a, bias_b


def _swa_sample_kernel(q_ref, k_ref, v_ref, kp_ref, vp_ref, ba_ref, bb_ref,
                       o_ref, ko_ref, vo_ref, *, n_new):
    width = H_B * DH_B
    lane_h = lax.broadcasted_iota(jnp.int32, (n_new, width), 1) // DH_B
    q = q_ref[...] * (DH_B ** -0.5)
    qm = jnp.concatenate([jnp.where(lane_h == h, q, 0.0) for h in range(H_B)], axis=0).astype(BF16)
    pad = jnp.zeros((LANES - n_new, width), F32)
    k_new, v_new = k_ref[...], v_ref[...]
    kp, vp = kp_ref[0], vp_ref[0]
    s_a = _dot_nt(qm, kp.astype(BF16))
    s_b = _dot_nt(qm, jnp.concatenate([k_new, pad], axis=0).astype(BF16))
    sa = [s_a + ba_ref[g] for g in range(N_BRANCH)]
    sb = [s_b + bb_ref[g] for g in range(N_BRANCH)]
    mx = functools.reduce(jnp.maximum, [jnp.max(x, axis=-1, keepdims=True) for x in sa + sb])
    p_a = sum(jnp.exp(x - mx) for x in sa).astype(BF16)
    p_b = sum(jnp.exp(x - mx) for x in sb).astype(BF16)
    den = (jnp.sum(p_a.astype(F32), axis=-1, keepdims=True)
           + jnp.sum(p_b.astype(F32), axis=-1, keepdims=True))
    o = (_dot(p_a, vp.astype(BF16)) + _dot(p_b, jnp.concatenate([v_new, pad], axis=0).astype(BF16))) / den
    o_ref[...] = sum(jnp.where(lane_h == h, o[h * n_new:(h + 1) * n_new], 0.0) for h in range(H_B))
    ko_ref[0, 0:SWA_BUF - n_new, :] = kp[n_new:]
    ko_ref[0, SWA_BUF - n_new:, :] = k_new
    vo_ref[0, 0:SWA_BUF - n_new, :] = vp[n_new:]
    vo_ref[0, SWA_BUF - n_new:, :] = v_new


def _swa_sample(qkv, k_past, v_past, n_new):
    bsz = k_past.shape[0]
    width = H_B * DH_B
    bias_a, bias_b = _swa_sample_bias(n_new)
    state_blk = pl.BlockSpec((1, SWA_BUF, width), lambda b: (b, 0, 0))
    new_blk = lambda col: pl.BlockSpec((n_new, width), lambda b, col=col: (b, col))
    return pl.pallas_call(
        functools.partial(_swa_sample_kernel, n_new=n_new),
        grid=(bsz,),
        in_specs=[new_blk(0), new_blk(1), new_blk(2), state_blk, state_blk,
                  _const_spec(bias_a.shape), _const_spec(bias_b.shape)],
        out_specs=[pl.BlockSpec((n_new, width), lambda b: (b, 0)), state_blk, state_blk],
        out_shape=[jax.ShapeDtypeStruct((bsz * n_new, width), F32),
                   jax.ShapeDtypeStruct(k_past.shape, F32), jax.ShapeDtypeStruct(v_past.shape, F32)],
        compiler_params=_params(("arbitrary",)),
        name="swa_sample",
    )(qkv, qkv, qkv, k_past, v_past, bias_a, bias_b)


FFN_CHUNK = 256


def _ffn(y1, g_ffn_ref, w_in_ref, w_o_ref, acc_ref):
    d_ff = w_o_ref.shape[0]
    h = _rms(y1, g_ffn_ref[...]).astype(BF16)
    for ci, c0 in enumerate(range(0, d_ff, FFN_CHUNK)):
        gate = _dot(h, w_in_ref[:, c0:c0 + FFN_CHUNK])
        up = _dot(h, w_in_ref[:, d_ff + c0:d_ff + c0 + FFN_CHUNK])
        act = (gate * jax.nn.sigmoid(gate) * up).astype(BF16)
        part = _dot(act, w_o_ref[c0:c0 + FFN_CHUNK, :])
        if ci == 0:
            acc_ref[...] = part
        else:
            acc_ref[...] += part
    return y1 + acc_ref[...]


def _ab_tail_kernel(x_ref, a_ref, ob_ref, w_out_ref, g_ffn_ref, w_in_ref, w_o_ref, o_ref, acc_ref):
    mix = (_dot(a_ref[...].astype(BF16), w_out_ref[0:VA_W, :])
           + _dot(ob_ref[...].astype(BF16), w_out_ref[VA_W:, :]))
    o_ref[...] = _ffn(x_ref[...] + mix, g_ffn_ref, w_in_ref, w_o_ref, acc_ref)


def _s5_tail_kernel(x_ref, ys_ref, g_mix_ref, dskip_ref, w_glu_ref, g_ffn_ref, w_in_ref, w_o_ref,
                    g_fin_ref, o_ref, acc_ref):
    x = x_ref[...]
    d = x.shape[-1]
    u = _rms(x, g_mix_ref[...])
    z = jax.nn.gelu(ys_ref[...].astype(F32) + dskip_ref[...] * u, approximate=True).astype(BF16)
    val = _dot(z, w_glu_ref[:, 0:d])
    gate = _dot(z, w_glu_ref[:, d:2 * d])
    y2 = _ffn(x + val * jax.nn.sigmoid(gate), g_ffn_ref, w_in_ref, w_o_ref, acc_ref)
    o_ref[...] = _rms(y2, g_fin_ref[...])


def _row_spec(tm, width):
    return pl.BlockSpec((tm, width), lambda i: (i, 0))


def _ab_tail(x, a, ob, w_out, g_ffn, w_in, w_o, tm):
    t, d = x.shape
    return pl.pallas_call(
        _ab_tail_kernel,
        grid=(t // tm,),
        in_specs=[_row_spec(tm, d), _row_spec(tm, a.shape[1]), _row_spec(tm, ob.shape[1]),
                  _const_spec(w_out.shape), _const_spec((1, d)), _const_spec(w_in.shape), _const_spec(w_o.shape)],
        out_specs=_row_spec(tm, d),
        out_shape=jax.ShapeDtypeStruct((t, d), F32),
        scratch_shapes=[pltpu.VMEM((tm, d), F32)],
        compiler_params=_params(("arbitrary",)),
        name="ab_tail",
    )(x, a, ob, w_out, g_ffn, w_in, w_o)


def _s5_tail(x, ys, g_mix, dskip, w_glu, g_ffn, w_in, w_o, g_fin, tm):
    t, d = x.shape
    return pl.pallas_call(
        _s5_tail_kernel,
        grid=(t // tm,),
        in_specs=[_row_spec(tm, d), _row_spec(tm, d), _const_spec((1, d)), _const_spec((1, d)),
                  _const_spec(w_glu.shape), _const_spec((1, d)), _const_spec(w_in.shape), _const_spec(w_o.shape),
                  _const_spec((1, d))],
        out_specs=_row_spec(tm, d),
        out_shape=jax.ShapeDtypeStruct((t, d), F32),
        scratch_shapes=[pltpu.VMEM((tm, d), F32)],
        compiler_params=_params(("arbitrary",)),
        name="s5_tail",
    )(x, ys, g_mix, dskip, w_glu, g_ffn, w_in, w_o, g_fin)


def _norm_kernel(x_ref, g_ref, o_ref):
    o_ref[...] = _rms(x_ref[...], g_ref[...]).astype(o_ref.dtype)


def _norm_bf16(x, g, tm):
    t, d = x.shape
    return pl.pallas_call(
        _norm_kernel,
        grid=(t // tm,),
        in_specs=[_row_spec(tm, d), _const_spec((1, d))],
        out_specs=_row_spec(tm, d),
        out_shape=jax.ShapeDtypeStruct((t, d), BF16),
        compiler_params=_params(("arbitrary",)),
        name="s5_norm",
    )(x, g)


S5_LP = S5_CHUNK * S5_GROUP
S5_RI = 2 * S5_STATE


def _s5_prep_kernel(lam_row_ref, lam_col_ref, ls_ref, bta_ref, btb_ref, p1_ref, p2_ref,
                    bs_ref, cs_ref, tp_ref, ap_ref):
    n_pow = S5_CHUNK + 1
    dt = jnp.exp(ls_ref[...])

    def powers(lr, li):
        mag = jnp.exp(lr * dt)
        a_re, a_im = mag * jnp.cos(li * dt), mag * jnp.sin(li * dt)
        pw = [(jnp.ones_like(a_re), jnp.zeros_like(a_im))]
        for _ in range(n_pow - 1):
            pr, pi = pw[-1]
            pw.append((pr * a_re - pi * a_im, pr * a_im + pi * a_re))
        return pw

    lr, li = lam_row_ref[0:1, :], lam_row_ref[1:2, :]
    pw_row = powers(lr, li)
    a_re, a_im = pw_row[1]
    den = lr * lr + li * li
    f_re = ((a_re - 1.0) * lr + a_im * li) / den
    f_im = (a_im * lr - (a_re - 1.0) * li) / den
    blk = (S5_GROUP, S5_RI)
    w_re = jnp.concatenate([jnp.broadcast_to(f_re * pw_row[S5_CHUNK - 1 - l][0] - f_im * pw_row[S5_CHUNK - 1 - l][1], blk)
                            for l in range(S5_CHUNK)], axis=0)
    w_im = jnp.concatenate([jnp.broadcast_to(f_re * pw_row[S5_CHUNK - 1 - l][1] + f_im * pw_row[S5_CHUNK - 1 - l][0], blk)
                            for l in range(S5_CHUNK)], axis=0)
    bs_ref[...] = (w_re * bta_ref[...] + w_im * btb_ref[...]).astype(bs_ref.dtype)
    bf_t = f_re * bta_ref[0:S5_GROUP, :] + f_im * btb_ref[0:S5_GROUP, :]

    lane = lax.broadcasted_iota(jnp.int32, (1, S5_RI), 1)
    pick = lambda z: jnp.where(lane < S5_STATE, z[0], z[1])
    ap_ref[...] = jnp.zeros_like(ap_ref)
    ap_ref[0:1, :] = pick(pw_row[S5_CHUNK])
    ap_ref[1:2, :] = pick(pw_row[S5_CHUNK // 2])

    pw_col = powers(lam_col_ref[:, 0:1], lam_col_ref[:, 1:2])
    lblk = lax.broadcasted_iota(jnp.int32, (1, S5_LP), 1) // S5_GROUP

    def spread(j0, part):
        out = jnp.zeros((S5_RI, S5_LP), F32)
        for l in range(S5_CHUNK):
            out = jnp.where(lblk == l, pw_col[j0 + l][part], out)
        return out

    p1, p2 = p1_ref[...], p2_ref[...]
    cs0 = p1 * spread(0, 0) + p2 * spread(0, 1)
    cs_ref[...] = (p1 * spread(1, 0) + p2 * spread(1, 1)).astype(cs_ref.dtype)
    r = jnp.dot(bf_t, cs0, preferred_element_type=F32, precision=lax.Precision.HIGHEST)
    lane_lp = lax.broadcasted_iota(jnp.int32, (S5_GROUP, S5_LP), 1)
    for l in range(S5_CHUNK):
        sh = l * S5_GROUP
        blk_l = r if l == 0 else jnp.where(lane_lp >= sh, pltpu.roll(r, sh, axis=1), 0.0)
        tp_ref[l * S5_GROUP:(l + 1) * S5_GROUP, :] = blk_l.astype(tp_ref.dtype)


def _s5_prep(lam_re, lam_im, log_step, b_re, b_im, c_re, c_im):
    g = lam_re.shape[0]
    dup = lambda z: jnp.concatenate([z, z], axis=-1)
    lam_row = jnp.stack([dup(lam_re), dup(lam_im)], axis=1)
    lam_row = jnp.pad(lam_row, ((0, 0), (0, 6), (0, 0)))
    lam_col = jnp.stack([dup(lam_re), dup(lam_im)], axis=2)
    ls = log_step.reshape(g, 1, 1)
    bt_re = jnp.swapaxes(b_re, 1, 2)
    bt_im = jnp.swapaxes(b_im, 1, 2)
    bta = jnp.tile(jnp.concatenate([bt_re, bt_im], axis=-1), (1, S5_CHUNK, 1))
    btb = jnp.tile(jnp.concatenate([-bt_im, bt_re], axis=-1), (1, S5_CHUNK, 1))
    ct_re = jnp.tile(jnp.swapaxes(c_re, 1, 2), (1, 1, S5_CHUNK))
    ct_im = jnp.tile(jnp.swapaxes(c_im, 1, 2), (1, 1, S5_CHUNK))
    p1 = jnp.concatenate([ct_re, -ct_im], axis=1)
    p2 = jnp.concatenate([-ct_im, -ct_re], axis=1)
    gspec = lambda *s: pl.BlockSpec((None,) + s, lambda i: (i,) + (0,) * len(s))
    return pl.pallas_call(
        _s5_prep_kernel,
        grid=(g,),
        in_specs=[gspec(8, S5_RI), gspec(S5_RI, 2), gspec(1, 1), gspec(S5_LP, S5_RI), gspec(S5_LP, S5_RI),
                  gspec(S5_RI, S5_LP), gspec(S5_RI, S5_LP)],
        out_specs=[gspec(S5_LP, S5_RI), gspec(S5_RI, S5_LP), gspec(S5_LP, S5_LP), gspec(8, S5_RI)],
        out_shape=[jax.ShapeDtypeStruct((g, S5_LP, S5_RI), BF16), jax.ShapeDtypeStruct((g, S5_RI, S5_LP), BF16),
                   jax.ShapeDtypeStruct((g, S5_LP, S5_LP), BF16), jax.ShapeDtypeStruct((g, 8, S5_RI), F32)],
        compiler_params=_params(("arbitrary",)),
        name="s5_prep",
    )(lam_row, lam_col, ls, bta, btb, p1, p2)


def _cmul_const(x, a):
    lane = lax.broadcasted_iota(jnp.int32, (8, S5_RI), 1)
    a_sw = pltpu.roll(a, S5_STATE, axis=1)
    a_rr = jnp.where(lane < S5_STATE, a, a_sw)
    a_is = jnp.where(lane < S5_STATE, -a_sw, a)
    if x.shape[0] != 8:
        a_rr, a_is = a_rr[0:1], a_is[0:1]
    return x * a_rr + pltpu.roll(x, S5_STATE, axis=1) * a_is


def _s5_scan_kernel(u_ref, us_ref, x0_ref, bs_ref, cs_ref, tp_ref, ap_ref,
                    y_ref, ys_ref, fin_ref, fins_ref, *, bsz, chunks):
    n_rows = bsz * chunks
    u = u_ref[...]
    bs, cs, tp = bs_ref[...], cs_ref[...], tp_ref[...]
    x = _dot(u, bs)
    row = lax.broadcasted_iota(jnp.int32, (n_rows, 1), 0) % chunks
    a = jnp.broadcast_to(ap_ref[0:1, :], (8, S5_RI))
    shift = 1
    while shift < chunks:
        xs = jnp.where(row >= shift, pltpu.roll(x, shift, axis=0), 0.0)
        x = x + _cmul_const(xs, a)
        a = _cmul_const(a, a)
        shift *= 2
    prev = jnp.where(row >= 1, pltpu.roll(x, 1, axis=0), 0.0)
    y_ref[...] = (_dot(u, tp) + _dot(prev.astype(BF16), cs)).astype(y_ref.dtype)
    fin_ref[...] = jnp.zeros_like(fin_ref)
    for b in range(bsz):
        fin_ref[b:b + 1, :] = x[(b + 1) * chunks - 1:(b + 1) * chunks]

    half = S5_LP // 2
    us, x0 = us_ref[...], x0_ref[...]
    a_half = jnp.broadcast_to(ap_ref[1:2, :], (8, S5_RI))
    fins_ref[...] = _cmul_const(x0, a_half) + _dot(us, bs[half:, :])
    ys_ref[...] = (_dot(us, tp[:half, :half]) + _dot(x0.astype(BF16), cs[:, :half])).astype(ys_ref.dtype)


def _s5_scan(u_g, us_g, x0_g, bs, cs, tp, ap, bsz):
    g, n_rows, _ = u_g.shape
    n_s = us_g.shape[1]
    gspec = lambda *s: pl.BlockSpec((None,) + s, lambda i: (i,) + (0,) * len(s))
    half = S5_LP // 2
    return pl.pallas_call(
        functools.partial(_s5_scan_kernel, bsz=bsz, chunks=n_rows // bsz),
        grid=(g,),
        in_specs=[gspec(n_rows, S5_LP), gspec(n_s, half), gspec(n_s, S5_RI),
                  gspec(S5_LP, S5_RI), gspec(S5_RI, S5_LP), gspec(S5_LP, S5_LP), gspec(8, S5_RI)],
        out_specs=[gspec(n_rows, S5_LP), gspec(n_s, half), gspec(8, S5_RI), gspec(n_s, S5_RI)],
        out_shape=[jax.ShapeDtypeStruct((g, n_rows, S5_LP), BF16), jax.ShapeDtypeStruct((g, n_s, half), BF16),
                   jax.ShapeDtypeStruct((g, 8, S5_RI), F32), jax.ShapeDtypeStruct((g, n_s, S5_RI), F32)],
        compiler_params=_params(("arbitrary",)),
        name="s5_scan",
    )(u_g, us_g, x0_g, bs, cs, tp, ap)


def _to_groups(u, chunk):
    t, d = u.shape
    g = d // S5_GROUP
    return u.reshape(t // chunk, chunk, g, S5_GROUP).transpose(2, 0, 1, 3).reshape(g, t // chunk, chunk * S5_GROUP)


def _from_groups(y, chunk):
    g, n, _ = y.shape
    return y.reshape(g, n, chunk, S5_GROUP).transpose(1, 2, 0, 3).reshape(n * chunk, g * S5_GROUP)


def kernel(x_prompt, x_sample, state_ret, state_swa_k, state_swa_v, state_ssm_re, state_ssm_im, norm_mix, norm_ffn, norm_final, w_in_ab, ret_gn, w_out_ab, ssm_lam_re, ssm_lam_im, ssm_log_step, ssm_b_re, ssm_b_im, ssm_c_re, ssm_c_im, ssm_d, w_glu, w_ffn_in, w_ffn_out):
    bsz, seq, d = x_prompt.shape
    dbsz, n_new, _ = x_sample.shape
    assert state_swa_k.shape[2] == SWA_BUF and n_new == S5_CHUNK // 2 and seq % SWA_ROWS == 0
    xp = x_prompt.reshape(bsz * seq, d)
    xs = x_sample.reshape(dbsz * n_new, d)
    tm_p, tm_s = 512, dbsz * n_new
    row = lambda v: v.reshape(1, -1)

    w_in0 = w_in_ab[0].astype(BF16)
    w_out0 = w_out_ab[0].astype(BF16)
    w_f_in0, w_f_out0 = w_ffn_in[0].astype(BF16), w_ffn_out[0].astype(BF16)
    g_mix0, g_ffn0, gn0 = row(norm_mix[0]), row(norm_ffn[0]), row(ret_gn[0])

    ra_p, qkv_p = _ab_proj(xp, g_mix0, w_in0, tm_p)
    ra_s, qkv_s = _ab_proj(xs, g_mix0, w_in0, tm_s)

    zero_ret = jnp.zeros((bsz, H_A, DK_A, DV_A), F32)
    a_p, ret_p = _retention(ra_p, zero_ret, gn0, seq=seq, c_real=RET_CHUNK, n_chunks=4, out_dtype=BF16)
    a_s, ret_s = _retention(ra_s, state_ret[0], gn0, seq=n_new, c_real=n_new, n_chunks=1, out_dtype=F32)

    ob_p = _swa_prompt(qkv_p, bsz, seq)
    width_b = H_B * DH_B
    ob_s, swk_s, swv_s = _swa_sample(qkv_s, state_swa_k[0].reshape(dbsz, SWA_BUF, width_b),
                                     state_swa_v[0].reshape(dbsz, SWA_BUF, width_b), n_new)
    kv_tail = qkv_p.reshape(bsz, seq, QKV_W)[:, seq - SWA_BUF:, width_b:]
    swk_p = kv_tail[..., :width_b].reshape(bsz, SWA_BUF, H_B, DH_B)
    swv_p = kv_tail[..., width_b:].reshape(bsz, SWA_BUF, H_B, DH_B)

    yp = _ab_tail(xp, a_p, ob_p, w_out0, g_ffn0, w_f_in0, w_f_out0, tm_p)
    ys = _ab_tail(xs, a_s, ob_s, w_out0, g_ffn0, w_f_in0, w_f_out0, tm_s)

    g_mix1, g_ffn1 = row(norm_mix[1]), row(norm_ffn[1])
    bs, cs, tp, ap = _s5_prep(ssm_lam_re[0], ssm_lam_im[0], ssm_log_step[0],
                              ssm_b_re[0], ssm_b_im[0], ssm_c_re[0], ssm_c_im[0])
    u_p = _to_groups(_norm_bf16(yp, g_mix1, 1024), S5_CHUNK)
    u_s = _to_groups(_norm_bf16(ys, g_mix1, tm_s), n_new)
    x0 = jnp.concatenate([state_ssm_re[0], state_ssm_im[0]], axis=-1).transpose(1, 0, 2)
    y5_p, y5_s, fin_p, fin_s = _s5_scan(u_p, u_s, x0, bs, cs, tp, ap, bsz)
    y5_p = _from_groups(y5_p, S5_CHUNK)
    y5_s = _from_groups(y5_s, n_new)

    tail1 = (g_mix1, row(ssm_d[0]), w_glu[0].astype(BF16), g_ffn1,
             w_ffn_in[1].astype(BF16), w_ffn_out[1].astype(BF16), row(norm_final))
    yp = _s5_tail(yp, y5_p, *tail1, tm_p)
    ys = _s5_tail(ys, y5_s, *tail1, tm_s)

    fin_p = fin_p[:, :bsz].transpose(1, 0, 2)
    fin_s = fin_s.transpose(1, 0, 2)
    return (yp.reshape(bsz, seq, d), ys.reshape(dbsz, n_new, d),
            ret_p[None], ret_s[None],
            swk_p[None], swv_p[None],
            swk_s.reshape(1, dbsz, SWA_BUF, H_B, DH_B), swv_s.reshape(1, dbsz, SWA_BUF, H_B, DH_B),
            fin_p[None, ..., :S5_STATE], fin_p[None, ..., S5_STATE:],
            fin_s[None, ..., :S5_STATE], fin_s[None, ..., S5_STATE:])
```

```python
import functools
import math

import jax
import jax.numpy as jnp
from jax import lax
from jax.experimental import pallas as pl
from jax.experimental.pallas import tpu as pltpu

F32 = jnp.float32
BF16 = jnp.bfloat16

H_A, DK_A, DV_A = 4, 64, 128
H_B, DH_B = 8, 64
SWA_PAIRS = ((128, 1), (512, 4), (2048, 16))
SPAN = 128
SWA_BUF = 2048
PAST_LEN = 16384
RET_CHUNK = 128
S5_GROUP, S5_STATE = 16, 64
S5_CHUNK = 16
EPS = 1e-6
NEG_INF = -1e30
QA_W, KA_W, VA_W, GA_W = H_A * DK_A, H_A * DK_A, H_A * DV_A, H_A * DV_A
RA_W = QA_W + KA_W + VA_W + GA_W
QKV_W = 3 * H_B * DH_B

LANES = 128
VMEM_LIMIT = 56 * 1024 * 1024


def _params(sem):
    return pltpu.CompilerParams(dimension_semantics=sem, vmem_limit_bytes=VMEM_LIMIT)


def _const_spec(shape):
    nd = len(shape)
    return pl.BlockSpec(shape, lambda *_: (0,) * nd, pipeline_mode=pl.Buffered(1))


def _rms(x, g):
    return x * lax.rsqrt(jnp.mean(x * x, axis=-1, keepdims=True) + EPS) * g


def _dot(a, b):
    return jnp.dot(a, b, preferred_element_type=F32)


def _dot_nt(a, b):
    return lax.dot_general(a, b, (((1,), (1,)), ((), ())), preferred_element_type=F32)


def _dot_tn(a, b):
    return lax.dot_general(a, b, (((0,), (0,)), ((), ())), preferred_element_type=F32)


def _ab_proj_kernel(x_ref, g_ref, w_ref, ra_ref, qkv_ref):
    h = _rms(x_ref[...], g_ref[...]).astype(BF16)
    step = 512
    for n0 in range(0, RA_W, step):
        ra_ref[:, n0:n0 + step] = _dot(h, w_ref[:, n0:n0 + step])
    for n0 in range(0, QKV_W, step):
        qkv_ref[:, n0:n0 + step] = _dot(h, w_ref[:, RA_W + n0:RA_W + n0 + step])


def _ab_proj(x, g, w_bf16, tm):
    t, d = x.shape
    return pl.pallas_call(
        _ab_proj_kernel,
        grid=(t // tm,),
        in_specs=[pl.BlockSpec((tm, d), lambda i: (i, 0)),
                  _const_spec((1, d)),
                  _const_spec((d, RA_W + QKV_W))],
        out_specs=[pl.BlockSpec((tm, RA_W), lambda i: (i, 0)),
                   pl.BlockSpec((tm, QKV_W), lambda i: (i, 0))],
        out_shape=[jax.ShapeDtypeStruct((t, RA_W), F32), jax.ShapeDtypeStruct((t, QKV_W), F32)],
        compiler_params=_params(("arbitrary",)),
        name="ab_proj",
    )(x, g, w_bf16)


def _retention_tables(c_real):
    c = RET_CHUNK
    log_g = jnp.log1p(-jnp.exp2(-5.0 - jnp.arange(H_A, dtype=F32)))
    idx = jnp.arange(c, dtype=F32)
    rel = idx[:, None] - idx[None, :]
    dec = jnp.where(rel >= 0, jnp.exp(jnp.maximum(rel, 0.0)[None] * log_g[:, None, None]), 0.0)
    dec = dec.reshape(H_A * c, c)
    real = (idx < c_real)[:, None]
    qd = jnp.exp((idx + 1.0)[:, None] * log_g[None, :])
    qd = jnp.repeat(qd, DV_A, axis=1)
    kd = jnp.where(real, jnp.exp((c_real - 1.0 - idx)[:, None] * log_g[None, :]), 0.0)
    kd = jnp.repeat(kd, DK_A, axis=1)
    row_h = jnp.arange(H_A * DK_A)[:, None] // DK_A
    col_h = jnp.arange(H_A * DV_A)[None, :] // DV_A
    bd = (row_h == col_h).astype(F32)
    dm = bd * jnp.repeat(jnp.exp(c_real * log_g), DV_A)[None, :]
    return dec, qd, kd, dm, bd


def _retention_kernel(ra_ref, s0_ref, gain_ref, dec_ref, qd_ref, kd_ref, dm_ref, bd_ref,
                      a_ref, sout_ref, sbd_ref, *, c_real, n_chunks):
    c = RET_CHUNK
    j = pl.program_id(1)

    @pl.when(j == 0)
    def _():
        sbd_ref[...] = jnp.zeros_like(sbd_ref)
        for h in range(H_A):
            sbd_ref[h * DK_A:(h + 1) * DK_A, h * DV_A:(h + 1) * DV_A] = s0_ref[0, h]

    lane_q = lax.broadcasted_iota(jnp.int32, (c, QA_W), 1) // DK_A
    for ci in range(n_chunks):
        rows = ra_ref[ci * c_real:(ci + 1) * c_real, :]
        gate = rows[:, QA_W + KA_W + VA_W:]
        if c_real < c:
            rows = jnp.concatenate([rows, jnp.zeros((c - c_real, RA_W), F32)], axis=0)
        q = rows[:, :QA_W]
        k = rows[:, QA_W:QA_W + KA_W] * (DK_A ** -0.5)
        v = rows[:, QA_W + KA_W:QA_W + KA_W + VA_W]
        vb = v.astype(BF16)
        qm = jnp.concatenate([jnp.where(lane_q == h, q, 0.0) for h in range(H_A)], axis=0).astype(BF16)
        s = _dot_nt(qm, k.astype(BF16)) * dec_ref[...]
        sb = s.astype(BF16)
        sbd = sbd_ref[...]
        cross = _dot(q.astype(BF16), sbd.astype(BF16)) * qd_ref[...]
        upd = _dot_tn((k * kd_ref[...]).astype(BF16), vb)
        sbd_ref[...] = sbd * dm_ref[...] + upd * bd_ref[...]
        for h in range(H_A):
            sl = slice(h * DV_A, (h + 1) * DV_A)
            o = _dot(sb[h * c:(h + 1) * c], vb[:, sl]) + cross[:, sl]
            o = o[:c_real]
            mu = jnp.mean(o, axis=-1, keepdims=True)
            var = jnp.mean(jnp.square(o - mu), axis=-1, keepdims=True)
            y = (o - mu) * lax.rsqrt(var + EPS) * gain_ref[:, sl]
            gh = gate[:, sl]
            a_ref[ci * c_real:(ci + 1) * c_real, sl] = (gh * jax.nn.sigmoid(gh) * y).astype(a_ref.dtype)

    @pl.when(j == pl.num_programs(1) - 1)
    def _():
        for h in range(H_A):
            sout_ref[0, h] = sbd_ref[h * DK_A:(h + 1) * DK_A, h * DV_A:(h + 1) * DV_A]


def _retention(ra, state0, gain, *, seq, c_real, n_chunks, out_dtype):
    t = ra.shape[0]
    bsz = t // seq
    rows = c_real * n_chunks
    steps = seq // rows
    tables = _retention_tables(c_real)
    kern = functools.partial(_retention_kernel, c_real=c_real, n_chunks=n_chunks)
    return pl.pallas_call(
        kern,
        grid=(bsz, steps),
        in_specs=[pl.BlockSpec((rows, RA_W), lambda b, j: (b * steps + j, 0)),
                  pl.BlockSpec((1, H_A, DK_A, DV_A), lambda b, j: (b, 0, 0, 0)),
                  _const_spec((1, VA_W))] + [_const_spec(tb.shape) for tb in tables],
        out_specs=[pl.BlockSpec((rows, VA_W), lambda b, j: (b * steps + j, 0)),
                   pl.BlockSpec((1, H_A, DK_A, DV_A), lambda b, j: (b, 0, 0, 0))],
        out_shape=[jax.ShapeDtypeStruct((t, VA_W), out_dtype),
                   jax.ShapeDtypeStruct((bsz, H_A, DK_A, DV_A), F32)],
        scratch_shapes=[pltpu.VMEM((H_A * DK_A, H_A * DV_A), F32)],
        compiler_params=_params(("arbitrary", "arbitrary")),
        name="retention",
    )(ra, state0, gain, *tables)


SWA_ROWS = 2048
N_BRANCH = len(SWA_PAIRS)


def _alibi_slopes():
    return jnp.exp2(-8.0 * jnp.arange(1, H_B + 1, dtype=F32) / H_B)


def _swa_prompt_bias():
    qi = jnp.arange(SPAN)[:, None]
    kj = jnp.arange(2 * SPAN)[None, :]
    dist = SPAN + qi - kj
    band = (dist >= 0) & (dist <= SPAN)
    slopes = _alibi_slopes()
    out = []
    for (_, dil) in SWA_PAIRS:
        pen = -slopes[:, None, None] * (dil * dist).astype(F32)[None]
        normal = jnp.where(band[None], pen, NEG_INF)
        first = jnp.where((band & (kj >= SPAN))[None], pen, NEG_INF)
        out.append(jnp.stack([normal, first], axis=1))
    tab = jnp.stack(out, axis=1)
    tab = tab.reshape(H_B // 2, 2, N_BRANCH, 2, SPAN, 2 * SPAN).transpose(0, 2, 3, 1, 4, 5)
    return tab.reshape(H_B // 2, N_BRANCH, 2, 2 * SPAN, 2 * SPAN)


def _swa_prompt_kernel(q_ref, k_ref, v_ref, bias_ref, o_ref, kbuf, vbuf, acc_ref, l_ref, m_ref):
    j = pl.program_id(2)
    rows = SWA_ROWS

    @pl.when(j == 0)
    def _():
        kbuf[0:rows, :] = jnp.zeros((rows, LANES), F32)
        vbuf[0:rows, :] = jnp.zeros((rows, LANES), F32)

    @pl.when(j > 0)
    def _():
        kbuf[0:rows, :] = kbuf[rows:2 * rows, :]
        vbuf[0:rows, :] = vbuf[rows:2 * rows, :]

    kbuf[rows:2 * rows, :] = k_ref[...]
    vbuf[rows:2 * rows, :] = v_ref[...]

    lane = lax.broadcasted_iota(jnp.int32, (SPAN, LANES), 1)
    head0 = lane < DH_B
    ones = jnp.ones((2 * SPAN, LANES), BF16)
    first_step = (j == 0).astype(jnp.int32)

    for g, (_, dil) in enumerate(SWA_PAIRS):
        blocks = rows // (SPAN * dil)

        def unit(u, carry, g=g, dil=dil):
            wb = u // dil
            r = u % dil
            q_start = wb * (SPAN * dil) + r
            k_start = rows + (wb - 1) * (SPAN * dil) + r
            q = q_ref[pl.ds(q_start, SPAN, stride=dil), :] * (DH_B ** -0.5)
            qm = jnp.concatenate([jnp.where(head0, q, 0.0), jnp.where(head0, 0.0, q)], axis=0).astype(BF16)
            kk = kbuf[pl.ds(k_start, 2 * SPAN, stride=dil), :].astype(BF16)
            vv = vbuf[pl.ds(k_start, 2 * SPAN, stride=dil), :].astype(BF16)
            variant = jnp.where(wb == 0, first_step, 0)
            s = _dot_nt(qm, kk) + bias_ref[0, g, variant]
            m = jnp.max(s, axis=-1, keepdims=True)
            p = jnp.exp(s - m).astype(BF16)
            res = _dot(p, jnp.concatenate([vv, ones], axis=1))
            acc = jnp.where(head0, res[:SPAN, :LANES], res[SPAN:, :LANES])
            den = jnp.where(head0, res[:SPAN, LANES:], res[SPAN:, LANES:])
            mm = jnp.where(head0, jnp.broadcast_to(m[:SPAN], (SPAN, LANES)),
                           jnp.broadcast_to(m[SPAN:], (SPAN, LANES)))
            acc_ref[g, pl.ds(q_start, SPAN, stride=dil), :] = acc
            l_ref[g, pl.ds(q_start, SPAN, stride=dil), :] = den
            m_ref[g, pl.ds(q_start, SPAN, stride=dil), :] = mm
            return carry

        lax.fori_loop(0, blocks * dil, unit, 0, unroll=4)

    tile = 256
    for r0 in range(0, rows, tile):
        sl = slice(r0, r0 + tile)
        ms = [m_ref[g, sl, :] for g in range(N_BRANCH)]
        mx = functools.reduce(jnp.maximum, ms)
        ws = [jnp.exp(mg - mx) for mg in ms]
        num = sum(w * acc_ref[g, sl, :] for g, w in enumerate(ws))
        den = sum(w * l_ref[g, sl, :] for g, w in enumerate(ws))
        o_ref[sl, :] = (num / den).astype(o_ref.dtype)


def _swa_prompt(qkv, bsz, seq):
    t = qkv.shape[0]
    steps = seq // SWA_ROWS
    npair = H_B // 2
    bias = _swa_prompt_bias()
    blk = (SWA_ROWS, LANES)
    return pl.pallas_call(
        _swa_prompt_kernel,
        grid=(bsz, npair, steps),
        in_specs=[pl.BlockSpec(blk, lambda b, hp, j: (b * steps + j, hp)),
                  pl.BlockSpec(blk, lambda b, hp, j: (b * steps + j, npair + hp)),
                  pl.BlockSpec(blk, lambda b, hp, j: (b * steps + j, 2 * npair + hp)),
                  pl.BlockSpec((1, N_BRANCH, 2, 2 * SPAN, 2 * SPAN), lambda b, hp, j: (hp, 0, 0, 0, 0))],
        out_specs=pl.BlockSpec(blk, lambda b, hp, j: (b * steps + j, hp)),
        out_shape=jax.ShapeDtypeStruct((t, H_B * DH_B), BF16),
        scratch_shapes=[pltpu.VMEM((2 * SWA_ROWS, LANES), F32), pltpu.VMEM((2 * SWA_ROWS, LANES), F32),
                        pltpu.VMEM((N_BRANCH, SWA_ROWS, LANES), F32),
                        pltpu.VMEM((N_BRANCH, SWA_ROWS, LANES), F32),
                        pltpu.VMEM((N_BRANCH, SWA_ROWS, LANES), F32)],
        compiler_params=_params(("arbitrary", "arbitrary", "arbitrary")),
        name="swa_prompt",
    )(qkv, qkv, qkv, bias)


def _swa_sample_bias(n_new):
    t = jnp.arange(n_new)[:, None]
    slopes = _alibi_slopes()

    def table(j):
        dist = SWA_BUF + t - j
        out = []
        for (window, dil) in SWA_PAIRS:
            valid = (dist >= 0) & (dist <= window) & (dist % dil == 0) & (PAST_LEN + t - dist >= 0)
            pen = -slopes[:, None, None] * dist.astype(F32)[None]
            out.append(jnp.where(valid[None], pen, NEG_INF).reshape(H_B * n_new, -1))
        return jnp.stack(out)

    bias_a = table(jnp.arange(SWA_BUF)[None, :])
    jb = jnp.arange(LANES)[None, :]
    bias_b = jnp.where(jb < n_new, table(SWA_BUF + jb), NEG_INF)
    return bias_a, bias_b


def _swa_sample_kernel(q_ref, k_ref, v_ref, kp_ref, vp_ref, ba_ref, bb_ref,
                       o_ref, ko_ref, vo_ref, *, n_new):
    width = H_B * DH_B
    lane_h = lax.broadcasted_iota(jnp.int32, (n_new, width), 1) // DH_B
    q = q_ref[...] * (DH_B ** -0.5)
    qm = jnp.concatenate([jnp.where(lane_h == h, q, 0.0) for h in range(H_B)], axis=0).astype(BF16)
    pad = jnp.zeros((LANES - n_new, width), F32)
    k_new, v_new = k_ref[...], v_ref[...]
    kp, vp = kp_ref[0], vp_ref[0]
    s_a = _dot_nt(qm, kp.astype(BF16))
    s_b = _dot_nt(qm, jnp.concatenate([k_new, pad], axis=0).astype(BF16))
    sa = [s_a + ba_ref[g] for g in range(N_BRANCH)]
    sb = [s_b + bb_ref[g] for g in range(N_BRANCH)]
    mx = functools.reduce(jnp.maximum, [jnp.max(x, axis=-1, keepdims=True) for x in sa + sb])
    p_a = sum(jnp.exp(x - mx) for x in sa).astype(BF16)
    p_b = sum(jnp.exp(x - mx) for x in sb).astype(BF16)
    den = (jnp.sum(p_a.astype(F32), axis=-1, keepdims=True)
           + jnp.sum(p_b.astype(F32), axis=-1, keepdims=True))
    o = (_dot(p_a, vp.astype(BF16)) + _dot(p_b, jnp.concatenate([v_new, pad], axis=0).astype(BF16))) / den
    o_ref[...] = sum(jnp.where(lane_h == h, o[h * n_new:(h + 1) * n_new], 0.0) for h in range(H_B))
    ko_ref[0, 0:SWA_BUF - n_new, :] = kp[n_new:]
    ko_ref[0, SWA_BUF - n_new:, :] = k_new
    vo_ref[0, 0:SWA_BUF - n_new, :] = vp[n_new:]
    vo_ref[0, SWA_BUF - n_new:, :] = v_new


def _swa_sample(qkv, k_past, v_past, n_new):
    bsz = k_past.shape[0]
    width = H_B * DH_B
    bias_a, bias_b = _swa_sample_bias(n_new)
    state_blk = pl.BlockSpec((1, SWA_BUF, width), lambda b: (b, 0, 0))
    new_blk = lambda col: pl.BlockSpec((n_new, width), lambda b, col=col: (b, col))
    return pl.pallas_call(
        functools.partial(_swa_sample_kernel, n_new=n_new),
        grid=(bsz,),
        in_specs=[new_blk(0), new_blk(1), new_blk(2), state_blk, state_blk,
                  _const_spec(bias_a.shape), _const_spec(bias_b.shape)],
        out_specs=[pl.BlockSpec((n_new, width), lambda b: (b, 0)), state_blk, state_blk],
        out_shape=[jax.ShapeDtypeStruct((bsz * n_new, width), F32),
                   jax.ShapeDtypeStruct(k_past.shape, F32), jax.ShapeDtypeStruct(v_past.shape, F32)],
        compiler_params=_params(("arbitrary",)),
        name="swa_sample",
    )(qkv, qkv, qkv, k_past, v_past, bias_a, bias_b)


FFN_CHUNK = 256


def _ffn(y1, g_ffn_ref, w_in_ref, w_o_ref, acc_ref):
    d_ff = w_o_ref.shape[0]
    h = _rms(y1, g_ffn_ref[...]).astype(BF16)
    for ci, c0 in enumerate(range(0, d_ff, FFN_CHUNK)):
        gate = _dot(h, w_in_ref[:, c0:c0 + FFN_CHUNK])
        up = _dot(h, w_in_ref[:, d_ff + c0:d_ff + c0 + FFN_CHUNK])
        act = (gate * jax.nn.sigmoid(gate) * up).astype(BF16)
        part = _dot(act, w_o_ref[c0:c0 + FFN_CHUNK, :])
        if ci == 0:
            acc_ref[...] = part
        else:
            acc_ref[...] += part
    return y1 + acc_ref[...]


def _ab_tail_kernel(x_ref, a_ref, ob_ref, w_out_ref, g_ffn_ref, w_in_ref, w_o_ref, o_ref, acc_ref):
    mix = (_dot(a_ref[...].astype(BF16), w_out_ref[0:VA_W, :])
           + _dot(ob_ref[...].astype(BF16), w_out_ref[VA_W:, :]))
    o_ref[...] = _ffn(x_ref[...] + mix, g_ffn_ref, w_in_ref, w_o_ref, acc_ref)


def _s5_tail_kernel(x_ref, ys_ref, g_mix_ref, dskip_ref, w_glu_ref, g_ffn_ref, w_in_ref, w_o_ref,
                    g_fin_ref, o_ref, acc_ref):
    x = x_ref[...]
    d = x.shape[-1]
    u = _rms(x, g_mix_ref[...])
    z = jax.nn.gelu(ys_ref[...].astype(F32) + dskip_ref[...] * u, approximate=True).astype(BF16)
    val = _dot(z, w_glu_ref[:, 0:d])
    gate = _dot(z, w_glu_ref[:, d:2 * d])
    y2 = _ffn(x + val * jax.nn.sigmoid(gate), g_ffn_ref, w_in_ref, w_o_ref, acc_ref)
    o_ref[...] = _rms(y2, g_fin_ref[...])


def _row_spec(tm, width):
    return pl.BlockSpec((tm, width), lambda i: (i, 0))


def _ab_tail(x, a, ob, w_out, g_ffn, w_in, w_o, tm):
    t, d = x.shape
    return pl.pallas_call(
        _ab_tail_kernel,
        grid=(t // tm,),
        in_specs=[_row_spec(tm, d), _row_spec(tm, a.shape[1]), _row_spec(tm, ob.shape[1]),
                  _const_spec(w_out.shape), _const_spec((1, d)), _const_spec(w_in.shape), _const_spec(w_o.shape)],
        out_specs=_row_spec(tm, d),
        out_shape=jax.ShapeDtypeStruct((t, d), F32),
        scratch_shapes=[pltpu.VMEM((tm, d), F32)],
        compiler_params=_params(("arbitrary",)),
        name="ab_tail",
    )(x, a, ob, w_out, g_ffn, w_in, w_o)


def _s5_tail(x, ys, g_mix, dskip, w_glu, g_ffn, w_in, w_o, g_fin, tm):
    t, d = x.shape
    return pl.pallas_call(
        _s5_tail_kernel,
        grid=(t // tm,),
        in_specs=[_row_spec(tm, d), _row_spec(tm, d), _const_spec((1, d)), _const_spec((1, d)),
                  _const_spec(w_glu.shape), _const_spec((1, d)), _const_spec(w_in.shape), _const_spec(w_o.shape),
                  _const_spec((1, d))],
        out_specs=_row_spec(tm, d),
        out_shape=jax.ShapeDtypeStruct((t, d), F32),
        scratch_shapes=[pltpu.VMEM((tm, d), F32)],
        compiler_params=_params(("arbitrary",)),
        name="s5_tail",
    )(x, ys, g_mix, dskip, w_glu, g_ffn, w_in, w_o, g_fin)


SLAB_GROUPS = LANES // S5_GROUP
S5_MM_ROWS = 256


def _slab_perm():
    idx = jnp.arange(SLAB_GROUPS * LANES)
    l8, g8, p = idx // LANES, (idx % LANES) // S5_GROUP, idx % S5_GROUP
    dst = g8 * LANES + l8 * S5_GROUP + p
    return (dst[:, None] == idx[None, :]).astype(BF16)


def _permute_rows(z_ref, perm_ref, emit):
    for r0 in range(0, z_ref.shape[0], S5_MM_ROWS):
        emit(r0, _dot(z_ref[r0:r0 + S5_MM_ROWS, :], perm_ref[...]))


def _s5_group_kernel(x_ref, g_ref, perm_ref, o_ref, h_ref, z_ref, *, chunk):
    rows = x_ref.shape[0]
    nc = rows // chunk
    n_slab = x_ref.shape[1] // LANES
    n_col = chunk // SLAB_GROUPS
    h = _rms(x_ref[...], g_ref[...])
    for v in range(n_slab):
        h_ref[v] = h[:, v * LANES:(v + 1) * LANES]
    for v in range(n_slab):
        for l in range(chunk):
            j, l8 = l // SLAB_GROUPS, l % SLAB_GROUPS
            r0 = (v * n_col + j) * nc
            z_ref[r0:r0 + nc, l8 * LANES:(l8 + 1) * LANES] = h_ref[v, pl.ds(l, nc, stride=chunk), :].astype(BF16)

    def emit(r0, blk):
        for q in range(S5_MM_ROWS // nc):
            v, j = divmod(r0 // nc + q, n_col)
            for g8 in range(SLAB_GROUPS):
                o_ref[v * SLAB_GROUPS + g8, :, j * LANES:(j + 1) * LANES] = (
                    blk[q * nc:(q + 1) * nc, g8 * LANES:(g8 + 1) * LANES].astype(o_ref.dtype))

    _permute_rows(z_ref, perm_ref, emit)


def _s5_group(x, g, perm, chunk, rows):
    t, d = x.shape
    groups, n_slab, nc = d // S5_GROUP, d // LANES, rows // chunk
    n_col = chunk // SLAB_GROUPS
    assert S5_MM_ROWS % nc == 0 and (n_slab * n_col * nc) % S5_MM_ROWS == 0
    return pl.pallas_call(
        functools.partial(_s5_group_kernel, chunk=chunk),
        grid=(t // rows,),
        in_specs=[_row_spec(rows, d), _const_spec((1, d)), _const_spec(perm.shape)],
        out_specs=pl.BlockSpec((groups, nc, chunk * S5_GROUP), lambda i: (0, i, 0)),
        out_shape=jax.ShapeDtypeStruct((groups, t // chunk, chunk * S5_GROUP), BF16),
        scratch_shapes=[pltpu.VMEM((n_slab, rows, LANES), F32),
                        pltpu.VMEM((n_slab * n_col * nc, SLAB_GROUPS * LANES), BF16)],
        compiler_params=_params(("arbitrary",)),
        name="s5_group",
    )(x, g, perm)


def _s5_ungroup_kernel(y_ref, perm_ref, o_ref, z_ref, t_ref, *, chunk):
    rows = o_ref.shape[0]
    nc = rows // chunk
    n_slab = o_ref.shape[1] // LANES
    n_col = chunk // SLAB_GROUPS
    for v in range(n_slab):
        for j in range(n_col):
            r0 = (v * n_col + j) * nc
            for g8 in range(SLAB_GROUPS):
                z_ref[r0:r0 + nc, g8 * LANES:(g8 + 1) * LANES] = y_ref[v * SLAB_GROUPS + g8, :, j * LANES:(j + 1) * LANES]

    def emit(r0, blk):
        for q in range(S5_MM_ROWS // nc):
            v, j = divmod(r0 // nc + q, n_col)
            for l8 in range(SLAB_GROUPS):
                t_ref[v, pl.ds(j * SLAB_GROUPS + l8, nc, stride=chunk), :] = (
                    blk[q * nc:(q + 1) * nc, l8 * LANES:(l8 + 1) * LANES])

    _permute_rows(z_ref, perm_ref, emit)
    for v in range(n_slab):
        o_ref[:, v * LANES:(v + 1) * LANES] = t_ref[v].astype(o_ref.dtype)


def _s5_ungroup(y, perm_t, chunk, rows):
    groups, n_chunks, _ = y.shape
    d, t = groups * S5_GROUP, n_chunks * chunk
    n_slab, nc, n_col = d // LANES, rows // chunk, chunk // SLAB_GROUPS
    assert S5_MM_ROWS % nc == 0 and (n_slab * n_col * nc) % S5_MM_ROWS == 0
    return pl.pallas_call(
        functools.partial(_s5_ungroup_kernel, chunk=chunk),
        grid=(t // rows,),
        in_specs=[pl.BlockSpec((groups, nc, chunk * S5_GROUP), lambda i: (0, i, 0)), _const_spec(perm_t.shape)],
        out_specs=_row_spec(rows, d),
        out_shape=jax.ShapeDtypeStruct((t, d), BF16),
        scratch_shapes=[pltpu.VMEM((n_slab * n_col * nc, SLAB_GROUPS * LANES), BF16),
                        pltpu.VMEM((n_slab, rows, LANES), F32)],
        compiler_params=_params(("arbitrary",)),
        name="s5_ungroup",
    )(y, perm_t)


S5_LP = S5_CHUNK * S5_GROUP
S5_RI = 2 * S5_STATE


def _s5_prep_kernel(lam_row_ref, ls_ref, bta_ref, btb_ref, ca_ref, cb_ref,
                    bs_ref, cs_ref, tp_ref, ap_ref):
    n_pow = S5_CHUNK + 1
    dt = jnp.exp(ls_ref[...])

    def powers(lr, li):
        mag = jnp.exp(lr * dt)
        a_re, a_im = mag * jnp.cos(li * dt), mag * jnp.sin(li * dt)
        pw = [(jnp.ones_like(a_re), jnp.zeros_like(a_im))]
        for _ in range(n_pow - 1):
            pr, pi = pw[-1]
            pw.append((pr * a_re - pi * a_im, pr * a_im + pi * a_re))
        return pw

    lr, li = lam_row_ref[0:1, :], lam_row_ref[1:2, :]
    pw_row = powers(lr, li)
    a_re, a_im = pw_row[1]
    den = lr * lr + li * li
    f_re = ((a_re - 1.0) * lr + a_im * li) / den
    f_im = (a_im * lr - (a_re - 1.0) * li) / den
    blk = (S5_GROUP, S5_RI)
    w_re = jnp.concatenate([jnp.broadcast_to(f_re * pw_row[S5_CHUNK - 1 - l][0] - f_im * pw_row[S5_CHUNK - 1 - l][1], blk)
                            for l in range(S5_CHUNK)], axis=0)
    w_im = jnp.concatenate([jnp.broadcast_to(f_re * pw_row[S5_CHUNK - 1 - l][1] + f_im * pw_row[S5_CHUNK - 1 - l][0], blk)
                            for l in range(S5_CHUNK)], axis=0)
    bs_ref[...] = (w_re * bta_ref[...] + w_im * btb_ref[...]).astype(bs_ref.dtype)
    bf_t = f_re * bta_ref[0:S5_GROUP, :] + f_im * btb_ref[0:S5_GROUP, :]

    lane = lax.broadcasted_iota(jnp.int32, (1, S5_RI), 1)
    pick = lambda z: jnp.where(lane < S5_STATE, z[0], z[1])
    ap_ref[...] = jnp.zeros_like(ap_ref)
    ap_ref[0:1, :] = pick(pw_row[S5_CHUNK])
    ap_ref[1:2, :] = pick(pw_row[S5_CHUNK // 2])

    def spread(j0, part):
        return jnp.concatenate([jnp.broadcast_to(pw_row[j0 + l][part], blk) for l in range(S5_CHUNK)], axis=0)

    ca, cb = ca_ref[...], cb_ref[...]
    cs0 = ca * spread(0, 0) + cb * spread(0, 1)
    cs_ref[...] = (ca * spread(1, 0) + cb * spread(1, 1)).astype(cs_ref.dtype)
    r = lax.dot_general(bf_t, cs0, (((1,), (1,)), ((), ())), preferred_element_type=F32,
                        precision=lax.Precision.HIGHEST)
    lane_lp = lax.broadcasted_iota(jnp.int32, (S5_GROUP, S5_LP), 1)
    for l in range(S5_CHUNK):
        sh = l * S5_GROUP
        blk_l = r if l == 0 else jnp.where(lane_lp >= sh, pltpu.roll(r, sh, axis=1), 0.0)
        tp_ref[l * S5_GROUP:(l + 1) * S5_GROUP, :] = blk_l.astype(tp_ref.dtype)


def _s5_prep(lam_re, lam_im, log_step, b_re, b_im, c_re, c_im):
    g = lam_re.shape[0]
    dup = lambda z: jnp.concatenate([z, z], axis=-1)
    lam_row = jnp.stack([dup(lam_re), dup(lam_im)], axis=1)
    lam_row = jnp.pad(lam_row, ((0, 0), (0, 6), (0, 0)))
    ls = log_step.reshape(g, 1, 1)
    bt_re = jnp.swapaxes(b_re, 1, 2)
    bt_im = jnp.swapaxes(b_im, 1, 2)
    bta = jnp.tile(jnp.concatenate([bt_re, bt_im], axis=-1), (1, S5_CHUNK, 1))
    btb = jnp.tile(jnp.concatenate([-bt_im, bt_re], axis=-1), (1, S5_CHUNK, 1))
    ca = jnp.tile(jnp.concatenate([c_re, -c_im], axis=-1), (1, S5_CHUNK, 1))
    cb = jnp.tile(jnp.concatenate([-c_im, -c_re], axis=-1), (1, S5_CHUNK, 1))
    gspec = lambda *s: pl.BlockSpec((None,) + s, lambda i: (i,) + (0,) * len(s))
    return pl.pallas_call(
        _s5_prep_kernel,
        grid=(g,),
        in_specs=[gspec(8, S5_RI), gspec(1, 1)] + [gspec(S5_LP, S5_RI)] * 4,
        out_specs=[gspec(S5_LP, S5_RI), gspec(S5_LP, S5_RI), gspec(S5_LP, S5_LP), gspec(8, S5_RI)],
        out_shape=[jax.ShapeDtypeStruct((g, S5_LP, S5_RI), BF16), jax.ShapeDtypeStruct((g, S5_LP, S5_RI), BF16),
                   jax.ShapeDtypeStruct((g, S5_LP, S5_LP), BF16), jax.ShapeDtypeStruct((g, 8, S5_RI), F32)],
        compiler_params=_params(("arbitrary",)),
        name="s5_prep",
    )(lam_row, ls, bta, btb, ca, cb)


def _cmul_const(x, a):
    lane = lax.broadcasted_iota(jnp.int32, (8, S5_RI), 1)
    a_sw = pltpu.roll(a, S5_STATE, axis=1)
    a_rr = jnp.where(lane < S5_STATE, a, a_sw)
    a_is = jnp.where(lane < S5_STATE, -a_sw, a)
    if x.shape[0] != 8:
        a_rr, a_is = a_rr[0:1], a_is[0:1]
    return x * a_rr + pltpu.roll(x, S5_STATE, axis=1) * a_is


def _s5_scan_kernel(u_ref, us_ref, x0_ref, bs_ref, cs_ref, tp_ref, ap_ref,
                    y_ref, ys_ref, fin_ref, fins_ref, *, bsz, chunks):
    n_rows = bsz * chunks
    u = u_ref[...]
    bs, cs, tp = bs_ref[...], cs_ref[...], tp_ref[...]
    x = _dot(u, bs)
    row = lax.broadcasted_iota(jnp.int32, (n_rows, 1), 0) % chunks
    a = jnp.broadcast_to(ap_ref[0:1, :], (8, S5_RI))
    shift = 1
    while shift < chunks:
        xs = jnp.where(row >= shift, pltpu.roll(x, shift, axis=0), 0.0)
        x = x + _cmul_const(xs, a)
        a = _cmul_const(a, a)
        shift *= 2
    prev = jnp.where(row >= 1, pltpu.roll(x, 1, axis=0), 0.0)
    y_ref[...] = (_dot(u, tp) + _dot_nt(prev.astype(BF16), cs)).astype(y_ref.dtype)
    fin_ref[...] = jnp.zeros_like(fin_ref)
    for b in range(bsz):
        fin_ref[b:b + 1, :] = x[(b + 1) * chunks - 1:(b + 1) * chunks]

    half = S5_LP // 2
    us, x0 = us_ref[...], x0_ref[...]
    a_half = jnp.broadcast_to(ap_ref[1:2, :], (8, S5_RI))
    fins_ref[...] = _cmul_const(x0, a_half) + _dot(us, bs[half:, :])
    ys_ref[...] = (_dot(us, tp[:half, :half]) + _dot_nt(x0.astype(BF16), cs[:half, :])).astype(ys_ref.dtype)


def _s5_scan(u_g, us_g, x0_g, bs, cs, tp, ap, bsz):
    g, n_rows, _ = u_g.shape
    n_s = us_g.shape[1]
    gspec = lambda *s: pl.BlockSpec((None,) + s, lambda i: (i,) + (0,) * len(s))
    half = S5_LP // 2
    return pl.pallas_call(
        functools.partial(_s5_scan_kernel, bsz=bsz, chunks=n_rows // bsz),
        grid=(g,),
        in_specs=[gspec(n_rows, S5_LP), gspec(n_s, half), gspec(n_s, S5_RI),
                  gspec(S5_LP, S5_RI), gspec(S5_LP, S5_RI), gspec(S5_LP, S5_LP), gspec(8, S5_RI)],
        out_specs=[gspec(n_rows, S5_LP), gspec(n_s, half), gspec(8, S5_RI), gspec(n_s, S5_RI)],
        out_shape=[jax.ShapeDtypeStruct((g, n_rows, S5_LP), BF16), jax.ShapeDtypeStruct((g, n_s, half), BF16),
                   jax.ShapeDtypeStruct((g, 8, S5_RI), F32), jax.ShapeDtypeStruct((g, n_s, S5_RI), F32)],
        compiler_params=_params(("arbitrary",)),
        name="s5_scan",
    )(u_g, us_g, x0_g, bs, cs, tp, ap)


def kernel(x_prompt, x_sample, state_ret, state_swa_k, state_swa_v, state_ssm_re, state_ssm_im, norm_mix, norm_ffn, norm_final, w_in_ab, ret_gn, w_out_ab, ssm_lam_re, ssm_lam_im, ssm_log_step, ssm_b_re, ssm_b_im, ssm_c_re, ssm_c_im, ssm_d, w_glu, w_ffn_in, w_ffn_out):
    bsz, seq, d = x_prompt.shape
    dbsz, n_new, _ = x_sample.shape
    assert state_swa_k.shape[2] == SWA_BUF and n_new == S5_CHUNK // 2 and seq % SWA_ROWS == 0
    xp = x_prompt.reshape(bsz * seq, d)
    xs = x_sample.reshape(dbsz * n_new, d)
    tm_p, tm_s = 512, dbsz * n_new
    row = lambda v: v.reshape(1, -1)

    w_in0 = w_in_ab[0].astype(BF16)
    w_out0 = w_out_ab[0].astype(BF16)
    w_f_in0, w_f_out0 = w_ffn_in[0].astype(BF16), w_ffn_out[0].astype(BF16)
    g_mix0, g_ffn0, gn0 = row(norm_mix[0]), row(norm_ffn[0]), row(ret_gn[0])

    ra_p, qkv_p = _ab_proj(xp, g_mix0, w_in0, tm_p)
    ra_s, qkv_s = _ab_proj(xs, g_mix0, w_in0, tm_s)

    zero_ret = jnp.zeros((bsz, H_A, DK_A, DV_A), F32)
    a_p, ret_p = _retention(ra_p, zero_ret, gn0, seq=seq, c_real=RET_CHUNK, n_chunks=4, out_dtype=BF16)
    a_s, ret_s = _retention(ra_s, state_ret[0], gn0, seq=n_new, c_real=n_new, n_chunks=1, out_dtype=F32)

    ob_p = _swa_prompt(qkv_p, bsz, seq)
    width_b = H_B * DH_B
    ob_s, swk_s, swv_s = _swa_sample(qkv_s, state_swa_k[0].reshape(dbsz, SWA_BUF, width_b),
                                     state_swa_v[0].reshape(dbsz, SWA_BUF, width_b), n_new)
    kv_tail = qkv_p.reshape(bsz, seq, QKV_W)[:, seq - SWA_BUF:, width_b:]
    swk_p = kv_tail[..., :width_b].reshape(bsz, SWA_BUF, H_B, DH_B)
    swv_p = kv_tail[..., width_b:].reshape(bsz, SWA_BUF, H_B, DH_B)

    yp = _ab_tail(xp, a_p, ob_p, w_out0, g_ffn0, w_f_in0, w_f_out0, tm_p)
    ys = _ab_tail(xs, a_s, ob_s, w_out0, g_ffn0, w_f_in0, w_f_out0, tm_s)

    g_mix1, g_ffn1 = row(norm_mix[1]), row(norm_ffn[1])
    bs, cs, tp, ap = _s5_prep(ssm_lam_re[0], ssm_lam_im[0], ssm_log_step[0],
                              ssm_b_re[0], ssm_b_im[0], ssm_c_re[0], ssm_c_im[0])
    perm = _slab_perm()
    u_p = _s5_group(yp, g_mix1, perm, S5_CHUNK, 1024)
    u_s = _s5_group(ys, g_mix1, perm, n_new, tm_s)
    x0 = jnp.concatenate([state_ssm_re[0], state_ssm_im[0]], axis=-1).transpose(1, 0, 2)
    y5_p, y5_s, fin_p, fin_s = _s5_scan(u_p, u_s, x0, bs, cs, tp, ap, bsz)
    y5_p = _s5_ungroup(y5_p, perm.T, S5_CHUNK, 1024)
    y5_s = _s5_ungroup(y5_s, perm.T, n_new, tm_s)

    tail1 = (g_mix1, row(ssm_d[0]), w_glu[0].astype(BF16), g_ffn1,
             w_ffn_in[1].astype(BF16), w_ffn_out[1].astype(BF16), row(norm_final))
    yp = _s5_tail(yp, y5_p, *tail1, tm_p)
    ys = _s5_tail(ys, y5_s, *tail1, tm_s)

    fin_p = fin_p[:, :bsz].transpose(1, 0, 2)
    fin_s = fin_s.transpose(1, 0, 2)
    return (yp.reshape(bsz, seq, d), ys.reshape(dbsz, n_new, d),
            ret_p[None], ret_s[None],
            swk_p[None], swv_p[None],
            swk_s.reshape(1, dbsz, SWA_BUF, H_B, DH_B), swv_s.reshape(1, dbsz, SWA_BUF, H_B, DH_B),
            fin_p[None, ..., :S5_STATE], fin_p[None, ..., S5_STATE:],
            fin_s[None, ..., :S5_STATE], fin_s[None, ..., S5_STATE:])
```

```python
import functools
import math

import jax
import jax.numpy as jnp
from jax import lax
from jax.experimental import pallas as pl
from jax.experimental.pallas import tpu as pltpu

F32 = jnp.float32
BF16 = jnp.bfloat16

H_A, DK_A, DV_A = 4, 64, 128
H_B, DH_B = 8, 64
SWA_PAIRS = ((128, 1), (512, 4), (2048, 16))
SPAN = 128
SWA_BUF = 2048
PAST_LEN = 16384
RET_CHUNK = 128
S5_GROUP, S5_STATE = 16, 64
S5_CHUNK = 16
EPS = 1e-6
NEG_INF = -1e30
QA_W, KA_W, VA_W, GA_W = H_A * DK_A, H_A * DK_A, H_A * DV_A, H_A * DV_A
RA_W = QA_W + KA_W + VA_W + GA_W
QKV_W = 3 * H_B * DH_B

LANES = 128
VMEM_LIMIT = 56 * 1024 * 1024


def _params(sem):
    return pltpu.CompilerParams(dimension_semantics=sem, vmem_limit_bytes=VMEM_LIMIT)


def _const_spec(shape):
    nd = len(shape)
    return pl.BlockSpec(shape, lambda *_: (0,) * nd, pipeline_mode=pl.Buffered(1))


def _rms(x, g):
    return x * lax.rsqrt(jnp.mean(x * x, axis=-1, keepdims=True) + EPS) * g


def _dot(a, b):
    return jnp.dot(a, b, preferred_element_type=F32)


def _dot_nt(a, b):
    return lax.dot_general(a, b, (((1,), (1,)), ((), ())), preferred_element_type=F32)


def _dot_tn(a, b):
    return lax.dot_general(a, b, (((0,), (0,)), ((), ())), preferred_element_type=F32)


def _ab_proj_kernel(x_ref, g_ref, w_ref, ra_ref, qkv_ref):
    h = _rms(x_ref[...], g_ref[...]).astype(BF16)
    step = 512
    for n0 in range(0, RA_W, step):
        ra_ref[:, n0:n0 + step] = _dot(h, w_ref[:, n0:n0 + step])
    for n0 in range(0, QKV_W, step):
        qkv_ref[:, n0:n0 + step] = _dot(h, w_ref[:, RA_W + n0:RA_W + n0 + step])


def _ab_proj(x, g, w_bf16, tm):
    t, d = x.shape
    return pl.pallas_call(
        _ab_proj_kernel,
        grid=(t // tm,),
        in_specs=[pl.BlockSpec((tm, d), lambda i: (i, 0)),
                  _const_spec((1, d)),
                  _const_spec((d, RA_W + QKV_W))],
        out_specs=[pl.BlockSpec((tm, RA_W), lambda i: (i, 0)),
                   pl.BlockSpec((tm, QKV_W), lambda i: (i, 0))],
        out_shape=[jax.ShapeDtypeStruct((t, RA_W), F32), jax.ShapeDtypeStruct((t, QKV_W), F32)],
        compiler_params=_params(("arbitrary",)),
        name="ab_proj",
    )(x, g, w_bf16)


def _retention_tables(c_real):
    c = RET_CHUNK
    log_g = jnp.log1p(-jnp.exp2(-5.0 - jnp.arange(H_A, dtype=F32)))
    idx = jnp.arange(c, dtype=F32)
    rel = idx[:, None] - idx[None, :]
    dec = jnp.where(rel >= 0, jnp.exp(jnp.maximum(rel, 0.0)[None] * log_g[:, None, None]), 0.0)
    dec = dec.reshape(H_A * c, c)
    real = (idx < c_real)[:, None]
    qd = jnp.exp((idx + 1.0)[:, None] * log_g[None, :])
    qd = jnp.repeat(qd, DV_A, axis=1)
    kd = jnp.where(real, jnp.exp((c_real - 1.0 - idx)[:, None] * log_g[None, :]), 0.0)
    kd = jnp.repeat(kd, DK_A, axis=1)
    row_h = jnp.arange(H_A * DK_A)[:, None] // DK_A
    col_h = jnp.arange(H_A * DV_A)[None, :] // DV_A
    bd = (row_h == col_h).astype(F32)
    dm = bd * jnp.repeat(jnp.exp(c_real * log_g), DV_A)[None, :]
    return dec, qd, kd, dm, bd


def _retention_kernel(ra_ref, s0_ref, gain_ref, dec_ref, qd_ref, kd_ref, dm_ref, bd_ref,
                      a_ref, sout_ref, sbd_ref, *, c_real, n_chunks):
    c = RET_CHUNK
    j = pl.program_id(1)

    @pl.when(j == 0)
    def _():
        sbd_ref[...] = jnp.zeros_like(sbd_ref)
        for h in range(H_A):
            sbd_ref[h * DK_A:(h + 1) * DK_A, h * DV_A:(h + 1) * DV_A] = s0_ref[0, h]

    lane_q = lax.broadcasted_iota(jnp.int32, (c, QA_W), 1) // DK_A
    for ci in range(n_chunks):
        rows = ra_ref[ci * c_real:(ci + 1) * c_real, :]
        gate = rows[:, QA_W + KA_W + VA_W:]
        if c_real < c:
            rows = jnp.concatenate([rows, jnp.zeros((c - c_real, RA_W), F32)], axis=0)
        q = rows[:, :QA_W]
        k = rows[:, QA_W:QA_W + KA_W] * (DK_A ** -0.5)
        v = rows[:, QA_W + KA_W:QA_W + KA_W + VA_W]
        vb = v.astype(BF16)
        qm = jnp.concatenate([jnp.where(lane_q == h, q, 0.0) for h in range(H_A)], axis=0).astype(BF16)
        s = _dot_nt(qm, k.astype(BF16)) * dec_ref[...]
        sb = s.astype(BF16)
        sbd = sbd_ref[...]
        cross = _dot(q.astype(BF16), sbd.astype(BF16)) * qd_ref[...]
        upd = _dot_tn((k * kd_ref[...]).astype(BF16), vb)
        sbd_ref[...] = sbd * dm_ref[...] + upd * bd_ref[...]
        for h in range(H_A):
            sl = slice(h * DV_A, (h + 1) * DV_A)
            o = _dot(sb[h * c:(h + 1) * c], vb[:, sl]) + cross[:, sl]
            o = o[:c_real]
            mu = jnp.mean(o, axis=-1, keepdims=True)
            var = jnp.mean(jnp.square(o - mu), axis=-1, keepdims=True)
            y = (o - mu) * lax.rsqrt(var + EPS) * gain_ref[:, sl]
            gh = gate[:, sl]
            a_ref[ci * c_real:(ci + 1) * c_real, sl] = (gh * jax.nn.sigmoid(gh) * y).astype(a_ref.dtype)

    @pl.when(j == pl.num_programs(1) - 1)
    def _():
        for h in range(H_A):
            sout_ref[0, h] = sbd_ref[h * DK_A:(h + 1) * DK_A, h * DV_A:(h + 1) * DV_A]


def _retention(ra, state0, gain, *, seq, c_real, n_chunks, out_dtype):
    t = ra.shape[0]
    bsz = t // seq
    rows = c_real * n_chunks
    steps = seq // rows
    tables = _retention_tables(c_real)
    kern = functools.partial(_retention_kernel, c_real=c_real, n_chunks=n_chunks)
    return pl.pallas_call(
        kern,
        grid=(bsz, steps),
        in_specs=[pl.BlockSpec((rows, RA_W), lambda b, j: (b * steps + j, 0)),
                  pl.BlockSpec((1, H_A, DK_A, DV_A), lambda b, j: (b, 0, 0, 0)),
                  _const_spec((1, VA_W))] + [_const_spec(tb.shape) for tb in tables],
        out_specs=[pl.BlockSpec((rows, VA_W), lambda b, j: (b * steps + j, 0)),
                   pl.BlockSpec((1, H_A, DK_A, DV_A), lambda b, j: (b, 0, 0, 0))],
        out_shape=[jax.ShapeDtypeStruct((t, VA_W), out_dtype),
                   jax.ShapeDtypeStruct((bsz, H_A, DK_A, DV_A), F32)],
        scratch_shapes=[pltpu.VMEM((H_A * DK_A, H_A * DV_A), F32)],
        compiler_params=_params(("arbitrary", "arbitrary")),
        name="retention",
    )(ra, state0, gain, *tables)


SWA_ROWS = 2048
N_BRANCH = len(SWA_PAIRS)


def _alibi_slopes():
    return jnp.exp2(-8.0 * jnp.arange(1, H_B + 1, dtype=F32) / H_B)


def _swa_prompt_bias():
    qi = jnp.arange(SPAN)[:, None]
    kj = jnp.arange(2 * SPAN)[None, :]
    dist = SPAN + qi - kj
    band = (dist >= 0) & (dist <= SPAN)
    slopes = _alibi_slopes()
    out = []
    for (_, dil) in SWA_PAIRS:
        pen = -slopes[:, None, None] * (dil * dist).astype(F32)[None]
        normal = jnp.where(band[None], pen, NEG_INF)
        first = jnp.where((band & (kj >= SPAN))[None], pen, NEG_INF)
        out.append(jnp.stack([normal, first], axis=1))
    tab = jnp.stack(out, axis=1)
    tab = tab.reshape(H_B // 2, 2, N_BRANCH, 2, SPAN, 2 * SPAN).transpose(0, 2, 3, 1, 4, 5)
    return tab.reshape(H_B // 2, N_BRANCH, 2, 2 * SPAN, 2 * SPAN)


def _swa_prompt_kernel(q_ref, k_ref, v_ref, bias_ref, o_ref, kbuf, vbuf, acc_ref, l_ref, m_ref):
    j = pl.program_id(2)
    rows = SWA_ROWS

    @pl.when(j == 0)
    def _():
        kbuf[0:rows, :] = jnp.zeros((rows, LANES), F32)
        vbuf[0:rows, :] = jnp.zeros((rows, LANES), F32)

    @pl.when(j > 0)
    def _():
        kbuf[0:rows, :] = kbuf[rows:2 * rows, :]
        vbuf[0:rows, :] = vbuf[rows:2 * rows, :]

    kbuf[rows:2 * rows, :] = k_ref[...]
    vbuf[rows:2 * rows, :] = v_ref[...]

    lane = lax.broadcasted_iota(jnp.int32, (SPAN, LANES), 1)
    head0 = lane < DH_B
    ones = jnp.ones((2 * SPAN, LANES), BF16)
    first_step = (j == 0).astype(jnp.int32)

    for g, (_, dil) in enumerate(SWA_PAIRS):
        blocks = rows // (SPAN * dil)

        def unit(u, carry, g=g, dil=dil):
            wb = u // dil
            r = u % dil
            q_start = wb * (SPAN * dil) + r
            k_start = rows + (wb - 1) * (SPAN * dil) + r
            q = q_ref[pl.ds(q_start, SPAN, stride=dil), :] * (DH_B ** -0.5)
            qm = jnp.concatenate([jnp.where(head0, q, 0.0), jnp.where(head0, 0.0, q)], axis=0).astype(BF16)
            kk = kbuf[pl.ds(k_start, 2 * SPAN, stride=dil), :].astype(BF16)
            vv = vbuf[pl.ds(k_start, 2 * SPAN, stride=dil), :].astype(BF16)
            variant = jnp.where(wb == 0, first_step, 0)
            s = _dot_nt(qm, kk) + bias_ref[0, g, variant]
            m = jnp.max(s, axis=-1, keepdims=True)
            p = jnp.exp(s - m).astype(BF16)
            res = _dot(p, jnp.concatenate([vv, ones], axis=1))
            acc = jnp.where(head0, res[:SPAN, :LANES], res[SPAN:, :LANES])
            den = jnp.where(head0, res[:SPAN, LANES:], res[SPAN:, LANES:])
            mm = jnp.where(head0, jnp.broadcast_to(m[:SPAN], (SPAN, LANES)),
                           jnp.broadcast_to(m[SPAN:], (SPAN, LANES)))
            acc_ref[g, pl.ds(q_start, SPAN, stride=dil), :] = acc
            l_ref[g, pl.ds(q_start, SPAN, stride=dil), :] = den
            m_ref[g, pl.ds(q_start, SPAN, stride=dil), :] = mm
            return carry

        lax.fori_loop(0, blocks * dil, unit, 0, unroll=4)

    tile = 256
    for r0 in range(0, rows, tile):
        sl = slice(r0, r0 + tile)
        ms = [m_ref[g, sl, :] for g in range(N_BRANCH)]
        mx = functools.reduce(jnp.maximum, ms)
        ws = [jnp.exp(mg - mx) for mg in ms]
        num = sum(w * acc_ref[g, sl, :] for g, w in enumerate(ws))
        den = sum(w * l_ref[g, sl, :] for g, w in enumerate(ws))
        o_ref[sl, :] = (num / den).astype(o_ref.dtype)


def _swa_prompt(qkv, bsz, seq):
    t = qkv.shape[0]
    steps = seq // SWA_ROWS
    npair = H_B // 2
    bias = _swa_prompt_bias()
    blk = (SWA_ROWS, LANES)
    return pl.pallas_call(
        _swa_prompt_kernel,
        grid=(bsz, npair, steps),
        in_specs=[pl.BlockSpec(blk, lambda b, hp, j: (b * steps + j, hp)),
                  pl.BlockSpec(blk, lambda b, hp, j: (b * steps + j, npair + hp)),
                  pl.BlockSpec(blk, lambda b, hp, j: (b * steps + j, 2 * npair + hp)),
                  pl.BlockSpec((1, N_BRANCH, 2, 2 * SPAN, 2 * SPAN), lambda b, hp, j: (hp, 0, 0, 0, 0))],
        out_specs=pl.BlockSpec(blk, lambda b, hp, j: (b * steps + j, hp)),
        out_shape=jax.ShapeDtypeStruct((t, H_B * DH_B), BF16),
        scratch_shapes=[pltpu.VMEM((2 * SWA_ROWS, LANES), F32), pltpu.VMEM((2 * SWA_ROWS, LANES), F32),
                        pltpu.VMEM((N_BRANCH, SWA_ROWS, LANES), F32),
                        pltpu.VMEM((N_BRANCH, SWA_ROWS, LANES), F32),
                        pltpu.VMEM((N_BRANCH, SWA_ROWS, LANES), F32)],
        compiler_params=_params(("arbitrary", "arbitrary", "arbitrary")),
        name="swa_prompt",
    )(qkv, qkv, qkv, bias)


def _swa_sample_bias(n_new):
    t = jnp.arange(n_new)[:, None]
    slopes = _alibi_slopes()

    def table(j):
        dist = SWA_BUF + t - j
        out = []
        for (window, dil) in SWA_PAIRS:
            valid = (dist >= 0) & (dist <= window) & (dist % dil == 0) & (PAST_LEN + t - dist >= 0)
            pen = -slopes[:, None, None] * dist.astype(F32)[None]
            out.append(jnp.where(valid[None], pen, NEG_INF).reshape(H_B * n_new, -1))
        return jnp.stack(out)

    bias_a = table(jnp.arange(SWA_BUF)[None, :])
    jb = jnp.arange(LANES)[None, :]
    bias_b = jnp.where(jb < n_new, table(SWA_BUF + jb), NEG_INF)
    return bias_a, bias_b


def _swa_sample_kernel(q_ref, k_ref, v_ref, kp_ref, vp_ref, ba_ref, bb_ref,
                       o_ref, ko_ref, vo_ref, *, n_new):
    width = H_B * DH_B
    lane_h = lax.broadcasted_iota(jnp.int32, (n_new, width), 1) // DH_B
    q = q_ref[...] * (DH_B ** -0.5)
    qm = jnp.concatenate([jnp.where(lane_h == h, q, 0.0) for h in range(H_B)], axis=0).astype(BF16)
    pad = jnp.zeros((LANES - n_new, width), F32)
    k_new = jnp.concatenate([k_ref[...], pad], axis=0)
    v_new = jnp.concatenate([v_ref[...], pad], axis=0)
    s_a = _dot(qm, kp_ref[0].astype(BF16))
    s_b = _dot_nt(qm, k_new.astype(BF16))
    sa = [s_a + ba_ref[g] for g in range(N_BRANCH)]
    sb = [s_b + bb_ref[g] for g in range(N_BRANCH)]
    mx = functools.reduce(jnp.maximum, [jnp.max(x, axis=-1, keepdims=True) for x in sa + sb])
    p_a = sum(jnp.exp(x - mx) for x in sa).astype(BF16)
    p_b = sum(jnp.exp(x - mx) for x in sb).astype(BF16)
    den = (jnp.sum(p_a.astype(F32), axis=-1, keepdims=True)
           + jnp.sum(p_b.astype(F32), axis=-1, keepdims=True))
    o = (_dot_nt(p_a, vp_ref[0].astype(BF16)) + _dot(p_b, v_new.astype(BF16))) / den
    o_ref[...] = sum(jnp.where(lane_h == h, o[h * n_new:(h + 1) * n_new], 0.0) for h in range(H_B))

    lane = lax.broadcasted_iota(jnp.int32, (DH_B, LANES), 1)
    for src_ref, new, dst_ref in ((kp_ref, k_new, ko_ref), (vp_ref, v_new, vo_ref)):
        new_t = pltpu.roll(new.T, LANES - n_new, axis=1)
        for h in range(H_B):
            rows = slice(h * DH_B, (h + 1) * DH_B)
            shifted = pltpu.roll(src_ref[0, rows, :], SWA_BUF - n_new, axis=1)
            dst_ref[0, rows, 0:SWA_BUF - LANES] = shifted[:, 0:SWA_BUF - LANES]
            dst_ref[0, rows, SWA_BUF - LANES:] = jnp.where(lane >= LANES - n_new, new_t[rows],
                                                           shifted[:, SWA_BUF - LANES:])


def _swa_sample(qkv, k_past, v_past, n_new):
    bsz = k_past.shape[0]
    width = H_B * DH_B
    bias_a, bias_b = _swa_sample_bias(n_new)
    state_blk = pl.BlockSpec((1, width, SWA_BUF), lambda b: (b, 0, 0))
    new_blk = lambda col: pl.BlockSpec((n_new, width), lambda b, col=col: (b, col))
    return pl.pallas_call(
        functools.partial(_swa_sample_kernel, n_new=n_new),
        grid=(bsz,),
        in_specs=[new_blk(0), new_blk(1), new_blk(2), state_blk, state_blk,
                  _const_spec(bias_a.shape), _const_spec(bias_b.shape)],
        out_specs=[pl.BlockSpec((n_new, width), lambda b: (b, 0)), state_blk, state_blk],
        out_shape=[jax.ShapeDtypeStruct((bsz * n_new, width), F32),
                   jax.ShapeDtypeStruct(k_past.shape, F32), jax.ShapeDtypeStruct(v_past.shape, F32)],
        compiler_params=_params(("arbitrary",)),
        name="swa_sample",
    )(qkv, qkv, qkv, k_past, v_past, bias_a, bias_b)


FFN_CHUNK = 256


def _ffn(y1, g_ffn_ref, w_in_ref, w_o_ref, acc_ref):
    d_ff = w_o_ref.shape[0]
    h = _rms(y1, g_ffn_ref[...]).astype(BF16)
    for ci, c0 in enumerate(range(0, d_ff, FFN_CHUNK)):
        gate = _dot(h, w_in_ref[:, c0:c0 + FFN_CHUNK])
        up = _dot(h, w_in_ref[:, d_ff + c0:d_ff + c0 + FFN_CHUNK])
        act = (gate * jax.nn.sigmoid(gate) * up).astype(BF16)
        part = _dot(act, w_o_ref[c0:c0 + FFN_CHUNK, :])
        if ci == 0:
            acc_ref[...] = part
        else:
            acc_ref[...] += part
    return y1 + acc_ref[...]


def _ab_tail_kernel(x_ref, a_ref, ob_ref, w_out_ref, g_ffn_ref, w_in_ref, w_o_ref, o_ref, acc_ref):
    mix = (_dot(a_ref[...].astype(BF16), w_out_ref[0:VA_W, :])
           + _dot(ob_ref[...].astype(BF16), w_out_ref[VA_W:, :]))
    o_ref[...] = _ffn(x_ref[...] + mix, g_ffn_ref, w_in_ref, w_o_ref, acc_ref)


def _s5_tail_kernel(x_ref, ys_ref, g_mix_ref, dskip_ref, w_glu_ref, g_ffn_ref, w_in_ref, w_o_ref,
                    g_fin_ref, o_ref, acc_ref):
    x = x_ref[...]
    d = x.shape[-1]
    u = _rms(x, g_mix_ref[...])
    z = jax.nn.gelu(ys_ref[...].astype(F32) + dskip_ref[...] * u, approximate=True).astype(BF16)
    val = _dot(z, w_glu_ref[:, 0:d])
    gate = _dot(z, w_glu_ref[:, d:2 * d])
    y2 = _ffn(x + val * jax.nn.sigmoid(gate), g_ffn_ref, w_in_ref, w_o_ref, acc_ref)
    o_ref[...] = _rms(y2, g_fin_ref[...])


def _row_spec(tm, width):
    return pl.BlockSpec((tm, width), lambda i: (i, 0))


def _ab_tail(x, a, ob, w_out, g_ffn, w_in, w_o, tm):
    t, d = x.shape
    return pl.pallas_call(
        _ab_tail_kernel,
        grid=(t // tm,),
        in_specs=[_row_spec(tm, d), _row_spec(tm, a.shape[1]), _row_spec(tm, ob.shape[1]),
                  _const_spec(w_out.shape), _const_spec((1, d)), _const_spec(w_in.shape), _const_spec(w_o.shape)],
        out_specs=_row_spec(tm, d),
        out_shape=jax.ShapeDtypeStruct((t, d), F32),
        scratch_shapes=[pltpu.VMEM((tm, d), F32)],
        compiler_params=_params(("arbitrary",)),
        name="ab_tail",
    )(x, a, ob, w_out, g_ffn, w_in, w_o)


def _s5_tail(x, ys, g_mix, dskip, w_glu, g_ffn, w_in, w_o, g_fin, tm):
    t, d = x.shape
    return pl.pallas_call(
        _s5_tail_kernel,
        grid=(t // tm,),
        in_specs=[_row_spec(tm, d), _row_spec(tm, d), _const_spec((1, d)), _const_spec((1, d)),
                  _const_spec(w_glu.shape), _const_spec((1, d)), _const_spec(w_in.shape), _const_spec(w_o.shape),
                  _const_spec((1, d))],
        out_specs=_row_spec(tm, d),
        out_shape=jax.ShapeDtypeStruct((t, d), F32),
        scratch_shapes=[pltpu.VMEM((tm, d), F32)],
        compiler_params=_params(("arbitrary",)),
        name="s5_tail",
    )(x, ys, g_mix, dskip, w_glu, g_ffn, w_in, w_o, g_fin)


SLAB_GROUPS = LANES // S5_GROUP
S5_MM_ROWS = 256


def _slab_perm():
    idx = jnp.arange(SLAB_GROUPS * LANES)
    l8, g8, p = idx // LANES, (idx % LANES) // S5_GROUP, idx % S5_GROUP
    dst = g8 * LANES + l8 * S5_GROUP + p
    return (dst[:, None] == idx[None, :]).astype(BF16)


def _permute_rows(z_ref, perm_ref, emit):
    for r0 in range(0, z_ref.shape[0], S5_MM_ROWS):
        emit(r0, _dot(z_ref[r0:r0 + S5_MM_ROWS, :], perm_ref[...]))


def _s5_group_kernel(x_ref, g_ref, perm_ref, o_ref, h_ref, z_ref, *, chunk):
    rows = x_ref.shape[0]
    nc = rows // chunk
    n_slab = x_ref.shape[1] // LANES
    n_col = chunk // SLAB_GROUPS
    h = _rms(x_ref[...], g_ref[...])
    for v in range(n_slab):
        h_ref[v] = h[:, v * LANES:(v + 1) * LANES]
    for v in range(n_slab):
        for l in range(chunk):
            j, l8 = l // SLAB_GROUPS, l % SLAB_GROUPS
            r0 = (v * n_col + j) * nc
            z_ref[r0:r0 + nc, l8 * LANES:(l8 + 1) * LANES] = h_ref[v, pl.ds(l, nc, stride=chunk), :].astype(BF16)

    def emit(r0, blk):
        for q in range(S5_MM_ROWS // nc):
            v, j = divmod(r0 // nc + q, n_col)
            for g8 in range(SLAB_GROUPS):
                o_ref[v * SLAB_GROUPS + g8, :, j * LANES:(j + 1) * LANES] = (
                    blk[q * nc:(q + 1) * nc, g8 * LANES:(g8 + 1) * LANES].astype(o_ref.dtype))

    _permute_rows(z_ref, perm_ref, emit)


def _s5_group(x, g, perm, chunk, rows):
    t, d = x.shape
    groups, n_slab, nc = d // S5_GROUP, d // LANES, rows // chunk
    n_col = chunk // SLAB_GROUPS
    assert S5_MM_ROWS % nc == 0 and (n_slab * n_col * nc) % S5_MM_ROWS == 0
    return pl.pallas_call(
        functools.partial(_s5_group_kernel, chunk=chunk),
        grid=(t // rows,),
        in_specs=[_row_spec(rows, d), _const_spec((1, d)), _const_spec(perm.shape)],
        out_specs=pl.BlockSpec((groups, nc, chunk * S5_GROUP), lambda i: (0, i, 0)),
        out_shape=jax.ShapeDtypeStruct((groups, t // chunk, chunk * S5_GROUP), BF16),
        scratch_shapes=[pltpu.VMEM((n_slab, rows, LANES), F32),
                        pltpu.VMEM((n_slab * n_col * nc, SLAB_GROUPS * LANES), BF16)],
        compiler_params=_params(("arbitrary",)),
        name="s5_group",
    )(x, g, perm)


def _s5_ungroup_kernel(y_ref, perm_ref, o_ref, z_ref, t_ref, *, chunk):
    rows = o_ref.shape[0]
    nc = rows // chunk
    n_slab = o_ref.shape[1] // LANES
    n_col = chunk // SLAB_GROUPS
    for v in range(n_slab):
        for j in range(n_col):
            r0 = (v * n_col + j) * nc
            for g8 in range(SLAB_GROUPS):
                z_ref[r0:r0 + nc, g8 * LANES:(g8 + 1) * LANES] = y_ref[v * SLAB_GROUPS + g8, :, j * LANES:(j + 1) * LANES]

    def emit(r0, blk):
        for q in range(S5_MM_ROWS // nc):
            v, j = divmod(r0 // nc + q, n_col)
            for l8 in range(SLAB_GROUPS):
                t_ref[v, pl.ds(j * SLAB_GROUPS + l8, nc, stride=chunk), :] = (
                    blk[q * nc:(q + 1) * nc, l8 * LANES:(l8 + 1) * LANES])

    _permute_rows(z_ref, perm_ref, emit)
    for v in range(n_slab):
        o_ref[:, v * LANES:(v + 1) * LANES] = t_ref[v].astype(o_ref.dtype)


def _s5_ungroup(y, perm_t, chunk, rows):
    groups, n_chunks, _ = y.shape
    d, t = groups * S5_GROUP, n_chunks * chunk
    n_slab, nc, n_col = d // LANES, rows // chunk, chunk // SLAB_GROUPS
    assert S5_MM_ROWS % nc == 0 and (n_slab * n_col * nc) % S5_MM_ROWS == 0
    return pl.pallas_call(
        functools.partial(_s5_ungroup_kernel, chunk=chunk),
        grid=(t // rows,),
        in_specs=[pl.BlockSpec((groups, nc, chunk * S5_GROUP), lambda i: (0, i, 0)), _const_spec(perm_t.shape)],
        out_specs=_row_spec(rows, d),
        out_shape=jax.ShapeDtypeStruct((t, d), BF16),
        scratch_shapes=[pltpu.VMEM((n_slab * n_col * nc, SLAB_GROUPS * LANES), BF16),
                        pltpu.VMEM((n_slab, rows, LANES), F32)],
        compiler_params=_params(("arbitrary",)),
        name="s5_ungroup",
    )(y, perm_t)


S5_LP = S5_CHUNK * S5_GROUP
S5_RI = 2 * S5_STATE


def _s5_prep_kernel(lam_row_ref, ls_ref, bta_ref, btb_ref, ca_ref, cb_ref,
                    bs_ref, cs_ref, tp_ref, ap_ref):
    n_pow = S5_CHUNK + 1
    dt = jnp.exp(ls_ref[...])

    def powers(lr, li):
        mag = jnp.exp(lr * dt)
        a_re, a_im = mag * jnp.cos(li * dt), mag * jnp.sin(li * dt)
        pw = [(jnp.ones_like(a_re), jnp.zeros_like(a_im))]
        for _ in range(n_pow - 1):
            pr, pi = pw[-1]
            pw.append((pr * a_re - pi * a_im, pr * a_im + pi * a_re))
        return pw

    lr, li = lam_row_ref[0:1, :], lam_row_ref[1:2, :]
    pw_row = powers(lr, li)
    a_re, a_im = pw_row[1]
    den = lr * lr + li * li
    f_re = ((a_re - 1.0) * lr + a_im * li) / den
    f_im = (a_im * lr - (a_re - 1.0) * li) / den
    blk = (S5_GROUP, S5_RI)
    w_re = jnp.concatenate([jnp.broadcast_to(f_re * pw_row[S5_CHUNK - 1 - l][0] - f_im * pw_row[S5_CHUNK - 1 - l][1], blk)
                            for l in range(S5_CHUNK)], axis=0)
    w_im = jnp.concatenate([jnp.broadcast_to(f_re * pw_row[S5_CHUNK - 1 - l][1] + f_im * pw_row[S5_CHUNK - 1 - l][0], blk)
                            for l in range(S5_CHUNK)], axis=0)
    bs_ref[...] = (w_re * bta_ref[...] + w_im * btb_ref[...]).astype(bs_ref.dtype)
    bf_t = f_re * bta_ref[0:S5_GROUP, :] + f_im * btb_ref[0:S5_GROUP, :]

    lane = lax.broadcasted_iota(jnp.int32, (1, S5_RI), 1)
    pick = lambda z: jnp.where(lane < S5_STATE, z[0], z[1])
    ap_ref[...] = jnp.zeros_like(ap_ref)
    ap_ref[0:1, :] = pick(pw_row[S5_CHUNK])
    ap_ref[1:2, :] = pick(pw_row[S5_CHUNK // 2])

    def spread(j0, part):
        return jnp.concatenate([jnp.broadcast_to(pw_row[j0 + l][part], blk) for l in range(S5_CHUNK)], axis=0)

    ca, cb = ca_ref[...], cb_ref[...]
    cs0 = ca * spread(0, 0) + cb * spread(0, 1)
    cs_ref[...] = (ca * spread(1, 0) + cb * spread(1, 1)).astype(cs_ref.dtype)
    r = lax.dot_general(bf_t, cs0, (((1,), (1,)), ((), ())), preferred_element_type=F32,
                        precision=lax.Precision.HIGHEST)
    lane_lp = lax.broadcasted_iota(jnp.int32, (S5_GROUP, S5_LP), 1)
    for l in range(S5_CHUNK):
        sh = l * S5_GROUP
        blk_l = r if l == 0 else jnp.where(lane_lp >= sh, pltpu.roll(r, sh, axis=1), 0.0)
        tp_ref[l * S5_GROUP:(l + 1) * S5_GROUP, :] = blk_l.astype(tp_ref.dtype)


def _s5_prep(lam_re, lam_im, log_step, b_re, b_im, c_re, c_im):
    g = lam_re.shape[0]
    dup = lambda z: jnp.concatenate([z, z], axis=-1)
    lam_row = jnp.stack([dup(lam_re), dup(lam_im)], axis=1)
    lam_row = jnp.pad(lam_row, ((0, 0), (0, 6), (0, 0)))
    ls = log_step.reshape(g, 1, 1)
    bt_re = jnp.swapaxes(b_re, 1, 2)
    bt_im = jnp.swapaxes(b_im, 1, 2)
    bta = jnp.tile(jnp.concatenate([bt_re, bt_im], axis=-1), (1, S5_CHUNK, 1))
    btb = jnp.tile(jnp.concatenate([-bt_im, bt_re], axis=-1), (1, S5_CHUNK, 1))
    ca = jnp.tile(jnp.concatenate([c_re, -c_im], axis=-1), (1, S5_CHUNK, 1))
    cb = jnp.tile(jnp.concatenate([-c_im, -c_re], axis=-1), (1, S5_CHUNK, 1))
    gspec = lambda *s: pl.BlockSpec((None,) + s, lambda i: (i,) + (0,) * len(s))
    return pl.pallas_call(
        _s5_prep_kernel,
        grid=(g,),
        in_specs=[gspec(8, S5_RI), gspec(1, 1)] + [gspec(S5_LP, S5_RI)] * 4,
        out_specs=[gspec(S5_LP, S5_RI), gspec(S5_LP, S5_RI), gspec(S5_LP, S5_LP), gspec(8, S5_RI)],
        out_shape=[jax.ShapeDtypeStruct((g, S5_LP, S5_RI), BF16), jax.ShapeDtypeStruct((g, S5_LP, S5_RI), BF16),
                   jax.ShapeDtypeStruct((g, S5_LP, S5_LP), BF16), jax.ShapeDtypeStruct((g, 8, S5_RI), F32)],
        compiler_params=_params(("arbitrary",)),
        name="s5_prep",
    )(lam_row, ls, bta, btb, ca, cb)


def _cmul_const(x, a):
    lane = lax.broadcasted_iota(jnp.int32, (8, S5_RI), 1)
    a_sw = pltpu.roll(a, S5_STATE, axis=1)
    a_rr = jnp.where(lane < S5_STATE, a, a_sw)
    a_is = jnp.where(lane < S5_STATE, -a_sw, a)
    if x.shape[0] != 8:
        a_rr, a_is = a_rr[0:1], a_is[0:1]
    return x * a_rr + pltpu.roll(x, S5_STATE, axis=1) * a_is


def _s5_scan_kernel(u_ref, us_ref, x0_ref, bs_ref, cs_ref, tp_ref, ap_ref,
                    y_ref, ys_ref, fin_ref, fins_ref, *, bsz, chunks):
    n_rows = bsz * chunks
    u = u_ref[...]
    bs, cs, tp = bs_ref[...], cs_ref[...], tp_ref[...]
    x = _dot(u, bs)
    row = lax.broadcasted_iota(jnp.int32, (n_rows, 1), 0) % chunks
    a = jnp.broadcast_to(ap_ref[0:1, :], (8, S5_RI))
    shift = 1
    while shift < chunks:
        xs = jnp.where(row >= shift, pltpu.roll(x, shift, axis=0), 0.0)
        x = x + _cmul_const(xs, a)
        a = _cmul_const(a, a)
        shift *= 2
    prev = jnp.where(row >= 1, pltpu.roll(x, 1, axis=0), 0.0)
    y_ref[...] = (_dot(u, tp) + _dot_nt(prev.astype(BF16), cs)).astype(y_ref.dtype)
    fin_ref[...] = jnp.zeros_like(fin_ref)
    for b in range(bsz):
        fin_ref[b:b + 1, :] = x[(b + 1) * chunks - 1:(b + 1) * chunks]

    half = S5_LP // 2
    us, x0 = us_ref[...], x0_ref[...]
    a_half = jnp.broadcast_to(ap_ref[1:2, :], (8, S5_RI))
    fins_ref[...] = _cmul_const(x0, a_half) + _dot(us, bs[half:, :])
    ys_ref[...] = (_dot(us, tp[:half, :half]) + _dot_nt(x0.astype(BF16), cs[:half, :])).astype(ys_ref.dtype)


def _s5_scan(u_g, us_g, x0_g, bs, cs, tp, ap, bsz):
    g, n_rows, _ = u_g.shape
    n_s = us_g.shape[1]
    gspec = lambda *s: pl.BlockSpec((None,) + s, lambda i: (i,) + (0,) * len(s))
    half = S5_LP // 2
    return pl.pallas_call(
        functools.partial(_s5_scan_kernel, bsz=bsz, chunks=n_rows // bsz),
        grid=(g,),
        in_specs=[gspec(n_rows, S5_LP), gspec(n_s, half), gspec(n_s, S5_RI),
                  gspec(S5_LP, S5_RI), gspec(S5_LP, S5_RI), gspec(S5_LP, S5_LP), gspec(8, S5_RI)],
        out_specs=[gspec(n_rows, S5_LP), gspec(n_s, half), gspec(8, S5_RI), gspec(n_s, S5_RI)],
        out_shape=[jax.ShapeDtypeStruct((g, n_rows, S5_LP), BF16), jax.ShapeDtypeStruct((g, n_s, half), BF16),
                   jax.ShapeDtypeStruct((g, 8, S5_RI), F32), jax.ShapeDtypeStruct((g, n_s, S5_RI), F32)],
        compiler_params=_params(("arbitrary",)),
        name="s5_scan",
    )(u_g, us_g, x0_g, bs, cs, tp, ap)


def kernel(x_prompt, x_sample, state_ret, state_swa_k, state_swa_v, state_ssm_re, state_ssm_im, norm_mix, norm_ffn, norm_final, w_in_ab, ret_gn, w_out_ab, ssm_lam_re, ssm_lam_im, ssm_log_step, ssm_b_re, ssm_b_im, ssm_c_re, ssm_c_im, ssm_d, w_glu, w_ffn_in, w_ffn_out):
    bsz, seq, d = x_prompt.shape
    dbsz, n_new, _ = x_sample.shape
    assert state_swa_k.shape[2] == SWA_BUF and n_new == S5_CHUNK // 2 and seq % SWA_ROWS == 0
    xp = x_prompt.reshape(bsz * seq, d)
    xs = x_sample.reshape(dbsz * n_new, d)
    tm_p, tm_s = 512, dbsz * n_new
    row = lambda v: v.reshape(1, -1)

    w_in0 = w_in_ab[0].astype(BF16)
    w_out0 = w_out_ab[0].astype(BF16)
    w_f_in0, w_f_out0 = w_ffn_in[0].astype(BF16), w_ffn_out[0].astype(BF16)
    g_mix0, g_ffn0, gn0 = row(norm_mix[0]), row(norm_ffn[0]), row(ret_gn[0])

    ra_p, qkv_p = _ab_proj(xp, g_mix0, w_in0, tm_p)
    ra_s, qkv_s = _ab_proj(xs, g_mix0, w_in0, tm_s)

    zero_ret = jnp.zeros((bsz, H_A, DK_A, DV_A), F32)
    a_p, ret_p = _retention(ra_p, zero_ret, gn0, seq=seq, c_real=RET_CHUNK, n_chunks=4, out_dtype=BF16)
    a_s, ret_s = _retention(ra_s, state_ret[0], gn0, seq=n_new, c_real=n_new, n_chunks=1, out_dtype=F32)

    ob_p = _swa_prompt(qkv_p, bsz, seq)
    width_b = H_B * DH_B
    rows_last = lambda w: w.transpose(0, 2, 3, 1).reshape(dbsz, width_b, SWA_BUF)
    rows_first = lambda w: w.reshape(dbsz, H_B, DH_B, SWA_BUF).transpose(0, 3, 1, 2)[None]
    ob_s, swk_s, swv_s = _swa_sample(qkv_s, rows_last(state_swa_k[0]), rows_last(state_swa_v[0]), n_new)
    kv_tail = qkv_p.reshape(bsz, seq, QKV_W)[:, seq - SWA_BUF:, width_b:]
    swk_p = kv_tail[..., :width_b].reshape(bsz, SWA_BUF, H_B, DH_B)
    swv_p = kv_tail[..., width_b:].reshape(bsz, SWA_BUF, H_B, DH_B)

    yp = _ab_tail(xp, a_p, ob_p, w_out0, g_ffn0, w_f_in0, w_f_out0, tm_p)
    ys = _ab_tail(xs, a_s, ob_s, w_out0, g_ffn0, w_f_in0, w_f_out0, tm_s)

    g_mix1, g_ffn1 = row(norm_mix[1]), row(norm_ffn[1])
    bs, cs, tp, ap = _s5_prep(ssm_lam_re[0], ssm_lam_im[0], ssm_log_step[0],
                              ssm_b_re[0], ssm_b_im[0], ssm_c_re[0], ssm_c_im[0])
    perm = _slab_perm()
    u_p = _s5_group(yp, g_mix1, perm, S5_CHUNK, 1024)
    u_s = _s5_group(ys, g_mix1, perm, n_new, tm_s)
    x0 = jnp.concatenate([state_ssm_re[0], state_ssm_im[0]], axis=-1).transpose(1, 0, 2)
    y5_p, y5_s, fin_p, fin_s = _s5_scan(u_p, u_s, x0, bs, cs, tp, ap, bsz)
    y5_p = _s5_ungroup(y5_p, perm.T, S5_CHUNK, 1024)
    y5_s = _s5_ungroup(y5_s, perm.T, n_new, tm_s)

    tail1 = (g_mix1, row(ssm_d[0]), w_glu[0].astype(BF16), g_ffn1,
             w_ffn_in[1].astype(BF16), w_ffn_out[1].astype(BF16), row(norm_final))
    yp = _s5_tail(yp, y5_p, *tail1, tm_p)
    ys = _s5_tail(ys, y5_s, *tail1, tm_s)

    fin_p = fin_p[:, :bsz].transpose(1, 0, 2)
    fin_s = fin_s.transpose(1, 0, 2)
    return (yp.reshape(bsz, seq, d), ys.reshape(dbsz, n_new, d),
            ret_p[None], ret_s[None],
            swk_p[None], swv_p[None],
            rows_first(swk_s), rows_first(swv_s),
            fin_p[None, ..., :S5_STATE], fin_p[None, ..., S5_STATE:],
            fin_s[None, ..., :S5_STATE], fin_s[None, ..., S5_STATE:])
```

```python
import functools
import math

import jax
import jax.numpy as jnp
from jax import lax
from jax.experimental import pallas as pl
from jax.experimental.pallas import tpu as pltpu

F32 = jnp.float32
BF16 = jnp.bfloat16

H_A, DK_A, DV_A = 4, 64, 128
H_B, DH_B = 8, 64
SWA_PAIRS = ((128, 1), (512, 4), (2048, 16))
SPAN = 128
SWA_BUF = 2048
PAST_LEN = 16384
RET_CHUNK = 128
S5_GROUP, S5_STATE = 16, 64
S5_CHUNK = 16
EPS = 1e-6
NEG_INF = -1e30
QA_W, KA_W, VA_W, GA_W = H_A * DK_A, H_A * DK_A, H_A * DV_A, H_A * DV_A
RA_W = QA_W + KA_W + VA_W + GA_W
QKV_W = 3 * H_B * DH_B

LANES = 128
VMEM_LIMIT = 56 * 1024 * 1024


def _params(sem):
    return pltpu.CompilerParams(dimension_semantics=sem, vmem_limit_bytes=VMEM_LIMIT)


def _const_spec(shape):
    nd = len(shape)
    return pl.BlockSpec(shape, lambda *_: (0,) * nd, pipeline_mode=pl.Buffered(1))


def _rms(x, g):
    return x * lax.rsqrt(jnp.mean(x * x, axis=-1, keepdims=True) + EPS) * g


def _dot(a, b):
    return jnp.dot(a, b, preferred_element_type=F32)


def _dot_nt(a, b):
    return lax.dot_general(a, b, (((1,), (1,)), ((), ())), preferred_element_type=F32)


def _dot_tn(a, b):
    return lax.dot_general(a, b, (((0,), (0,)), ((), ())), preferred_element_type=F32)


def _ab_proj_kernel(x_ref, g_ref, w_ref, ra_ref, qkv_ref):
    h = _rms(x_ref[...], g_ref[...]).astype(BF16)
    step = 512
    for n0 in range(0, RA_W, step):
        ra_ref[:, n0:n0 + step] = _dot(h, w_ref[:, n0:n0 + step])
    for n0 in range(0, QKV_W, step):
        qkv_ref[:, n0:n0 + step] = _dot(h, w_ref[:, RA_W + n0:RA_W + n0 + step])


def _ab_proj(x, g, w_bf16, tm):
    t, d = x.shape
    return pl.pallas_call(
        _ab_proj_kernel,
        grid=(t // tm,),
        in_specs=[pl.BlockSpec((tm, d), lambda i: (i, 0)),
                  _const_spec((1, d)),
                  _const_spec((d, RA_W + QKV_W))],
        out_specs=[pl.BlockSpec((tm, RA_W), lambda i: (i, 0)),
                   pl.BlockSpec((tm, QKV_W), lambda i: (i, 0))],
        out_shape=[jax.ShapeDtypeStruct((t, RA_W), F32), jax.ShapeDtypeStruct((t, QKV_W), F32)],
        compiler_params=_params(("arbitrary",)),
        name="ab_proj",
    )(x, g, w_bf16)


def _retention_tables(c_real):
    c = RET_CHUNK
    log_g = jnp.log1p(-jnp.exp2(-5.0 - jnp.arange(H_A, dtype=F32)))
    idx = jnp.arange(c, dtype=F32)
    rel = idx[:, None] - idx[None, :]
    dec = jnp.where(rel >= 0, jnp.exp(jnp.maximum(rel, 0.0)[None] * log_g[:, None, None]), 0.0)
    dec = dec.reshape(H_A * c, c)
    real = (idx < c_real)[:, None]
    qd = jnp.exp((idx + 1.0)[:, None] * log_g[None, :])
    qd = jnp.repeat(qd, DV_A, axis=1)
    kd = jnp.where(real, jnp.exp((c_real - 1.0 - idx)[:, None] * log_g[None, :]), 0.0)
    kd = jnp.repeat(kd, DK_A, axis=1)
    row_h = jnp.arange(H_A * DK_A)[:, None] // DK_A
    col_h = jnp.arange(H_A * DV_A)[None, :] // DV_A
    bd = (row_h == col_h).astype(F32)
    dm = bd * jnp.repeat(jnp.exp(c_real * log_g), DV_A)[None, :]
    return dec, qd, kd, dm, bd


def _retention_kernel(ra_ref, s0_ref, gain_ref, dec_ref, qd_ref, kd_ref, dm_ref, bd_ref,
                      a_ref, sout_ref, sbd_ref, *, c_real, n_chunks):
    c = RET_CHUNK
    j = pl.program_id(1)

    @pl.when(j == 0)
    def _():
        sbd_ref[...] = jnp.zeros_like(sbd_ref)
        for h in range(H_A):
            sbd_ref[h * DK_A:(h + 1) * DK_A, h * DV_A:(h + 1) * DV_A] = s0_ref[0, h]

    lane_q = lax.broadcasted_iota(jnp.int32, (c, QA_W), 1) // DK_A
    for ci in range(n_chunks):
        rows = ra_ref[ci * c_real:(ci + 1) * c_real, :]
        gate = rows[:, QA_W + KA_W + VA_W:]
        if c_real < c:
            rows = jnp.concatenate([rows, jnp.zeros((c - c_real, RA_W), F32)], axis=0)
        q = rows[:, :QA_W]
        k = rows[:, QA_W:QA_W + KA_W] * (DK_A ** -0.5)
        v = rows[:, QA_W + KA_W:QA_W + KA_W + VA_W]
        vb = v.astype(BF16)
        qm = jnp.concatenate([jnp.where(lane_q == h, q, 0.0) for h in range(H_A)], axis=0).astype(BF16)
        s = _dot_nt(qm, k.astype(BF16)) * dec_ref[...]
        sb = s.astype(BF16)
        sbd = sbd_ref[...]
        cross = _dot(q.astype(BF16), sbd.astype(BF16)) * qd_ref[...]
        upd = _dot_tn((k * kd_ref[...]).astype(BF16), vb)
        sbd_ref[...] = sbd * dm_ref[...] + upd * bd_ref[...]
        for h in range(H_A):
            sl = slice(h * DV_A, (h + 1) * DV_A)
            o = _dot(sb[h * c:(h + 1) * c], vb[:, sl]) + cross[:, sl]
            o = o[:c_real]
            mu = jnp.mean(o, axis=-1, keepdims=True)
            var = jnp.mean(jnp.square(o - mu), axis=-1, keepdims=True)
            y = (o - mu) * lax.rsqrt(var + EPS) * gain_ref[:, sl]
            gh = gate[:, sl]
            a_ref[ci * c_real:(ci + 1) * c_real, sl] = (gh * jax.nn.sigmoid(gh) * y).astype(a_ref.dtype)

    @pl.when(j == pl.num_programs(1) - 1)
    def _():
        for h in range(H_A):
            sout_ref[0, h] = sbd_ref[h * DK_A:(h + 1) * DK_A, h * DV_A:(h + 1) * DV_A]


def _retention(ra, state0, gain, *, seq, c_real, n_chunks, out_dtype):
    t = ra.shape[0]
    bsz = t // seq
    rows = c_real * n_chunks
    steps = seq // rows
    tables = _retention_tables(c_real)
    kern = functools.partial(_retention_kernel, c_real=c_real, n_chunks=n_chunks)
    return pl.pallas_call(
        kern,
        grid=(bsz, steps),
        in_specs=[pl.BlockSpec((rows, RA_W), lambda b, j: (b * steps + j, 0)),
                  pl.BlockSpec((1, H_A, DK_A, DV_A), lambda b, j: (b, 0, 0, 0)),
                  _const_spec((1, VA_W))] + [_const_spec(tb.shape) for tb in tables],
        out_specs=[pl.BlockSpec((rows, VA_W), lambda b, j: (b * steps + j, 0)),
                   pl.BlockSpec((1, H_A, DK_A, DV_A), lambda b, j: (b, 0, 0, 0))],
        out_shape=[jax.ShapeDtypeStruct((t, VA_W), out_dtype),
                   jax.ShapeDtypeStruct((bsz, H_A, DK_A, DV_A), F32)],
        scratch_shapes=[pltpu.VMEM((H_A * DK_A, H_A * DV_A), F32)],
        compiler_params=_params(("arbitrary", "arbitrary")),
        name="retention",
    )(ra, state0, gain, *tables)


SWA_ROWS = 2048
N_BRANCH = len(SWA_PAIRS)


def _alibi_slopes():
    return jnp.exp2(-8.0 * jnp.arange(1, H_B + 1, dtype=F32) / H_B)


def _swa_prompt_bias():
    qi = jnp.arange(SPAN)[:, None]
    kj = jnp.arange(2 * SPAN)[None, :]
    dist = SPAN + qi - kj
    band = (dist >= 0) & (dist <= SPAN)
    slopes = _alibi_slopes()
    out = []
    for (_, dil) in SWA_PAIRS:
        pen = -slopes[:, None, None] * (dil * dist).astype(F32)[None]
        normal = jnp.where(band[None], pen, NEG_INF)
        first = jnp.where((band & (kj >= SPAN))[None], pen, NEG_INF)
        out.append(jnp.stack([normal, first], axis=1))
    tab = jnp.stack(out, axis=1)
    tab = tab.reshape(H_B // 2, 2, N_BRANCH, 2, SPAN, 2 * SPAN).transpose(0, 2, 3, 1, 4, 5)
    return tab.reshape(H_B // 2, N_BRANCH, 2, 2 * SPAN, 2 * SPAN)


def _swa_prompt_kernel(q_ref, k_ref, v_ref, bias_ref, o_ref, kbuf, vbuf, acc_ref, l_ref, m_ref):
    j = pl.program_id(2)
    rows = SWA_ROWS

    @pl.when(j == 0)
    def _():
        kbuf[0:rows, :] = jnp.zeros((rows, LANES), F32)
        vbuf[0:rows, :] = jnp.zeros((rows, LANES), F32)

    @pl.when(j > 0)
    def _():
        kbuf[0:rows, :] = kbuf[rows:2 * rows, :]
        vbuf[0:rows, :] = vbuf[rows:2 * rows, :]

    kbuf[rows:2 * rows, :] = k_ref[...]
    vbuf[rows:2 * rows, :] = v_ref[...]

    lane = lax.broadcasted_iota(jnp.int32, (SPAN, LANES), 1)
    head0 = lane < DH_B
    ones = jnp.ones((2 * SPAN, LANES), BF16)
    first_step = (j == 0).astype(jnp.int32)

    for g, (_, dil) in enumerate(SWA_PAIRS):
        blocks = rows // (SPAN * dil)

        def unit(u, carry, g=g, dil=dil):
            wb = u // dil
            r = u % dil
            q_start = wb * (SPAN * dil) + r
            k_start = rows + (wb - 1) * (SPAN * dil) + r
            q = q_ref[pl.ds(q_start, SPAN, stride=dil), :] * (DH_B ** -0.5)
            qm = jnp.concatenate([jnp.where(head0, q, 0.0), jnp.where(head0, 0.0, q)], axis=0).astype(BF16)
            kk = kbuf[pl.ds(k_start, 2 * SPAN, stride=dil), :].astype(BF16)
            vv = vbuf[pl.ds(k_start, 2 * SPAN, stride=dil), :].astype(BF16)
            variant = jnp.where(wb == 0, first_step, 0)
            s = _dot_nt(qm, kk) + bias_ref[0, g, variant]
            m = jnp.max(s, axis=-1, keepdims=True)
            p = jnp.exp(s - m).astype(BF16)
            res = _dot(p, jnp.concatenate([vv, ones], axis=1))
            acc = jnp.where(head0, res[:SPAN, :LANES], res[SPAN:, :LANES])
            den = jnp.where(head0, res[:SPAN, LANES:], res[SPAN:, LANES:])
            mm = jnp.where(head0, jnp.broadcast_to(m[:SPAN], (SPAN, LANES)),
                           jnp.broadcast_to(m[SPAN:], (SPAN, LANES)))
            acc_ref[g, pl.ds(q_start, SPAN, stride=dil), :] = acc
            l_ref[g, pl.ds(q_start, SPAN, stride=dil), :] = den
            m_ref[g, pl.ds(q_start, SPAN, stride=dil), :] = mm
            return carry

        lax.fori_loop(0, blocks * dil, unit, 0, unroll=8)

    tile = 256
    for r0 in range(0, rows, tile):
        sl = slice(r0, r0 + tile)
        ms = [m_ref[g, sl, :] for g in range(N_BRANCH)]
        mx = functools.reduce(jnp.maximum, ms)
        ws = [jnp.exp(mg - mx) for mg in ms]
        num = sum(w * acc_ref[g, sl, :] for g, w in enumerate(ws))
        den = sum(w * l_ref[g, sl, :] for g, w in enumerate(ws))
        o_ref[sl, :] = (num / den).astype(o_ref.dtype)


def _swa_prompt(qkv, bsz, seq):
    t = qkv.shape[0]
    steps = seq // SWA_ROWS
    npair = H_B // 2
    bias = _swa_prompt_bias()
    blk = (SWA_ROWS, LANES)
    return pl.pallas_call(
        _swa_prompt_kernel,
        grid=(bsz, npair, steps),
        in_specs=[pl.BlockSpec(blk, lambda b, hp, j: (b * steps + j, hp)),
                  pl.BlockSpec(blk, lambda b, hp, j: (b * steps + j, npair + hp)),
                  pl.BlockSpec(blk, lambda b, hp, j: (b * steps + j, 2 * npair + hp)),
                  pl.BlockSpec((1, N_BRANCH, 2, 2 * SPAN, 2 * SPAN), lambda b, hp, j: (hp, 0, 0, 0, 0))],
        out_specs=pl.BlockSpec(blk, lambda b, hp, j: (b * steps + j, hp)),
        out_shape=jax.ShapeDtypeStruct((t, H_B * DH_B), BF16),
        scratch_shapes=[pltpu.VMEM((2 * SWA_ROWS, LANES), F32), pltpu.VMEM((2 * SWA_ROWS, LANES), F32),
                        pltpu.VMEM((N_BRANCH, SWA_ROWS, LANES), F32),
                        pltpu.VMEM((N_BRANCH, SWA_ROWS, LANES), F32),
                        pltpu.VMEM((N_BRANCH, SWA_ROWS, LANES), F32)],
        compiler_params=_params(("arbitrary", "arbitrary", "arbitrary")),
        name="swa_prompt",
    )(qkv, qkv, qkv, bias)


def _swa_sample_bias(n_new):
    t = jnp.arange(n_new)[:, None]
    slopes = _alibi_slopes()

    def table(j):
        dist = SWA_BUF + t - j
        out = []
        for (window, dil) in SWA_PAIRS:
            valid = (dist >= 0) & (dist <= window) & (dist % dil == 0) & (PAST_LEN + t - dist >= 0)
            pen = -slopes[:, None, None] * dist.astype(F32)[None]
            out.append(jnp.where(valid[None], pen, NEG_INF).reshape(H_B * n_new, -1))
        return jnp.stack(out)

    bias_a = table(jnp.arange(SWA_BUF)[None, :])
    jb = jnp.arange(LANES)[None, :]
    bias_b = jnp.where(jb < n_new, table(SWA_BUF + jb), NEG_INF)
    return bias_a, bias_b


def _swa_sample_kernel(q_ref, k_ref, v_ref, kp_ref, vp_ref, ba_ref, bb_ref,
                       o_ref, ko_ref, vo_ref, *, n_new):
    width = H_B * DH_B
    lane_h = lax.broadcasted_iota(jnp.int32, (n_new, width), 1) // DH_B
    q = q_ref[...] * (DH_B ** -0.5)
    qm = jnp.concatenate([jnp.where(lane_h == h, q, 0.0) for h in range(H_B)], axis=0).astype(BF16)
    pad = jnp.zeros((LANES - n_new, width), F32)
    k_new = jnp.concatenate([k_ref[...], pad], axis=0)
    v_new = jnp.concatenate([v_ref[...], pad], axis=0)
    s_a = _dot(qm, kp_ref[0].astype(BF16))
    s_b = _dot_nt(qm, k_new.astype(BF16))
    sa = [s_a + ba_ref[g] for g in range(N_BRANCH)]
    sb = [s_b + bb_ref[g] for g in range(N_BRANCH)]
    mx = functools.reduce(jnp.maximum, [jnp.max(x, axis=-1, keepdims=True) for x in sa + sb])
    p_a = sum(jnp.exp(x - mx) for x in sa).astype(BF16)
    p_b = sum(jnp.exp(x - mx) for x in sb).astype(BF16)
    den = (jnp.sum(p_a.astype(F32), axis=-1, keepdims=True)
           + jnp.sum(p_b.astype(F32), axis=-1, keepdims=True))
    o = (_dot_nt(p_a, vp_ref[0].astype(BF16)) + _dot(p_b, v_new.astype(BF16))) / den
    o_ref[...] = sum(jnp.where(lane_h == h, o[h * n_new:(h + 1) * n_new], 0.0) for h in range(H_B))

    lane = lax.broadcasted_iota(jnp.int32, (DH_B, LANES), 1)
    for src_ref, new, dst_ref in ((kp_ref, k_new, ko_ref), (vp_ref, v_new, vo_ref)):
        new_t = pltpu.roll(new.T, LANES - n_new, axis=1)
        for h in range(H_B):
            rows = slice(h * DH_B, (h + 1) * DH_B)
            shifted = pltpu.roll(src_ref[0, rows, :], SWA_BUF - n_new, axis=1)
            dst_ref[0, rows, 0:SWA_BUF - LANES] = shifted[:, 0:SWA_BUF - LANES]
            dst_ref[0, rows, SWA_BUF - LANES:] = jnp.where(lane >= LANES - n_new, new_t[rows],
                                                           shifted[:, SWA_BUF - LANES:])


def _swa_sample(qkv, k_past, v_past, n_new):
    bsz = k_past.shape[0]
    width = H_B * DH_B
    bias_a, bias_b = _swa_sample_bias(n_new)
    state_blk = pl.BlockSpec((1, width, SWA_BUF), lambda b: (b, 0, 0))
    new_blk = lambda col: pl.BlockSpec((n_new, width), lambda b, col=col: (b, col))
    return pl.pallas_call(
        functools.partial(_swa_sample_kernel, n_new=n_new),
        grid=(bsz,),
        in_specs=[new_blk(0), new_blk(1), new_blk(2), state_blk, state_blk,
                  _const_spec(bias_a.shape), _const_spec(bias_b.shape)],
        out_specs=[pl.BlockSpec((n_new, width), lambda b: (b, 0)), state_blk, state_blk],
        out_shape=[jax.ShapeDtypeStruct((bsz * n_new, width), F32),
                   jax.ShapeDtypeStruct(k_past.shape, F32), jax.ShapeDtypeStruct(v_past.shape, F32)],
        compiler_params=_params(("arbitrary",)),
        name="swa_sample",
    )(qkv, qkv, qkv, k_past, v_past, bias_a, bias_b)


FFN_CHUNK = 256


def _ffn(y1, g_ffn_ref, w_in_ref, w_o_ref, acc_ref):
    d_ff = w_o_ref.shape[0]
    h = _rms(y1, g_ffn_ref[...]).astype(BF16)
    for ci, c0 in enumerate(range(0, d_ff, FFN_CHUNK)):
        gate = _dot(h, w_in_ref[:, c0:c0 + FFN_CHUNK])
        up = _dot(h, w_in_ref[:, d_ff + c0:d_ff + c0 + FFN_CHUNK])
        act = (gate * jax.nn.sigmoid(gate) * up).astype(BF16)
        part = _dot(act, w_o_ref[c0:c0 + FFN_CHUNK, :])
        if ci == 0:
            acc_ref[...] = part
        else:
            acc_ref[...] += part
    return y1 + acc_ref[...]


def _ab_tail_kernel(x_ref, a_ref, ob_ref, w_out_ref, g_ffn_ref, w_in_ref, w_o_ref, g_next_ref, perm_ref,
                    o_ref, u_ref, acc_ref, h_ref, z_ref, *, chunk):
    mix = (_dot(a_ref[...].astype(BF16), w_out_ref[0:VA_W, :])
           + _dot(ob_ref[...].astype(BF16), w_out_ref[VA_W:, :]))
    y = _ffn(x_ref[...] + mix, g_ffn_ref, w_in_ref, w_o_ref, acc_ref)
    o_ref[...] = y
    _group_rows(_rms(y, g_next_ref[...]), perm_ref, u_ref, h_ref, z_ref, chunk)


def _s5_tail_kernel(x_ref, yg_ref, perm_ref, g_mix_ref, dskip_ref, w_glu_ref, g_ffn_ref, w_in_ref, w_o_ref,
                    g_fin_ref, o_ref, acc_ref, z_ref, t_ref, *, chunk):
    x = x_ref[...]
    d = x.shape[-1]
    u = _rms(x, g_mix_ref[...])
    _ungroup_rows(yg_ref, perm_ref, z_ref, t_ref, chunk)
    ys = jnp.concatenate([t_ref[v] for v in range(d // LANES)], axis=1)
    z = jax.nn.gelu(ys + dskip_ref[...] * u, approximate=True).astype(BF16)
    val = _dot(z, w_glu_ref[:, 0:d])
    gate = _dot(z, w_glu_ref[:, d:2 * d])
    y2 = _ffn(x + val * jax.nn.sigmoid(gate), g_ffn_ref, w_in_ref, w_o_ref, acc_ref)
    o_ref[...] = _rms(y2, g_fin_ref[...])


def _row_spec(tm, width):
    return pl.BlockSpec((tm, width), lambda i: (i, 0))


def _group_scratch(tm, d, chunk):
    n_slab, nc, n_col = d // LANES, tm // chunk, chunk // SLAB_GROUPS
    assert min(S5_MM_ROWS, n_slab * n_col * nc) % nc == 0
    return (pltpu.VMEM((n_slab, tm, LANES), F32), pltpu.VMEM((n_slab * n_col * nc, SLAB_GROUPS * LANES), BF16))


def _group_spec(tm, d, chunk):
    return pl.BlockSpec((d // S5_GROUP, tm // chunk, chunk * S5_GROUP), lambda i: (0, i, 0))


def _ab_tail(x, a, ob, w_out, g_ffn, w_in, w_o, g_next, perm, tm, chunk):
    t, d = x.shape
    h_scr, z_scr = _group_scratch(tm, d, chunk)
    return pl.pallas_call(
        functools.partial(_ab_tail_kernel, chunk=chunk),
        grid=(t // tm,),
        in_specs=[_row_spec(tm, d), _row_spec(tm, a.shape[1]), _row_spec(tm, ob.shape[1]),
                  _const_spec(w_out.shape), _const_spec((1, d)), _const_spec(w_in.shape), _const_spec(w_o.shape),
                  _const_spec((1, d)), _const_spec(perm.shape)],
        out_specs=[_row_spec(tm, d), _group_spec(tm, d, chunk)],
        out_shape=[jax.ShapeDtypeStruct((t, d), F32),
                   jax.ShapeDtypeStruct((d // S5_GROUP, t // chunk, chunk * S5_GROUP), BF16)],
        scratch_shapes=[pltpu.VMEM((tm, d), F32), h_scr, z_scr],
        compiler_params=_params(("arbitrary",)),
        name="ab_tail",
    )(x, a, ob, w_out, g_ffn, w_in, w_o, g_next, perm)


def _s5_tail(x, yg, perm_t, g_mix, dskip, w_glu, g_ffn, w_in, w_o, g_fin, tm, chunk):
    t, d = x.shape
    t_scr, z_scr = _group_scratch(tm, d, chunk)
    return pl.pallas_call(
        functools.partial(_s5_tail_kernel, chunk=chunk),
        grid=(t // tm,),
        in_specs=[_row_spec(tm, d), _group_spec(tm, d, chunk), _const_spec(perm_t.shape),
                  _const_spec((1, d)), _const_spec((1, d)),
                  _const_spec(w_glu.shape), _const_spec((1, d)), _const_spec(w_in.shape), _const_spec(w_o.shape),
                  _const_spec((1, d))],
        out_specs=_row_spec(tm, d),
        out_shape=jax.ShapeDtypeStruct((t, d), F32),
        scratch_shapes=[pltpu.VMEM((tm, d), F32), z_scr, t_scr],
        compiler_params=_params(("arbitrary",)),
        name="s5_tail",
    )(x, yg, perm_t, g_mix, dskip, w_glu, g_ffn, w_in, w_o, g_fin)


SLAB_GROUPS = LANES // S5_GROUP
S5_MM_ROWS = 256


def _slab_perm():
    idx = jnp.arange(SLAB_GROUPS * LANES)
    l8, g8, p = idx // LANES, (idx % LANES) // S5_GROUP, idx % S5_GROUP
    dst = g8 * LANES + l8 * S5_GROUP + p
    return (dst[:, None] == idx[None, :]).astype(BF16)


def _permute_rows(z_ref, perm_ref, emit):
    step = min(S5_MM_ROWS, z_ref.shape[0])
    for r0 in range(0, z_ref.shape[0], step):
        emit(r0, _dot(z_ref[r0:r0 + step, :], perm_ref[...]))


def _group_rows(h, perm_ref, o_ref, h_ref, z_ref, chunk):
    rows = h.shape[0]
    nc = rows // chunk
    n_slab = h.shape[1] // LANES
    n_col = chunk // SLAB_GROUPS
    for v in range(n_slab):
        h_ref[v] = h[:, v * LANES:(v + 1) * LANES]
    for v in range(n_slab):
        for l in range(chunk):
            j, l8 = l // SLAB_GROUPS, l % SLAB_GROUPS
            r0 = (v * n_col + j) * nc
            z_ref[r0:r0 + nc, l8 * LANES:(l8 + 1) * LANES] = h_ref[v, pl.ds(l, nc, stride=chunk), :].astype(BF16)

    def emit(r0, blk):
        for q in range(blk.shape[0] // nc):
            v, j = divmod(r0 // nc + q, n_col)
            for g8 in range(SLAB_GROUPS):
                o_ref[v * SLAB_GROUPS + g8, :, j * LANES:(j + 1) * LANES] = (
                    blk[q * nc:(q + 1) * nc, g8 * LANES:(g8 + 1) * LANES].astype(o_ref.dtype))

    _permute_rows(z_ref, perm_ref, emit)


def _ungroup_rows(y_ref, perm_ref, z_ref, t_ref, chunk):
    n_slab, rows, _ = t_ref.shape
    nc = rows // chunk
    n_col = chunk // SLAB_GROUPS
    for v in range(n_slab):
        for j in range(n_col):
            r0 = (v * n_col + j) * nc
            for g8 in range(SLAB_GROUPS):
                z_ref[r0:r0 + nc, g8 * LANES:(g8 + 1) * LANES] = y_ref[v * SLAB_GROUPS + g8, :, j * LANES:(j + 1) * LANES]

    def emit(r0, blk):
        for q in range(blk.shape[0] // nc):
            v, j = divmod(r0 // nc + q, n_col)
            for l8 in range(SLAB_GROUPS):
                t_ref[v, pl.ds(j * SLAB_GROUPS + l8, nc, stride=chunk), :] = (
                    blk[q * nc:(q + 1) * nc, l8 * LANES:(l8 + 1) * LANES])

    _permute_rows(z_ref, perm_ref, emit)


S5_LP = S5_CHUNK * S5_GROUP
S5_RI = 2 * S5_STATE


def _s5_prep_kernel(lam_row_ref, ls_ref, btr_ref, bti_ref, cr_ref, ci_ref,
                    bs_ref, cs_ref, tp_ref, ap_ref):
    n_pow = S5_CHUNK + 1
    dt = jnp.exp(ls_ref[...])

    def powers(lr, li):
        mag = jnp.exp(lr * dt)
        a_re, a_im = mag * jnp.cos(li * dt), mag * jnp.sin(li * dt)
        pw = [(jnp.ones_like(a_re), jnp.zeros_like(a_im))]
        for _ in range(n_pow - 1):
            pr, pi = pw[-1]
            pw.append((pr * a_re - pi * a_im, pr * a_im + pi * a_re))
        return pw

    lr, li = lam_row_ref[0:1, :], lam_row_ref[1:2, :]
    pw_row = powers(lr, li)
    a_re, a_im = pw_row[1]
    den = lr * lr + li * li
    f_re = ((a_re - 1.0) * lr + a_im * li) / den
    f_im = (a_im * lr - (a_re - 1.0) * li) / den
    blk = (S5_GROUP, S5_RI)
    w_re = jnp.concatenate([jnp.broadcast_to(f_re * pw_row[S5_CHUNK - 1 - l][0] - f_im * pw_row[S5_CHUNK - 1 - l][1], blk)
                            for l in range(S5_CHUNK)], axis=0)
    w_im = jnp.concatenate([jnp.broadcast_to(f_re * pw_row[S5_CHUNK - 1 - l][1] + f_im * pw_row[S5_CHUNK - 1 - l][0], blk)
                            for l in range(S5_CHUNK)], axis=0)

    lane = lax.broadcasted_iota(jnp.int32, (1, S5_RI), 1)
    own = (lane // S5_STATE) == (pl.program_id(0) % 2)
    low = lane < S5_STATE
    keep = lambda z: jnp.where(own, z, 0.0)
    btr, bti = btr_ref[...], bti_ref[...]
    bs_ref[:, :S5_RI] = keep(w_re * btr - w_im * bti).astype(bs_ref.dtype)
    bs_ref[:, S5_RI:] = keep(w_re * bti + w_im * btr).astype(bs_ref.dtype)
    bf_t = jnp.where(low, f_re * btr[0:S5_GROUP] - f_im * bti[0:S5_GROUP],
                     f_re * bti[0:S5_GROUP] + f_im * btr[0:S5_GROUP])
    ap_ref[...] = jnp.zeros_like(ap_ref)
    for i, (j, part) in enumerate(((S5_CHUNK, 0), (S5_CHUNK, 1), (S5_CHUNK // 2, 0), (S5_CHUNK // 2, 1))):
        ap_ref[i:i + 1, :] = keep(pw_row[j][part])

    def spread(j0, part):
        return jnp.concatenate([jnp.broadcast_to(pw_row[j0 + l][part], blk) for l in range(S5_CHUNK)], axis=0)

    cr, ci = cr_ref[...], ci_ref[...]
    cs0 = jnp.where(low, cr * spread(0, 0) - ci * spread(0, 1),
                    -(cr * spread(0, 1) + ci * spread(0, 0)))
    cs_ref[:, :S5_RI] = keep(cr * spread(1, 0) - ci * spread(1, 1)).astype(cs_ref.dtype)
    cs_ref[:, S5_RI:] = keep(-(cr * spread(1, 1) + ci * spread(1, 0))).astype(cs_ref.dtype)
    r = lax.dot_general(bf_t, cs0, (((1,), (1,)), ((), ())), preferred_element_type=F32,
                        precision=lax.Precision.HIGHEST)
    lane_lp = lax.broadcasted_iota(jnp.int32, (S5_GROUP, S5_LP), 1)
    for l in range(S5_CHUNK):
        sh = l * S5_GROUP
        blk_l = r if l == 0 else jnp.where(lane_lp >= sh, pltpu.roll(r, sh, axis=1), 0.0)
        tp_ref[l * S5_GROUP:(l + 1) * S5_GROUP, :] = blk_l.astype(tp_ref.dtype)


def _s5_prep(lam_re, lam_im, log_step, b_re, b_im, c_re, c_im):
    g = lam_re.shape[0]
    dup = lambda z: jnp.concatenate([z, z], axis=-1)
    lam_row = jnp.stack([dup(lam_re), dup(lam_im)], axis=1)
    lam_row = jnp.pad(lam_row, ((0, 0), (0, 6), (0, 0)))
    ls = log_step.reshape(g, 1, 1)
    tiled = lambda z: jnp.tile(dup(z), (1, S5_CHUNK, 1))
    btr, bti = tiled(jnp.swapaxes(b_re, 1, 2)), tiled(jnp.swapaxes(b_im, 1, 2))
    cr, ci = tiled(c_re), tiled(c_im)
    gspec = lambda *s: pl.BlockSpec((None,) + s, lambda i: (i,) + (0,) * len(s))
    table = jax.ShapeDtypeStruct((g, S5_LP, 2 * S5_RI), BF16)
    return pl.pallas_call(
        _s5_prep_kernel,
        grid=(g,),
        in_specs=[gspec(8, S5_RI), gspec(1, 1)] + [gspec(S5_LP, S5_RI)] * 4,
        out_specs=[gspec(S5_LP, 2 * S5_RI)] * 2 + [gspec(S5_LP, S5_LP), gspec(8, S5_RI)],
        out_shape=[table] * 2 + [jax.ShapeDtypeStruct((g, S5_LP, S5_LP), BF16),
                                 jax.ShapeDtypeStruct((g, 8, S5_RI), F32)],
        compiler_params=_params(("arbitrary",)),
        name="s5_prep",
    )(lam_row, ls, btr, bti, cr, ci)


def _s5_scan_kernel(u_ref, us_ref, x0r_ref, x0i_ref, bs_ref, cs_ref, tp_ref, ap_ref,
                    y_ref, ys_ref, finr_ref, fini_ref, finsr_ref, finsi_ref, *, bsz, chunks):
    n_rows = bsz * chunks
    pair = range(2)
    us_in = [u_ref[g] for g in pair]
    x = sum(_dot(us_in[g], bs_ref[g]) for g in pair)
    xr, xi = x[:, :S5_RI], x[:, S5_RI:]
    row = lax.broadcasted_iota(jnp.int32, (n_rows, 1), 0) % chunks
    ap = ap_ref[0] + ap_ref[1]
    ar, ai = ap[0:1, :], ap[1:2, :]
    shift = 1
    while shift < chunks:
        sr = jnp.where(row >= shift, pltpu.roll(xr, shift, axis=0), 0.0)
        si = jnp.where(row >= shift, pltpu.roll(xi, shift, axis=0), 0.0)
        xr, xi = xr + (sr * ar - si * ai), xi + (sr * ai + si * ar)
        ar, ai = ar * ar - ai * ai, 2.0 * (ar * ai)
        shift *= 2
    pr = jnp.where(row >= 1, pltpu.roll(xr, 1, axis=0), 0.0).astype(BF16)
    pi = jnp.where(row >= 1, pltpu.roll(xi, 1, axis=0), 0.0).astype(BF16)
    prev = jnp.concatenate([pr, pi], axis=1)
    for g in pair:
        y_ref[g] = (_dot(us_in[g], tp_ref[g]) + _dot_nt(prev, cs_ref[g])).astype(y_ref.dtype)
    finr_ref[...] = jnp.zeros_like(finr_ref)
    fini_ref[...] = jnp.zeros_like(fini_ref)
    for b in range(bsz):
        last = slice((b + 1) * chunks - 1, (b + 1) * chunks)
        finr_ref[b:b + 1, :] = xr[last]
        fini_ref[b:b + 1, :] = xi[last]

    half = S5_LP // 2
    x0r, x0i = x0r_ref[...], x0i_ref[...]
    hr, hi = ap[2:3, :], ap[3:4, :]
    xs = sum(_dot(us_ref[g], bs_ref[g, half:, :]) for g in pair)
    finsr_ref[...] = x0r * hr - x0i * hi + xs[:, :S5_RI]
    finsi_ref[...] = x0r * hi + x0i * hr + xs[:, S5_RI:]
    x0 = jnp.concatenate([x0r, x0i], axis=1).astype(BF16)
    for g in pair:
        ys_ref[g] = (_dot(us_ref[g], tp_ref[g, :half, :half]) + _dot_nt(x0, cs_ref[g, :half, :])).astype(ys_ref.dtype)


def _s5_scan(u_g, us_g, x0r, x0i, bs, cs, tp, ap, bsz):
    g, n_rows, _ = u_g.shape
    n_s = us_g.shape[1]
    half = S5_LP // 2
    pspec = lambda *s: pl.BlockSpec((2,) + s, lambda i: (i,) + (0,) * len(s))
    ospec = lambda *s: pl.BlockSpec((None,) + s, lambda i: (i,) + (0,) * len(s))
    packed = lambda n: jax.ShapeDtypeStruct((g // 2, n, S5_RI), F32)
    return pl.pallas_call(
        functools.partial(_s5_scan_kernel, bsz=bsz, chunks=n_rows // bsz),
        grid=(g // 2,),
        in_specs=[pspec(n_rows, S5_LP), pspec(n_s, half), ospec(n_s, S5_RI), ospec(n_s, S5_RI)]
                 + [pspec(S5_LP, 2 * S5_RI)] * 2 + [pspec(S5_LP, S5_LP), pspec(8, S5_RI)],
        out_specs=[pspec(n_rows, S5_LP), pspec(n_s, half), ospec(8, S5_RI), ospec(8, S5_RI),
                   ospec(n_s, S5_RI), ospec(n_s, S5_RI)],
        out_shape=[jax.ShapeDtypeStruct((g, n_rows, S5_LP), BF16), jax.ShapeDtypeStruct((g, n_s, half), BF16),
                   packed(8), packed(8), packed(n_s), packed(n_s)],
        compiler_params=_params(("arbitrary",)),
        name="s5_scan",
    )(u_g, us_g, x0r, x0i, bs, cs, tp, ap)


def kernel(x_prompt, x_sample, state_ret, state_swa_k, state_swa_v, state_ssm_re, state_ssm_im, norm_mix, norm_ffn, norm_final, w_in_ab, ret_gn, w_out_ab, ssm_lam_re, ssm_lam_im, ssm_log_step, ssm_b_re, ssm_b_im, ssm_c_re, ssm_c_im, ssm_d, w_glu, w_ffn_in, w_ffn_out):
    bsz, seq, d = x_prompt.shape
    dbsz, n_new, _ = x_sample.shape
    assert state_swa_k.shape[2] == SWA_BUF and n_new == S5_CHUNK // 2 and seq % SWA_ROWS == 0
    xp = x_prompt.reshape(bsz * seq, d)
    xs = x_sample.reshape(dbsz * n_new, d)
    tm_p, tm_s = 512, dbsz * n_new
    row = lambda v: v.reshape(1, -1)

    w_in0 = w_in_ab[0].astype(BF16)
    w_out0 = w_out_ab[0].astype(BF16)
    w_f_in0, w_f_out0 = w_ffn_in[0].astype(BF16), w_ffn_out[0].astype(BF16)
    g_mix0, g_ffn0, gn0 = row(norm_mix[0]), row(norm_ffn[0]), row(ret_gn[0])

    ra_p, qkv_p = _ab_proj(xp, g_mix0, w_in0, tm_p)
    ra_s, qkv_s = _ab_proj(xs, g_mix0, w_in0, tm_s)

    zero_ret = jnp.zeros((bsz, H_A, DK_A, DV_A), F32)
    a_p, ret_p = _retention(ra_p, zero_ret, gn0, seq=seq, c_real=RET_CHUNK, n_chunks=4, out_dtype=BF16)
    a_s, ret_s = _retention(ra_s, state_ret[0], gn0, seq=n_new, c_real=n_new, n_chunks=1, out_dtype=F32)

    ob_p = _swa_prompt(qkv_p, bsz, seq)
    width_b = H_B * DH_B
    rows_last = lambda w: w.transpose(0, 2, 3, 1).reshape(dbsz, width_b, SWA_BUF)
    rows_first = lambda w: w.reshape(dbsz, H_B, DH_B, SWA_BUF).transpose(0, 3, 1, 2)[None]
    ob_s, swk_s, swv_s = _swa_sample(qkv_s, rows_last(state_swa_k[0]), rows_last(state_swa_v[0]), n_new)
    kv_tail = qkv_p.reshape(bsz, seq, QKV_W)[:, seq - SWA_BUF:, width_b:]
    swk_p = kv_tail[..., :width_b].reshape(bsz, SWA_BUF, H_B, DH_B)
    swv_p = kv_tail[..., width_b:].reshape(bsz, SWA_BUF, H_B, DH_B)

    g_mix1, g_ffn1 = row(norm_mix[1]), row(norm_ffn[1])
    perm = _slab_perm()
    yp, u_p = _ab_tail(xp, a_p, ob_p, w_out0, g_ffn0, w_f_in0, w_f_out0, g_mix1, perm, tm_p, S5_CHUNK)
    ys, u_s = _ab_tail(xs, a_s, ob_s, w_out0, g_ffn0, w_f_in0, w_f_out0, g_mix1, perm, tm_s, n_new)

    operators = _s5_prep(ssm_lam_re[0], ssm_lam_im[0], ssm_log_step[0],
                         ssm_b_re[0], ssm_b_im[0], ssm_c_re[0], ssm_c_im[0])
    pack = lambda st: st.reshape(st.shape[0], -1, S5_RI).transpose(1, 0, 2)
    unpack = lambda st: st.transpose(1, 0, 2).reshape(st.shape[1], -1, S5_STATE)[None]
    y5_p, y5_s, finr_p, fini_p, finr_s, fini_s = _s5_scan(
        u_p, u_s, pack(state_ssm_re[0]), pack(state_ssm_im[0]), *operators, bsz)

    tail1 = (perm.T, g_mix1, row(ssm_d[0]), w_glu[0].astype(BF16), g_ffn1,
             w_ffn_in[1].astype(BF16), w_ffn_out[1].astype(BF16), row(norm_final))
    yp = _s5_tail(yp, y5_p, *tail1, tm_p, S5_CHUNK)
    ys = _s5_tail(ys, y5_s, *tail1, tm_s, n_new)

    return (yp.reshape(bsz, seq, d), ys.reshape(dbsz, n_new, d),
            ret_p[None], ret_s[None],
            swk_p[None], swv_p[None],
            rows_first(swk_s), rows_first(swv_s),
            unpack(finr_p[:, :bsz]), unpack(fini_p[:, :bsz]), unpack(finr_s), unpack(fini_s))
```

```python
import functools
import math

import jax
import jax.numpy as jnp
from jax import lax
from jax.experimental import pallas as pl
from jax.experimental.pallas import tpu as pltpu

F32 = jnp.float32
BF16 = jnp.bfloat16

H_A, DK_A, DV_A = 4, 64, 128
H_B, DH_B = 8, 64
SWA_PAIRS = ((128, 1), (512, 4), (2048, 16))
SPAN = 128
SWA_BUF = 2048
PAST_LEN = 16384
RET_CHUNK = 128
S5_GROUP, S5_STATE = 16, 64
S5_CHUNK = 16
EPS = 1e-6
NEG_INF = -1e30
QA_W, KA_W, VA_W, GA_W = H_A * DK_A, H_A * DK_A, H_A * DV_A, H_A * DV_A
RA_W = QA_W + KA_W + VA_W + GA_W
QKV_W = 3 * H_B * DH_B

LANES = 128
VMEM_LIMIT = 56 * 1024 * 1024


def _params(sem):
    return pltpu.CompilerParams(dimension_semantics=sem, vmem_limit_bytes=VMEM_LIMIT)


def _const_spec(shape):
    nd = len(shape)
    return pl.BlockSpec(shape, lambda *_: (0,) * nd, pipeline_mode=pl.Buffered(1))


def _rms(x, g):
    return x * lax.rsqrt(jnp.mean(x * x, axis=-1, keepdims=True) + EPS) * g


def _dot(a, b):
    return jnp.dot(a, b, preferred_element_type=F32)


def _dot_nt(a, b):
    return lax.dot_general(a, b, (((1,), (1,)), ((), ())), preferred_element_type=F32)


def _dot_tn(a, b):
    return lax.dot_general(a, b, (((0,), (0,)), ((), ())), preferred_element_type=F32)


def _ab_proj_kernel(x_ref, g_ref, w_ref, ra_ref, qkv_ref):
    h = _rms(x_ref[...], g_ref[...]).astype(BF16)
    step = 512
    for n0 in range(0, RA_W, step):
        ra_ref[:, n0:n0 + step] = _dot(h, w_ref[:, n0:n0 + step])
    for n0 in range(0, QKV_W, step):
        qkv_ref[:, n0:n0 + step] = _dot(h, w_ref[:, RA_W + n0:RA_W + n0 + step])


def _ab_proj(x, g, w_bf16, tm):
    t, d = x.shape
    return pl.pallas_call(
        _ab_proj_kernel,
        grid=(t // tm,),
        in_specs=[pl.BlockSpec((tm, d), lambda i: (i, 0)),
                  _const_spec((1, d)),
                  _const_spec((d, RA_W + QKV_W))],
        out_specs=[pl.BlockSpec((tm, RA_W), lambda i: (i, 0)),
                   pl.BlockSpec((tm, QKV_W), lambda i: (i, 0))],
        out_shape=[jax.ShapeDtypeStruct((t, RA_W), F32), jax.ShapeDtypeStruct((t, QKV_W), F32)],
        compiler_params=_params(("arbitrary",)),
        name="ab_proj",
    )(x, g, w_bf16)


def _retention_tables(c_real):
    c = RET_CHUNK
    log_g = jnp.log1p(-jnp.exp2(-5.0 - jnp.arange(H_A, dtype=F32)))
    idx = jnp.arange(c, dtype=F32)
    rel = idx[:, None] - idx[None, :]
    dec = jnp.where(rel >= 0, jnp.exp(jnp.maximum(rel, 0.0)[None] * log_g[:, None, None]), 0.0)
    dec = dec.reshape(H_A * c, c)
    real = (idx < c_real)[:, None]
    qd = jnp.exp((idx + 1.0)[:, None] * log_g[None, :])
    qd = jnp.repeat(qd, DV_A, axis=1)
    kd = jnp.where(real, jnp.exp((c_real - 1.0 - idx)[:, None] * log_g[None, :]), 0.0)
    kd = jnp.repeat(kd, DK_A, axis=1)
    row_h = jnp.arange(H_A * DK_A)[:, None] // DK_A
    col_h = jnp.arange(H_A * DV_A)[None, :] // DV_A
    bd = (row_h == col_h).astype(F32)
    dm = bd * jnp.repeat(jnp.exp(c_real * log_g), DV_A)[None, :]
    return dec, qd, kd, dm, bd


def _retention_kernel(ra_ref, s0_ref, gain_ref, dec_ref, qd_ref, kd_ref, dm_ref, bd_ref,
                      a_ref, sout_ref, sbd_ref, *, c_real, n_chunks, n_seq):
    c = RET_CHUNK
    j = pl.program_id(1)

    @pl.when(j == 0)
    def _():
        sbd_ref[...] = jnp.zeros_like(sbd_ref)
        for s in range(n_seq):
            for h in range(H_A):
                sbd_ref[s, h * DK_A:(h + 1) * DK_A, h * DV_A:(h + 1) * DV_A] = s0_ref[s, h]

    lane_q = lax.broadcasted_iota(jnp.int32, (c, QA_W), 1) // DK_A
    for ci in range(n_chunks):
        for s in range(n_seq):
            _retention_chunk(ra_ref.at[s], a_ref.at[s], sbd_ref.at[s], gain_ref, dec_ref, qd_ref, kd_ref, dm_ref,
                             bd_ref, lane_q, ci, c_real)

    @pl.when(j == pl.num_programs(1) - 1)
    def _():
        for s in range(n_seq):
            for h in range(H_A):
                sout_ref[s, h] = sbd_ref[s, h * DK_A:(h + 1) * DK_A, h * DV_A:(h + 1) * DV_A]


def _retention_chunk(ra_ref, a_ref, sbd_ref, gain_ref, dec_ref, qd_ref, kd_ref, dm_ref, bd_ref, lane_q, ci, c_real):
    c = RET_CHUNK
    rows = ra_ref[ci * c_real:(ci + 1) * c_real, :]
    gate = rows[:, QA_W + KA_W + VA_W:]
    if c_real < c:
        rows = jnp.concatenate([rows, jnp.zeros((c - c_real, RA_W), F32)], axis=0)
    q = rows[:, :QA_W]
    k = rows[:, QA_W:QA_W + KA_W] * (DK_A ** -0.5)
    v = rows[:, QA_W + KA_W:QA_W + KA_W + VA_W]
    vb = v.astype(BF16)
    qm = jnp.concatenate([jnp.where(lane_q == h, q, 0.0) for h in range(H_A)], axis=0).astype(BF16)
    s = _dot_nt(qm, k.astype(BF16)) * dec_ref[...]
    sb = s.astype(BF16)
    sbd = sbd_ref[...]
    cross = _dot(q.astype(BF16), sbd.astype(BF16)) * qd_ref[...]
    upd = _dot_tn((k * kd_ref[...]).astype(BF16), vb)
    sbd_ref[...] = sbd * dm_ref[...] + upd * bd_ref[...]
    for h in range(H_A):
        sl = slice(h * DV_A, (h + 1) * DV_A)
        o = _dot(sb[h * c:(h + 1) * c], vb[:, sl]) + cross[:, sl]
        o = o[:c_real]
        mu = jnp.mean(o, axis=-1, keepdims=True)
        var = jnp.mean(jnp.square(o - mu), axis=-1, keepdims=True)
        y = (o - mu) * lax.rsqrt(var + EPS) * gain_ref[:, sl]
        gh = gate[:, sl]
        a_ref[ci * c_real:(ci + 1) * c_real, sl] = (gh * jax.nn.sigmoid(gh) * y).astype(a_ref.dtype)


def _retention(ra, state0, gain, *, seq, c_real, n_chunks, n_seq, out_dtype):
    t = ra.shape[0]
    bsz = t // seq
    rows = c_real * n_chunks
    steps = seq // rows
    tables = _retention_tables(c_real)
    kern = functools.partial(_retention_kernel, c_real=c_real, n_chunks=n_chunks, n_seq=n_seq)
    a, s_out = pl.pallas_call(
        kern,
        grid=(bsz // n_seq, steps),
        in_specs=[pl.BlockSpec((n_seq, rows, RA_W), lambda b, j: (b, j, 0)),
                  pl.BlockSpec((n_seq, H_A, DK_A, DV_A), lambda b, j: (b, 0, 0, 0)),
                  _const_spec((1, VA_W))] + [_const_spec(tb.shape) for tb in tables],
        out_specs=[pl.BlockSpec((n_seq, rows, VA_W), lambda b, j: (b, j, 0)),
                   pl.BlockSpec((n_seq, H_A, DK_A, DV_A), lambda b, j: (b, 0, 0, 0))],
        out_shape=[jax.ShapeDtypeStruct((bsz, seq, VA_W), out_dtype),
                   jax.ShapeDtypeStruct((bsz, H_A, DK_A, DV_A), F32)],
        scratch_shapes=[pltpu.VMEM((n_seq, H_A * DK_A, H_A * DV_A), F32)],
        compiler_params=_params(("arbitrary", "arbitrary")),
        name="retention",
    )(ra.reshape(bsz, seq, RA_W), state0, gain, *tables)
    return a.reshape(t, VA_W), s_out


SWA_ROWS = 2048
N_BRANCH = len(SWA_PAIRS)


def _alibi_slopes():
    return jnp.exp2(-8.0 * jnp.arange(1, H_B + 1, dtype=F32) / H_B)


def _swa_prompt_bias():
    qi = jnp.arange(SPAN)[:, None]
    kj = jnp.arange(2 * SPAN)[None, :]
    dist = SPAN + qi - kj
    band = (dist >= 0) & (dist <= SPAN)
    slopes = _alibi_slopes()
    out = []
    for (_, dil) in SWA_PAIRS:
        pen = -slopes[:, None, None] * (dil * dist).astype(F32)[None]
        normal = jnp.where(band[None], pen, NEG_INF)
        first = jnp.where((band & (kj >= SPAN))[None], pen, NEG_INF)
        out.append(jnp.stack([normal, first], axis=1))
    tab = jnp.stack(out, axis=1)
    tab = tab.reshape(H_B // 2, 2, N_BRANCH, 2, SPAN, 2 * SPAN).transpose(0, 2, 3, 1, 4, 5)
    return tab.reshape(H_B // 2, N_BRANCH, 2, 2 * SPAN, 2 * SPAN)


COARSE = 4


def _swa_unit(q, kk, vv, bias, head0):
    q = q * (DH_B ** -0.5)
    qm = jnp.concatenate([jnp.where(head0, q, 0.0), jnp.where(head0, 0.0, q)], axis=0).astype(BF16)
    s = _dot_nt(qm, kk.astype(BF16)) + bias
    m = jnp.max(s, axis=-1, keepdims=True)
    p = jnp.exp(s - m).astype(BF16)
    ones = jnp.ones((2 * SPAN, LANES), BF16)
    res = _dot(p, jnp.concatenate([vv.astype(BF16), ones], axis=1))
    acc = jnp.where(head0, res[:SPAN, :LANES], res[SPAN:, :LANES])
    den = jnp.where(head0, res[:SPAN, LANES:], res[SPAN:, LANES:])
    mm = jnp.where(head0, jnp.broadcast_to(m[:SPAN], (SPAN, LANES)), jnp.broadcast_to(m[SPAN:], (SPAN, LANES)))
    return acc, den, mm


def _swa_prompt_kernel(q_ref, k_ref, v_ref, bias_ref, o_ref, kbuf, vbuf, acc_ref, l_ref, m_ref,
                       q4, k4, v4, acc4, l4, m4):
    j = pl.program_id(2)
    rows = SWA_ROWS
    sub = rows // COARSE
    slot = j % 2

    @pl.when(j == 0)
    def _():
        k4[1] = jnp.zeros(k4.shape[1:], F32)
        v4[1] = jnp.zeros(v4.shape[1:], F32)

    for c in range(COARSE):
        q4[c] = q_ref[pl.ds(c, sub, stride=COARSE), :]
        k4[slot, c] = k_ref[pl.ds(c, sub, stride=COARSE), :]
        v4[slot, c] = v_ref[pl.ds(c, sub, stride=COARSE), :]

    @pl.when(j == 0)
    def _():
        kbuf[0:rows, :] = jnp.zeros((rows, LANES), F32)
        vbuf[0:rows, :] = jnp.zeros((rows, LANES), F32)

    @pl.when(j > 0)
    def _():
        kbuf[0:rows, :] = kbuf[rows:2 * rows, :]
        vbuf[0:rows, :] = vbuf[rows:2 * rows, :]

    kbuf[rows:2 * rows, :] = k_ref[...]
    vbuf[rows:2 * rows, :] = v_ref[...]

    lane = lax.broadcasted_iota(jnp.int32, (SPAN, LANES), 1)
    head0 = lane < DH_B
    first_step = (j == 0).astype(jnp.int32)

    for g, (_, dil) in enumerate(SWA_PAIRS):
        blocks = rows // (SPAN * dil)

        def unit(u, carry, g=g, dil=dil):
            wb = u // dil
            r = u % dil
            q_start = wb * (SPAN * dil) + r
            k_start = rows + (wb - 1) * (SPAN * dil) + r
            variant = jnp.where(wb == 0, first_step, 0)
            acc, den, mm = _swa_unit(q_ref[pl.ds(q_start, SPAN, stride=dil), :],
                                     kbuf[pl.ds(k_start, 2 * SPAN, stride=dil), :],
                                     vbuf[pl.ds(k_start, 2 * SPAN, stride=dil), :],
                                     bias_ref[0, g, variant], head0)
            acc_ref[g, pl.ds(q_start, SPAN, stride=dil), :] = acc
            l_ref[g, pl.ds(q_start, SPAN, stride=dil), :] = den
            m_ref[g, pl.ds(q_start, SPAN, stride=dil), :] = mm
            return carry

        def unit_two_level(u, carry, g=g, fine=dil // COARSE):
            c = u % COARSE
            f = u // COARSE
            pick = pl.ds(f, SPAN, stride=fine)
            acc, den, mm = _swa_unit(q4[c, pick, :],
                                     jnp.concatenate([k4[1 - slot, c, pick, :], k4[slot, c, pick, :]], axis=0),
                                     jnp.concatenate([v4[1 - slot, c, pick, :], v4[slot, c, pick, :]], axis=0),
                                     bias_ref[0, g, first_step], head0)
            dst = pl.ds(c * sub + f, SPAN, stride=fine)
            acc4[dst, :] = acc
            l4[dst, :] = den
            m4[dst, :] = mm
            return carry

        if dil % (COARSE * COARSE) == 0:
            assert blocks == 1
            lax.fori_loop(0, dil, unit_two_level, 0, unroll=8)
            for c in range(COARSE):
                src, dst = slice(c * sub, (c + 1) * sub), pl.ds(c, sub, stride=COARSE)
                acc_ref[g, dst, :] = acc4[src, :]
                l_ref[g, dst, :] = l4[src, :]
                m_ref[g, dst, :] = m4[src, :]
        else:
            lax.fori_loop(0, blocks * dil, unit, 0, unroll=8)

    tile = 256
    for r0 in range(0, rows, tile):
        sl = slice(r0, r0 + tile)
        ms = [m_ref[g, sl, :] for g in range(N_BRANCH)]
        mx = functools.reduce(jnp.maximum, ms)
        ws = [jnp.exp(mg - mx) for mg in ms]
        num = sum(w * acc_ref[g, sl, :] for g, w in enumerate(ws))
        den = sum(w * l_ref[g, sl, :] for g, w in enumerate(ws))
        o_ref[sl, :] = (num / den).astype(o_ref.dtype)


def _swa_prompt(qkv, bsz, seq):
    t = qkv.shape[0]
    steps = seq // SWA_ROWS
    npair = H_B // 2
    bias = _swa_prompt_bias()
    blk = (SWA_ROWS, LANES)
    return pl.pallas_call(
        _swa_prompt_kernel,
        grid=(bsz, npair, steps),
        in_specs=[pl.BlockSpec(blk, lambda b, hp, j: (b * steps + j, hp)),
                  pl.BlockSpec(blk, lambda b, hp, j: (b * steps + j, npair + hp)),
                  pl.BlockSpec(blk, lambda b, hp, j: (b * steps + j, 2 * npair + hp)),
                  pl.BlockSpec((1, N_BRANCH, 2, 2 * SPAN, 2 * SPAN), lambda b, hp, j: (hp, 0, 0, 0, 0))],
        out_specs=pl.BlockSpec(blk, lambda b, hp, j: (b * steps + j, hp)),
        out_shape=jax.ShapeDtypeStruct((t, H_B * DH_B), BF16),
        scratch_shapes=[pltpu.VMEM((2 * SWA_ROWS, LANES), F32), pltpu.VMEM((2 * SWA_ROWS, LANES), F32)]
                       + [pltpu.VMEM((N_BRANCH, SWA_ROWS, LANES), F32)] * 3
                       + [pltpu.VMEM((COARSE, SWA_ROWS // COARSE, LANES), F32)]
                       + [pltpu.VMEM((2, COARSE, SWA_ROWS // COARSE, LANES), F32)] * 2
                       + [pltpu.VMEM((SWA_ROWS, LANES), F32)] * 3,
        compiler_params=_params(("arbitrary", "arbitrary", "arbitrary")),
        name="swa_prompt",
    )(qkv, qkv, qkv, bias)


def _swa_sample_bias(n_new):
    t = jnp.arange(n_new)[:, None]
    slopes = _alibi_slopes()

    def table(j):
        dist = SWA_BUF + t - j
        out = []
        for (window, dil) in SWA_PAIRS:
            valid = (dist >= 0) & (dist <= window) & (dist % dil == 0) & (PAST_LEN + t - dist >= 0)
            pen = -slopes[:, None, None] * dist.astype(F32)[None]
            out.append(jnp.where(valid[None], pen, NEG_INF).reshape(H_B * n_new, -1))
        return jnp.stack(out)

    bias_a = table(jnp.arange(SWA_BUF)[None, :])
    jb = jnp.arange(LANES)[None, :]
    bias_b = jnp.where(jb < n_new, table(SWA_BUF + jb), NEG_INF)
    return bias_a, bias_b


def _swa_sample_kernel(q_ref, k_ref, v_ref, kp_ref, vp_ref, ba_ref, bb_ref,
                       o_ref, ko_ref, vo_ref, *, n_new):
    width = H_B * DH_B
    lane_h = lax.broadcasted_iota(jnp.int32, (n_new, width), 1) // DH_B
    q = q_ref[...] * (DH_B ** -0.5)
    qm = jnp.concatenate([jnp.where(lane_h == h, q, 0.0) for h in range(H_B)], axis=0).astype(BF16)
    pad = jnp.zeros((LANES - n_new, width), F32)
    k_new = jnp.concatenate([k_ref[...], pad], axis=0)
    v_new = jnp.concatenate([v_ref[...], pad], axis=0)
    s_a = _dot(qm, kp_ref[0].astype(BF16))
    s_b = _dot_nt(qm, k_new.astype(BF16))
    sa = [s_a + ba_ref[g] for g in range(N_BRANCH)]
    sb = [s_b + bb_ref[g] for g in range(N_BRANCH)]
    mx = functools.reduce(jnp.maximum, [jnp.max(x, axis=-1, keepdims=True) for x in sa + sb])
    p_a = sum(jnp.exp(x - mx) for x in sa).astype(BF16)
    p_b = sum(jnp.exp(x - mx) for x in sb).astype(BF16)
    den = (jnp.sum(p_a.astype(F32), axis=-1, keepdims=True)
           + jnp.sum(p_b.astype(F32), axis=-1, keepdims=True))
    o = (_dot_nt(p_a, vp_ref[0].astype(BF16)) + _dot(p_b, v_new.astype(BF16))) / den
    o_ref[...] = sum(jnp.where(lane_h == h, o[h * n_new:(h + 1) * n_new], 0.0) for h in range(H_B))

    lane = lax.broadcasted_iota(jnp.int32, (DH_B, LANES), 1)
    for src_ref, new, dst_ref in ((kp_ref, k_new, ko_ref), (vp_ref, v_new, vo_ref)):
        new_t = pltpu.roll(new.T, LANES - n_new, axis=1)
        for h in range(H_B):
            rows = slice(h * DH_B, (h + 1) * DH_B)
            shifted = pltpu.roll(src_ref[0, rows, :], SWA_BUF - n_new, axis=1)
            dst_ref[0, rows, 0:SWA_BUF - LANES] = shifted[:, 0:SWA_BUF - LANES]
            dst_ref[0, rows, SWA_BUF - LANES:] = jnp.where(lane >= LANES - n_new, new_t[rows],
                                                           shifted[:, SWA_BUF - LANES:])


def _swa_sample(qkv, k_past, v_past, n_new):
    bsz = k_past.shape[0]
    width = H_B * DH_B
    bias_a, bias_b = _swa_sample_bias(n_new)
    state_blk = pl.BlockSpec((1, width, SWA_BUF), lambda b: (b, 0, 0))
    new_blk = lambda col: pl.BlockSpec((n_new, width), lambda b, col=col: (b, col))
    return pl.pallas_call(
        functools.partial(_swa_sample_kernel, n_new=n_new),
        grid=(bsz,),
        in_specs=[new_blk(0), new_blk(1), new_blk(2), state_blk, state_blk,
                  _const_spec(bias_a.shape), _const_spec(bias_b.shape)],
        out_specs=[pl.BlockSpec((n_new, width), lambda b: (b, 0)), state_blk, state_blk],
        out_shape=[jax.ShapeDtypeStruct((bsz * n_new, width), F32),
                   jax.ShapeDtypeStruct(k_past.shape, F32), jax.ShapeDtypeStruct(v_past.shape, F32)],
        compiler_params=_params(("arbitrary",)),
        name="swa_sample",
    )(qkv, qkv, qkv, k_past, v_past, bias_a, bias_b)


FFN_CHUNK = 256


def _ffn(y1, g_ffn_ref, w_in_ref, w_o_ref, acc_ref):
    d_ff = w_o_ref.shape[0]
    h = _rms(y1, g_ffn_ref[...]).astype(BF16)
    for ci, c0 in enumerate(range(0, d_ff, FFN_CHUNK)):
        gate = _dot(h, w_in_ref[:, c0:c0 + FFN_CHUNK])
        up = _dot(h, w_in_ref[:, d_ff + c0:d_ff + c0 + FFN_CHUNK])
        act = (gate * jax.nn.sigmoid(gate) * up).astype(BF16)
        part = _dot(act, w_o_ref[c0:c0 + FFN_CHUNK, :])
        if ci == 0:
            acc_ref[...] = part
        else:
            acc_ref[...] += part
    return y1 + acc_ref[...]


def _ab_tail_kernel(x_ref, a_ref, ob_ref, w_out_ref, g_ffn_ref, w_in_ref, w_o_ref, g_next_ref, perm_ref,
                    o_ref, u_ref, acc_ref, h_ref, z_ref, *, chunk):
    mix = (_dot(a_ref[...].astype(BF16), w_out_ref[0:VA_W, :])
           + _dot(ob_ref[...].astype(BF16), w_out_ref[VA_W:, :]))
    y = _ffn(x_ref[...] + mix, g_ffn_ref, w_in_ref, w_o_ref, acc_ref)
    o_ref[...] = y
    _group_rows(_rms(y, g_next_ref[...]), perm_ref, u_ref, h_ref, z_ref, chunk)


def _s5_tail_kernel(x_ref, yg_ref, perm_ref, g_mix_ref, dskip_ref, w_glu_ref, g_ffn_ref, w_in_ref, w_o_ref,
                    g_fin_ref, o_ref, acc_ref, z_ref, t_ref, *, chunk):
    x = x_ref[...]
    d = x.shape[-1]
    u = _rms(x, g_mix_ref[...])
    _ungroup_rows(yg_ref, perm_ref, z_ref, t_ref, chunk)
    ys = jnp.concatenate([t_ref[v] for v in range(d // LANES)], axis=1)
    z = jax.nn.gelu(ys + dskip_ref[...] * u, approximate=True).astype(BF16)
    val = _dot(z, w_glu_ref[:, 0:d])
    gate = _dot(z, w_glu_ref[:, d:2 * d])
    y2 = _ffn(x + val * jax.nn.sigmoid(gate), g_ffn_ref, w_in_ref, w_o_ref, acc_ref)
    o_ref[...] = _rms(y2, g_fin_ref[...])


def _row_spec(tm, width):
    return pl.BlockSpec((tm, width), lambda i: (i, 0))


def _group_scratch(tm, d, chunk):
    n_slab, nc, n_col = d // LANES, tm // chunk, chunk // SLAB_GROUPS
    assert min(S5_MM_ROWS, n_slab * n_col * nc) % nc == 0
    return (pltpu.VMEM((n_slab, tm, LANES), F32), pltpu.VMEM((n_slab * n_col * nc, SLAB_GROUPS * LANES), BF16))


def _group_spec(tm, d, chunk):
    return pl.BlockSpec((d // S5_GROUP, tm // chunk, chunk * S5_GROUP), lambda i: (0, i, 0))


def _ab_tail(x, a, ob, w_out, g_ffn, w_in, w_o, g_next, perm, tm, chunk):
    t, d = x.shape
    h_scr, z_scr = _group_scratch(tm, d, chunk)
    return pl.pallas_call(
        functools.partial(_ab_tail_kernel, chunk=chunk),
        grid=(t // tm,),
        in_specs=[_row_spec(tm, d), _row_spec(tm, a.shape[1]), _row_spec(tm, ob.shape[1]),
                  _const_spec(w_out.shape), _const_spec((1, d)), _const_spec(w_in.shape), _const_spec(w_o.shape),
                  _const_spec((1, d)), _const_spec(perm.shape)],
        out_specs=[_row_spec(tm, d), _group_spec(tm, d, chunk)],
        out_shape=[jax.ShapeDtypeStruct((t, d), F32),
                   jax.ShapeDtypeStruct((d // S5_GROUP, t // chunk, chunk * S5_GROUP), BF16)],
        scratch_shapes=[pltpu.VMEM((tm, d), F32), h_scr, z_scr],
        compiler_params=_params(("arbitrary",)),
        name="ab_tail",
    )(x, a, ob, w_out, g_ffn, w_in, w_o, g_next, perm)


def _s5_tail(x, yg, perm_t, g_mix, dskip, w_glu, g_ffn, w_in, w_o, g_fin, tm, chunk):
    t, d = x.shape
    t_scr, z_scr = _group_scratch(tm, d, chunk)
    return pl.pallas_call(
        functools.partial(_s5_tail_kernel, chunk=chunk),
        grid=(t // tm,),
        in_specs=[_row_spec(tm, d), _group_spec(tm, d, chunk), _const_spec(perm_t.shape),
                  _const_spec((1, d)), _const_spec((1, d)),
                  _const_spec(w_glu.shape), _const_spec((1, d)), _const_spec(w_in.shape), _const_spec(w_o.shape),
                  _const_spec((1, d))],
        out_specs=_row_spec(tm, d),
        out_shape=jax.ShapeDtypeStruct((t, d), F32),
        scratch_shapes=[pltpu.VMEM((tm, d), F32), z_scr, t_scr],
        compiler_params=_params(("arbitrary",)),
        name="s5_tail",
    )(x, yg, perm_t, g_mix, dskip, w_glu, g_ffn, w_in, w_o, g_fin)


SLAB_GROUPS = LANES // S5_GROUP
S5_MM_ROWS = 256


def _slab_perm():
    idx = jnp.arange(SLAB_GROUPS * LANES)
    l8, g8, p = idx // LANES, (idx % LANES) // S5_GROUP, idx % S5_GROUP
    dst = g8 * LANES + l8 * S5_GROUP + p
    return (dst[:, None] == idx[None, :]).astype(BF16)


def _permute_rows(z_ref, perm_ref, emit):
    step = min(S5_MM_ROWS, z_ref.shape[0])
    for r0 in range(0, z_ref.shape[0], step):
        emit(r0, _dot(z_ref[r0:r0 + step, :], perm_ref[...]))


def _group_rows(h, perm_ref, o_ref, h_ref, z_ref, chunk):
    rows = h.shape[0]
    nc = rows // chunk
    n_slab = h.shape[1] // LANES
    n_col = chunk // SLAB_GROUPS
    for v in range(n_slab):
        h_ref[v] = h[:, v * LANES:(v + 1) * LANES]
    for v in range(n_slab):
        for l in range(chunk):
            j, l8 = l // SLAB_GROUPS, l % SLAB_GROUPS
            r0 = (v * n_col + j) * nc
            z_ref[r0:r0 + nc, l8 * LANES:(l8 + 1) * LANES] = h_ref[v, pl.ds(l, nc, stride=chunk), :].astype(BF16)

    def emit(r0, blk):
        for q in range(blk.shape[0] // nc):
            v, j = divmod(r0 // nc + q, n_col)
            for g8 in range(SLAB_GROUPS):
                o_ref[v * SLAB_GROUPS + g8, :, j * LANES:(j + 1) * LANES] = (
                    blk[q * nc:(q + 1) * nc, g8 * LANES:(g8 + 1) * LANES].astype(o_ref.dtype))

    _permute_rows(z_ref, perm_ref, emit)


def _ungroup_rows(y_ref, perm_ref, z_ref, t_ref, chunk):
    n_slab, rows, _ = t_ref.shape
    nc = rows // chunk
    n_col = chunk // SLAB_GROUPS
    for v in range(n_slab):
        for j in range(n_col):
            r0 = (v * n_col + j) * nc
            for g8 in range(SLAB_GROUPS):
                z_ref[r0:r0 + nc, g8 * LANES:(g8 + 1) * LANES] = y_ref[v * SLAB_GROUPS + g8, :, j * LANES:(j + 1) * LANES]

    def emit(r0, blk):
        for q in range(blk.shape[0] // nc):
            v, j = divmod(r0 // nc + q, n_col)
            for l8 in range(SLAB_GROUPS):
                t_ref[v, pl.ds(j * SLAB_GROUPS + l8, nc, stride=chunk), :] = (
                    blk[q * nc:(q + 1) * nc, l8 * LANES:(l8 + 1) * LANES])

    _permute_rows(z_ref, perm_ref, emit)


S5_LP = S5_CHUNK * S5_GROUP
S5_RI = 2 * S5_STATE


def _s5_prep_kernel(*refs):
    for parity in range(2):
        _s5_prep_group(parity, *[r.at[parity] for r in refs])


def _s5_prep_group(parity, lam_row_ref, ls_ref, btr_ref, bti_ref, cr_ref, ci_ref,
                   bs_ref, cs_ref, tp_ref, ap_ref):
    n_pow = S5_CHUNK + 1
    dt = jnp.exp(ls_ref[...])

    def powers(lr, li):
        mag = jnp.exp(lr * dt)
        a_re, a_im = mag * jnp.cos(li * dt), mag * jnp.sin(li * dt)
        pw = [(jnp.ones_like(a_re), jnp.zeros_like(a_im))]
        for _ in range(n_pow - 1):
            pr, pi = pw[-1]
            pw.append((pr * a_re - pi * a_im, pr * a_im + pi * a_re))
        return pw

    lr, li = lam_row_ref[0:1, :], lam_row_ref[1:2, :]
    pw_row = powers(lr, li)
    a_re, a_im = pw_row[1]
    den = lr * lr + li * li
    f_re = ((a_re - 1.0) * lr + a_im * li) / den
    f_im = (a_im * lr - (a_re - 1.0) * li) / den
    blk = (S5_GROUP, S5_RI)
    w_re = jnp.concatenate([jnp.broadcast_to(f_re * pw_row[S5_CHUNK - 1 - l][0] - f_im * pw_row[S5_CHUNK - 1 - l][1], blk)
                            for l in range(S5_CHUNK)], axis=0)
    w_im = jnp.concatenate([jnp.broadcast_to(f_re * pw_row[S5_CHUNK - 1 - l][1] + f_im * pw_row[S5_CHUNK - 1 - l][0], blk)
                            for l in range(S5_CHUNK)], axis=0)

    lane = lax.broadcasted_iota(jnp.int32, (1, S5_RI), 1)
    own = (lane // S5_STATE) == parity
    low = lane < S5_STATE
    keep = lambda z: jnp.where(own, z, 0.0)
    btr, bti = btr_ref[...], bti_ref[...]
    bs_ref[:, :S5_RI] = keep(w_re * btr - w_im * bti).astype(bs_ref.dtype)
    bs_ref[:, S5_RI:] = keep(w_re * bti + w_im * btr).astype(bs_ref.dtype)
    bf_t = jnp.where(low, f_re * btr[0:S5_GROUP] - f_im * bti[0:S5_GROUP],
                     f_re * bti[0:S5_GROUP] + f_im * btr[0:S5_GROUP])
    ap_ref[...] = jnp.zeros_like(ap_ref)
    for i, (j, part) in enumerate(((S5_CHUNK, 0), (S5_CHUNK, 1), (S5_CHUNK // 2, 0), (S5_CHUNK // 2, 1))):
        ap_ref[i:i + 1, :] = keep(pw_row[j][part])

    def spread(j0, part):
        return jnp.concatenate([jnp.broadcast_to(pw_row[j0 + l][part], blk) for l in range(S5_CHUNK)], axis=0)

    cr, ci = cr_ref[...], ci_ref[...]
    cs0 = jnp.where(low, cr * spread(0, 0) - ci * spread(0, 1),
                    -(cr * spread(0, 1) + ci * spread(0, 0)))
    cs_ref[:, :S5_RI] = keep(cr * spread(1, 0) - ci * spread(1, 1)).astype(cs_ref.dtype)
    cs_ref[:, S5_RI:] = keep(-(cr * spread(1, 1) + ci * spread(1, 0))).astype(cs_ref.dtype)
    r = lax.dot_general(bf_t, cs0, (((1,), (1,)), ((), ())), preferred_element_type=F32,
                        precision=lax.Precision.HIGHEST)
    lane_lp = lax.broadcasted_iota(jnp.int32, (S5_GROUP, S5_LP), 1)
    for l in range(S5_CHUNK):
        sh = l * S5_GROUP
        blk_l = r if l == 0 else jnp.where(lane_lp >= sh, pltpu.roll(r, sh, axis=1), 0.0)
        tp_ref[l * S5_GROUP:(l + 1) * S5_GROUP, :] = blk_l.astype(tp_ref.dtype)


def _s5_prep(lam_re, lam_im, log_step, b_re, b_im, c_re, c_im):
    g = lam_re.shape[0]
    dup = lambda z: jnp.concatenate([z, z], axis=-1)
    lam_row = jnp.stack([dup(lam_re), dup(lam_im)], axis=1)
    lam_row = jnp.pad(lam_row, ((0, 0), (0, 6), (0, 0)))
    ls = log_step.reshape(g, 1, 1)
    tiled = lambda z: jnp.tile(dup(z), (1, S5_CHUNK, 1))
    btr, bti = tiled(jnp.swapaxes(b_re, 1, 2)), tiled(jnp.swapaxes(b_im, 1, 2))
    cr, ci = tiled(c_re), tiled(c_im)
    gspec = lambda *s: pl.BlockSpec((2,) + s, lambda i: (i,) + (0,) * len(s))
    table = jax.ShapeDtypeStruct((g, S5_LP, 2 * S5_RI), BF16)
    return pl.pallas_call(
        _s5_prep_kernel,
        grid=(g // 2,),
        in_specs=[gspec(8, S5_RI), gspec(1, 1)] + [gspec(S5_LP, S5_RI)] * 4,
        out_specs=[gspec(S5_LP, 2 * S5_RI)] * 2 + [gspec(S5_LP, S5_LP), gspec(8, S5_RI)],
        out_shape=[table] * 2 + [jax.ShapeDtypeStruct((g, S5_LP, S5_LP), BF16),
                                 jax.ShapeDtypeStruct((g, 8, S5_RI), F32)],
        compiler_params=_params(("arbitrary",)),
        name="s5_prep",
    )(lam_row, ls, btr, bti, cr, ci)


def _s5_scan_kernel(u_ref, us_ref, x0r_ref, x0i_ref, bs_ref, cs_ref, tp_ref, ap_ref,
                    y_ref, ys_ref, finr_ref, fini_ref, finsr_ref, finsi_ref, *, bsz, chunks):
    n_rows = bsz * chunks
    pair = range(2)
    us_in = [u_ref[g] for g in pair]
    x = sum(_dot(us_in[g], bs_ref[g]) for g in pair)
    xr, xi = x[:, :S5_RI], x[:, S5_RI:]
    row = lax.broadcasted_iota(jnp.int32, (n_rows, 1), 0) % chunks
    ap = ap_ref[0] + ap_ref[1]
    ar, ai = ap[0:1, :], ap[1:2, :]
    shift = 1
    while shift < chunks:
        sr = jnp.where(row >= shift, pltpu.roll(xr, shift, axis=0), 0.0)
        si = jnp.where(row >= shift, pltpu.roll(xi, shift, axis=0), 0.0)
        xr, xi = xr + (sr * ar - si * ai), xi + (sr * ai + si * ar)
        ar, ai = ar * ar - ai * ai, 2.0 * (ar * ai)
        shift *= 2
    pr = jnp.where(row >= 1, pltpu.roll(xr, 1, axis=0), 0.0).astype(BF16)
    pi = jnp.where(row >= 1, pltpu.roll(xi, 1, axis=0), 0.0).astype(BF16)
    prev = jnp.concatenate([pr, pi], axis=1)
    for g in pair:
        y_ref[g] = (_dot(us_in[g], tp_ref[g]) + _dot_nt(prev, cs_ref[g])).astype(y_ref.dtype)
    finr_ref[...] = jnp.zeros_like(finr_ref)
    fini_ref[...] = jnp.zeros_like(fini_ref)
    for b in range(bsz):
        last = slice((b + 1) * chunks - 1, (b + 1) * chunks)
        finr_ref[b:b + 1, :] = xr[last]
        fini_ref[b:b + 1, :] = xi[last]

    half = S5_LP // 2
    x0r, x0i = x0r_ref[...], x0i_ref[...]
    hr, hi = ap[2:3, :], ap[3:4, :]
    xs = sum(_dot(us_ref[g], bs_ref[g, half:, :]) for g in pair)
    finsr_ref[...] = x0r * hr - x0i * hi + xs[:, :S5_RI]
    finsi_ref[...] = x0r * hi + x0i * hr + xs[:, S5_RI:]
    x0 = jnp.concatenate([x0r, x0i], axis=1).astype(BF16)
    for g in pair:
        ys_ref[g] = (_dot(us_ref[g], tp_ref[g, :half, :half]) + _dot_nt(x0, cs_ref[g, :half, :])).astype(ys_ref.dtype)


def _s5_scan(u_g, us_g, x0r, x0i, bs, cs, tp, ap, bsz):
    g, n_rows, _ = u_g.shape
    n_s = us_g.shape[1]
    half = S5_LP // 2
    pspec = lambda *s: pl.BlockSpec((2,) + s, lambda i: (i,) + (0,) * len(s))
    ospec = lambda *s: pl.BlockSpec((None,) + s, lambda i: (i,) + (0,) * len(s))
    packed = lambda n: jax.ShapeDtypeStruct((g // 2, n, S5_RI), F32)
    return pl.pallas_call(
        functools.partial(_s5_scan_kernel, bsz=bsz, chunks=n_rows // bsz),
        grid=(g // 2,),
        in_specs=[pspec(n_rows, S5_LP), pspec(n_s, half), ospec(n_s, S5_RI), ospec(n_s, S5_RI)]
                 + [pspec(S5_LP, 2 * S5_RI)] * 2 + [pspec(S5_LP, S5_LP), pspec(8, S5_RI)],
        out_specs=[pspec(n_rows, S5_LP), pspec(n_s, half), ospec(8, S5_RI), ospec(8, S5_RI),
                   ospec(n_s, S5_RI), ospec(n_s, S5_RI)],
        out_shape=[jax.ShapeDtypeStruct((g, n_rows, S5_LP), BF16), jax.ShapeDtypeStruct((g, n_s, half), BF16),
                   packed(8), packed(8), packed(n_s), packed(n_s)],
        compiler_params=_params(("arbitrary",)),
        name="s5_scan",
    )(u_g, us_g, x0r, x0i, bs, cs, tp, ap)


def kernel(x_prompt, x_sample, state_ret, state_swa_k, state_swa_v, state_ssm_re, state_ssm_im, norm_mix, norm_ffn, norm_final, w_in_ab, ret_gn, w_out_ab, ssm_lam_re, ssm_lam_im, ssm_log_step, ssm_b_re, ssm_b_im, ssm_c_re, ssm_c_im, ssm_d, w_glu, w_ffn_in, w_ffn_out):
    bsz, seq, d = x_prompt.shape
    dbsz, n_new, _ = x_sample.shape
    assert state_swa_k.shape[2] == SWA_BUF and n_new == S5_CHUNK // 2 and seq % SWA_ROWS == 0
    xp = x_prompt.reshape(bsz * seq, d)
    xs = x_sample.reshape(dbsz * n_new, d)
    tm_p, tm_s = 512, dbsz * n_new
    row = lambda v: v.reshape(1, -1)

    w_in0 = w_in_ab[0].astype(BF16)
    w_out0 = w_out_ab[0].astype(BF16)
    w_f_in0, w_f_out0 = w_ffn_in[0].astype(BF16), w_ffn_out[0].astype(BF16)
    g_mix0, g_ffn0, gn0 = row(norm_mix[0]), row(norm_ffn[0]), row(ret_gn[0])

    ra_p, qkv_p = _ab_proj(xp, g_mix0, w_in0, tm_p)
    ra_s, qkv_s = _ab_proj(xs, g_mix0, w_in0, tm_s)

    zero_ret = jnp.zeros((bsz, H_A, DK_A, DV_A), F32)
    a_p, ret_p = _retention(ra_p, zero_ret, gn0, seq=seq, c_real=RET_CHUNK, n_chunks=2, n_seq=bsz, out_dtype=BF16)
    a_s, ret_s = _retention(ra_s, state_ret[0], gn0, seq=n_new, c_real=n_new, n_chunks=1, n_seq=4, out_dtype=F32)

    ob_p = _swa_prompt(qkv_p, bsz, seq)
    width_b = H_B * DH_B
    rows_last = lambda w: w.transpose(0, 2, 3, 1).reshape(dbsz, width_b, SWA_BUF)
    rows_first = lambda w: w.reshape(dbsz, H_B, DH_B, SWA_BUF).transpose(0, 3, 1, 2)[None]
    ob_s, swk_s, swv_s = _swa_sample(qkv_s, rows_last(state_swa_k[0]), rows_last(state_swa_v[0]), n_new)
    kv_tail = qkv_p.reshape(bsz, seq, QKV_W)[:, seq - SWA_BUF:, width_b:]
    swk_p = kv_tail[..., :width_b].reshape(bsz, SWA_BUF, H_B, DH_B)
    swv_p = kv_tail[..., width_b:].reshape(bsz, SWA_BUF, H_B, DH_B)

    g_mix1, g_ffn1 = row(norm_mix[1]), row(norm_ffn[1])
    perm = _slab_perm()
    yp, u_p = _ab_tail(xp, a_p, ob_p, w_out0, g_ffn0, w_f_in0, w_f_out0, g_mix1, perm, tm_p, S5_CHUNK)
    ys, u_s = _ab_tail(xs, a_s, ob_s, w_out0, g_ffn0, w_f_in0, w_f_out0, g_mix1, perm, tm_s, n_new)

    operators = _s5_prep(ssm_lam_re[0], ssm_lam_im[0], ssm_log_step[0],
                         ssm_b_re[0], ssm_b_im[0], ssm_c_re[0], ssm_c_im[0])
    pack = lambda st: st.reshape(st.shape[0], -1, S5_RI).transpose(1, 0, 2)
    unpack = lambda st: st.transpose(1, 0, 2).reshape(st.shape[1], -1, S5_STATE)[None]
    y5_p, y5_s, finr_p, fini_p, finr_s, fini_s = _s5_scan(
        u_p, u_s, pack(state_ssm_re[0]), pack(state_ssm_im[0]), *operators, bsz)

    tail1 = (perm.T, g_mix1, row(ssm_d[0]), w_glu[0].astype(BF16), g_ffn1,
             w_ffn_in[1].astype(BF16), w_ffn_out[1].astype(BF16), row(norm_final))
    yp = _s5_tail(yp, y5_p, *tail1, tm_p, S5_CHUNK)
    ys = _s5_tail(ys, y5_s, *tail1, tm_s, n_new)

    return (yp.reshape(bsz, seq, d), ys.reshape(dbsz, n_new, d),
            ret_p[None], ret_s[None],
            swk_p[None], swv_p[None],
            rows_first(swk_s), rows_first(swv_s),
            unpack(finr_p[:, :bsz]), unpack(fini_p[:, :bsz]), unpack(finr_s), unpack(fini_s))
```

```python
import functools
import math

import jax
import jax.numpy as jnp
from jax import lax
from jax.experimental import pallas as pl
from jax.experimental.pallas import tpu as pltpu

F32 = jnp.float32
BF16 = jnp.bfloat16

H_A, DK_A, DV_A = 4, 64, 128
H_B, DH_B = 8, 64
SWA_PAIRS = ((128, 1), (512, 4), (2048, 16))
SPAN = 128
SWA_BUF = 2048
PAST_LEN = 16384
RET_CHUNK = 128
S5_GROUP, S5_STATE = 16, 64
S5_CHUNK = 16
EPS = 1e-6
NEG_INF = -1e30
QA_W, KA_W, VA_W, GA_W = H_A * DK_A, H_A * DK_A, H_A * DV_A, H_A * DV_A
RA_W = QA_W + KA_W + VA_W + GA_W
QKV_W = 3 * H_B * DH_B

LANES = 128
VMEM_LIMIT = 56 * 1024 * 1024


def _params(sem):
    return pltpu.CompilerParams(dimension_semantics=sem, vmem_limit_bytes=VMEM_LIMIT)


def _const_spec(shape):
    nd = len(shape)
    return pl.BlockSpec(shape, lambda *_: (0,) * nd, pipeline_mode=pl.Buffered(1))


def _rms(x, g):
    return x * lax.rsqrt(jnp.mean(x * x, axis=-1, keepdims=True) + EPS) * g


def _dot(a, b):
    return jnp.dot(a, b, preferred_element_type=F32)


def _dot_nt(a, b):
    return lax.dot_general(a, b, (((1,), (1,)), ((), ())), preferred_element_type=F32)


def _dot_tn(a, b):
    return lax.dot_general(a, b, (((0,), (0,)), ((), ())), preferred_element_type=F32)


def _ab_proj_kernel(x_ref, g_ref, w_ref, ra_ref, qkv_ref):
    h = _rms(x_ref[...], g_ref[...]).astype(BF16)
    step = 512
    for n0 in range(0, RA_W, step):
        ra_ref[:, n0:n0 + step] = _dot(h, w_ref[:, n0:n0 + step])
    for n0 in range(0, QKV_W, step):
        qkv_ref[:, n0:n0 + step] = _dot(h, w_ref[:, RA_W + n0:RA_W + n0 + step])


def _ab_proj(x, g, w_bf16, tm):
    t, d = x.shape
    return pl.pallas_call(
        _ab_proj_kernel,
        grid=(t // tm,),
        in_specs=[pl.BlockSpec((tm, d), lambda i: (i, 0)),
                  _const_spec((1, d)),
                  _const_spec((d, RA_W + QKV_W))],
        out_specs=[pl.BlockSpec((tm, RA_W), lambda i: (i, 0)),
                   pl.BlockSpec((tm, QKV_W), lambda i: (i, 0))],
        out_shape=[jax.ShapeDtypeStruct((t, RA_W), F32), jax.ShapeDtypeStruct((t, QKV_W), F32)],
        compiler_params=_params(("arbitrary",)),
        name="ab_proj",
    )(x, g, w_bf16)


def _retention_tables(c_real):
    c = RET_CHUNK
    log_g = jnp.log1p(-jnp.exp2(-5.0 - jnp.arange(H_A, dtype=F32)))
    idx = jnp.arange(c, dtype=F32)
    rel = idx[:, None] - idx[None, :]
    dec = jnp.where(rel >= 0, jnp.exp(jnp.maximum(rel, 0.0)[None] * log_g[:, None, None]), 0.0)
    dec = dec.reshape(H_A * c, c)
    real = (idx < c_real)[:, None]
    qd = jnp.exp((idx + 1.0)[:, None] * log_g[None, :])
    qd = jnp.repeat(qd, DV_A, axis=1)
    kd = jnp.where(real, jnp.exp((c_real - 1.0 - idx)[:, None] * log_g[None, :]), 0.0)
    kd = jnp.repeat(kd, DK_A, axis=1)
    row_h = jnp.arange(H_A * DK_A)[:, None] // DK_A
    col_h = jnp.arange(H_A * DV_A)[None, :] // DV_A
    bd = (row_h == col_h).astype(F32)
    dm = bd * jnp.repeat(jnp.exp(c_real * log_g), DV_A)[None, :]
    return dec, qd, kd, dm, bd


def _retention_kernel(ra_ref, s0_ref, gain_ref, dec_ref, qd_ref, kd_ref, dm_ref, bd_ref,
                      a_ref, sout_ref, sbd_ref, *, c_real, n_chunks, n_seq):
    c = RET_CHUNK
    j = pl.program_id(1)

    @pl.when(j == 0)
    def _():
        sbd_ref[...] = jnp.zeros_like(sbd_ref)
        for s in range(n_seq):
            for h in range(H_A):
                sbd_ref[s, h * DK_A:(h + 1) * DK_A, h * DV_A:(h + 1) * DV_A] = s0_ref[s, h]

    lane_q = lax.broadcasted_iota(jnp.int32, (c, QA_W), 1) // DK_A
    for ci in range(n_chunks):
        for s in range(n_seq):
            _retention_chunk(ra_ref.at[s], a_ref.at[s], sbd_ref.at[s], gain_ref, dec_ref, qd_ref, kd_ref, dm_ref,
                             bd_ref, lane_q, ci, c_real)

    @pl.when(j == pl.num_programs(1) - 1)
    def _():
        for s in range(n_seq):
            for h in range(H_A):
                sout_ref[s, h] = sbd_ref[s, h * DK_A:(h + 1) * DK_A, h * DV_A:(h + 1) * DV_A]


def _retention_chunk(ra_ref, a_ref, sbd_ref, gain_ref, dec_ref, qd_ref, kd_ref, dm_ref, bd_ref, lane_q, ci, c_real):
    c = RET_CHUNK
    rows = ra_ref[ci * c_real:(ci + 1) * c_real, :]
    gate = rows[:, QA_W + KA_W + VA_W:]
    if c_real < c:
        rows = jnp.concatenate([rows, jnp.zeros((c - c_real, RA_W), F32)], axis=0)
    q = rows[:, :QA_W]
    k = rows[:, QA_W:QA_W + KA_W] * (DK_A ** -0.5)
    v = rows[:, QA_W + KA_W:QA_W + KA_W + VA_W]
    vb = v.astype(BF16)
    qm = jnp.concatenate([jnp.where(lane_q == h, q, 0.0) for h in range(H_A)], axis=0).astype(BF16)
    s = _dot_nt(qm, k.astype(BF16)) * dec_ref[...]
    sb = s.astype(BF16)
    sbd = sbd_ref[...]
    cross = _dot(q.astype(BF16), sbd.astype(BF16)) * qd_ref[...]
    upd = _dot_tn((k * kd_ref[...]).astype(BF16), vb)
    sbd_ref[...] = sbd * dm_ref[...] + upd * bd_ref[...]
    for h in range(H_A):
        sl = slice(h * DV_A, (h + 1) * DV_A)
        o = _dot(sb[h * c:(h + 1) * c], vb[:, sl]) + cross[:, sl]
        o = o[:c_real]
        mu = jnp.mean(o, axis=-1, keepdims=True)
        var = jnp.mean(jnp.square(o - mu), axis=-1, keepdims=True)
        y = (o - mu) * lax.rsqrt(var + EPS) * gain_ref[:, sl]
        gh = gate[:, sl]
        a_ref[ci * c_real:(ci + 1) * c_real, sl] = (gh * jax.nn.sigmoid(gh) * y).astype(a_ref.dtype)


def _retention(ra, state0, gain, *, seq, c_real, n_chunks, n_seq, out_dtype):
    t = ra.shape[0]
    bsz = t // seq
    rows = c_real * n_chunks
    steps = seq // rows
    tables = _retention_tables(c_real)
    kern = functools.partial(_retention_kernel, c_real=c_real, n_chunks=n_chunks, n_seq=n_seq)
    a, s_out = pl.pallas_call(
        kern,
        grid=(bsz // n_seq, steps),
        in_specs=[pl.BlockSpec((n_seq, rows, RA_W), lambda b, j: (b, j, 0)),
                  pl.BlockSpec((n_seq, H_A, DK_A, DV_A), lambda b, j: (b, 0, 0, 0)),
                  _const_spec((1, VA_W))] + [_const_spec(tb.shape) for tb in tables],
        out_specs=[pl.BlockSpec((n_seq, rows, VA_W), lambda b, j: (b, j, 0)),
                   pl.BlockSpec((n_seq, H_A, DK_A, DV_A), lambda b, j: (b, 0, 0, 0))],
        out_shape=[jax.ShapeDtypeStruct((bsz, seq, VA_W), out_dtype),
                   jax.ShapeDtypeStruct((bsz, H_A, DK_A, DV_A), F32)],
        scratch_shapes=[pltpu.VMEM((n_seq, H_A * DK_A, H_A * DV_A), F32)],
        compiler_params=_params(("arbitrary", "arbitrary")),
        name="retention",
    )(ra.reshape(bsz, seq, RA_W), state0, gain, *tables)
    return a.reshape(t, VA_W), s_out


SWA_ROWS = 2048
N_BRANCH = len(SWA_PAIRS)


def _alibi_slopes():
    return jnp.exp2(-8.0 * jnp.arange(1, H_B + 1, dtype=F32) / H_B)


def _swa_prompt_bias():
    qi = jnp.arange(SPAN)[:, None]
    kj = jnp.arange(2 * SPAN)[None, :]
    dist = SPAN + qi - kj
    band = (dist >= 0) & (dist <= SPAN)
    slopes = _alibi_slopes()
    out = []
    for (_, dil) in SWA_PAIRS:
        pen = -slopes[:, None, None] * (dil * dist).astype(F32)[None]
        normal = jnp.where(band[None], pen, NEG_INF)
        first = jnp.where((band & (kj >= SPAN))[None], pen, NEG_INF)
        out.append(jnp.stack([normal, first], axis=1))
    tab = jnp.stack(out, axis=1)
    tab = tab.reshape(H_B // 2, 2, N_BRANCH, 2, SPAN, 2 * SPAN).transpose(0, 2, 3, 1, 4, 5)
    return tab.reshape(H_B // 2, N_BRANCH, 2, 2 * SPAN, 2 * SPAN)


COARSE = 4


def _swa_unit(q, kk, vv, bias, head0):
    q = q * (DH_B ** -0.5)
    qm = jnp.concatenate([jnp.where(head0, q, 0.0), jnp.where(head0, 0.0, q)], axis=0).astype(BF16)
    s = _dot_nt(qm, kk.astype(BF16)) + bias
    m = jnp.max(s, axis=-1, keepdims=True)
    p = jnp.exp(s - m).astype(BF16)
    ones = jnp.ones((2 * SPAN, LANES), BF16)
    res = _dot(p, jnp.concatenate([vv.astype(BF16), ones], axis=1))
    acc = jnp.where(head0, res[:SPAN, :LANES], res[SPAN:, :LANES])
    den = jnp.where(head0, res[:SPAN, LANES:], res[SPAN:, LANES:])
    mm = jnp.where(head0, jnp.broadcast_to(m[:SPAN], (SPAN, LANES)), jnp.broadcast_to(m[SPAN:], (SPAN, LANES)))
    return acc, den, mm


SWA_PREV = SPAN * COARSE


def _swa_prompt_half(half, q_ref, k_ref, v_ref, bias_ref, o_ref, kbuf, vbuf, acc_ref, l_ref, m_ref,
                     q4, k4, v4, acc4, l4, m4):
    j = pl.program_id(2)
    rows = SWA_ROWS
    sub = rows // COARSE
    slot = j % 2

    @pl.when((half == 0) & (j > 0))
    def _():
        kbuf[0:SWA_PREV, :] = kbuf[rows:rows + SWA_PREV, :]
        vbuf[0:SWA_PREV, :] = vbuf[rows:rows + SWA_PREV, :]

    @pl.when((half == 0) & (j == 0))
    def _():
        k4[1] = jnp.zeros(k4.shape[1:], F32)
        v4[1] = jnp.zeros(v4.shape[1:], F32)
        kbuf[0:SWA_PREV, :] = jnp.zeros((SWA_PREV, LANES), F32)
        vbuf[0:SWA_PREV, :] = jnp.zeros((SWA_PREV, LANES), F32)

    @pl.when(half == 0)
    def _():
        kbuf[SWA_PREV:, :] = k_ref[...]
        vbuf[SWA_PREV:, :] = v_ref[...]
        for c in range(COARSE):
            q4[c] = q_ref[pl.ds(c, sub, stride=COARSE), :]
            k4[slot, c] = k_ref[pl.ds(c, sub, stride=COARSE), :]
            v4[slot, c] = v_ref[pl.ds(c, sub, stride=COARSE), :]

    lane = lax.broadcasted_iota(jnp.int32, (SPAN, LANES), 1)
    head0 = lane < DH_B
    first_step = (j == 0).astype(jnp.int32)

    for g, (_, dil) in enumerate(SWA_PAIRS):
        blocks = rows // (SPAN * dil)
        per_half = blocks * dil // 2

        def unit(u, g=g, dil=dil):
            wb = u // dil
            r = u % dil
            q_start = wb * (SPAN * dil) + r
            k_start = SWA_PREV + (wb - 1) * (SPAN * dil) + r
            variant = jnp.where(wb == 0, first_step, 0)
            acc, den, mm = _swa_unit(q_ref[pl.ds(q_start, SPAN, stride=dil), :],
                                     kbuf[pl.ds(k_start, 2 * SPAN, stride=dil), :],
                                     vbuf[pl.ds(k_start, 2 * SPAN, stride=dil), :],
                                     bias_ref[0, g, variant], head0)
            acc_ref[g, pl.ds(q_start, SPAN, stride=dil), :] = acc
            l_ref[g, pl.ds(q_start, SPAN, stride=dil), :] = den
            m_ref[g, pl.ds(q_start, SPAN, stride=dil), :] = mm

        def unit_two_level(u, g=g, fine=dil // COARSE):
            c = u % COARSE
            f = u // COARSE
            pick = pl.ds(f, SPAN, stride=fine)
            acc, den, mm = _swa_unit(q4[c, pick, :],
                                     jnp.concatenate([k4[1 - slot, c, pick, :], k4[slot, c, pick, :]], axis=0),
                                     jnp.concatenate([v4[1 - slot, c, pick, :], v4[slot, c, pick, :]], axis=0),
                                     bias_ref[0, g, first_step], head0)
            dst = pl.ds(c * sub + f, SPAN, stride=fine)
            acc4[dst, :] = acc
            l4[dst, :] = den
            m4[dst, :] = mm

        two_level = dil % (COARSE * COARSE) == 0
        assert not two_level or blocks == 1
        body = unit_two_level if two_level else unit

        @pl.when(half >= 0)
        def _(body=body, per_half=per_half):
            for i in range(per_half):
                body(half * per_half + i)

        if two_level:
            @pl.when(half == 1)
            def _(g=g):
                for c in range(COARSE):
                    src, dst = slice(c * sub, (c + 1) * sub), pl.ds(c, sub, stride=COARSE)
                    acc_ref[g, dst, :] = acc4[src, :]
                    l_ref[g, dst, :] = l4[src, :]
                    m_ref[g, dst, :] = m4[src, :]

    @pl.when(half == 1)
    def _():
        tile = 256
        for r0 in range(0, rows, tile):
            sl = slice(r0, r0 + tile)
            ms = [m_ref[g, sl, :] for g in range(N_BRANCH)]
            mx = functools.reduce(jnp.maximum, ms)
            ws = [jnp.exp(mg - mx) for mg in ms]
            num = sum(w * acc_ref[g, sl, :] for g, w in enumerate(ws))
            den = sum(w * l_ref[g, sl, :] for g, w in enumerate(ws))
            o_ref[sl, :] = (num / den).astype(o_ref.dtype)


def _swa_sample_bias(n_new):
    t = jnp.arange(n_new)[:, None]
    slopes = _alibi_slopes()

    def table(j):
        dist = SWA_BUF + t - j
        out = []
        for (window, dil) in SWA_PAIRS:
            valid = (dist >= 0) & (dist <= window) & (dist % dil == 0) & (PAST_LEN + t - dist >= 0)
            pen = -slopes[:, None, None] * dist.astype(F32)[None]
            out.append(jnp.where(valid[None], pen, NEG_INF).reshape(H_B * n_new, -1))
        return jnp.stack(out)

    bias_a = table(jnp.arange(SWA_BUF)[None, :])
    jb = jnp.arange(LANES)[None, :]
    bias_b = jnp.where(jb < n_new, table(SWA_BUF + jb), NEG_INF)
    return bias_a, bias_b


def _swa_sample_heads(q_ref, k_ref, v_ref, kp_ref, vp_ref, ba_ref, bb_ref, o_ref, ko_ref, vo_ref, n_new):
    width = q_ref.shape[1]
    n_heads = width // DH_B
    lane_h = lax.broadcasted_iota(jnp.int32, (n_new, width), 1) // DH_B
    q = q_ref[...] * (DH_B ** -0.5)
    qm = jnp.concatenate([jnp.where(lane_h == h, q, 0.0) for h in range(n_heads)], axis=0).astype(BF16)
    pad = jnp.zeros((LANES - n_new, width), F32)
    k_new = jnp.concatenate([k_ref[...], pad], axis=0)
    v_new = jnp.concatenate([v_ref[...], pad], axis=0)
    s_a = _dot(qm, kp_ref[0].astype(BF16))
    s_b = _dot_nt(qm, k_new.astype(BF16))
    sa = [s_a + ba_ref[g] for g in range(N_BRANCH)]
    sb = [s_b + bb_ref[g] for g in range(N_BRANCH)]
    mx = functools.reduce(jnp.maximum, [jnp.max(x, axis=-1, keepdims=True) for x in sa + sb])
    p_a = sum(jnp.exp(x - mx) for x in sa).astype(BF16)
    p_b = sum(jnp.exp(x - mx) for x in sb).astype(BF16)
    den = (jnp.sum(p_a.astype(F32), axis=-1, keepdims=True)
           + jnp.sum(p_b.astype(F32), axis=-1, keepdims=True))
    o = (_dot_nt(p_a, vp_ref[0].astype(BF16)) + _dot(p_b, v_new.astype(BF16))) / den
    o_ref[...] = sum(jnp.where(lane_h == h, o[h * n_new:(h + 1) * n_new], 0.0) for h in range(n_heads))

    lane = lax.broadcasted_iota(jnp.int32, (DH_B, LANES), 1)
    for src_ref, new, dst_ref in ((kp_ref, k_new, ko_ref), (vp_ref, v_new, vo_ref)):
        new_t = pltpu.roll(new.T, LANES - n_new, axis=1)
        for h in range(n_heads):
            rows = slice(h * DH_B, (h + 1) * DH_B)
            shifted = pltpu.roll(src_ref[0, rows, :], SWA_BUF - n_new, axis=1)
            dst_ref[0, rows, 0:SWA_BUF - LANES] = shifted[:, 0:SWA_BUF - LANES]
            dst_ref[0, rows, SWA_BUF - LANES:] = jnp.where(lane >= LANES - n_new, new_t[rows],
                                                           shifted[:, SWA_BUF - LANES:])


def _swa_kernel(q_ref, k_ref, v_ref, bias_ref, qs_ref, ks_ref, vs_ref, kp_ref, vp_ref, ba_ref, bb_ref,
                o_ref, os_ref, ko_ref, vo_ref, *scratch, n_new):
    _swa_prompt_half(pl.program_id(3), q_ref, k_ref, v_ref, bias_ref, o_ref, *scratch)
    _swa_sample_heads(qs_ref, ks_ref, vs_ref, kp_ref, vp_ref, ba_ref, bb_ref, os_ref, ko_ref, vo_ref, n_new)


def _swa(qkv, bsz, seq, qkv_s, k_past, v_past, n_new):
    t = qkv.shape[0]
    dbsz = k_past.shape[0]
    steps = seq // SWA_ROWS
    npair = H_B // 2
    width = H_B * DH_B
    half_w = width // 2
    assert dbsz == bsz * npair * steps, "one sample sequence per prompt (sequence, head pair, row block) step"
    bias = _swa_prompt_bias()
    bias_a, bias_b = _swa_sample_bias(n_new)
    blk = (SWA_ROWS, LANES)
    rows_of = lambda b, j: b * steps + j
    seq_of = lambda b, hp, j: (b * npair + hp) * steps + j
    new_blk = lambda col: pl.BlockSpec((n_new, half_w), lambda b, hp, j, h, col=col: (seq_of(b, hp, j), 2 * col + h))
    state_blk = pl.BlockSpec((1, half_w, SWA_BUF), lambda b, hp, j, h: (seq_of(b, hp, j), h, 0))
    half_rows = H_B // 2 * n_new
    sub = SWA_ROWS // COARSE
    return pl.pallas_call(
        functools.partial(_swa_kernel, n_new=n_new),
        grid=(bsz, npair, steps, 2),
        in_specs=[pl.BlockSpec(blk, lambda b, hp, j, h: (rows_of(b, j), hp)),
                  pl.BlockSpec(blk, lambda b, hp, j, h: (rows_of(b, j), npair + hp)),
                  pl.BlockSpec(blk, lambda b, hp, j, h: (rows_of(b, j), 2 * npair + hp)),
                  pl.BlockSpec((1, N_BRANCH, 2, 2 * SPAN, 2 * SPAN), lambda b, hp, j, h: (hp, 0, 0, 0, 0)),
                  new_blk(0), new_blk(1), new_blk(2), state_blk, state_blk,
                  pl.BlockSpec((N_BRANCH, half_rows, SWA_BUF), lambda b, hp, j, h: (0, h, 0)),
                  pl.BlockSpec((N_BRANCH, half_rows, LANES), lambda b, hp, j, h: (0, h, 0))],
        out_specs=[pl.BlockSpec(blk, lambda b, hp, j, h: (rows_of(b, j), hp)),
                   pl.BlockSpec((n_new, half_w), lambda b, hp, j, h: (seq_of(b, hp, j), h)),
                   state_blk, state_blk],
        out_shape=[jax.ShapeDtypeStruct((t, width), BF16),
                   jax.ShapeDtypeStruct((dbsz * n_new, width), F32),
                   jax.ShapeDtypeStruct(k_past.shape, F32), jax.ShapeDtypeStruct(v_past.shape, F32)],
        scratch_shapes=[pltpu.VMEM((SWA_PREV + SWA_ROWS, LANES), F32)] * 2
                       + [pltpu.VMEM((N_BRANCH, SWA_ROWS, LANES), F32)] * 3
                       + [pltpu.VMEM((COARSE, sub, LANES), F32)]
                       + [pltpu.VMEM((2, COARSE, sub, LANES), F32)] * 2
                       + [pltpu.VMEM((SWA_ROWS, LANES), F32)] * 3,
        compiler_params=_params(("arbitrary",) * 4),
        name="swa",
    )(qkv, qkv, qkv, bias, qkv_s, qkv_s, qkv_s, k_past, v_past, bias_a, bias_b)


FFN_CHUNK = 256


def _ffn(y1, g_ffn_ref, w_in_ref, w_o_ref, acc_ref):
    d_ff = w_o_ref.shape[0]
    h = _rms(y1, g_ffn_ref[...]).astype(BF16)
    for ci, c0 in enumerate(range(0, d_ff, FFN_CHUNK)):
        gate = _dot(h, w_in_ref[:, c0:c0 + FFN_CHUNK])
        up = _dot(h, w_in_ref[:, d_ff + c0:d_ff + c0 + FFN_CHUNK])
        act = (gate * jax.nn.sigmoid(gate) * up).astype(BF16)
        part = _dot(act, w_o_ref[c0:c0 + FFN_CHUNK, :])
        if ci == 0:
            acc_ref[...] = part
        else:
            acc_ref[...] += part
    return y1 + acc_ref[...]


def _ab_tail_kernel(x_ref, a_ref, ob_ref, w_out_ref, g_ffn_ref, w_in_ref, w_o_ref, g_next_ref, perm_ref,
                    o_ref, u_ref, acc_ref, h_ref, z_ref, *, chunk):
    mix = (_dot(a_ref[...].astype(BF16), w_out_ref[0:VA_W, :])
           + _dot(ob_ref[...].astype(BF16), w_out_ref[VA_W:, :]))
    y = _ffn(x_ref[...] + mix, g_ffn_ref, w_in_ref, w_o_ref, acc_ref)
    o_ref[...] = y
    _group_rows(_rms(y, g_next_ref[...]), perm_ref, u_ref, h_ref, z_ref, chunk)


def _s5_tail_kernel(x_ref, yg_ref, perm_ref, g_mix_ref, dskip_ref, w_glu_ref, g_ffn_ref, w_in_ref, w_o_ref,
                    g_fin_ref, o_ref, acc_ref, z_ref, t_ref, *, chunk):
    x = x_ref[...]
    d = x.shape[-1]
    u = _rms(x, g_mix_ref[...])
    _ungroup_rows(yg_ref, perm_ref, z_ref, t_ref, chunk)
    ys = jnp.concatenate([t_ref[v] for v in range(d // LANES)], axis=1)
    z = jax.nn.gelu(ys + dskip_ref[...] * u, approximate=True).astype(BF16)
    val = _dot(z, w_glu_ref[:, 0:d])
    gate = _dot(z, w_glu_ref[:, d:2 * d])
    y2 = _ffn(x + val * jax.nn.sigmoid(gate), g_ffn_ref, w_in_ref, w_o_ref, acc_ref)
    o_ref[...] = _rms(y2, g_fin_ref[...])


def _row_spec(tm, width):
    return pl.BlockSpec((tm, width), lambda i: (i, 0))


def _group_scratch(tm, d, chunk):
    n_slab, nc, n_col = d // LANES, tm // chunk, chunk // SLAB_GROUPS
    assert min(S5_MM_ROWS, n_slab * n_col * nc) % nc == 0
    return (pltpu.VMEM((n_slab, tm, LANES), F32), pltpu.VMEM((n_slab * n_col * nc, SLAB_GROUPS * LANES), BF16))


def _group_spec(tm, d, chunk):
    return pl.BlockSpec((d // S5_GROUP, tm // chunk, chunk * S5_GROUP), lambda i: (0, i, 0))


def _layer_spec(stacked, layer):
    nd = stacked.ndim - 1
    return pl.BlockSpec((None,) + stacked.shape[1:], lambda *_: (layer,) + (0,) * nd, pipeline_mode=pl.Buffered(1))


def _ab_tail(x, a, ob, w_out, g_ffn, w_in, w_o, layer, g_next, perm, tm, chunk):
    t, d = x.shape
    h_scr, z_scr = _group_scratch(tm, d, chunk)
    return pl.pallas_call(
        functools.partial(_ab_tail_kernel, chunk=chunk),
        grid=(t // tm,),
        in_specs=[_row_spec(tm, d), _row_spec(tm, a.shape[1]), _row_spec(tm, ob.shape[1]),
                  _const_spec(w_out.shape), _const_spec((1, d)), _layer_spec(w_in, layer), _layer_spec(w_o, layer),
                  _const_spec((1, d)), _const_spec(perm.shape)],
        out_specs=[_row_spec(tm, d), _group_spec(tm, d, chunk)],
        out_shape=[jax.ShapeDtypeStruct((t, d), F32),
                   jax.ShapeDtypeStruct((d // S5_GROUP, t // chunk, chunk * S5_GROUP), BF16)],
        scratch_shapes=[pltpu.VMEM((tm, d), F32), h_scr, z_scr],
        compiler_params=_params(("arbitrary",)),
        name="ab_tail",
    )(x, a, ob, w_out, g_ffn, w_in, w_o, g_next, perm)


def _s5_tail(x, yg, perm_t, g_mix, dskip, w_glu, g_ffn, w_in, w_o, layer, g_fin, tm, chunk):
    t, d = x.shape
    t_scr, z_scr = _group_scratch(tm, d, chunk)
    return pl.pallas_call(
        functools.partial(_s5_tail_kernel, chunk=chunk),
        grid=(t // tm,),
        in_specs=[_row_spec(tm, d), _group_spec(tm, d, chunk), _const_spec(perm_t.shape),
                  _const_spec((1, d)), _const_spec((1, d)),
                  _const_spec(w_glu.shape), _const_spec((1, d)), _layer_spec(w_in, layer), _layer_spec(w_o, layer),
                  _const_spec((1, d))],
        out_specs=_row_spec(tm, d),
        out_shape=jax.ShapeDtypeStruct((t, d), F32),
        scratch_shapes=[pltpu.VMEM((tm, d), F32), z_scr, t_scr],
        compiler_params=_params(("arbitrary",)),
        name="s5_tail",
    )(x, yg, perm_t, g_mix, dskip, w_glu, g_ffn, w_in, w_o, g_fin)


SLAB_GROUPS = LANES // S5_GROUP
S5_MM_ROWS = 256


def _slab_perm():
    idx = jnp.arange(SLAB_GROUPS * LANES)
    l8, g8, p = idx // LANES, (idx % LANES) // S5_GROUP, idx % S5_GROUP
    dst = g8 * LANES + l8 * S5_GROUP + p
    return (dst[:, None] == idx[None, :]).astype(BF16)


def _permute_rows(z_ref, perm_ref, emit):
    step = min(S5_MM_ROWS, z_ref.shape[0])
    for r0 in range(0, z_ref.shape[0], step):
        emit(r0, _dot(z_ref[r0:r0 + step, :], perm_ref[...]))


def _group_rows(h, perm_ref, o_ref, h_ref, z_ref, chunk):
    rows = h.shape[0]
    nc = rows // chunk
    n_slab = h.shape[1] // LANES
    n_col = chunk // SLAB_GROUPS
    for v in range(n_slab):
        h_ref[v] = h[:, v * LANES:(v + 1) * LANES]
    for v in range(n_slab):
        for l in range(chunk):
            j, l8 = l // SLAB_GROUPS, l % SLAB_GROUPS
            r0 = (v * n_col + j) * nc
            z_ref[r0:r0 + nc, l8 * LANES:(l8 + 1) * LANES] = h_ref[v, pl.ds(l, nc, stride=chunk), :].astype(BF16)

    def emit(r0, blk):
        for q in range(blk.shape[0] // nc):
            v, j = divmod(r0 // nc + q, n_col)
            for g8 in range(SLAB_GROUPS):
                o_ref[v * SLAB_GROUPS + g8, :, j * LANES:(j + 1) * LANES] = (
                    blk[q * nc:(q + 1) * nc, g8 * LANES:(g8 + 1) * LANES].astype(o_ref.dtype))

    _permute_rows(z_ref, perm_ref, emit)


def _ungroup_rows(y_ref, perm_ref, z_ref, t_ref, chunk):
    n_slab, rows, _ = t_ref.shape
    nc = rows // chunk
    n_col = chunk // SLAB_GROUPS
    for v in range(n_slab):
        for j in range(n_col):
            r0 = (v * n_col + j) * nc
            for g8 in range(SLAB_GROUPS):
                z_ref[r0:r0 + nc, g8 * LANES:(g8 + 1) * LANES] = y_ref[v * SLAB_GROUPS + g8, :, j * LANES:(j + 1) * LANES]

    def emit(r0, blk):
        for q in range(blk.shape[0] // nc):
            v, j = divmod(r0 // nc + q, n_col)
            for l8 in range(SLAB_GROUPS):
                t_ref[v, pl.ds(j * SLAB_GROUPS + l8, nc, stride=chunk), :] = (
                    blk[q * nc:(q + 1) * nc, l8 * LANES:(l8 + 1) * LANES])

    _permute_rows(z_ref, perm_ref, emit)


S5_LP = S5_CHUNK * S5_GROUP
S5_RI = 2 * S5_STATE


def _s5_prep_kernel(*refs):
    for parity in range(2):
        _s5_prep_group(parity, *[r.at[parity] for r in refs])


def _s5_prep_group(parity, lam_row_ref, ls_ref, btr_ref, bti_ref, cr_ref, ci_ref,
                   bs_ref, cs_ref, tp_ref, ap_ref):
    n_pow = S5_CHUNK + 1
    dt = jnp.exp(ls_ref[...])

    def powers(lr, li):
        mag = jnp.exp(lr * dt)
        a_re, a_im = mag * jnp.cos(li * dt), mag * jnp.sin(li * dt)
        pw = [(jnp.ones_like(a_re), jnp.zeros_like(a_im))]
        for _ in range(n_pow - 1):
            pr, pi = pw[-1]
            pw.append((pr * a_re - pi * a_im, pr * a_im + pi * a_re))
        return pw

    lr, li = lam_row_ref[0:1, :], lam_row_ref[1:2, :]
    pw_row = powers(lr, li)
    a_re, a_im = pw_row[1]
    den = lr * lr + li * li
    f_re = ((a_re - 1.0) * lr + a_im * li) / den
    f_im = (a_im * lr - (a_re - 1.0) * li) / den
    blk = (S5_GROUP, S5_RI)
    w_re = jnp.concatenate([jnp.broadcast_to(f_re * pw_row[S5_CHUNK - 1 - l][0] - f_im * pw_row[S5_CHUNK - 1 - l][1], blk)
                            for l in range(S5_CHUNK)], axis=0)
    w_im = jnp.concatenate([jnp.broadcast_to(f_re * pw_row[S5_CHUNK - 1 - l][1] + f_im * pw_row[S5_CHUNK - 1 - l][0], blk)
                            for l in range(S5_CHUNK)], axis=0)

    lane = lax.broadcasted_iota(jnp.int32, (1, S5_RI), 1)
    own = (lane // S5_STATE) == parity
    low = lane < S5_STATE
    keep = lambda z: jnp.where(own, z, 0.0)
    btr, bti = btr_ref[...], bti_ref[...]
    bs_ref[:, :S5_RI] = keep(w_re * btr - w_im * bti).astype(bs_ref.dtype)
    bs_ref[:, S5_RI:] = keep(w_re * bti + w_im * btr).astype(bs_ref.dtype)
    bf_t = jnp.where(low, f_re * btr[0:S5_GROUP] - f_im * bti[0:S5_GROUP],
                     f_re * bti[0:S5_GROUP] + f_im * btr[0:S5_GROUP])
    ap_ref[...] = jnp.zeros_like(ap_ref)
    for i, (j, part) in enumerate(((S5_CHUNK, 0), (S5_CHUNK, 1), (S5_CHUNK // 2, 0), (S5_CHUNK // 2, 1))):
        ap_ref[i:i + 1, :] = keep(pw_row[j][part])

    def spread(j0, part):
        return jnp.concatenate([jnp.broadcast_to(pw_row[j0 + l][part], blk) for l in range(S5_CHUNK)], axis=0)

    cr, ci = cr_ref[...], ci_ref[...]
    cs0 = jnp.where(low, cr * spread(0, 0) - ci * spread(0, 1),
                    -(cr * spread(0, 1) + ci * spread(0, 0)))
    cs_ref[:, :S5_RI] = keep(cr * spread(1, 0) - ci * spread(1, 1)).astype(cs_ref.dtype)
    cs_ref[:, S5_RI:] = keep(-(cr * spread(1, 1) + ci * spread(1, 0))).astype(cs_ref.dtype)
    r = lax.dot_general(bf_t, cs0, (((1,), (1,)), ((), ())), preferred_element_type=F32,
                        precision=lax.Precision.HIGHEST)
    lane_lp = lax.broadcasted_iota(jnp.int32, (S5_GROUP, S5_LP), 1)
    for l in range(S5_CHUNK):
        sh = l * S5_GROUP
        blk_l = r if l == 0 else jnp.where(lane_lp >= sh, pltpu.roll(r, sh, axis=1), 0.0)
        tp_ref[l * S5_GROUP:(l + 1) * S5_GROUP, :] = blk_l.astype(tp_ref.dtype)


def _s5_prep(lam_re, lam_im, log_step, b_re, b_im, c_re, c_im):
    g = lam_re.shape[0]
    dup = lambda z: jnp.concatenate([z, z], axis=-1)
    lam_row = jnp.stack([dup(lam_re), dup(lam_im)], axis=1)
    lam_row = jnp.pad(lam_row, ((0, 0), (0, 6), (0, 0)))
    ls = log_step.reshape(g, 1, 1)
    tiled = lambda z: jnp.tile(dup(z), (1, S5_CHUNK, 1))
    btr, bti = tiled(jnp.swapaxes(b_re, 1, 2)), tiled(jnp.swapaxes(b_im, 1, 2))
    cr, ci = tiled(c_re), tiled(c_im)
    gspec = lambda *s: pl.BlockSpec((2,) + s, lambda i: (i,) + (0,) * len(s))
    table = jax.ShapeDtypeStruct((g, S5_LP, 2 * S5_RI), BF16)
    return pl.pallas_call(
        _s5_prep_kernel,
        grid=(g // 2,),
        in_specs=[gspec(8, S5_RI), gspec(1, 1)] + [gspec(S5_LP, S5_RI)] * 4,
        out_specs=[gspec(S5_LP, 2 * S5_RI)] * 2 + [gspec(S5_LP, S5_LP), gspec(8, S5_RI)],
        out_shape=[table] * 2 + [jax.ShapeDtypeStruct((g, S5_LP, S5_LP), BF16),
                                 jax.ShapeDtypeStruct((g, 8, S5_RI), F32)],
        compiler_params=_params(("arbitrary",)),
        name="s5_prep",
    )(lam_row, ls, btr, bti, cr, ci)


def _s5_scan_kernel(u_ref, us_ref, x0r_ref, x0i_ref, bs_ref, cs_ref, tp_ref, ap_ref,
                    y_ref, ys_ref, finr_ref, fini_ref, finsr_ref, finsi_ref, *, bsz, chunks):
    n_rows = bsz * chunks
    pair = range(2)
    us_in = [u_ref[g] for g in pair]
    x = sum(_dot(us_in[g], bs_ref[g]) for g in pair)
    xr, xi = x[:, :S5_RI], x[:, S5_RI:]
    row = lax.broadcasted_iota(jnp.int32, (n_rows, 1), 0) % chunks
    ap = ap_ref[0] + ap_ref[1]
    ar, ai = ap[0:1, :], ap[1:2, :]
    shift = 1
    while shift < chunks:
        sr = jnp.where(row >= shift, pltpu.roll(xr, shift, axis=0), 0.0)
        si = jnp.where(row >= shift, pltpu.roll(xi, shift, axis=0), 0.0)
        xr, xi = xr + (sr * ar - si * ai), xi + (sr * ai + si * ar)
        ar, ai = ar * ar - ai * ai, 2.0 * (ar * ai)
        shift *= 2
    pr = jnp.where(row >= 1, pltpu.roll(xr, 1, axis=0), 0.0).astype(BF16)
    pi = jnp.where(row >= 1, pltpu.roll(xi, 1, axis=0), 0.0).astype(BF16)
    prev = jnp.concatenate([pr, pi], axis=1)
    for g in pair:
        y_ref[g] = (_dot(us_in[g], tp_ref[g]) + _dot_nt(prev, cs_ref[g])).astype(y_ref.dtype)
    finr_ref[...] = jnp.zeros_like(finr_ref)
    fini_ref[...] = jnp.zeros_like(fini_ref)
    for b in range(bsz):
        last = slice((b + 1) * chunks - 1, (b + 1) * chunks)
        finr_ref[b:b + 1, :] = xr[last]
        fini_ref[b:b + 1, :] = xi[last]

    half = S5_LP // 2
    x0r, x0i = x0r_ref[...], x0i_ref[...]
    hr, hi = ap[2:3, :], ap[3:4, :]
    xs = sum(_dot(us_ref[g], bs_ref[g, half:, :]) for g in pair)
    finsr_ref[...] = x0r * hr - x0i * hi + xs[:, :S5_RI]
    finsi_ref[...] = x0r * hi + x0i * hr + xs[:, S5_RI:]
    x0 = jnp.concatenate([x0r, x0i], axis=1).astype(BF16)
    for g in pair:
        ys_ref[g] = (_dot(us_ref[g], tp_ref[g, :half, :half]) + _dot_nt(x0, cs_ref[g, :half, :])).astype(ys_ref.dtype)


def _s5_scan(u_g, us_g, x0r, x0i, bs, cs, tp, ap, bsz):
    g, n_rows, _ = u_g.shape
    n_s = us_g.shape[1]
    half = S5_LP // 2
    pspec = lambda *s: pl.BlockSpec((2,) + s, lambda i: (i,) + (0,) * len(s))
    ospec = lambda *s: pl.BlockSpec((None,) + s, lambda i: (i,) + (0,) * len(s))
    packed = lambda n: jax.ShapeDtypeStruct((g // 2, n, S5_RI), F32)
    return pl.pallas_call(
        functools.partial(_s5_scan_kernel, bsz=bsz, chunks=n_rows // bsz),
        grid=(g // 2,),
        in_specs=[pspec(n_rows, S5_LP), pspec(n_s, half), ospec(n_s, S5_RI), ospec(n_s, S5_RI)]
                 + [pspec(S5_LP, 2 * S5_RI)] * 2 + [pspec(S5_LP, S5_LP), pspec(8, S5_RI)],
        out_specs=[pspec(n_rows, S5_LP), pspec(n_s, half), ospec(8, S5_RI), ospec(8, S5_RI),
                   ospec(n_s, S5_RI), ospec(n_s, S5_RI)],
        out_shape=[jax.ShapeDtypeStruct((g, n_rows, S5_LP), BF16), jax.ShapeDtypeStruct((g, n_s, half), BF16),
                   packed(8), packed(8), packed(n_s), packed(n_s)],
        compiler_params=_params(("arbitrary",)),
        name="s5_scan",
    )(u_g, us_g, x0r, x0i, bs, cs, tp, ap)


def kernel(x_prompt, x_sample, state_ret, state_swa_k, state_swa_v, state_ssm_re, state_ssm_im, norm_mix, norm_ffn, norm_final, w_in_ab, ret_gn, w_out_ab, ssm_lam_re, ssm_lam_im, ssm_log_step, ssm_b_re, ssm_b_im, ssm_c_re, ssm_c_im, ssm_d, w_glu, w_ffn_in, w_ffn_out):
    bsz, seq, d = x_prompt.shape
    dbsz, n_new, _ = x_sample.shape
    assert state_swa_k.shape[2] == SWA_BUF and n_new == S5_CHUNK // 2 and seq % SWA_ROWS == 0
    xp = x_prompt.reshape(bsz * seq, d)
    xs = x_sample.reshape(dbsz * n_new, d)
    tm_p, tm_s = 512, dbsz * n_new
    row = lambda v: v.reshape(1, -1)

    w_in0 = w_in_ab[0].astype(BF16)
    w_out0 = w_out_ab[0].astype(BF16)
    w_f_in, w_f_out = w_ffn_in.astype(BF16), w_ffn_out.astype(BF16)
    g_mix0, g_ffn0, gn0 = row(norm_mix[0]), row(norm_ffn[0]), row(ret_gn[0])

    ra_p, qkv_p = _ab_proj(xp, g_mix0, w_in0, 2 * tm_p)
    ra_s, qkv_s = _ab_proj(xs, g_mix0, w_in0, tm_s)

    zero_ret = jnp.zeros((bsz, H_A, DK_A, DV_A), F32)
    a_p, ret_p = _retention(ra_p, zero_ret, gn0, seq=seq, c_real=RET_CHUNK, n_chunks=2, n_seq=bsz, out_dtype=BF16)
    a_s, ret_s = _retention(ra_s, state_ret[0], gn0, seq=n_new, c_real=n_new, n_chunks=1, n_seq=4, out_dtype=F32)

    width_b = H_B * DH_B
    rows_last = lambda w: w.transpose(0, 2, 3, 1).reshape(dbsz, width_b, SWA_BUF)
    rows_first = lambda w: w.reshape(dbsz, H_B, DH_B, SWA_BUF).transpose(0, 3, 1, 2)[None]
    ob_p, ob_s, swk_s, swv_s = _swa(qkv_p, bsz, seq, qkv_s,
                                    rows_last(state_swa_k[0]), rows_last(state_swa_v[0]), n_new)
    kv_tail = qkv_p.reshape(bsz, seq, QKV_W)[:, seq - SWA_BUF:, width_b:]
    swk_p = kv_tail[..., :width_b].reshape(bsz, SWA_BUF, H_B, DH_B)
    swv_p = kv_tail[..., width_b:].reshape(bsz, SWA_BUF, H_B, DH_B)

    g_mix1, g_ffn1 = row(norm_mix[1]), row(norm_ffn[1])
    perm = _slab_perm()
    yp, u_p = _ab_tail(xp, a_p, ob_p, w_out0, g_ffn0, w_f_in, w_f_out, 0, g_mix1, perm, tm_p, S5_CHUNK)
    ys, u_s = _ab_tail(xs, a_s, ob_s, w_out0, g_ffn0, w_f_in, w_f_out, 0, g_mix1, perm, tm_s, n_new)

    operators = _s5_prep(ssm_lam_re[0], ssm_lam_im[0], ssm_log_step[0],
                         ssm_b_re[0], ssm_b_im[0], ssm_c_re[0], ssm_c_im[0])
    pack = lambda st: st.reshape(st.shape[0], -1, S5_RI).transpose(1, 0, 2)
    unpack = lambda st: st.transpose(1, 0, 2).reshape(st.shape[1], -1, S5_STATE)[None]
    y5_p, y5_s, finr_p, fini_p, finr_s, fini_s = _s5_scan(
        u_p, u_s, pack(state_ssm_re[0]), pack(state_ssm_im[0]), *operators, bsz)

    tail1 = (perm.T, g_mix1, row(ssm_d[0]), w_glu[0].astype(BF16), g_ffn1,
             w_f_in, w_f_out, 1, row(norm_final))
    yp = _s5_tail(yp, y5_p, *tail1, tm_p, S5_CHUNK)
    ys = _s5_tail(ys, y5_s, *tail1, tm_s, n_new)

    return (yp.reshape(bsz, seq, d), ys.reshape(dbsz, n_new, d),
            ret_p[None], ret_s[None],
            swk_p[None], swv_p[None],
            rows_first(swk_s), rows_first(swv_s),
            unpack(finr_p[:, :bsz]), unpack(fini_p[:, :bsz]), unpack(finr_s), unpack(fini_s))
```

```python
import functools
import math

import jax
import jax.numpy as jnp
from jax import lax
from jax.experimental import pallas as pl
from jax.experimental.pallas import tpu as pltpu

F32 = jnp.float32
BF16 = jnp.bfloat16

H_A, DK_A, DV_A = 4, 64, 128
H_B, DH_B = 8, 64
SWA_PAIRS = ((128, 1), (512, 4), (2048, 16))
SPAN = 128
SWA_BUF = 2048
PAST_LEN = 16384
RET_CHUNK = 128
S5_GROUP, S5_STATE = 16, 64
S5_CHUNK = 16
EPS = 1e-6
NEG_INF = -1e30
QA_W, KA_W, VA_W, GA_W = H_A * DK_A, H_A * DK_A, H_A * DV_A, H_A * DV_A
RA_W = QA_W + KA_W + VA_W + GA_W
QKV_W = 3 * H_B * DH_B

LANES = 128
SUBLANES = 8
VMEM_LIMIT = 56 * 1024 * 1024


def _params(sem):
    return pltpu.CompilerParams(dimension_semantics=sem, vmem_limit_bytes=VMEM_LIMIT)


def _const_spec(shape):
    nd = len(shape)
    return pl.BlockSpec(shape, lambda *_: (0,) * nd, pipeline_mode=pl.Buffered(1))


def _rms(x, g):
    return x * lax.rsqrt(jnp.mean(x * x, axis=-1, keepdims=True) + EPS) * g


def _dot(a, b):
    return jnp.dot(a, b, preferred_element_type=F32)


def _dot_nt(a, b):
    return lax.dot_general(a, b, (((1,), (1,)), ((), ())), preferred_element_type=F32)


def _dot_tn(a, b):
    return lax.dot_general(a, b, (((0,), (0,)), ((), ())), preferred_element_type=F32)


def _ab_proj_kernel(x_ref, g_ref, w_ref, ra_ref, qkv_ref):
    h = _rms(x_ref[...], g_ref[...]).astype(BF16)
    step = 512
    for n0 in range(0, RA_W, step):
        ra_ref[:, n0:n0 + step] = _dot(h, w_ref[:, n0:n0 + step])
    for n0 in range(0, QKV_W, step):
        qkv_ref[:, n0:n0 + step] = _dot(h, w_ref[:, RA_W + n0:RA_W + n0 + step])


def _ab_proj(x, g, w_bf16, tm):
    t, d = x.shape
    return pl.pallas_call(
        _ab_proj_kernel,
        grid=(t // tm,),
        in_specs=[pl.BlockSpec((tm, d), lambda i: (i, 0)),
                  _const_spec((1, d)),
                  _const_spec((d, RA_W + QKV_W))],
        out_specs=[pl.BlockSpec((tm, RA_W), lambda i: (i, 0)),
                   pl.BlockSpec((tm, QKV_W), lambda i: (i, 0))],
        out_shape=[jax.ShapeDtypeStruct((t, RA_W), F32), jax.ShapeDtypeStruct((t, QKV_W), F32)],
        compiler_params=_params(("arbitrary",)),
        name="ab_proj",
    )(x, g, w_bf16)


def _retention_tables(c_real):
    c = RET_CHUNK
    log_g = jnp.log1p(-jnp.exp2(-5.0 - jnp.arange(H_A, dtype=F32)))
    idx = jnp.arange(c, dtype=F32)
    rel = idx[:, None] - idx[None, :]
    dec = jnp.where(rel >= 0, jnp.exp(jnp.maximum(rel, 0.0)[None] * log_g[:, None, None]), 0.0)
    dec = dec.reshape(H_A * c, c)
    real = (idx < c_real)[:, None]
    qd = jnp.exp((idx + 1.0)[:, None] * log_g[None, :])
    qd = jnp.repeat(qd, DV_A, axis=1)
    kd = jnp.where(real, jnp.exp((c_real - 1.0 - idx)[:, None] * log_g[None, :]), 0.0)
    kd = jnp.repeat(kd, DK_A, axis=1)
    row_h = jnp.arange(H_A * DK_A)[:, None] // DK_A
    col_h = jnp.arange(H_A * DV_A)[None, :] // DV_A
    bd = (row_h == col_h).astype(F32)
    dm = bd * jnp.repeat(jnp.exp(c_real * log_g), DV_A)[None, :]
    return dec, qd, kd, dm, bd


def _retention_kernel(ra_ref, s0_ref, gain_ref, dec_ref, qd_ref, kd_ref, dm_ref, bd_ref,
                      a_ref, sout_ref, sbd_ref, *, c_real, n_chunks, n_seq):
    c = RET_CHUNK
    j = pl.program_id(1)

    @pl.when(j == 0)
    def _():
        sbd_ref[...] = jnp.zeros_like(sbd_ref)
        for s in range(n_seq):
            for h in range(H_A):
                sbd_ref[s, h * DK_A:(h + 1) * DK_A, h * DV_A:(h + 1) * DV_A] = s0_ref[s, h]

    lane_q = lax.broadcasted_iota(jnp.int32, (c, QA_W), 1) // DK_A
    for ci in range(n_chunks):
        for s in range(n_seq):
            _retention_chunk(ra_ref.at[s], a_ref.at[s], sbd_ref.at[s], gain_ref, dec_ref, qd_ref, kd_ref, dm_ref,
                             bd_ref, lane_q, ci, c_real)

    @pl.when(j == pl.num_programs(1) - 1)
    def _():
        for s in range(n_seq):
            for h in range(H_A):
                sout_ref[s, h] = sbd_ref[s, h * DK_A:(h + 1) * DK_A, h * DV_A:(h + 1) * DV_A]


def _retention_chunk(ra_ref, a_ref, sbd_ref, gain_ref, dec_ref, qd_ref, kd_ref, dm_ref, bd_ref, lane_q, ci, c_real):
    c = RET_CHUNK
    rows = ra_ref[ci * c_real:(ci + 1) * c_real, :]
    gate = rows[:, QA_W + KA_W + VA_W:]
    if c_real < c:
        rows = jnp.concatenate([rows, jnp.zeros((c - c_real, RA_W), F32)], axis=0)
    q = rows[:, :QA_W]
    k = rows[:, QA_W:QA_W + KA_W] * (DK_A ** -0.5)
    v = rows[:, QA_W + KA_W:QA_W + KA_W + VA_W]
    vb = v.astype(BF16)
    qm = jnp.concatenate([jnp.where(lane_q == h, q, 0.0) for h in range(H_A)], axis=0).astype(BF16)
    s = _dot_nt(qm, k.astype(BF16)) * dec_ref[...]
    sb = s.astype(BF16)
    sbd = sbd_ref[...]
    cross = _dot(q.astype(BF16), sbd.astype(BF16)) * qd_ref[...]
    upd = _dot_tn((k * kd_ref[...]).astype(BF16), vb)
    sbd_ref[...] = sbd * dm_ref[...] + upd * bd_ref[...]
    for h in range(H_A):
        sl = slice(h * DV_A, (h + 1) * DV_A)
        o = _dot(sb[h * c:(h + 1) * c], vb[:, sl]) + cross[:, sl]
        o = o[:c_real]
        mu = jnp.mean(o, axis=-1, keepdims=True)
        var = jnp.mean(jnp.square(o - mu), axis=-1, keepdims=True)
        y = (o - mu) * lax.rsqrt(var + EPS) * gain_ref[:, sl]
        gh = gate[:, sl]
        a_ref[ci * c_real:(ci + 1) * c_real, sl] = (gh * jax.nn.sigmoid(gh) * y).astype(a_ref.dtype)


def _retention(ra, state0, gain, *, seq, c_real, n_chunks, n_seq, out_dtype):
    t = ra.shape[0]
    bsz = t // seq
    rows = c_real * n_chunks
    steps = seq // rows
    tables = _retention_tables(c_real)
    kern = functools.partial(_retention_kernel, c_real=c_real, n_chunks=n_chunks, n_seq=n_seq)
    a, s_out = pl.pallas_call(
        kern,
        grid=(bsz // n_seq, steps),
        in_specs=[pl.BlockSpec((n_seq, rows, RA_W), lambda b, j: (b, j, 0)),
                  pl.BlockSpec((n_seq, H_A, DK_A, DV_A), lambda b, j: (b, 0, 0, 0)),
                  _const_spec((1, VA_W))] + [_const_spec(tb.shape) for tb in tables],
        out_specs=[pl.BlockSpec((n_seq, rows, VA_W), lambda b, j: (b, j, 0)),
                   pl.BlockSpec((n_seq, H_A, DK_A, DV_A), lambda b, j: (b, 0, 0, 0))],
        out_shape=[jax.ShapeDtypeStruct((bsz, seq, VA_W), out_dtype),
                   jax.ShapeDtypeStruct((bsz, H_A, DK_A, DV_A), F32)],
        scratch_shapes=[pltpu.VMEM((n_seq, H_A * DK_A, H_A * DV_A), F32)],
        compiler_params=_params(("arbitrary", "arbitrary")),
        name="retention",
    )(ra.reshape(bsz, seq, RA_W), state0, gain, *tables)
    return a.reshape(t, VA_W), s_out


SWA_ROWS = 2048
N_BRANCH = len(SWA_PAIRS)


def _alibi_slopes():
    return jnp.exp2(-8.0 * jnp.arange(1, H_B + 1, dtype=F32) / H_B)


def _swa_prompt_bias():
    qi = jnp.arange(SPAN)[:, None]
    kj = jnp.arange(2 * SPAN)[None, :]
    dist = SPAN + qi - kj
    band = (dist >= 0) & (dist <= SPAN)
    slopes = _alibi_slopes()
    out = []
    for (_, dil) in SWA_PAIRS:
        pen = -slopes[:, None, None] * (dil * dist).astype(F32)[None]
        normal = jnp.where(band[None], pen, NEG_INF)
        first = jnp.where((band & (kj >= SPAN))[None], pen, NEG_INF)
        out.append(jnp.stack([normal, first], axis=1))
    tab = jnp.stack(out, axis=1)
    tab = tab.reshape(H_B // 2, 2, N_BRANCH, 2, SPAN, 2 * SPAN).transpose(0, 2, 3, 1, 4, 5)
    return tab.reshape(H_B // 2, N_BRANCH, 2, 2 * SPAN, 2 * SPAN)


COARSE = 4


def _swa_unit(q, kk, vv, bias, head0):
    q = q * (DH_B ** -0.5)
    qm = jnp.concatenate([jnp.where(head0, q, 0.0), jnp.where(head0, 0.0, q)], axis=0).astype(BF16)
    s = _dot_nt(qm, kk.astype(BF16)) + bias
    m = jnp.max(s, axis=-1, keepdims=True)
    p = jnp.exp(s - m).astype(BF16)
    ones = jnp.ones((2 * SPAN, LANES), BF16)
    res = _dot(p, jnp.concatenate([vv.astype(BF16), ones], axis=1))
    acc = jnp.where(head0, res[:SPAN, :LANES], res[SPAN:, :LANES])
    den = jnp.where(head0, res[:SPAN, LANES:], res[SPAN:, LANES:])
    mm = jnp.where(head0, jnp.broadcast_to(m[:SPAN], (SPAN, LANES)), jnp.broadcast_to(m[SPAN:], (SPAN, LANES)))
    return acc, den, mm


SWA_PREV = SPAN * COARSE


def _swa_prompt_half(half, q_ref, k_ref, v_ref, bias_ref, o_ref, kbuf, vbuf, acc_ref, l_ref, m_ref,
                     q4, k4, v4, acc4, l4, m4):
    j = pl.program_id(2)
    rows = SWA_ROWS
    sub = rows // COARSE
    slot = j % 2

    @pl.when((half == 0) & (j > 0))
    def _():
        kbuf[0:SWA_PREV, :] = kbuf[rows:rows + SWA_PREV, :]
        vbuf[0:SWA_PREV, :] = vbuf[rows:rows + SWA_PREV, :]

    @pl.when((half == 0) & (j == 0))
    def _():
        k4[1] = jnp.zeros(k4.shape[1:], F32)
        v4[1] = jnp.zeros(v4.shape[1:], F32)
        kbuf[0:SWA_PREV, :] = jnp.zeros((SWA_PREV, LANES), F32)
        vbuf[0:SWA_PREV, :] = jnp.zeros((SWA_PREV, LANES), F32)

    @pl.when(half == 0)
    def _():
        kbuf[SWA_PREV:, :] = k_ref[...]
        vbuf[SWA_PREV:, :] = v_ref[...]
        for c in range(COARSE):
            q4[c] = q_ref[pl.ds(c, sub, stride=COARSE), :]
            k4[slot, c] = k_ref[pl.ds(c, sub, stride=COARSE), :]
            v4[slot, c] = v_ref[pl.ds(c, sub, stride=COARSE), :]

    lane = lax.broadcasted_iota(jnp.int32, (SPAN, LANES), 1)
    head0 = lane < DH_B
    first_step = (j == 0).astype(jnp.int32)

    for g, (_, dil) in enumerate(SWA_PAIRS):
        blocks = rows // (SPAN * dil)
        per_half = blocks * dil // 2

        def unit(u, g=g, dil=dil):
            wb = u // dil
            r = u % dil
            q_start = wb * (SPAN * dil) + r
            k_start = SWA_PREV + (wb - 1) * (SPAN * dil) + r
            variant = jnp.where(wb == 0, first_step, 0)
            acc, den, mm = _swa_unit(q_ref[pl.ds(q_start, SPAN, stride=dil), :],
                                     kbuf[pl.ds(k_start, 2 * SPAN, stride=dil), :],
                                     vbuf[pl.ds(k_start, 2 * SPAN, stride=dil), :],
                                     bias_ref[0, g, variant], head0)
            acc_ref[g, pl.ds(q_start, SPAN, stride=dil), :] = acc
            l_ref[g, pl.ds(q_start, SPAN, stride=dil), :] = den
            m_ref[g, pl.ds(q_start, SPAN, stride=dil), :] = mm

        def unit_two_level(u, g=g, fine=dil // COARSE):
            c = u % COARSE
            f = u // COARSE
            pick = pl.ds(f, SPAN, stride=fine)
            acc, den, mm = _swa_unit(q4[c, pick, :],
                                     jnp.concatenate([k4[1 - slot, c, pick, :], k4[slot, c, pick, :]], axis=0),
                                     jnp.concatenate([v4[1 - slot, c, pick, :], v4[slot, c, pick, :]], axis=0),
                                     bias_ref[0, g, first_step], head0)
            dst = pl.ds(c * sub + f, SPAN, stride=fine)
            acc4[dst, :] = acc
            l4[dst, :] = den
            m4[dst, :] = mm

        two_level = dil % (COARSE * COARSE) == 0
        assert not two_level or blocks == 1
        body = unit_two_level if two_level else unit

        @pl.when(half >= 0)
        def _(body=body, per_half=per_half):
            for i in range(per_half):
                body(half * per_half + i)

        if two_level:
            @pl.when(half == 1)
            def _(g=g):
                for c in range(COARSE):
                    src, dst = slice(c * sub, (c + 1) * sub), pl.ds(c, sub, stride=COARSE)
                    acc_ref[g, dst, :] = acc4[src, :]
                    l_ref[g, dst, :] = l4[src, :]
                    m_ref[g, dst, :] = m4[src, :]

    @pl.when(half == 1)
    def _():
        tile = 256
        for r0 in range(0, rows, tile):
            sl = slice(r0, r0 + tile)
            ms = [m_ref[g, sl, :] for g in range(N_BRANCH)]
            mx = functools.reduce(jnp.maximum, ms)
            ws = [jnp.exp(mg - mx) for mg in ms]
            num = sum(w * acc_ref[g, sl, :] for g, w in enumerate(ws))
            den = sum(w * l_ref[g, sl, :] for g, w in enumerate(ws))
            o_ref[sl, :] = (num / den).astype(o_ref.dtype)


def _swa_sample_bias(n_new):
    t = jnp.arange(n_new)[:, None]
    slopes = _alibi_slopes()

    def table(j):
        dist = SWA_BUF + t - j
        out = []
        for (window, dil) in SWA_PAIRS:
            valid = (dist >= 0) & (dist <= window) & (dist % dil == 0) & (PAST_LEN + t - dist >= 0)
            pen = -slopes[:, None, None] * dist.astype(F32)[None]
            out.append(jnp.where(valid[None], pen, NEG_INF).reshape(H_B * n_new, -1))
        return jnp.stack(out)

    bias_a = table(jnp.arange(SWA_BUF)[None, :])
    jb = jnp.arange(LANES)[None, :]
    bias_b = jnp.where(jb < n_new, table(SWA_BUF + jb), NEG_INF)
    return bias_a, bias_b


def _swa_sample_heads(q_ref, k_ref, v_ref, kp_ref, vp_ref, ba_ref, bb_ref, o_ref, ko_ref, vo_ref, n_new):
    width = q_ref.shape[1]
    n_heads = width // DH_B
    lane_h = lax.broadcasted_iota(jnp.int32, (n_new, width), 1) // DH_B
    q = q_ref[...] * (DH_B ** -0.5)
    qm = jnp.concatenate([jnp.where(lane_h == h, q, 0.0) for h in range(n_heads)], axis=0).astype(BF16)
    pad = jnp.zeros((LANES - n_new, width), F32)
    k_new = jnp.concatenate([k_ref[...], pad], axis=0)
    v_new = jnp.concatenate([v_ref[...], pad], axis=0)
    s_a = _dot(qm, kp_ref[0].astype(BF16))
    s_b = _dot_nt(qm, k_new.astype(BF16))
    sa = [s_a + ba_ref[g] for g in range(N_BRANCH)]
    sb = [s_b + bb_ref[g] for g in range(N_BRANCH)]
    mx = functools.reduce(jnp.maximum, [jnp.max(x, axis=-1, keepdims=True) for x in sa + sb])
    p_a = sum(jnp.exp(x - mx) for x in sa).astype(BF16)
    p_b = sum(jnp.exp(x - mx) for x in sb).astype(BF16)
    den = (jnp.sum(p_a.astype(F32), axis=-1, keepdims=True)
           + jnp.sum(p_b.astype(F32), axis=-1, keepdims=True))
    o = (_dot_nt(p_a, vp_ref[0].astype(BF16)) + _dot(p_b, v_new.astype(BF16))) / den
    o_ref[...] = sum(jnp.where(lane_h == h, o[h * n_new:(h + 1) * n_new], 0.0) for h in range(n_heads))

    lane = lax.broadcasted_iota(jnp.int32, (DH_B, LANES), 1)
    for src_ref, new, dst_ref in ((kp_ref, k_new, ko_ref), (vp_ref, v_new, vo_ref)):
        new_t = pltpu.roll(new.T, LANES - n_new, axis=1)
        for h in range(n_heads):
            rows = slice(h * DH_B, (h + 1) * DH_B)
            shifted = pltpu.roll(src_ref[0, rows, :], SWA_BUF - n_new, axis=1)
            dst_ref[0, rows, 0:SWA_BUF - LANES] = shifted[:, 0:SWA_BUF - LANES]
            dst_ref[0, rows, SWA_BUF - LANES:] = jnp.where(lane >= LANES - n_new, new_t[rows],
                                                           shifted[:, SWA_BUF - LANES:])


def _swa_kernel(q_ref, k_ref, v_ref, bias_ref, qs_ref, ks_ref, vs_ref, kp_ref, vp_ref, ba_ref, bb_ref,
                o_ref, os_ref, ko_ref, vo_ref, *scratch, n_new):
    _swa_prompt_half(pl.program_id(3), q_ref, k_ref, v_ref, bias_ref, o_ref, *scratch)
    _swa_sample_heads(qs_ref, ks_ref, vs_ref, kp_ref, vp_ref, ba_ref, bb_ref, os_ref, ko_ref, vo_ref, n_new)


def _swa(qkv, bsz, seq, qkv_s, k_past, v_past, n_new):
    t = qkv.shape[0]
    dbsz = k_past.shape[0]
    steps = seq // SWA_ROWS
    npair = H_B // 2
    width = H_B * DH_B
    half_w = width // 2
    assert dbsz == bsz * npair * steps, "one sample sequence per prompt (sequence, head pair, row block) step"
    bias = _swa_prompt_bias()
    bias_a, bias_b = _swa_sample_bias(n_new)
    blk = (SWA_ROWS, LANES)
    rows_of = lambda b, j: b * steps + j
    seq_of = lambda b, hp, j: (b * npair + hp) * steps + j
    new_blk = lambda col: pl.BlockSpec((n_new, half_w), lambda b, hp, j, h, col=col: (seq_of(b, hp, j), 2 * col + h))
    state_blk = pl.BlockSpec((1, half_w, SWA_BUF), lambda b, hp, j, h: (seq_of(b, hp, j), h, 0))
    half_rows = H_B // 2 * n_new
    sub = SWA_ROWS // COARSE
    return pl.pallas_call(
        functools.partial(_swa_kernel, n_new=n_new),
        grid=(bsz, npair, steps, 2),
        in_specs=[pl.BlockSpec(blk, lambda b, hp, j, h: (rows_of(b, j), hp)),
                  pl.BlockSpec(blk, lambda b, hp, j, h: (rows_of(b, j), npair + hp)),
                  pl.BlockSpec(blk, lambda b, hp, j, h: (rows_of(b, j), 2 * npair + hp)),
                  pl.BlockSpec((1, N_BRANCH, 2, 2 * SPAN, 2 * SPAN), lambda b, hp, j, h: (hp, 0, 0, 0, 0)),
                  new_blk(0), new_blk(1), new_blk(2), state_blk, state_blk,
                  pl.BlockSpec((N_BRANCH, half_rows, SWA_BUF), lambda b, hp, j, h: (0, h, 0)),
                  pl.BlockSpec((N_BRANCH, half_rows, LANES), lambda b, hp, j, h: (0, h, 0))],
        out_specs=[pl.BlockSpec(blk, lambda b, hp, j, h: (rows_of(b, j), hp)),
                   pl.BlockSpec((n_new, half_w), lambda b, hp, j, h: (seq_of(b, hp, j), h)),
                   state_blk, state_blk],
        out_shape=[jax.ShapeDtypeStruct((t, width), BF16),
                   jax.ShapeDtypeStruct((dbsz * n_new, width), F32),
                   jax.ShapeDtypeStruct(k_past.shape, F32), jax.ShapeDtypeStruct(v_past.shape, F32)],
        scratch_shapes=[pltpu.VMEM((SWA_PREV + SWA_ROWS, LANES), F32)] * 2
                       + [pltpu.VMEM((N_BRANCH, SWA_ROWS, LANES), F32)] * 3
                       + [pltpu.VMEM((COARSE, sub, LANES), F32)]
                       + [pltpu.VMEM((2, COARSE, sub, LANES), F32)] * 2
                       + [pltpu.VMEM((SWA_ROWS, LANES), F32)] * 3,
        compiler_params=_params(("arbitrary",) * 4),
        name="swa",
    )(qkv, qkv, qkv, bias, qkv_s, qkv_s, qkv_s, k_past, v_past, bias_a, bias_b)


FFN_CHUNK = 256


def _ffn(y1, g_ffn_ref, w_in_ref, w_o_ref, acc_ref):
    d_ff = w_o_ref.shape[0]
    h = _rms(y1, g_ffn_ref[...]).astype(BF16)
    for ci, c0 in enumerate(range(0, d_ff, FFN_CHUNK)):
        gate = _dot(h, w_in_ref[:, c0:c0 + FFN_CHUNK])
        up = _dot(h, w_in_ref[:, d_ff + c0:d_ff + c0 + FFN_CHUNK])
        act = (gate * jax.nn.sigmoid(gate) * up).astype(BF16)
        part = _dot(act, w_o_ref[c0:c0 + FFN_CHUNK, :])
        if ci == 0:
            acc_ref[...] = part
        else:
            acc_ref[...] += part
    return y1 + acc_ref[...]


def _ab_tail_kernel(x_ref, a_ref, ob_ref, w_out_ref, g_ffn_ref, w_in_ref, w_o_ref, g_next_ref, perm_ref,
                    o_ref, u_ref, acc_ref, h_ref, z_ref, *, chunk):
    mix = (_dot(a_ref[...].astype(BF16), w_out_ref[0:VA_W, :])
           + _dot(ob_ref[...].astype(BF16), w_out_ref[VA_W:, :]))
    y = _ffn(x_ref[...] + mix, g_ffn_ref, w_in_ref, w_o_ref, acc_ref)
    o_ref[...] = y
    _group_rows(_rms(y, g_next_ref[...]), perm_ref, u_ref, h_ref, z_ref, chunk)


def _s5_tail_kernel(x_ref, yg_ref, perm_ref, g_mix_ref, dskip_ref, w_glu_ref, g_ffn_ref, w_in_ref, w_o_ref,
                    g_fin_ref, o_ref, acc_ref, z_ref, t_ref, *, chunk):
    x = x_ref[...]
    d = x.shape[-1]
    u = _rms(x, g_mix_ref[...])
    _ungroup_rows(yg_ref, perm_ref, z_ref, t_ref, chunk)
    ys = jnp.concatenate([t_ref[v] for v in range(d // LANES)], axis=1)
    z = jax.nn.gelu(ys + dskip_ref[...] * u, approximate=True).astype(BF16)
    val = _dot(z, w_glu_ref[:, 0:d])
    gate = _dot(z, w_glu_ref[:, d:2 * d])
    y2 = _ffn(x + val * jax.nn.sigmoid(gate), g_ffn_ref, w_in_ref, w_o_ref, acc_ref)
    o_ref[...] = _rms(y2, g_fin_ref[...])


def _row_spec(tm, width):
    return pl.BlockSpec((tm, width), lambda i: (i, 0))


def _group_scratch(tm, d, chunk):
    n_slab, nc, n_col = d // LANES, tm // chunk, chunk // SLAB_GROUPS
    assert min(S5_MM_ROWS, n_slab * n_col * nc) % nc == 0
    return (pltpu.VMEM((n_slab, tm, LANES), F32), pltpu.VMEM((n_slab * n_col * nc, SLAB_GROUPS * LANES), BF16))


def _group_spec(tm, d, chunk):
    return pl.BlockSpec((d // S5_GROUP, tm // chunk, chunk * S5_GROUP), lambda i: (0, i, 0))


def _layer_spec(stacked, layer):
    nd = stacked.ndim - 1
    return pl.BlockSpec((None,) + stacked.shape[1:], lambda *_: (layer,) + (0,) * nd, pipeline_mode=pl.Buffered(1))


def _ab_tail(x, a, ob, w_out, g_ffn, w_in, w_o, layer, g_next, perm, tm, chunk):
    t, d = x.shape
    h_scr, z_scr = _group_scratch(tm, d, chunk)
    return pl.pallas_call(
        functools.partial(_ab_tail_kernel, chunk=chunk),
        grid=(t // tm,),
        in_specs=[_row_spec(tm, d), _row_spec(tm, a.shape[1]), _row_spec(tm, ob.shape[1]),
                  _const_spec(w_out.shape), _const_spec((1, d)), _layer_spec(w_in, layer), _layer_spec(w_o, layer),
                  _const_spec((1, d)), _const_spec(perm.shape)],
        out_specs=[_row_spec(tm, d), _group_spec(tm, d, chunk)],
        out_shape=[jax.ShapeDtypeStruct((t, d), F32),
                   jax.ShapeDtypeStruct((d // S5_GROUP, t // chunk, chunk * S5_GROUP), BF16)],
        scratch_shapes=[pltpu.VMEM((tm, d), F32), h_scr, z_scr],
        compiler_params=_params(("arbitrary",)),
        name="ab_tail",
    )(x, a, ob, w_out, g_ffn, w_in, w_o, g_next, perm)


def _s5_tail(x, yg, perm_t, g_mix, dskip, w_glu, g_ffn, w_in, w_o, layer, g_fin, tm, chunk):
    t, d = x.shape
    t_scr, z_scr = _group_scratch(tm, d, chunk)
    return pl.pallas_call(
        functools.partial(_s5_tail_kernel, chunk=chunk),
        grid=(t // tm,),
        in_specs=[_row_spec(tm, d), _group_spec(tm, d, chunk), _const_spec(perm_t.shape),
                  _const_spec((1, d)), _const_spec((1, d)),
                  _const_spec(w_glu.shape), _const_spec((1, d)), _layer_spec(w_in, layer), _layer_spec(w_o, layer),
                  _const_spec((1, d))],
        out_specs=_row_spec(tm, d),
        out_shape=jax.ShapeDtypeStruct((t, d), F32),
        scratch_shapes=[pltpu.VMEM((tm, d), F32), z_scr, t_scr],
        compiler_params=_params(("arbitrary",)),
        name="s5_tail",
    )(x, yg, perm_t, g_mix, dskip, w_glu, g_ffn, w_in, w_o, g_fin)


SLAB_GROUPS = LANES // S5_GROUP
S5_MM_ROWS = 256


def _slab_perm():
    idx = jnp.arange(SLAB_GROUPS * LANES)
    l8, g8, p = idx // LANES, (idx % LANES) // S5_GROUP, idx % S5_GROUP
    dst = g8 * LANES + l8 * S5_GROUP + p
    return (dst[:, None] == idx[None, :]).astype(BF16)


def _permute_rows(z_ref, perm_ref, emit):
    step = min(S5_MM_ROWS, z_ref.shape[0])
    for r0 in range(0, z_ref.shape[0], step):
        emit(r0, _dot(z_ref[r0:r0 + step, :], perm_ref[...]))


def _group_rows(h, perm_ref, o_ref, h_ref, z_ref, chunk):
    rows = h.shape[0]
    nc = rows // chunk
    n_slab = h.shape[1] // LANES
    n_col = chunk // SLAB_GROUPS
    for v in range(n_slab):
        h_ref[v] = h[:, v * LANES:(v + 1) * LANES]
    for v in range(n_slab):
        for l in range(chunk):
            j, l8 = l // SLAB_GROUPS, l % SLAB_GROUPS
            r0 = (v * n_col + j) * nc
            z_ref[r0:r0 + nc, l8 * LANES:(l8 + 1) * LANES] = h_ref[v, pl.ds(l, nc, stride=chunk), :].astype(BF16)

    def emit(r0, blk):
        for q in range(blk.shape[0] // nc):
            v, j = divmod(r0 // nc + q, n_col)
            for g8 in range(SLAB_GROUPS):
                o_ref[v * SLAB_GROUPS + g8, :, j * LANES:(j + 1) * LANES] = (
                    blk[q * nc:(q + 1) * nc, g8 * LANES:(g8 + 1) * LANES].astype(o_ref.dtype))

    _permute_rows(z_ref, perm_ref, emit)


def _ungroup_rows(y_ref, perm_ref, z_ref, t_ref, chunk):
    n_slab, rows, _ = t_ref.shape
    nc = rows // chunk
    n_col = chunk // SLAB_GROUPS
    for v in range(n_slab):
        for j in range(n_col):
            r0 = (v * n_col + j) * nc
            for g8 in range(SLAB_GROUPS):
                z_ref[r0:r0 + nc, g8 * LANES:(g8 + 1) * LANES] = y_ref[v * SLAB_GROUPS + g8, :, j * LANES:(j + 1) * LANES]

    def emit(r0, blk):
        for q in range(blk.shape[0] // nc):
            v, j = divmod(r0 // nc + q, n_col)
            for l8 in range(SLAB_GROUPS):
                t_ref[v, pl.ds(j * SLAB_GROUPS + l8, nc, stride=chunk), :] = (
                    blk[q * nc:(q + 1) * nc, l8 * LANES:(l8 + 1) * LANES])

    _permute_rows(z_ref, perm_ref, emit)


S5_LP = S5_CHUNK * S5_GROUP
S5_RI = 2 * S5_STATE


def _s5_prep_kernel(*refs):
    for parity in range(2):
        _s5_prep_group(parity, *[r.at[parity] for r in refs])


def _s5_prep_group(parity, lam_row_ref, ls_ref, btr_ref, bti_ref, cr_ref, ci_ref,
                   bs_ref, cs_ref, tp_ref, ap_ref):
    n_pow = S5_CHUNK + 1
    dt = jnp.exp(ls_ref[...])

    def powers(lr, li):
        mag = jnp.exp(lr * dt)
        a_re, a_im = mag * jnp.cos(li * dt), mag * jnp.sin(li * dt)
        pw = [(jnp.ones_like(a_re), jnp.zeros_like(a_im))]
        for _ in range(n_pow - 1):
            pr, pi = pw[-1]
            pw.append((pr * a_re - pi * a_im, pr * a_im + pi * a_re))
        return pw

    lr, li = lam_row_ref[0:1, :], lam_row_ref[1:2, :]
    pw_row = powers(lr, li)
    a_re, a_im = pw_row[1]
    den = lr * lr + li * li
    f_re = ((a_re - 1.0) * lr + a_im * li) / den
    f_im = (a_im * lr - (a_re - 1.0) * li) / den
    blk = (S5_GROUP, S5_RI)
    w_re = jnp.concatenate([jnp.broadcast_to(f_re * pw_row[S5_CHUNK - 1 - l][0] - f_im * pw_row[S5_CHUNK - 1 - l][1], blk)
                            for l in range(S5_CHUNK)], axis=0)
    w_im = jnp.concatenate([jnp.broadcast_to(f_re * pw_row[S5_CHUNK - 1 - l][1] + f_im * pw_row[S5_CHUNK - 1 - l][0], blk)
                            for l in range(S5_CHUNK)], axis=0)

    lane = lax.broadcasted_iota(jnp.int32, (1, S5_RI), 1)
    own = (lane // S5_STATE) == parity
    low = lane < S5_STATE
    keep = lambda z: jnp.where(own, z, 0.0)
    per_pos = lambda ref: jnp.concatenate([ref[...]] * S5_CHUNK, axis=0)
    btr, bti = per_pos(btr_ref), per_pos(bti_ref)
    bs_ref[:, :S5_RI] = keep(w_re * btr - w_im * bti).astype(bs_ref.dtype)
    bs_ref[:, S5_RI:] = keep(w_re * bti + w_im * btr).astype(bs_ref.dtype)
    bf_t = jnp.where(low, f_re * btr[0:S5_GROUP] - f_im * bti[0:S5_GROUP],
                     f_re * bti[0:S5_GROUP] + f_im * btr[0:S5_GROUP])
    ap_ref[...] = jnp.zeros_like(ap_ref)
    for i, (j, part) in enumerate(((S5_CHUNK, 0), (S5_CHUNK, 1), (S5_CHUNK // 2, 0), (S5_CHUNK // 2, 1))):
        ap_ref[i:i + 1, :] = keep(pw_row[j][part])

    def spread(j0, part):
        return jnp.concatenate([jnp.broadcast_to(pw_row[j0 + l][part], blk) for l in range(S5_CHUNK)], axis=0)

    cr, ci = per_pos(cr_ref), per_pos(ci_ref)
    cs0 = jnp.where(low, cr * spread(0, 0) - ci * spread(0, 1),
                    -(cr * spread(0, 1) + ci * spread(0, 0)))
    cs_ref[:, :S5_RI] = keep(cr * spread(1, 0) - ci * spread(1, 1)).astype(cs_ref.dtype)
    cs_ref[:, S5_RI:] = keep(-(cr * spread(1, 1) + ci * spread(1, 0))).astype(cs_ref.dtype)
    r = lax.dot_general(bf_t, cs0, (((1,), (1,)), ((), ())), preferred_element_type=F32,
                        precision=lax.Precision.HIGHEST)
    lane_lp = lax.broadcasted_iota(jnp.int32, (S5_GROUP, S5_LP), 1)
    for l in range(S5_CHUNK):
        sh = l * S5_GROUP
        blk_l = r if l == 0 else jnp.where(lane_lp >= sh, pltpu.roll(r, sh, axis=1), 0.0)
        tp_ref[l * S5_GROUP:(l + 1) * S5_GROUP, :] = blk_l.astype(tp_ref.dtype)


def _s5_prep(lam_re, lam_im, log_step, b_re, b_im, c_re, c_im):
    g = lam_re.shape[0]
    dup = lambda z: jnp.concatenate([z, z], axis=-1)
    lam_row = jnp.stack([dup(lam_re), dup(lam_im)], axis=1)
    lam_row = jnp.pad(lam_row, ((0, 0), (0, 6), (0, 0)))
    ls = log_step.reshape(g, 1, 1)
    btr, bti = dup(jnp.swapaxes(b_re, 1, 2)), dup(jnp.swapaxes(b_im, 1, 2))
    cr, ci = dup(c_re), dup(c_im)
    gspec = lambda *s: pl.BlockSpec((2,) + s, lambda i: (i,) + (0,) * len(s))
    table = jax.ShapeDtypeStruct((g, S5_LP, 2 * S5_RI), BF16)
    return pl.pallas_call(
        _s5_prep_kernel,
        grid=(g // 2,),
        in_specs=[gspec(8, S5_RI), gspec(1, 1)] + [gspec(S5_GROUP, S5_RI)] * 4,
        out_specs=[gspec(S5_LP, 2 * S5_RI)] * 2 + [gspec(S5_LP, S5_LP), gspec(8, S5_RI)],
        out_shape=[table] * 2 + [jax.ShapeDtypeStruct((g, S5_LP, S5_LP), BF16),
                                 jax.ShapeDtypeStruct((g, 8, S5_RI), F32)],
        compiler_params=_params(("arbitrary",)),
        name="s5_prep",
    )(lam_row, ls, btr, bti, cr, ci)


def _s5_scan_kernel(u_ref, us_ref, x0r_ref, x0i_ref, bs_ref, cs_ref, tp_ref, ap_ref,
                    y_ref, ys_ref, finr_ref, fini_ref, finsr_ref, finsi_ref, xs_ref, *, bsz, chunks):
    n_rows = bsz * chunks
    pair = range(2)
    us_in = [u_ref[g] for g in pair]
    x = sum(_dot(us_in[g], bs_ref[g]) for g in pair)
    xr, xi = x[:, :S5_RI], x[:, S5_RI:]
    row = lax.broadcasted_iota(jnp.int32, (n_rows, 1), 0) % chunks
    ap = ap_ref[0] + ap_ref[1]
    ar, ai = ap[0:1, :], ap[1:2, :]

    def prefix(xr, xi, ar, ai, pos, length):
        shift = 1
        while shift < length:
            sr = jnp.where(pos >= shift, pltpu.roll(xr, shift, axis=0), 0.0)
            si = jnp.where(pos >= shift, pltpu.roll(xi, shift, axis=0), 0.0)
            xr, xi = xr + (sr * ar - si * ai), xi + (sr * ai + si * ar)
            ar, ai = ar * ar - ai * ai, 2.0 * (ar * ai)
            shift *= 2
        return xr, xi, ar, ai

    n_runs, runs = n_rows // SUBLANES, chunks // SUBLANES
    xr, xi, br, bi = prefix(xr, xi, ar, ai, row % SUBLANES, SUBLANES)
    xs_ref[0], xs_ref[1] = xr, xi
    last = pl.ds(SUBLANES - 1, n_runs, stride=SUBLANES)
    run = lax.broadcasted_iota(jnp.int32, (n_runs, 1), 0) % runs
    er, ei, _, _ = prefix(xs_ref[0, last, :], xs_ref[1, last, :], br, bi, run, runs)
    cr = jnp.where(run >= 1, pltpu.roll(er, 1, axis=0), 0.0)
    ci = jnp.where(run >= 1, pltpu.roll(ei, 1, axis=0), 0.0)
    pr, pi = ar, ai
    for r in range(SUBLANES):
        dst = pl.ds(r, n_runs, stride=SUBLANES)
        xs_ref[0, dst, :] = cr * pr - ci * pi
        xs_ref[1, dst, :] = cr * pi + ci * pr
        pr, pi = pr * ar - pi * ai, pr * ai + pi * ar
    xr, xi = xr + xs_ref[0], xi + xs_ref[1]
    pr = jnp.where(row >= 1, pltpu.roll(xr, 1, axis=0), 0.0).astype(BF16)
    pi = jnp.where(row >= 1, pltpu.roll(xi, 1, axis=0), 0.0).astype(BF16)
    prev = jnp.concatenate([pr, pi], axis=1)
    for g in pair:
        y_ref[g] = (_dot(us_in[g], tp_ref[g]) + _dot_nt(prev, cs_ref[g])).astype(y_ref.dtype)
    finr_ref[...] = jnp.zeros_like(finr_ref)
    fini_ref[...] = jnp.zeros_like(fini_ref)
    for b in range(bsz):
        last = slice((b + 1) * chunks - 1, (b + 1) * chunks)
        finr_ref[b:b + 1, :] = xr[last]
        fini_ref[b:b + 1, :] = xi[last]

    half = S5_LP // 2
    x0r, x0i = x0r_ref[...], x0i_ref[...]
    hr, hi = ap[2:3, :], ap[3:4, :]
    xs = sum(_dot(us_ref[g], bs_ref[g, half:, :]) for g in pair)
    finsr_ref[...] = x0r * hr - x0i * hi + xs[:, :S5_RI]
    finsi_ref[...] = x0r * hi + x0i * hr + xs[:, S5_RI:]
    x0 = jnp.concatenate([x0r, x0i], axis=1).astype(BF16)
    for g in pair:
        ys_ref[g] = (_dot(us_ref[g], tp_ref[g, :half, :half]) + _dot_nt(x0, cs_ref[g, :half, :])).astype(ys_ref.dtype)


def _s5_scan(u_g, us_g, x0r, x0i, bs, cs, tp, ap, bsz):
    g, n_rows, _ = u_g.shape
    n_s = us_g.shape[1]
    half = S5_LP // 2
    pspec = lambda *s: pl.BlockSpec((2,) + s, lambda i: (i,) + (0,) * len(s))
    ospec = lambda *s: pl.BlockSpec((None,) + s, lambda i: (i,) + (0,) * len(s))
    packed = lambda n: jax.ShapeDtypeStruct((g // 2, n, S5_RI), F32)
    return pl.pallas_call(
        functools.partial(_s5_scan_kernel, bsz=bsz, chunks=n_rows // bsz),
        grid=(g // 2,),
        in_specs=[pspec(n_rows, S5_LP), pspec(n_s, half), ospec(n_s, S5_RI), ospec(n_s, S5_RI)]
                 + [pspec(S5_LP, 2 * S5_RI)] * 2 + [pspec(S5_LP, S5_LP), pspec(8, S5_RI)],
        out_specs=[pspec(n_rows, S5_LP), pspec(n_s, half), ospec(8, S5_RI), ospec(8, S5_RI),
                   ospec(n_s, S5_RI), ospec(n_s, S5_RI)],
        out_shape=[jax.ShapeDtypeStruct((g, n_rows, S5_LP), BF16), jax.ShapeDtypeStruct((g, n_s, half), BF16),
                   packed(8), packed(8), packed(n_s), packed(n_s)],
        scratch_shapes=[pltpu.VMEM((2, n_rows, S5_RI), F32)],
        compiler_params=_params(("arbitrary",)),
        name="s5_scan",
    )(u_g, us_g, x0r, x0i, bs, cs, tp, ap)


def kernel(x_prompt, x_sample, state_ret, state_swa_k, state_swa_v, state_ssm_re, state_ssm_im, norm_mix, norm_ffn, norm_final, w_in_ab, ret_gn, w_out_ab, ssm_lam_re, ssm_lam_im, ssm_log_step, ssm_b_re, ssm_b_im, ssm_c_re, ssm_c_im, ssm_d, w_glu, w_ffn_in, w_ffn_out):
    bsz, seq, d = x_prompt.shape
    dbsz, n_new, _ = x_sample.shape
    assert state_swa_k.shape[2] == SWA_BUF and n_new == S5_CHUNK // 2 and seq % SWA_ROWS == 0
    xp = x_prompt.reshape(bsz * seq, d)
    xs = x_sample.reshape(dbsz * n_new, d)
    tm_p, tm_s = 512, dbsz * n_new
    row = lambda v: v.reshape(1, -1)

    w_in0 = w_in_ab[0].astype(BF16)
    w_out0 = w_out_ab[0].astype(BF16)
    w_f_in, w_f_out = w_ffn_in.astype(BF16), w_ffn_out.astype(BF16)
    g_mix0, g_ffn0, gn0 = row(norm_mix[0]), row(norm_ffn[0]), row(ret_gn[0])

    ra_p, qkv_p = _ab_proj(xp, g_mix0, w_in0, 2 * tm_p)
    ra_s, qkv_s = _ab_proj(xs, g_mix0, w_in0, tm_s)

    zero_ret = jnp.zeros((bsz, H_A, DK_A, DV_A), F32)
    a_p, ret_p = _retention(ra_p, zero_ret, gn0, seq=seq, c_real=RET_CHUNK, n_chunks=2, n_seq=bsz, out_dtype=BF16)
    a_s, ret_s = _retention(ra_s, state_ret[0], gn0, seq=n_new, c_real=n_new, n_chunks=1, n_seq=4, out_dtype=F32)

    width_b = H_B * DH_B
    rows_last = lambda w: w.transpose(0, 2, 3, 1).reshape(dbsz, width_b, SWA_BUF)
    rows_first = lambda w: w.reshape(dbsz, H_B, DH_B, SWA_BUF).transpose(0, 3, 1, 2)[None]
    ob_p, ob_s, swk_s, swv_s = _swa(qkv_p, bsz, seq, qkv_s,
                                    rows_last(state_swa_k[0]), rows_last(state_swa_v[0]), n_new)
    kv_tail = qkv_p.reshape(bsz, seq, QKV_W)[:, seq - SWA_BUF:, width_b:]
    swk_p = kv_tail[..., :width_b].reshape(bsz, SWA_BUF, H_B, DH_B)
    swv_p = kv_tail[..., width_b:].reshape(bsz, SWA_BUF, H_B, DH_B)

    g_mix1, g_ffn1 = row(norm_mix[1]), row(norm_ffn[1])
    perm = _slab_perm()
    yp, u_p = _ab_tail(xp, a_p, ob_p, w_out0, g_ffn0, w_f_in, w_f_out, 0, g_mix1, perm, tm_p, S5_CHUNK)
    ys, u_s = _ab_tail(xs, a_s, ob_s, w_out0, g_ffn0, w_f_in, w_f_out, 0, g_mix1, perm, tm_s, n_new)

    operators = _s5_prep(ssm_lam_re[0], ssm_lam_im[0], ssm_log_step[0],
                         ssm_b_re[0], ssm_b_im[0], ssm_c_re[0], ssm_c_im[0])
    pack = lambda st: st.reshape(st.shape[0], -1, S5_RI).transpose(1, 0, 2)
    unpack = lambda st: st.transpose(1, 0, 2).reshape(st.shape[1], -1, S5_STATE)[None]
    y5_p, y5_s, finr_p, fini_p, finr_s, fini_s = _s5_scan(
        u_p, u_s, pack(state_ssm_re[0]), pack(state_ssm_im[0]), *operators, bsz)

    tail1 = (perm.T, g_mix1, row(ssm_d[0]), w_glu[0].astype(BF16), g_ffn1,
             w_f_in, w_f_out, 1, row(norm_final))
    yp = _s5_tail(yp, y5_p, *tail1, tm_p, S5_CHUNK)
    ys = _s5_tail(ys, y5_s, *tail1, tm_s, n_new)

    return (yp.reshape(bsz, seq, d), ys.reshape(dbsz, n_new, d),
            ret_p[None], ret_s[None],
            swk_p[None], swv_p[None],
            rows_first(swk_s), rows_first(swv_s),
            unpack(finr_p[:, :bsz]), unpack(fini_p[:, :bsz]), unpack(finr_s), unpack(fini_s))
```

```python
import functools

import jax
import jax.numpy as jnp
from jax import lax
from jax.experimental import pallas as pl
from jax.experimental.pallas import tpu as pltpu

F32 = jnp.float32
BF16 = jnp.bfloat16

H_A, DK_A, DV_A = 4, 64, 128
H_B, DH_B = 8, 64
SWA_PAIRS = ((128, 1), (512, 4), (2048, 16))
SPAN = 128
SWA_BUF = 2048
PAST_LEN = 16384
RET_CHUNK = 128
S5_GROUP, S5_STATE = 16, 64
S5_CHUNK = 16
EPS = 1e-6
NEG_INF = -1e30
QA_W, KA_W, VA_W, GA_W = H_A * DK_A, H_A * DK_A, H_A * DV_A, H_A * DV_A
RA_W = QA_W + KA_W + VA_W + GA_W
QKV_W = 3 * H_B * DH_B

LANES = 128
SUBLANES = 8
VMEM_LIMIT = 56 * 1024 * 1024

PROJ_ROWS = 1024
PROJ_COLS = 512
TAIL_ROWS = 512
RET_STEP_CHUNKS = 4
RET_SAMPLE_SEQS = 4


def _params(sem):
    return pltpu.CompilerParams(dimension_semantics=sem, vmem_limit_bytes=VMEM_LIMIT)


def _const_spec(shape):
    nd = len(shape)
    return pl.BlockSpec(shape, lambda *_: (0,) * nd, pipeline_mode=pl.Buffered(1))


def _rms(x, g):
    return x * lax.rsqrt(jnp.mean(x * x, axis=-1, keepdims=True) + EPS) * g


def _dot(a, b):
    return jnp.dot(a, b, preferred_element_type=F32)


def _dot_nt(a, b):
    return lax.dot_general(a, b, (((1,), (1,)), ((), ())), preferred_element_type=F32)


def _dot_tn(a, b):
    return lax.dot_general(a, b, (((0,), (0,)), ((), ())), preferred_element_type=F32)


def _ab_proj_kernel(x_ref, g_ref, w_ref, ra_ref, qkv_ref):
    h = _rms(x_ref[...], g_ref[...]).astype(BF16)
    for n0 in range(0, RA_W, PROJ_COLS):
        ra_ref[:, n0:n0 + PROJ_COLS] = _dot(h, w_ref[:, n0:n0 + PROJ_COLS])
    for n0 in range(0, QKV_W, PROJ_COLS):
        qkv_ref[:, n0:n0 + PROJ_COLS] = _dot(h, w_ref[:, RA_W + n0:RA_W + n0 + PROJ_COLS])


def _ab_proj(x, g, w_bf16, tm):
    t, d = x.shape
    return pl.pallas_call(
        _ab_proj_kernel,
        grid=(t // tm,),
        in_specs=[pl.BlockSpec((tm, d), lambda i: (i, 0)),
                  _const_spec((1, d)),
                  _const_spec((d, RA_W + QKV_W))],
        out_specs=[pl.BlockSpec((tm, RA_W), lambda i: (i, 0)),
                   pl.BlockSpec((tm, QKV_W), lambda i: (i, 0))],
        out_shape=[jax.ShapeDtypeStruct((t, RA_W), F32), jax.ShapeDtypeStruct((t, QKV_W), F32)],
        compiler_params=_params(("arbitrary",)),
        name="ab_proj",
    )(x, g, w_bf16)


def _retention_tables(c_real):
    c = RET_CHUNK
    log_g = jnp.log1p(-jnp.exp2(-5.0 - jnp.arange(H_A, dtype=F32)))
    idx = jnp.arange(c, dtype=F32)
    rel = idx[:, None] - idx[None, :]
    dec = jnp.where(rel >= 0, jnp.exp(jnp.maximum(rel, 0.0)[None] * log_g[:, None, None]), 0.0)
    dec = dec.reshape(H_A * c, c)
    real = (idx < c_real)[:, None]
    qd = jnp.exp((idx + 1.0)[:, None] * log_g[None, :])
    qd = jnp.repeat(qd, DV_A, axis=1)
    kd = jnp.where(real, jnp.exp((c_real - 1.0 - idx)[:, None] * log_g[None, :]), 0.0)
    kd = jnp.repeat(kd, DK_A, axis=1)
    row_h = jnp.arange(H_A * DK_A)[:, None] // DK_A
    col_h = jnp.arange(H_A * DV_A)[None, :] // DV_A
    bd = (row_h == col_h).astype(F32)
    dm = bd * jnp.repeat(jnp.exp(c_real * log_g), DV_A)[None, :]
    return dec, qd, kd, dm, bd


def _retention_kernel(ra_ref, s0_ref, gain_ref, dec_ref, qd_ref, kd_ref, dm_ref, bd_ref,
                      a_ref, sout_ref, sbd_ref, *, c_real, n_chunks, n_seq):
    c = RET_CHUNK
    j = pl.program_id(1)

    @pl.when(j == 0)
    def _():
        sbd_ref[...] = jnp.zeros_like(sbd_ref)
        for s in range(n_seq):
            for h in range(H_A):
                sbd_ref[s, h * DK_A:(h + 1) * DK_A, h * DV_A:(h + 1) * DV_A] = s0_ref[s, h]

    lane_q = lax.broadcasted_iota(jnp.int32, (c, QA_W), 1) // DK_A
    for ci in range(n_chunks):
        for s in range(n_seq):
            _retention_chunk(ra_ref.at[s], a_ref.at[s], sbd_ref.at[s], gain_ref, dec_ref, qd_ref, kd_ref, dm_ref,
                             bd_ref, lane_q, ci, c_real)

    @pl.when(j == pl.num_programs(1) - 1)
    def _():
        for s in range(n_seq):
            for h in range(H_A):
                sout_ref[s, h] = sbd_ref[s, h * DK_A:(h + 1) * DK_A, h * DV_A:(h + 1) * DV_A]


def _retention_chunk(ra_ref, a_ref, sbd_ref, gain_ref, dec_ref, qd_ref, kd_ref, dm_ref, bd_ref, lane_q, ci, c_real):
    c = RET_CHUNK
    rows = ra_ref[ci * c_real:(ci + 1) * c_real, :]
    gate = rows[:, QA_W + KA_W + VA_W:]
    if c_real < c:
        rows = jnp.concatenate([rows, jnp.zeros((c - c_real, RA_W), F32)], axis=0)
    q = rows[:, :QA_W]
    k = rows[:, QA_W:QA_W + KA_W] * (DK_A ** -0.5)
    v = rows[:, QA_W + KA_W:QA_W + KA_W + VA_W]
    vb = v.astype(BF16)
    qm = jnp.concatenate([jnp.where(lane_q == h, q, 0.0) for h in range(H_A)], axis=0).astype(BF16)
    s = _dot_nt(qm, k.astype(BF16)) * dec_ref[...]
    sb = s.astype(BF16)
    sbd = sbd_ref[...]
    cross = _dot(q.astype(BF16), sbd.astype(BF16)) * qd_ref[...]
    upd = _dot_tn((k * kd_ref[...]).astype(BF16), vb)
    sbd_ref[...] = sbd * dm_ref[...] + upd * bd_ref[...]
    for h in range(H_A):
        sl = slice(h * DV_A, (h + 1) * DV_A)
        o = _dot(sb[h * c:(h + 1) * c], vb[:, sl]) + cross[:, sl]
        o = o[:c_real]
        mu = jnp.mean(o, axis=-1, keepdims=True)
        var = jnp.mean(jnp.square(o - mu), axis=-1, keepdims=True)
        y = (o - mu) * lax.rsqrt(var + EPS) * gain_ref[:, sl]
        gh = gate[:, sl]
        a_ref[ci * c_real:(ci + 1) * c_real, sl] = (gh * jax.nn.sigmoid(gh) * y).astype(a_ref.dtype)


def _retention(ra, state0, gain, *, seq, c_real, n_chunks, n_seq, out_dtype):
    t = ra.shape[0]
    bsz = t // seq
    rows = c_real * n_chunks
    steps = seq // rows
    tables = _retention_tables(c_real)
    kern = functools.partial(_retention_kernel, c_real=c_real, n_chunks=n_chunks, n_seq=n_seq)
    a, s_out = pl.pallas_call(
        kern,
        grid=(bsz // n_seq, steps),
        in_specs=[pl.BlockSpec((n_seq, rows, RA_W), lambda b, j: (b, j, 0)),
                  pl.BlockSpec((n_seq, H_A, DK_A, DV_A), lambda b, j: (b, 0, 0, 0)),
                  _const_spec((1, VA_W))] + [_const_spec(tb.shape) for tb in tables],
        out_specs=[pl.BlockSpec((n_seq, rows, VA_W), lambda b, j: (b, j, 0)),
                   pl.BlockSpec((n_seq, H_A, DK_A, DV_A), lambda b, j: (b, 0, 0, 0))],
        out_shape=[jax.ShapeDtypeStruct((bsz, seq, VA_W), out_dtype),
                   jax.ShapeDtypeStruct((bsz, H_A, DK_A, DV_A), F32)],
        scratch_shapes=[pltpu.VMEM((n_seq, H_A * DK_A, H_A * DV_A), F32)],
        compiler_params=_params(("arbitrary", "arbitrary")),
        name="retention",
    )(ra.reshape(bsz, seq, RA_W), state0, gain, *tables)
    return a.reshape(t, VA_W), s_out


SWA_ROWS = 2048
N_BRANCH = len(SWA_PAIRS)


def _alibi_slopes():
    return jnp.exp2(-8.0 * jnp.arange(1, H_B + 1, dtype=F32) / H_B)


def _swa_prompt_bias():
    qi = jnp.arange(SPAN)[:, None]
    kj = jnp.arange(2 * SPAN)[None, :]
    dist = SPAN + qi - kj
    band = (dist >= 0) & (dist <= SPAN)
    slopes = _alibi_slopes()
    out = []
    for (_, dil) in SWA_PAIRS:
        pen = -slopes[:, None, None] * (dil * dist).astype(F32)[None]
        normal = jnp.where(band[None], pen, NEG_INF)
        first = jnp.where((band & (kj >= SPAN))[None], pen, NEG_INF)
        out.append(jnp.stack([normal, first], axis=1))
    tab = jnp.stack(out, axis=1)
    tab = tab.reshape(H_B // 2, 2, N_BRANCH, 2, SPAN, 2 * SPAN).transpose(0, 2, 3, 1, 4, 5)
    return tab.reshape(H_B // 2, N_BRANCH, 2, 2 * SPAN, 2 * SPAN)


COARSE = 4


def _swa_unit(q, kk, vv, bias, head0):
    q = q * (DH_B ** -0.5)
    qm = jnp.concatenate([jnp.where(head0, q, 0.0), jnp.where(head0, 0.0, q)], axis=0).astype(BF16)
    s = _dot_nt(qm, kk.astype(BF16)) + bias
    m = jnp.max(s, axis=-1, keepdims=True)
    p = jnp.exp(s - m).astype(BF16)
    ones = jnp.ones((2 * SPAN, LANES), BF16)
    res = _dot(p, jnp.concatenate([vv.astype(BF16), ones], axis=1))
    acc = jnp.where(head0, res[:SPAN, :LANES], res[SPAN:, :LANES])
    den = jnp.where(head0, res[:SPAN, LANES:], res[SPAN:, LANES:])
    mm = jnp.where(head0, jnp.broadcast_to(m[:SPAN], (SPAN, LANES)), jnp.broadcast_to(m[SPAN:], (SPAN, LANES)))
    return acc, den, mm


SWA_PREV = SPAN * COARSE


def _swa_prompt_half(half, q_ref, k_ref, v_ref, bias_ref, o_ref, kbuf, vbuf, acc_ref, l_ref, m_ref,
                     q4, k4, v4, acc4, l4, m4):
    j = pl.program_id(2)
    rows = SWA_ROWS
    sub = rows // COARSE
    slot = j % 2

    @pl.when((half == 0) & (j > 0))
    def _():
        kbuf[0:SWA_PREV, :] = kbuf[rows:rows + SWA_PREV, :]
        vbuf[0:SWA_PREV, :] = vbuf[rows:rows + SWA_PREV, :]

    @pl.when((half == 0) & (j == 0))
    def _():
        k4[1] = jnp.zeros(k4.shape[1:], F32)
        v4[1] = jnp.zeros(v4.shape[1:], F32)
        kbuf[0:SWA_PREV, :] = jnp.zeros((SWA_PREV, LANES), F32)
        vbuf[0:SWA_PREV, :] = jnp.zeros((SWA_PREV, LANES), F32)

    @pl.when(half == 0)
    def _():
        kbuf[SWA_PREV:, :] = k_ref[...]
        vbuf[SWA_PREV:, :] = v_ref[...]
        for c in range(COARSE):
            q4[c] = q_ref[pl.ds(c, sub, stride=COARSE), :]
            k4[slot, c] = k_ref[pl.ds(c, sub, stride=COARSE), :]
            v4[slot, c] = v_ref[pl.ds(c, sub, stride=COARSE), :]

    lane = lax.broadcasted_iota(jnp.int32, (SPAN, LANES), 1)
    head0 = lane < DH_B
    first_step = (j == 0).astype(jnp.int32)

    for g, (_, dil) in enumerate(SWA_PAIRS):
        blocks = rows // (SPAN * dil)
        per_half = blocks * dil // 2

        def unit(u, g=g, dil=dil):
            wb = u // dil
            r = u % dil
            q_start = wb * (SPAN * dil) + r
            k_start = SWA_PREV + (wb - 1) * (SPAN * dil) + r
            variant = jnp.where(wb == 0, first_step, 0)
            acc, den, mm = _swa_unit(q_ref[pl.ds(q_start, SPAN, stride=dil), :],
                                     kbuf[pl.ds(k_start, 2 * SPAN, stride=dil), :],
                                     vbuf[pl.ds(k_start, 2 * SPAN, stride=dil), :],
                                     bias_ref[0, g, variant], head0)
            acc_ref[g, pl.ds(q_start, SPAN, stride=dil), :] = acc
            l_ref[g, pl.ds(q_start, SPAN, stride=dil), :] = den
            m_ref[g, pl.ds(q_start, SPAN, stride=dil), :] = mm

        def unit_two_level(u, g=g, fine=dil // COARSE):
            c = u % COARSE
            f = u // COARSE
            pick = pl.ds(f, SPAN, stride=fine)
            acc, den, mm = _swa_unit(q4[c, pick, :],
                                     jnp.concatenate([k4[1 - slot, c, pick, :], k4[slot, c, pick, :]], axis=0),
                                     jnp.concatenate([v4[1 - slot, c, pick, :], v4[slot, c, pick, :]], axis=0),
                                     bias_ref[0, g, first_step], head0)
            dst = pl.ds(c * sub + f, SPAN, stride=fine)
            acc4[dst, :] = acc
            l4[dst, :] = den
            m4[dst, :] = mm

        two_level = dil % (COARSE * COARSE) == 0
        assert not two_level or blocks == 1
        body = unit_two_level if two_level else unit

        @pl.when(half >= 0)
        def _(body=body, per_half=per_half):
            for i in range(per_half):
                body(half * per_half + i)

        if two_level:
            @pl.when(half == 1)
            def _(g=g):
                for c in range(COARSE):
                    src, dst = slice(c * sub, (c + 1) * sub), pl.ds(c, sub, stride=COARSE)
                    acc_ref[g, dst, :] = acc4[src, :]
                    l_ref[g, dst, :] = l4[src, :]
                    m_ref[g, dst, :] = m4[src, :]

    @pl.when(half == 1)
    def _():
        tile = 256
        for r0 in range(0, rows, tile):
            sl = slice(r0, r0 + tile)
            ms = [m_ref[g, sl, :] for g in range(N_BRANCH)]
            mx = functools.reduce(jnp.maximum, ms)
            ws = [jnp.exp(mg - mx) for mg in ms]
            num = sum(w * acc_ref[g, sl, :] for g, w in enumerate(ws))
            den = sum(w * l_ref[g, sl, :] for g, w in enumerate(ws))
            o_ref[sl, :] = (num / den).astype(o_ref.dtype)


def _swa_sample_bias(n_new):
    t = jnp.arange(n_new)[:, None]
    slopes = _alibi_slopes()

    def table(j):
        dist = SWA_BUF + t - j
        out = []
        for (window, dil) in SWA_PAIRS:
            valid = (dist >= 0) & (dist <= window) & (dist % dil == 0) & (PAST_LEN + t - dist >= 0)
            pen = -slopes[:, None, None] * dist.astype(F32)[None]
            out.append(jnp.where(valid[None], pen, NEG_INF).reshape(H_B * n_new, -1))
        return jnp.stack(out)

    bias_a = table(jnp.arange(SWA_BUF)[None, :])
    jb = jnp.arange(LANES)[None, :]
    bias_b = jnp.where(jb < n_new, table(SWA_BUF + jb), NEG_INF)
    return bias_a, bias_b


def _pad_new_rows(new_ref, n_new):
    return jnp.concatenate([new_ref[...], jnp.zeros((LANES - n_new, new_ref.shape[1]), F32)], axis=0)


def _swa_sample_attend(q_ref, k_ref, v_ref, kp_ref, vp_ref, ba_ref, bb_ref, o_ref, n_new):
    width = q_ref.shape[1]
    n_heads = width // DH_B
    lane_h = lax.broadcasted_iota(jnp.int32, (n_new, width), 1) // DH_B
    q = q_ref[...] * (DH_B ** -0.5)
    qm = jnp.concatenate([jnp.where(lane_h == h, q, 0.0) for h in range(n_heads)], axis=0).astype(BF16)
    k_new, v_new = _pad_new_rows(k_ref, n_new), _pad_new_rows(v_ref, n_new)
    s_a = _dot(qm, kp_ref[0].astype(BF16))
    s_b = _dot_nt(qm, k_new.astype(BF16))
    sa = [s_a + ba_ref[g] for g in range(N_BRANCH)]
    sb = [s_b + bb_ref[g] for g in range(N_BRANCH)]
    mx = functools.reduce(jnp.maximum, [jnp.max(x, axis=-1, keepdims=True) for x in sa + sb])
    p_a = sum(jnp.exp(x - mx) for x in sa).astype(BF16)
    p_b = sum(jnp.exp(x - mx) for x in sb).astype(BF16)
    den = (jnp.sum(p_a.astype(F32), axis=-1, keepdims=True)
           + jnp.sum(p_b.astype(F32), axis=-1, keepdims=True))
    o = (_dot_nt(p_a, vp_ref[0].astype(BF16)) + _dot(p_b, v_new.astype(BF16))) / den
    o_ref[...] = sum(jnp.where(lane_h == h, o[h * n_new:(h + 1) * n_new], 0.0) for h in range(n_heads))

def _swa_window_update(src_ref, new_ref, dst_ref, n_new):
    lane = lax.broadcasted_iota(jnp.int32, (DH_B, LANES), 1)
    new_t = pltpu.roll(_pad_new_rows(new_ref, n_new).T, LANES - n_new, axis=1)
    for h in range(new_ref.shape[1] // DH_B):
        rows = slice(h * DH_B, (h + 1) * DH_B)
        shifted = pltpu.roll(src_ref[0, rows, :], SWA_BUF - n_new, axis=1)
        dst_ref[0, rows, 0:SWA_BUF - LANES] = shifted[:, 0:SWA_BUF - LANES]
        dst_ref[0, rows, SWA_BUF - LANES:] = jnp.where(lane >= LANES - n_new, new_t[rows],
                                                       shifted[:, SWA_BUF - LANES:])


N_SWA_IN, N_SWA_OUT = 11, 4


def _swa_kernel(*refs, n_new, n_cast):
    (q_ref, k_ref, v_ref, bias_ref, qs_ref, ks_ref, vs_ref, kp_ref, vp_ref, ba_ref, bb_ref), refs = (
        refs[:N_SWA_IN], refs[N_SWA_IN:])
    cast_src, refs = refs[:n_cast], refs[n_cast:]
    (o_ref, os_ref, ko_ref, vo_ref), refs = refs[:N_SWA_OUT], refs[N_SWA_OUT:]
    cast_dst, scratch = refs[:n_cast], refs[n_cast:]
    half = pl.program_id(3)
    _swa_prompt_half(half, q_ref, k_ref, v_ref, bias_ref, o_ref, *scratch)
    _swa_sample_attend(qs_ref, ks_ref, vs_ref, kp_ref, vp_ref, ba_ref, bb_ref, os_ref, n_new)
    _swa_window_update(kp_ref, ks_ref, ko_ref, n_new)
    _swa_window_update(vp_ref, vs_ref, vo_ref, n_new)

    @pl.when(half == 0)
    def _():
        for src, dst in zip(cast_src, cast_dst):
            dst[...] = src[...].astype(dst.dtype)


def _swa(qkv, bsz, seq, qkv_s, k_past, v_past, n_new, weights):
    t = qkv.shape[0]
    dbsz = k_past.shape[0]
    steps = seq // SWA_ROWS
    npair = H_B // 2
    width = H_B * DH_B
    half_w = width // 2
    assert dbsz == bsz * npair * steps, "one sample sequence per prompt (sequence, head pair, row block) step"
    bias = _swa_prompt_bias()
    bias_a, bias_b = _swa_sample_bias(n_new)
    blk = (SWA_ROWS, LANES)
    rows_of = lambda b, j: b * steps + j
    seq_of = lambda b, hp, j: (b * npair + hp) * steps + j
    new_blk = lambda col: pl.BlockSpec((n_new, half_w), lambda b, hp, j, h, col=col: (seq_of(b, hp, j), 2 * col + h))
    state_blk = pl.BlockSpec((1, half_w, SWA_BUF), lambda b, hp, j, h: (seq_of(b, hp, j), h, 0))
    half_rows = H_B // 2 * n_new
    sub = SWA_ROWS // COARSE
    assert all(w.shape[0] % (dbsz * 2 * SUBLANES) == 0 for w in weights), "bf16 row blocks are 16-row tiles"
    cast_blk = [pl.BlockSpec((w.shape[0] // dbsz, w.shape[1]), lambda b, hp, j, h: (seq_of(b, hp, j), 0))
                for w in weights]
    in_specs = [pl.BlockSpec(blk, lambda b, hp, j, h: (rows_of(b, j), hp)),
                pl.BlockSpec(blk, lambda b, hp, j, h: (rows_of(b, j), npair + hp)),
                pl.BlockSpec(blk, lambda b, hp, j, h: (rows_of(b, j), 2 * npair + hp)),
                pl.BlockSpec((1, N_BRANCH, 2, 2 * SPAN, 2 * SPAN), lambda b, hp, j, h: (hp, 0, 0, 0, 0),
                             pipeline_mode=pl.Buffered(1)),
                new_blk(0), new_blk(1), new_blk(2), state_blk, state_blk,
                pl.BlockSpec((N_BRANCH, half_rows, SWA_BUF), lambda b, hp, j, h: (0, h, 0)),
                pl.BlockSpec((N_BRANCH, half_rows, LANES), lambda b, hp, j, h: (0, h, 0))]
    assert len(in_specs) == N_SWA_IN
    return pl.pallas_call(
        functools.partial(_swa_kernel, n_new=n_new, n_cast=len(weights)),
        grid=(bsz, npair, steps, 2),
        in_specs=in_specs + cast_blk,
        out_specs=[pl.BlockSpec(blk, lambda b, hp, j, h: (rows_of(b, j), hp)),
                   pl.BlockSpec((n_new, half_w), lambda b, hp, j, h: (seq_of(b, hp, j), h)),
                   state_blk, state_blk] + cast_blk,
        out_shape=[jax.ShapeDtypeStruct((t, width), BF16),
                   jax.ShapeDtypeStruct((dbsz * n_new, width), F32),
                   jax.ShapeDtypeStruct(k_past.shape, F32), jax.ShapeDtypeStruct(v_past.shape, F32)]
                  + [jax.ShapeDtypeStruct(w.shape, BF16) for w in weights],
        scratch_shapes=[pltpu.VMEM((SWA_PREV + SWA_ROWS, LANES), F32)] * 2
                       + [pltpu.VMEM((N_BRANCH, SWA_ROWS, LANES), F32)] * 3
                       + [pltpu.VMEM((COARSE, sub, LANES), F32)]
                       + [pltpu.VMEM((2, COARSE, sub, LANES), F32)] * 2
                       + [pltpu.VMEM((SWA_ROWS, LANES), F32)] * 3,
        compiler_params=_params(("arbitrary",) * 4),
        name="swa",
    )(qkv, qkv, qkv, bias, qkv_s, qkv_s, qkv_s, k_past, v_past, bias_a, bias_b, *weights)


FFN_CHUNK = 256


def _ffn(y1, g_ffn_ref, w_in_ref, w_o_ref, acc_ref):
    d_ff = w_o_ref.shape[0]
    h = _rms(y1, g_ffn_ref[...]).astype(BF16)
    for ci, c0 in enumerate(range(0, d_ff, FFN_CHUNK)):
        gate = _dot(h, w_in_ref[:, c0:c0 + FFN_CHUNK])
        up = _dot(h, w_in_ref[:, d_ff + c0:d_ff + c0 + FFN_CHUNK])
        act = (gate * jax.nn.sigmoid(gate) * up).astype(BF16)
        part = _dot(act, w_o_ref[c0:c0 + FFN_CHUNK, :])
        if ci == 0:
            acc_ref[...] = part
        else:
            acc_ref[...] += part
    return y1 + acc_ref[...]


def _ab_tail_kernel(x_ref, a_ref, ob_ref, w_out_ref, g_ffn_ref, w_in_ref, w_o_ref, g_next_ref, perm_ref,
                    o_ref, u_ref, acc_ref, h_ref, z_ref, *, chunk):
    mix = (_dot(a_ref[...].astype(BF16), w_out_ref[0:VA_W, :])
           + _dot(ob_ref[...].astype(BF16), w_out_ref[VA_W:, :]))
    y = _ffn(x_ref[...] + mix, g_ffn_ref, w_in_ref, w_o_ref, acc_ref)
    o_ref[...] = y
    _group_rows(_rms(y, g_next_ref[...]), perm_ref, u_ref, h_ref, z_ref, chunk)


def _s5_tail_kernel(x_ref, yg_ref, perm_ref, g_mix_ref, dskip_ref, w_glu_ref, g_ffn_ref, w_in_ref, w_o_ref,
                    g_fin_ref, o_ref, acc_ref, z_ref, t_ref, *, chunk):
    x = x_ref[...]
    d = x.shape[-1]
    u = _rms(x, g_mix_ref[...])
    _ungroup_rows(yg_ref, perm_ref, z_ref, t_ref, chunk)
    ys = jnp.concatenate([t_ref[v] for v in range(d // LANES)], axis=1)
    z = jax.nn.gelu(ys + dskip_ref[...] * u, approximate=True).astype(BF16)
    val = _dot(z, w_glu_ref[:, 0:d])
    gate = _dot(z, w_glu_ref[:, d:2 * d])
    y2 = _ffn(x + val * jax.nn.sigmoid(gate), g_ffn_ref, w_in_ref, w_o_ref, acc_ref)
    o_ref[...] = _rms(y2, g_fin_ref[...])


def _row_spec(tm, width):
    return pl.BlockSpec((tm, width), lambda i: (i, 0))


def _group_scratch(tm, d, chunk):
    n_slab, nc, n_col = d // LANES, tm // chunk, chunk // SLAB_GROUPS
    assert min(S5_MM_ROWS, n_slab * n_col * nc) % nc == 0
    return (pltpu.VMEM((n_slab, tm, LANES), F32), pltpu.VMEM((n_slab * n_col * nc, SLAB_GROUPS * LANES), BF16))


def _group_spec(tm, d, chunk):
    return pl.BlockSpec((d // S5_GROUP, tm // chunk, chunk * S5_GROUP), lambda i: (0, i, 0))


def _layer_spec(stacked, layer):
    nd = stacked.ndim - 1
    return pl.BlockSpec((None,) + stacked.shape[1:], lambda *_: (layer,) + (0,) * nd, pipeline_mode=pl.Buffered(1))


def _ab_tail(x, a, ob, w_out, g_ffn, w_in, w_o, layer, g_next, perm, tm, chunk):
    t, d = x.shape
    h_scr, z_scr = _group_scratch(tm, d, chunk)
    return pl.pallas_call(
        functools.partial(_ab_tail_kernel, chunk=chunk),
        grid=(t // tm,),
        in_specs=[_row_spec(tm, d), _row_spec(tm, a.shape[1]), _row_spec(tm, ob.shape[1]),
                  _const_spec(w_out.shape), _const_spec((1, d)), _layer_spec(w_in, layer), _layer_spec(w_o, layer),
                  _const_spec((1, d)), _const_spec(perm.shape)],
        out_specs=[_row_spec(tm, d), _group_spec(tm, d, chunk)],
        out_shape=[jax.ShapeDtypeStruct((t, d), F32),
                   jax.ShapeDtypeStruct((d // S5_GROUP, t // chunk, chunk * S5_GROUP), BF16)],
        scratch_shapes=[pltpu.VMEM((tm, d), F32), h_scr, z_scr],
        compiler_params=_params(("arbitrary",)),
        name="ab_tail",
    )(x, a, ob, w_out, g_ffn, w_in, w_o, g_next, perm)


def _s5_tail(x, yg, perm_t, g_mix, dskip, w_glu, g_ffn, w_in, w_o, layer, g_fin, tm, chunk):
    t, d = x.shape
    t_scr, z_scr = _group_scratch(tm, d, chunk)
    return pl.pallas_call(
        functools.partial(_s5_tail_kernel, chunk=chunk),
        grid=(t // tm,),
        in_specs=[_row_spec(tm, d), _group_spec(tm, d, chunk), _const_spec(perm_t.shape),
                  _const_spec((1, d)), _const_spec((1, d)),
                  _const_spec(w_glu.shape), _const_spec((1, d)), _layer_spec(w_in, layer), _layer_spec(w_o, layer),
                  _const_spec((1, d))],
        out_specs=_row_spec(tm, d),
        out_shape=jax.ShapeDtypeStruct((t, d), F32),
        scratch_shapes=[pltpu.VMEM((tm, d), F32), z_scr, t_scr],
        compiler_params=_params(("arbitrary",)),
        name="s5_tail",
    )(x, yg, perm_t, g_mix, dskip, w_glu, g_ffn, w_in, w_o, g_fin)


SLAB_GROUPS = LANES // S5_GROUP
S5_MM_ROWS = 256


def _slab_perm():
    idx = jnp.arange(SLAB_GROUPS * LANES)
    l8, g8, p = idx // LANES, (idx % LANES) // S5_GROUP, idx % S5_GROUP
    dst = g8 * LANES + l8 * S5_GROUP + p
    return (dst[:, None] == idx[None, :]).astype(BF16)


def _permute_rows(z_ref, perm_ref, emit):
    step = min(S5_MM_ROWS, z_ref.shape[0])
    for r0 in range(0, z_ref.shape[0], step):
        emit(r0, _dot(z_ref[r0:r0 + step, :], perm_ref[...]))


def _group_rows(h, perm_ref, o_ref, h_ref, z_ref, chunk):
    rows = h.shape[0]
    nc = rows // chunk
    n_slab = h.shape[1] // LANES
    n_col = chunk // SLAB_GROUPS
    for v in range(n_slab):
        h_ref[v] = h[:, v * LANES:(v + 1) * LANES]
    for v in range(n_slab):
        for l in range(chunk):
            j, l8 = l // SLAB_GROUPS, l % SLAB_GROUPS
            r0 = (v * n_col + j) * nc
            z_ref[r0:r0 + nc, l8 * LANES:(l8 + 1) * LANES] = h_ref[v, pl.ds(l, nc, stride=chunk), :].astype(BF16)

    def emit(r0, blk):
        for q in range(blk.shape[0] // nc):
            v, j = divmod(r0 // nc + q, n_col)
            for g8 in range(SLAB_GROUPS):
                o_ref[v * SLAB_GROUPS + g8, :, j * LANES:(j + 1) * LANES] = (
                    blk[q * nc:(q + 1) * nc, g8 * LANES:(g8 + 1) * LANES].astype(o_ref.dtype))

    _permute_rows(z_ref, perm_ref, emit)


def _ungroup_rows(y_ref, perm_ref, z_ref, t_ref, chunk):
    n_slab, rows, _ = t_ref.shape
    nc = rows // chunk
    n_col = chunk // SLAB_GROUPS
    for v in range(n_slab):
        for j in range(n_col):
            r0 = (v * n_col + j) * nc
            for g8 in range(SLAB_GROUPS):
                z_ref[r0:r0 + nc, g8 * LANES:(g8 + 1) * LANES] = y_ref[v * SLAB_GROUPS + g8, :, j * LANES:(j + 1) * LANES]

    def emit(r0, blk):
        for q in range(blk.shape[0] // nc):
            v, j = divmod(r0 // nc + q, n_col)
            for l8 in range(SLAB_GROUPS):
                t_ref[v, pl.ds(j * SLAB_GROUPS + l8, nc, stride=chunk), :] = (
                    blk[q * nc:(q + 1) * nc, l8 * LANES:(l8 + 1) * LANES])

    _permute_rows(z_ref, perm_ref, emit)


S5_LP = S5_CHUNK * S5_GROUP
S5_RI = 2 * S5_STATE


def _s5_prep_kernel(*refs):
    for parity in range(2):
        _s5_prep_group(parity, *[r.at[parity] for r in refs])


def _s5_prep_group(parity, lam_row_ref, ls_ref, btr_ref, bti_ref, cr_ref, ci_ref,
                   bs_ref, cs_ref, tp_ref, ap_ref):
    n_pow = S5_CHUNK + 1
    dt = jnp.exp(ls_ref[...])

    def powers(lr, li):
        mag = jnp.exp(lr * dt)
        a_re, a_im = mag * jnp.cos(li * dt), mag * jnp.sin(li * dt)
        pw = [(jnp.ones_like(a_re), jnp.zeros_like(a_im))]
        for _ in range(n_pow - 1):
            pr, pi = pw[-1]
            pw.append((pr * a_re - pi * a_im, pr * a_im + pi * a_re))
        return pw

    lr, li = lam_row_ref[0:1, :], lam_row_ref[1:2, :]
    pw_row = powers(lr, li)
    a_re, a_im = pw_row[1]
    den = lr * lr + li * li
    f_re = ((a_re - 1.0) * lr + a_im * li) / den
    f_im = (a_im * lr - (a_re - 1.0) * li) / den
    blk = (S5_GROUP, S5_RI)
    w_re = jnp.concatenate([jnp.broadcast_to(f_re * pw_row[S5_CHUNK - 1 - l][0] - f_im * pw_row[S5_CHUNK - 1 - l][1], blk)
                            for l in range(S5_CHUNK)], axis=0)
    w_im = jnp.concatenate([jnp.broadcast_to(f_re * pw_row[S5_CHUNK - 1 - l][1] + f_im * pw_row[S5_CHUNK - 1 - l][0], blk)
                            for l in range(S5_CHUNK)], axis=0)

    lane = lax.broadcasted_iota(jnp.int32, (1, S5_RI), 1)
    own = (lane // S5_STATE) == parity
    low = lane < S5_STATE
    keep = lambda z: jnp.where(own, z, 0.0)
    per_pos = lambda ref: jnp.concatenate([ref[...]] * S5_CHUNK, axis=0)
    btr, bti = per_pos(btr_ref), per_pos(bti_ref)
    bs_ref[:, :S5_RI] = keep(w_re * btr - w_im * bti).astype(bs_ref.dtype)
    bs_ref[:, S5_RI:] = keep(w_re * bti + w_im * btr).astype(bs_ref.dtype)
    bf_t = jnp.where(low, f_re * btr[0:S5_GROUP] - f_im * bti[0:S5_GROUP],
                     f_re * bti[0:S5_GROUP] + f_im * btr[0:S5_GROUP])
    ap_ref[...] = jnp.zeros_like(ap_ref)
    for i, (j, part) in enumerate(((S5_CHUNK, 0), (S5_CHUNK, 1), (S5_CHUNK // 2, 0), (S5_CHUNK // 2, 1))):
        ap_ref[i:i + 1, :] = keep(pw_row[j][part])

    def spread(j0, part):
        return jnp.concatenate([jnp.broadcast_to(pw_row[j0 + l][part], blk) for l in range(S5_CHUNK)], axis=0)

    cr, ci = per_pos(cr_ref), per_pos(ci_ref)
    cs0 = jnp.where(low, cr * spread(0, 0) - ci * spread(0, 1),
                    -(cr * spread(0, 1) + ci * spread(0, 0)))
    cs_ref[:, :S5_RI] = keep(cr * spread(1, 0) - ci * spread(1, 1)).astype(cs_ref.dtype)
    cs_ref[:, S5_RI:] = keep(-(cr * spread(1, 1) + ci * spread(1, 0))).astype(cs_ref.dtype)
    r = lax.dot_general(bf_t, cs0, (((1,), (1,)), ((), ())), preferred_element_type=F32,
                        precision=lax.Precision.HIGHEST)
    lane_lp = lax.broadcasted_iota(jnp.int32, (S5_GROUP, S5_LP), 1)
    for l in range(S5_CHUNK):
        sh = l * S5_GROUP
        blk_l = r if l == 0 else jnp.where(lane_lp >= sh, pltpu.roll(r, sh, axis=1), 0.0)
        tp_ref[l * S5_GROUP:(l + 1) * S5_GROUP, :] = blk_l.astype(tp_ref.dtype)


def _s5_prep(lam_re, lam_im, log_step, b_re, b_im, c_re, c_im):
    g = lam_re.shape[0]
    dup = lambda z: jnp.concatenate([z, z], axis=-1)
    lam_row = jnp.stack([dup(lam_re), dup(lam_im)], axis=1)
    lam_row = jnp.pad(lam_row, ((0, 0), (0, 6), (0, 0)))
    ls = log_step.reshape(g, 1, 1)
    btr, bti = dup(jnp.swapaxes(b_re, 1, 2)), dup(jnp.swapaxes(b_im, 1, 2))
    cr, ci = dup(c_re), dup(c_im)
    gspec = lambda *s: pl.BlockSpec((2,) + s, lambda i: (i,) + (0,) * len(s))
    table = jax.ShapeDtypeStruct((g, S5_LP, 2 * S5_RI), BF16)
    return pl.pallas_call(
        _s5_prep_kernel,
        grid=(g // 2,),
        in_specs=[gspec(8, S5_RI), gspec(1, 1)] + [gspec(S5_GROUP, S5_RI)] * 4,
        out_specs=[gspec(S5_LP, 2 * S5_RI)] * 2 + [gspec(S5_LP, S5_LP), gspec(8, S5_RI)],
        out_shape=[table] * 2 + [jax.ShapeDtypeStruct((g, S5_LP, S5_LP), BF16),
                                 jax.ShapeDtypeStruct((g, 8, S5_RI), F32)],
        compiler_params=_params(("arbitrary",)),
        name="s5_prep",
    )(lam_row, ls, btr, bti, cr, ci)


def _s5_scan_kernel(u_ref, us_ref, x0r_ref, x0i_ref, bs_ref, cs_ref, tp_ref, ap_ref,
                    y_ref, ys_ref, finr_ref, fini_ref, finsr_ref, finsi_ref, xs_ref, *, bsz, chunks):
    n_rows = bsz * chunks
    pair = range(2)
    us_in = [u_ref[g] for g in pair]
    x = sum(_dot(us_in[g], bs_ref[g]) for g in pair)
    xr, xi = x[:, :S5_RI], x[:, S5_RI:]
    row = lax.broadcasted_iota(jnp.int32, (n_rows, 1), 0) % chunks
    ap = ap_ref[0] + ap_ref[1]
    ar, ai = ap[0:1, :], ap[1:2, :]

    def prefix(xr, xi, ar, ai, pos, length):
        shift = 1
        while shift < length:
            sr = jnp.where(pos >= shift, pltpu.roll(xr, shift, axis=0), 0.0)
            si = jnp.where(pos >= shift, pltpu.roll(xi, shift, axis=0), 0.0)
            xr, xi = xr + (sr * ar - si * ai), xi + (sr * ai + si * ar)
            ar, ai = ar * ar - ai * ai, 2.0 * (ar * ai)
            shift *= 2
        return xr, xi, ar, ai

    n_runs, runs = n_rows // SUBLANES, chunks // SUBLANES
    xr, xi, br, bi = prefix(xr, xi, ar, ai, row % SUBLANES, SUBLANES)
    xs_ref[0], xs_ref[1] = xr, xi
    last = pl.ds(SUBLANES - 1, n_runs, stride=SUBLANES)
    run = lax.broadcasted_iota(jnp.int32, (n_runs, 1), 0) % runs
    er, ei, _, _ = prefix(xs_ref[0, last, :], xs_ref[1, last, :], br, bi, run, runs)
    cr = jnp.where(run >= 1, pltpu.roll(er, 1, axis=0), 0.0)
    ci = jnp.where(run >= 1, pltpu.roll(ei, 1, axis=0), 0.0)
    pr, pi = ar, ai
    for r in range(SUBLANES):
        dst = pl.ds(r, n_runs, stride=SUBLANES)
        xs_ref[0, dst, :] = cr * pr - ci * pi
        xs_ref[1, dst, :] = cr * pi + ci * pr
        pr, pi = pr * ar - pi * ai, pr * ai + pi * ar
    xr, xi = xr + xs_ref[0], xi + xs_ref[1]
    pr = jnp.where(row >= 1, pltpu.roll(xr, 1, axis=0), 0.0).astype(BF16)
    pi = jnp.where(row >= 1, pltpu.roll(xi, 1, axis=0), 0.0).astype(BF16)
    prev = jnp.concatenate([pr, pi], axis=1)
    for g in pair:
        y_ref[g] = (_dot(us_in[g], tp_ref[g]) + _dot_nt(prev, cs_ref[g])).astype(y_ref.dtype)
    finr_ref[...] = jnp.zeros_like(finr_ref)
    fini_ref[...] = jnp.zeros_like(fini_ref)
    for b in range(bsz):
        last = slice((b + 1) * chunks - 1, (b + 1) * chunks)
        finr_ref[b:b + 1, :] = xr[last]
        fini_ref[b:b + 1, :] = xi[last]

    half = S5_LP // 2
    x0r, x0i = x0r_ref[...], x0i_ref[...]
    hr, hi = ap[2:3, :], ap[3:4, :]
    xs = sum(_dot(us_ref[g], bs_ref[g, half:, :]) for g in pair)
    finsr_ref[...] = x0r * hr - x0i * hi + xs[:, :S5_RI]
    finsi_ref[...] = x0r * hi + x0i * hr + xs[:, S5_RI:]
    x0 = jnp.concatenate([x0r, x0i], axis=1).astype(BF16)
    for g in pair:
        ys_ref[g] = (_dot(us_ref[g], tp_ref[g, :half, :half]) + _dot_nt(x0, cs_ref[g, :half, :])).astype(ys_ref.dtype)


def _s5_scan(u_g, us_g, x0r, x0i, bs, cs, tp, ap, bsz):
    g, n_rows, _ = u_g.shape
    n_s = us_g.shape[1]
    half = S5_LP // 2
    pspec = lambda *s: pl.BlockSpec((2,) + s, lambda i: (i,) + (0,) * len(s))
    ospec = lambda *s: pl.BlockSpec((None,) + s, lambda i: (i,) + (0,) * len(s))
    packed = lambda n: jax.ShapeDtypeStruct((g // 2, n, S5_RI), F32)
    return pl.pallas_call(
        functools.partial(_s5_scan_kernel, bsz=bsz, chunks=n_rows // bsz),
        grid=(g // 2,),
        in_specs=[pspec(n_rows, S5_LP), pspec(n_s, half), ospec(n_s, S5_RI), ospec(n_s, S5_RI)]
                 + [pspec(S5_LP, 2 * S5_RI)] * 2 + [pspec(S5_LP, S5_LP), pspec(8, S5_RI)],
        out_specs=[pspec(n_rows, S5_LP), pspec(n_s, half), ospec(8, S5_RI), ospec(8, S5_RI),
                   ospec(n_s, S5_RI), ospec(n_s, S5_RI)],
        out_shape=[jax.ShapeDtypeStruct((g, n_rows, S5_LP), BF16), jax.ShapeDtypeStruct((g, n_s, half), BF16),
                   packed(8), packed(8), packed(n_s), packed(n_s)],
        scratch_shapes=[pltpu.VMEM((2, n_rows, S5_RI), F32)],
        compiler_params=_params(("arbitrary",)),
        name="s5_scan",
    )(u_g, us_g, x0r, x0i, bs, cs, tp, ap)


def kernel(x_prompt, x_sample, state_ret, state_swa_k, state_swa_v, state_ssm_re, state_ssm_im, norm_mix, norm_ffn, norm_final, w_in_ab, ret_gn, w_out_ab, ssm_lam_re, ssm_lam_im, ssm_log_step, ssm_b_re, ssm_b_im, ssm_c_re, ssm_c_im, ssm_d, w_glu, w_ffn_in, w_ffn_out):
    bsz, seq, d = x_prompt.shape
    dbsz, n_new, _ = x_sample.shape
    assert state_swa_k.shape[2] == SWA_BUF and n_new == S5_CHUNK // 2 and seq % SWA_ROWS == 0
    xp = x_prompt.reshape(bsz * seq, d)
    xs = x_sample.reshape(dbsz * n_new, d)
    tm_p, tm_s = TAIL_ROWS, dbsz * n_new
    row = lambda v: v.reshape(1, -1)

    w_in0 = w_in_ab[0].astype(BF16)
    g_mix0, g_ffn0, gn0 = row(norm_mix[0]), row(norm_ffn[0]), row(ret_gn[0])

    ra_p, qkv_p = _ab_proj(xp, g_mix0, w_in0, PROJ_ROWS)
    ra_s, qkv_s = _ab_proj(xs, g_mix0, w_in0, tm_s)

    zero_ret = jnp.zeros((bsz, H_A, DK_A, DV_A), F32)
    a_p, ret_p = _retention(ra_p, zero_ret, gn0, seq=seq, c_real=RET_CHUNK, n_chunks=RET_STEP_CHUNKS, n_seq=bsz,
                            out_dtype=BF16)
    a_s, ret_s = _retention(ra_s, state_ret[0], gn0, seq=n_new, c_real=n_new, n_chunks=1, n_seq=RET_SAMPLE_SEQS,
                            out_dtype=F32)

    width_b = H_B * DH_B
    rows_last = lambda w: w.transpose(0, 2, 3, 1).reshape(dbsz, width_b, SWA_BUF)
    rows_first = lambda w: w.reshape(dbsz, H_B, DH_B, SWA_BUF).transpose(0, 3, 1, 2)[None]
    later = (w_out_ab[0], w_ffn_in.reshape(-1, w_ffn_in.shape[-1]), w_ffn_out.reshape(-1, w_ffn_out.shape[-1]), w_glu[0])
    ob_p, ob_s, swk_s, swv_s, w_out0, w_f_in, w_f_out, w_glu0 = _swa(
        qkv_p, bsz, seq, qkv_s, rows_last(state_swa_k[0]), rows_last(state_swa_v[0]), n_new, later)
    w_f_in, w_f_out = w_f_in.reshape(w_ffn_in.shape), w_f_out.reshape(w_ffn_out.shape)
    kv_tail = qkv_p.reshape(bsz, seq, QKV_W)[:, seq - SWA_BUF:, width_b:]
    swk_p = kv_tail[..., :width_b].reshape(bsz, SWA_BUF, H_B, DH_B)
    swv_p = kv_tail[..., width_b:].reshape(bsz, SWA_BUF, H_B, DH_B)

    g_mix1, g_ffn1 = row(norm_mix[1]), row(norm_ffn[1])
    perm = _slab_perm()
    yp, u_p = _ab_tail(xp, a_p, ob_p, w_out0, g_ffn0, w_f_in, w_f_out, 0, g_mix1, perm, tm_p, S5_CHUNK)
    ys, u_s = _ab_tail(xs, a_s, ob_s, w_out0, g_ffn0, w_f_in, w_f_out, 0, g_mix1, perm, tm_s, n_new)

    operators = _s5_prep(ssm_lam_re[0], ssm_lam_im[0], ssm_log_step[0],
                         ssm_b_re[0], ssm_b_im[0], ssm_c_re[0], ssm_c_im[0])
    pack = lambda st: st.reshape(st.shape[0], -1, S5_RI).transpose(1, 0, 2)
    unpack = lambda st: st.transpose(1, 0, 2).reshape(st.shape[1], -1, S5_STATE)[None]
    y5_p, y5_s, finr_p, fini_p, finr_s, fini_s = _s5_scan(
        u_p, u_s, pack(state_ssm_re[0]), pack(state_ssm_im[0]), *operators, bsz)

    tail1 = (perm.T, g_mix1, row(ssm_d[0]), w_glu0, g_ffn1,
             w_f_in, w_f_out, 1, row(norm_final))
    yp = _s5_tail(yp, y5_p, *tail1, tm_p, S5_CHUNK)
    ys = _s5_tail(ys, y5_s, *tail1, tm_s, n_new)

    return (yp.reshape(bsz, seq, d), ys.reshape(dbsz, n_new, d),
            ret_p[None], ret_s[None],
            swk_p[None], swv_p[None],
            rows_first(swk_s), rows_first(swv_s),
            unpack(finr_p[:, :bsz]), unpack(fini_p[:, :bsz]), unpack(finr_s), unpack(fini_s))
```

```python
import functools

import jax
import jax.numpy as jnp
from jax import lax
from jax.experimental import pallas as pl
from jax.experimental.pallas import tpu as pltpu

F32 = jnp.float32
BF16 = jnp.bfloat16

H_A, DK_A, DV_A = 4, 64, 128
H_B, DH_B = 8, 64
SWA_PAIRS = ((128, 1), (512, 4), (2048, 16))
SPAN = 128
SWA_BUF = 2048
PAST_LEN = 16384
RET_CHUNK = 128
S5_GROUP, S5_STATE = 16, 64
S5_CHUNK = 16
EPS = 1e-6
NEG_INF = -1e30
QA_W, KA_W, VA_W, GA_W = H_A * DK_A, H_A * DK_A, H_A * DV_A, H_A * DV_A
RA_W = QA_W + KA_W + VA_W + GA_W
QKV_W = 3 * H_B * DH_B

LANES = 128
SUBLANES = 8
VMEM_LIMIT = 58 * 1024 * 1024

PROJ_ROWS = 1024
PROJ_COLS = 512
TAIL_ROWS = 512
RET_STEP_CHUNKS = 4
RET_SAMPLE_SEQS = 4


def _params(sem):
    return pltpu.CompilerParams(dimension_semantics=sem, vmem_limit_bytes=VMEM_LIMIT)


def _const_spec(shape):
    nd = len(shape)
    return pl.BlockSpec(shape, lambda *_: (0,) * nd, pipeline_mode=pl.Buffered(1))


def _rms(x, g):
    return x * lax.rsqrt(jnp.mean(x * x, axis=-1, keepdims=True) + EPS) * g


def _dot(a, b):
    return jnp.dot(a, b, preferred_element_type=F32)


def _dot_nt(a, b):
    return lax.dot_general(a, b, (((1,), (1,)), ((), ())), preferred_element_type=F32)


def _dot_tn(a, b):
    return lax.dot_general(a, b, (((0,), (0,)), ((), ())), preferred_element_type=F32)


def _ab_proj_kernel(x_ref, g_ref, w_ref, ra_ref, qkv_ref):
    h = _rms(x_ref[...], g_ref[...]).astype(BF16)
    for n0 in range(0, RA_W, PROJ_COLS):
        ra_ref[:, n0:n0 + PROJ_COLS] = _dot(h, w_ref[:, n0:n0 + PROJ_COLS])
    for n0 in range(0, QKV_W, PROJ_COLS):
        qkv_ref[:, n0:n0 + PROJ_COLS] = _dot(h, w_ref[:, RA_W + n0:RA_W + n0 + PROJ_COLS])


def _ab_proj(x, g, w_bf16, tm):
    t, d = x.shape
    return pl.pallas_call(
        _ab_proj_kernel,
        grid=(t // tm,),
        in_specs=[pl.BlockSpec((tm, d), lambda i: (i, 0)),
                  _const_spec((1, d)),
                  _const_spec((d, RA_W + QKV_W))],
        out_specs=[pl.BlockSpec((tm, RA_W), lambda i: (i, 0)),
                   pl.BlockSpec((tm, QKV_W), lambda i: (i, 0))],
        out_shape=[jax.ShapeDtypeStruct((t, RA_W), F32), jax.ShapeDtypeStruct((t, QKV_W), F32)],
        compiler_params=_params(("arbitrary",)),
        name="ab_proj",
    )(x, g, w_bf16)


def _retention_tables(c_real):
    c = RET_CHUNK
    log_g = jnp.log1p(-jnp.exp2(-5.0 - jnp.arange(H_A, dtype=F32)))
    idx = jnp.arange(c, dtype=F32)
    rel = idx[:, None] - idx[None, :]
    dec = jnp.where(rel >= 0, jnp.exp(jnp.maximum(rel, 0.0)[None] * log_g[:, None, None]), 0.0)
    dec = dec.reshape(H_A * c, c)
    real = (idx < c_real)[:, None]
    qd = jnp.exp((idx + 1.0)[:, None] * log_g[None, :])
    qd = jnp.repeat(qd, DV_A, axis=1)
    kd = jnp.where(real, jnp.exp((c_real - 1.0 - idx)[:, None] * log_g[None, :]), 0.0)
    kd = jnp.repeat(kd, DK_A, axis=1)
    row_h = jnp.arange(H_A * DK_A)[:, None] // DK_A
    col_h = jnp.arange(H_A * DV_A)[None, :] // DV_A
    bd = (row_h == col_h).astype(F32)
    dm = bd * jnp.repeat(jnp.exp(c_real * log_g), DV_A)[None, :]
    return dec, qd, kd, dm


def _retention_kernel(ra_ref, s0_ref, gain_ref, dec_ref, qd_ref, kd_ref, dm_ref,
                      a_ref, sout_ref, sbd_ref, *, c_real, n_chunks, n_seq):
    c = RET_CHUNK
    j = pl.program_id(1)

    @pl.when(j == 0)
    def _():
        sbd_ref[...] = jnp.zeros_like(sbd_ref)
        for s in range(n_seq):
            for h in range(H_A):
                sbd_ref[s, h * DK_A:(h + 1) * DK_A, h * DV_A:(h + 1) * DV_A] = s0_ref[s, h]

    lane_q = lax.broadcasted_iota(jnp.int32, (c, QA_W), 1) // DK_A
    for ci in range(n_chunks):
        for s in range(n_seq):
            _retention_chunk(ra_ref.at[s], a_ref.at[s], sbd_ref.at[s], gain_ref, dec_ref, qd_ref, kd_ref, dm_ref,
                             lane_q, ci, c_real)

    @pl.when(j == pl.num_programs(1) - 1)
    def _():
        for s in range(n_seq):
            for h in range(H_A):
                sout_ref[s, h] = sbd_ref[s, h * DK_A:(h + 1) * DK_A, h * DV_A:(h + 1) * DV_A]


def _retention_chunk(ra_ref, a_ref, sbd_ref, gain_ref, dec_ref, qd_ref, kd_ref, dm_ref, lane_q, ci, c_real):
    c = RET_CHUNK
    rows = ra_ref[ci * c_real:(ci + 1) * c_real, :]
    gate = rows[:, QA_W + KA_W + VA_W:]
    if c_real < c:
        rows = jnp.concatenate([rows, jnp.zeros((c - c_real, RA_W), F32)], axis=0)
    q = rows[:, :QA_W]
    k = rows[:, QA_W:QA_W + KA_W] * (DK_A ** -0.5)
    v = rows[:, QA_W + KA_W:QA_W + KA_W + VA_W]
    vb = v.astype(BF16)
    qm = jnp.concatenate([jnp.where(lane_q == h, q, 0.0) for h in range(H_A)], axis=0).astype(BF16)
    s = _dot_nt(qm, k.astype(BF16)) * dec_ref[...]
    sb = s.astype(BF16)
    sbd = sbd_ref[...]
    cross = _dot(q.astype(BF16), sbd.astype(BF16)) * qd_ref[...]
    upd = _dot_tn((k * kd_ref[...]).astype(BF16), vb)
    for h in range(H_A):
        blk = (slice(h * DK_A, (h + 1) * DK_A), slice(h * DV_A, (h + 1) * DV_A))
        sbd_ref[blk] = sbd[blk] * dm_ref[blk] + upd[blk]
    for h in range(H_A):
        sl = slice(h * DV_A, (h + 1) * DV_A)
        o = _dot(sb[h * c:(h + 1) * c], vb[:, sl]) + cross[:, sl]
        o = o[:c_real]
        mu = jnp.mean(o, axis=-1, keepdims=True)
        var = jnp.mean(jnp.square(o - mu), axis=-1, keepdims=True)
        y = (o - mu) * lax.rsqrt(var + EPS) * gain_ref[:, sl]
        gh = gate[:, sl]
        a_ref[ci * c_real:(ci + 1) * c_real, sl] = (gh * jax.nn.sigmoid(gh) * y).astype(a_ref.dtype)


def _retention(ra, state0, gain, *, seq, c_real, n_chunks, n_seq, out_dtype):
    t = ra.shape[0]
    bsz = t // seq
    rows = c_real * n_chunks
    steps = seq // rows
    tables = _retention_tables(c_real)
    kern = functools.partial(_retention_kernel, c_real=c_real, n_chunks=n_chunks, n_seq=n_seq)
    a, s_out = pl.pallas_call(
        kern,
        grid=(bsz // n_seq, steps),
        in_specs=[pl.BlockSpec((n_seq, rows, RA_W), lambda b, j: (b, j, 0)),
                  pl.BlockSpec((n_seq, H_A, DK_A, DV_A), lambda b, j: (b, 0, 0, 0)),
                  _const_spec((1, VA_W))] + [_const_spec(tb.shape) for tb in tables],
        out_specs=[pl.BlockSpec((n_seq, rows, VA_W), lambda b, j: (b, j, 0)),
                   pl.BlockSpec((n_seq, H_A, DK_A, DV_A), lambda b, j: (b, 0, 0, 0))],
        out_shape=[jax.ShapeDtypeStruct((bsz, seq, VA_W), out_dtype),
                   jax.ShapeDtypeStruct((bsz, H_A, DK_A, DV_A), F32)],
        scratch_shapes=[pltpu.VMEM((n_seq, H_A * DK_A, H_A * DV_A), F32)],
        compiler_params=_params(("arbitrary", "arbitrary")),
        name="retention",
    )(ra.reshape(bsz, seq, RA_W), state0, gain, *tables)
    return a.reshape(t, VA_W), s_out


SWA_ROWS = 2048
N_BRANCH = len(SWA_PAIRS)


def _alibi_slopes():
    return jnp.exp2(-8.0 * jnp.arange(1, H_B + 1, dtype=F32) / H_B)


def _swa_prompt_bias():
    qi = jnp.arange(SPAN)[:, None]
    kj = jnp.arange(2 * SPAN)[None, :]
    dist = SPAN + qi - kj
    band = (dist >= 0) & (dist <= SPAN)
    slopes = _alibi_slopes()
    out = []
    for (_, dil) in SWA_PAIRS:
        pen = -slopes[:, None, None] * (dil * dist).astype(F32)[None]
        normal = jnp.where(band[None], pen, NEG_INF)
        first = jnp.where((band & (kj >= SPAN))[None], pen, NEG_INF)
        out.append(jnp.stack([normal, first], axis=1))
    tab = jnp.stack(out, axis=1)
    tab = tab.reshape(H_B // 2, 2, N_BRANCH, 2, SPAN, 2 * SPAN).transpose(0, 2, 3, 1, 4, 5)
    return tab.reshape(H_B // 2, N_BRANCH, 2, 2 * SPAN, 2 * SPAN)


COARSE = 4


def _swa_unit(q, kk, vv, bias, head0):
    q = q * (DH_B ** -0.5)
    qm = jnp.concatenate([jnp.where(head0, q, 0.0), jnp.where(head0, 0.0, q)], axis=0).astype(BF16)
    s = _dot_nt(qm, kk.astype(BF16)) + bias
    m = jnp.max(s, axis=-1, keepdims=True)
    p = jnp.exp(s - m).astype(BF16)
    ones = jnp.ones((2 * SPAN, LANES), BF16)
    res = _dot(p, jnp.concatenate([vv.astype(BF16), ones], axis=1))
    acc = jnp.where(head0, res[:SPAN, :LANES], res[SPAN:, :LANES])
    den = jnp.where(head0, res[:SPAN, LANES:], res[SPAN:, LANES:])
    mm = jnp.where(head0, jnp.broadcast_to(m[:SPAN], (SPAN, LANES)), jnp.broadcast_to(m[SPAN:], (SPAN, LANES)))
    return acc, den, mm


SWA_PREV = SPAN * COARSE


def _swa_prompt_half(half, q_ref, k_ref, v_ref, bias_ref, o_ref, kbuf, vbuf, acc_ref, l_ref, m_ref,
                     q4, k4, v4, acc4, l4, m4):
    j = pl.program_id(2)
    rows = SWA_ROWS
    sub = rows // COARSE
    slot = j % 2

    @pl.when((half == 0) & (j > 0))
    def _():
        kbuf[0:SWA_PREV, :] = kbuf[rows:rows + SWA_PREV, :]
        vbuf[0:SWA_PREV, :] = vbuf[rows:rows + SWA_PREV, :]

    @pl.when((half == 0) & (j == 0))
    def _():
        k4[1] = jnp.zeros(k4.shape[1:], F32)
        v4[1] = jnp.zeros(v4.shape[1:], F32)
        kbuf[0:SWA_PREV, :] = jnp.zeros((SWA_PREV, LANES), F32)
        vbuf[0:SWA_PREV, :] = jnp.zeros((SWA_PREV, LANES), F32)

    @pl.when(half == 0)
    def _():
        kbuf[SWA_PREV:, :] = k_ref[...]
        vbuf[SWA_PREV:, :] = v_ref[...]
        for c in range(COARSE):
            q4[c] = q_ref[pl.ds(c, sub, stride=COARSE), :]
            k4[slot, c] = k_ref[pl.ds(c, sub, stride=COARSE), :]
            v4[slot, c] = v_ref[pl.ds(c, sub, stride=COARSE), :]

    lane = lax.broadcasted_iota(jnp.int32, (SPAN, LANES), 1)
    head0 = lane < DH_B
    first_step = (j == 0).astype(jnp.int32)

    for g, (_, dil) in enumerate(SWA_PAIRS):
        blocks = rows // (SPAN * dil)
        per_half = blocks * dil // 2

        def unit(u, g=g, dil=dil):
            wb = u // dil
            r = u % dil
            q_start = wb * (SPAN * dil) + r
            k_start = SWA_PREV + (wb - 1) * (SPAN * dil) + r
            variant = jnp.where(wb == 0, first_step, 0)
            acc, den, mm = _swa_unit(q_ref[pl.ds(q_start, SPAN, stride=dil), :],
                                     kbuf[pl.ds(k_start, 2 * SPAN, stride=dil), :],
                                     vbuf[pl.ds(k_start, 2 * SPAN, stride=dil), :],
                                     bias_ref[0, g, variant], head0)
            acc_ref[g, pl.ds(q_start, SPAN, stride=dil), :] = acc
            l_ref[g, pl.ds(q_start, SPAN, stride=dil), :] = den
            m_ref[g, pl.ds(q_start, SPAN, stride=dil), :] = mm

        def unit_two_level(u, g=g, fine=dil // COARSE):
            c = u % COARSE
            f = u // COARSE
            pick = pl.ds(f, SPAN, stride=fine)
            acc, den, mm = _swa_unit(q4[c, pick, :],
                                     jnp.concatenate([k4[1 - slot, c, pick, :], k4[slot, c, pick, :]], axis=0),
                                     jnp.concatenate([v4[1 - slot, c, pick, :], v4[slot, c, pick, :]], axis=0),
                                     bias_ref[0, g, first_step], head0)
            dst = pl.ds(c * sub + f, SPAN, stride=fine)
            acc4[dst, :] = acc
            l4[dst, :] = den
            m4[dst, :] = mm

        two_level = dil % (COARSE * COARSE) == 0
        assert not two_level or blocks == 1
        body = unit_two_level if two_level else unit

        @pl.when(half >= 0)
        def _(body=body, per_half=per_half):
            for i in range(per_half):
                body(half * per_half + i)

        if two_level:
            @pl.when(half == 1)
            def _(g=g):
                for c in range(COARSE):
                    src, dst = slice(c * sub, (c + 1) * sub), pl.ds(c, sub, stride=COARSE)
                    acc_ref[g, dst, :] = acc4[src, :]
                    l_ref[g, dst, :] = l4[src, :]
                    m_ref[g, dst, :] = m4[src, :]

    @pl.when(half == 1)
    def _():
        tile = 256
        for r0 in range(0, rows, tile):
            sl = slice(r0, r0 + tile)
            ms = [m_ref[g, sl, :] for g in range(N_BRANCH)]
            mx = functools.reduce(jnp.maximum, ms)
            ws = [jnp.exp(mg - mx) for mg in ms]
            num = sum(w * acc_ref[g, sl, :] for g, w in enumerate(ws))
            den = sum(w * l_ref[g, sl, :] for g, w in enumerate(ws))
            o_ref[sl, :] = (num / den).astype(o_ref.dtype)


def _swa_sample_bias(n_new):
    t = jnp.arange(n_new)[:, None]
    slopes = _alibi_slopes()

    def table(j):
        dist = SWA_BUF + t - j
        out = []
        for (window, dil) in SWA_PAIRS:
            valid = (dist >= 0) & (dist <= window) & (dist % dil == 0) & (PAST_LEN + t - dist >= 0)
            pen = -slopes[:, None, None] * dist.astype(F32)[None]
            out.append(jnp.where(valid[None], pen, NEG_INF).reshape(H_B * n_new, -1))
        return jnp.stack(out)

    bias_a = table(jnp.arange(SWA_BUF)[None, :])
    jb = jnp.arange(LANES)[None, :]
    bias_b = jnp.where(jb < n_new, table(SWA_BUF + jb), NEG_INF)
    return bias_a, bias_b


def _pad_new_rows(new_ref, n_new):
    return jnp.concatenate([new_ref[...], jnp.zeros((LANES - n_new, new_ref.shape[1]), F32)], axis=0)


def _swa_sample_attend(q_ref, k_ref, v_ref, kp_ref, vp_ref, ba_ref, bb_ref, o_ref, n_new):
    width = q_ref.shape[1]
    n_heads = width // DH_B
    lane_h = lax.broadcasted_iota(jnp.int32, (n_new, width), 1) // DH_B
    q = q_ref[...] * (DH_B ** -0.5)
    qm = jnp.concatenate([jnp.where(lane_h == h, q, 0.0) for h in range(n_heads)], axis=0).astype(BF16)
    k_new, v_new = _pad_new_rows(k_ref, n_new), _pad_new_rows(v_ref, n_new)
    s_a = _dot(qm, kp_ref[0].astype(BF16))
    s_b = _dot_nt(qm, k_new.astype(BF16))
    sa = [s_a + ba_ref[g] for g in range(N_BRANCH)]
    sb = [s_b + bb_ref[g] for g in range(N_BRANCH)]
    mx = functools.reduce(jnp.maximum, [jnp.max(x, axis=-1, keepdims=True) for x in sa + sb])
    p_a = sum(jnp.exp(x - mx) for x in sa).astype(BF16)
    p_b = sum(jnp.exp(x - mx) for x in sb).astype(BF16)
    den = (jnp.sum(p_a.astype(F32), axis=-1, keepdims=True)
           + jnp.sum(p_b.astype(F32), axis=-1, keepdims=True))
    o = (_dot_nt(p_a, vp_ref[0].astype(BF16)) + _dot(p_b, v_new.astype(BF16))) / den
    o_ref[...] = sum(jnp.where(lane_h == h, o[h * n_new:(h + 1) * n_new], 0.0) for h in range(n_heads))

def _swa_window_update(src_ref, new_ref, dst_ref, n_new):
    lane = lax.broadcasted_iota(jnp.int32, (DH_B, LANES), 1)
    new_t = pltpu.roll(_pad_new_rows(new_ref, n_new).T, LANES - n_new, axis=1)
    for h in range(new_ref.shape[1] // DH_B):
        rows = slice(h * DH_B, (h + 1) * DH_B)
        shifted = pltpu.roll(src_ref[0, rows, :], SWA_BUF - n_new, axis=1)
        dst_ref[0, rows, 0:SWA_BUF - LANES] = shifted[:, 0:SWA_BUF - LANES]
        dst_ref[0, rows, SWA_BUF - LANES:] = jnp.where(lane >= LANES - n_new, new_t[rows],
                                                       shifted[:, SWA_BUF - LANES:])


N_SWA_IN, N_SWA_OUT = 11, 4


def _swa_kernel(*refs, n_new, n_cast):
    (q_ref, k_ref, v_ref, bias_ref, qs_ref, ks_ref, vs_ref, kp_ref, vp_ref, ba_ref, bb_ref), refs = (
        refs[:N_SWA_IN], refs[N_SWA_IN:])
    cast_src, refs = refs[:n_cast], refs[n_cast:]
    (o_ref, os_ref, ko_ref, vo_ref), refs = refs[:N_SWA_OUT], refs[N_SWA_OUT:]
    cast_dst, scratch = refs[:n_cast], refs[n_cast:]
    half = pl.program_id(3)
    _swa_prompt_half(half, q_ref, k_ref, v_ref, bias_ref, o_ref, *scratch)
    _swa_sample_attend(qs_ref, ks_ref, vs_ref, kp_ref, vp_ref, ba_ref, bb_ref, os_ref, n_new)
    _swa_window_update(kp_ref, ks_ref, ko_ref, n_new)
    _swa_window_update(vp_ref, vs_ref, vo_ref, n_new)

    @pl.when(half == 0)
    def _():
        for src, dst in zip(cast_src, cast_dst):
            dst[...] = src[...].astype(dst.dtype)


def _swa(qkv, bsz, seq, qkv_s, k_past, v_past, n_new, weights):
    t = qkv.shape[0]
    dbsz = k_past.shape[0]
    steps = seq // SWA_ROWS
    npair = H_B // 2
    width = H_B * DH_B
    half_w = width // 2
    assert dbsz == bsz * npair * steps, "one sample sequence per prompt (sequence, head pair, row block) step"
    bias = _swa_prompt_bias()
    bias_a, bias_b = _swa_sample_bias(n_new)
    blk = (SWA_ROWS, LANES)
    rows_of = lambda b, j: b * steps + j
    seq_of = lambda b, hp, j: (b * npair + hp) * steps + j
    new_blk = lambda col: pl.BlockSpec((n_new, half_w), lambda b, hp, j, h, col=col: (seq_of(b, hp, j), 2 * col + h))
    state_blk = pl.BlockSpec((1, half_w, SWA_BUF), lambda b, hp, j, h: (seq_of(b, hp, j), h, 0))
    half_rows = H_B // 2 * n_new
    sub = SWA_ROWS // COARSE
    assert all(w.shape[0] % (dbsz * 2 * SUBLANES) == 0 for w in weights), "bf16 row blocks are 16-row tiles"
    cast_blk = [pl.BlockSpec((w.shape[0] // dbsz, w.shape[1]), lambda b, hp, j, h: (seq_of(b, hp, j), 0))
                for w in weights]
    in_specs = [pl.BlockSpec(blk, lambda b, hp, j, h: (rows_of(b, j), hp)),
                pl.BlockSpec(blk, lambda b, hp, j, h: (rows_of(b, j), npair + hp)),
                pl.BlockSpec(blk, lambda b, hp, j, h: (rows_of(b, j), 2 * npair + hp)),
                pl.BlockSpec((1, N_BRANCH, 2, 2 * SPAN, 2 * SPAN), lambda b, hp, j, h: (hp, 0, 0, 0, 0)),
                new_blk(0), new_blk(1), new_blk(2), state_blk, state_blk,
                pl.BlockSpec((N_BRANCH, half_rows, SWA_BUF), lambda b, hp, j, h: (0, h, 0)),
                pl.BlockSpec((N_BRANCH, half_rows, LANES), lambda b, hp, j, h: (0, h, 0))]
    assert len(in_specs) == N_SWA_IN
    return pl.pallas_call(
        functools.partial(_swa_kernel, n_new=n_new, n_cast=len(weights)),
        grid=(bsz, npair, steps, 2),
        in_specs=in_specs + cast_blk,
        out_specs=[pl.BlockSpec(blk, lambda b, hp, j, h: (rows_of(b, j), hp)),
                   pl.BlockSpec((n_new, half_w), lambda b, hp, j, h: (seq_of(b, hp, j), h)),
                   state_blk, state_blk] + cast_blk,
        out_shape=[jax.ShapeDtypeStruct((t, width), BF16),
                   jax.ShapeDtypeStruct((dbsz * n_new, width), F32),
                   jax.ShapeDtypeStruct(k_past.shape, F32), jax.ShapeDtypeStruct(v_past.shape, F32)]
                  + [jax.ShapeDtypeStruct(w.shape, BF16) for w in weights],
        scratch_shapes=[pltpu.VMEM((SWA_PREV + SWA_ROWS, LANES), F32)] * 2
                       + [pltpu.VMEM((N_BRANCH, SWA_ROWS, LANES), F32)] * 3
                       + [pltpu.VMEM((COARSE, sub, LANES), F32)]
                       + [pltpu.VMEM((2, COARSE, sub, LANES), F32)] * 2
                       + [pltpu.VMEM((SWA_ROWS, LANES), F32)] * 3,
        compiler_params=_params(("arbitrary",) * 4),
        name="swa",
    )(qkv, qkv, qkv, bias, qkv_s, qkv_s, qkv_s, k_past, v_past, bias_a, bias_b, *weights)


FFN_CHUNK = 256


def _ffn(y1, g_ffn_ref, w_in_ref, w_o_ref, acc_ref):
    d_ff = w_o_ref.shape[0]
    h = _rms(y1, g_ffn_ref[...]).astype(BF16)
    for ci, c0 in enumerate(range(0, d_ff, FFN_CHUNK)):
        gate = _dot(h, w_in_ref[:, c0:c0 + FFN_CHUNK])
        up = _dot(h, w_in_ref[:, d_ff + c0:d_ff + c0 + FFN_CHUNK])
        act = (gate * jax.nn.sigmoid(gate) * up).astype(BF16)
        part = _dot(act, w_o_ref[c0:c0 + FFN_CHUNK, :])
        if ci == 0:
            acc_ref[...] = part
        else:
            acc_ref[...] += part
    return y1 + acc_ref[...]


def _ab_tail_kernel(x_ref, a_ref, ob_ref, w_out_ref, g_ffn_ref, w_in_ref, w_o_ref, g_next_ref, perm_ref,
                    o_ref, u_ref, acc_ref, h_ref, z_ref, *, chunk):
    mix = (_dot(a_ref[...].astype(BF16), w_out_ref[0:VA_W, :])
           + _dot(ob_ref[...].astype(BF16), w_out_ref[VA_W:, :]))
    y = _ffn(x_ref[...] + mix, g_ffn_ref, w_in_ref, w_o_ref, acc_ref)
    o_ref[...] = y
    _group_rows(_rms(y, g_next_ref[...]), perm_ref, u_ref, h_ref, z_ref, chunk)


def _s5_tail_kernel(x_ref, yg_ref, perm_ref, g_mix_ref, dskip_ref, w_glu_ref, g_ffn_ref, w_in_ref, w_o_ref,
                    g_fin_ref, o_ref, acc_ref, z_ref, t_ref, *, chunk):
    x = x_ref[...]
    d = x.shape[-1]
    u = _rms(x, g_mix_ref[...])
    _ungroup_rows(yg_ref, perm_ref, z_ref, t_ref, chunk)
    ys = jnp.concatenate([t_ref[v] for v in range(d // LANES)], axis=1)
    z = jax.nn.gelu(ys + dskip_ref[...] * u, approximate=True).astype(BF16)
    val = _dot(z, w_glu_ref[:, 0:d])
    gate = _dot(z, w_glu_ref[:, d:2 * d])
    y2 = _ffn(x + val * jax.nn.sigmoid(gate), g_ffn_ref, w_in_ref, w_o_ref, acc_ref)
    o_ref[...] = _rms(y2, g_fin_ref[...])


def _row_spec(tm, width):
    return pl.BlockSpec((tm, width), lambda i: (i, 0))


def _group_scratch(tm, d, chunk):
    n_slab, nc, n_col = d // LANES, tm // chunk, chunk // SLAB_GROUPS
    assert min(S5_MM_ROWS, n_slab * n_col * nc) % nc == 0
    return (pltpu.VMEM((n_slab, tm, LANES), F32), pltpu.VMEM((n_slab * n_col * nc, SLAB_GROUPS * LANES), BF16))


def _group_spec(tm, d, chunk):
    return pl.BlockSpec((d // S5_GROUP, tm // chunk, chunk * S5_GROUP), lambda i: (0, i, 0))


def _layer_spec(stacked, layer):
    nd = stacked.ndim - 1
    return pl.BlockSpec((None,) + stacked.shape[1:], lambda *_: (layer,) + (0,) * nd, pipeline_mode=pl.Buffered(1))


def _ab_tail(x, a, ob, w_out, g_ffn, w_in, w_o, layer, g_next, perm, tm, chunk):
    t, d = x.shape
    h_scr, z_scr = _group_scratch(tm, d, chunk)
    return pl.pallas_call(
        functools.partial(_ab_tail_kernel, chunk=chunk),
        grid=(t // tm,),
        in_specs=[_row_spec(tm, d), _row_spec(tm, a.shape[1]), _row_spec(tm, ob.shape[1]),
                  _const_spec(w_out.shape), _const_spec((1, d)), _layer_spec(w_in, layer), _layer_spec(w_o, layer),
                  _const_spec((1, d)), _const_spec(perm.shape)],
        out_specs=[_row_spec(tm, d), _group_spec(tm, d, chunk)],
        out_shape=[jax.ShapeDtypeStruct((t, d), F32),
                   jax.ShapeDtypeStruct((d // S5_GROUP, t // chunk, chunk * S5_GROUP), BF16)],
        scratch_shapes=[pltpu.VMEM((tm, d), F32), h_scr, z_scr],
        compiler_params=_params(("arbitrary",)),
        name="ab_tail",
    )(x, a, ob, w_out, g_ffn, w_in, w_o, g_next, perm)


def _s5_tail(x, yg, perm_t, g_mix, dskip, w_glu, g_ffn, w_in, w_o, layer, g_fin, tm, chunk):
    t, d = x.shape
    t_scr, z_scr = _group_scratch(tm, d, chunk)
    return pl.pallas_call(
        functools.partial(_s5_tail_kernel, chunk=chunk),
        grid=(t // tm,),
        in_specs=[_row_spec(tm, d), _group_spec(tm, d, chunk), _const_spec(perm_t.shape),
                  _const_spec((1, d)), _const_spec((1, d)),
                  _const_spec(w_glu.shape), _const_spec((1, d)), _layer_spec(w_in, layer), _layer_spec(w_o, layer),
                  _const_spec((1, d))],
        out_specs=_row_spec(tm, d),
        out_shape=jax.ShapeDtypeStruct((t, d), F32),
        scratch_shapes=[pltpu.VMEM((tm, d), F32), z_scr, t_scr],
        compiler_params=_params(("arbitrary",)),
        name="s5_tail",
    )(x, yg, perm_t, g_mix, dskip, w_glu, g_ffn, w_in, w_o, g_fin)


SLAB_GROUPS = LANES // S5_GROUP
S5_MM_ROWS = 256


def _slab_perm():
    idx = jnp.arange(SLAB_GROUPS * LANES)
    l8, g8, p = idx // LANES, (idx % LANES) // S5_GROUP, idx % S5_GROUP
    dst = g8 * LANES + l8 * S5_GROUP + p
    return (dst[:, None] == idx[None, :]).astype(BF16)


def _permute_rows(z_ref, perm_ref, emit):
    step = min(S5_MM_ROWS, z_ref.shape[0])
    for r0 in range(0, z_ref.shape[0], step):
        emit(r0, _dot(z_ref[r0:r0 + step, :], perm_ref[...]))


def _group_rows(h, perm_ref, o_ref, h_ref, z_ref, chunk):
    rows = h.shape[0]
    nc = rows // chunk
    n_slab = h.shape[1] // LANES
    n_col = chunk // SLAB_GROUPS
    for v in range(n_slab):
        h_ref[v] = h[:, v * LANES:(v + 1) * LANES]
    for v in range(n_slab):
        for l in range(chunk):
            j, l8 = l // SLAB_GROUPS, l % SLAB_GROUPS
            r0 = (v * n_col + j) * nc
            z_ref[r0:r0 + nc, l8 * LANES:(l8 + 1) * LANES] = h_ref[v, pl.ds(l, nc, stride=chunk), :].astype(BF16)

    def emit(r0, blk):
        for q in range(blk.shape[0] // nc):
            v, j = divmod(r0 // nc + q, n_col)
            for g8 in range(SLAB_GROUPS):
                o_ref[v * SLAB_GROUPS + g8, :, j * LANES:(j + 1) * LANES] = (
                    blk[q * nc:(q + 1) * nc, g8 * LANES:(g8 + 1) * LANES].astype(o_ref.dtype))

    _permute_rows(z_ref, perm_ref, emit)


def _ungroup_rows(y_ref, perm_ref, z_ref, t_ref, chunk):
    n_slab, rows, _ = t_ref.shape
    nc = rows // chunk
    n_col = chunk // SLAB_GROUPS
    for v in range(n_slab):
        for j in range(n_col):
            r0 = (v * n_col + j) * nc
            for g8 in range(SLAB_GROUPS):
                z_ref[r0:r0 + nc, g8 * LANES:(g8 + 1) * LANES] = y_ref[v * SLAB_GROUPS + g8, :, j * LANES:(j + 1) * LANES]

    def emit(r0, blk):
        for q in range(blk.shape[0] // nc):
            v, j = divmod(r0 // nc + q, n_col)
            for l8 in range(SLAB_GROUPS):
                t_ref[v, pl.ds(j * SLAB_GROUPS + l8, nc, stride=chunk), :] = (
                    blk[q * nc:(q + 1) * nc, l8 * LANES:(l8 + 1) * LANES])

    _permute_rows(z_ref, perm_ref, emit)


S5_LP = S5_CHUNK * S5_GROUP
S5_RI = 2 * S5_STATE


def _s5_prep_kernel(*refs):
    for parity in range(2):
        _s5_prep_group(parity, *[r.at[parity] for r in refs])


def _s5_prep_group(parity, lam_row_ref, ls_ref, btr_ref, bti_ref, cr_ref, ci_ref,
                   bs_ref, cs_ref, tp_ref, ap_ref):
    n_pow = S5_CHUNK + 1
    dt = jnp.exp(ls_ref[...])

    def powers(lr, li):
        mag = jnp.exp(lr * dt)
        a_re, a_im = mag * jnp.cos(li * dt), mag * jnp.sin(li * dt)
        pw = [(jnp.ones_like(a_re), jnp.zeros_like(a_im))]
        for _ in range(n_pow - 1):
            pr, pi = pw[-1]
            pw.append((pr * a_re - pi * a_im, pr * a_im + pi * a_re))
        return pw

    lr, li = lam_row_ref[0:1, :], lam_row_ref[1:2, :]
    pw_row = powers(lr, li)
    a_re, a_im = pw_row[1]
    den = lr * lr + li * li
    f_re = ((a_re - 1.0) * lr + a_im * li) / den
    f_im = (a_im * lr - (a_re - 1.0) * li) / den
    blk = (S5_GROUP, S5_RI)
    w_re = jnp.concatenate([jnp.broadcast_to(f_re * pw_row[S5_CHUNK - 1 - l][0] - f_im * pw_row[S5_CHUNK - 1 - l][1], blk)
                            for l in range(S5_CHUNK)], axis=0)
    w_im = jnp.concatenate([jnp.broadcast_to(f_re * pw_row[S5_CHUNK - 1 - l][1] + f_im * pw_row[S5_CHUNK - 1 - l][0], blk)
                            for l in range(S5_CHUNK)], axis=0)

    lane = lax.broadcasted_iota(jnp.int32, (1, S5_RI), 1)
    own = (lane // S5_STATE) == parity
    low = lane < S5_STATE
    keep = lambda z: jnp.where(own, z, 0.0)
    per_pos = lambda ref: jnp.concatenate([ref[...]] * S5_CHUNK, axis=0)
    btr, bti = per_pos(btr_ref), per_pos(bti_ref)
    bs_ref[:, :S5_RI] = keep(w_re * btr - w_im * bti).astype(bs_ref.dtype)
    bs_ref[:, S5_RI:] = keep(w_re * bti + w_im * btr).astype(bs_ref.dtype)
    bf_t = jnp.where(low, f_re * btr[0:S5_GROUP] - f_im * bti[0:S5_GROUP],
                     f_re * bti[0:S5_GROUP] + f_im * btr[0:S5_GROUP])
    ap_ref[...] = jnp.zeros_like(ap_ref)
    for i, (j, part) in enumerate(((S5_CHUNK, 0), (S5_CHUNK, 1), (S5_CHUNK // 2, 0), (S5_CHUNK // 2, 1))):
        ap_ref[i:i + 1, :] = keep(pw_row[j][part])

    def spread(j0, part):
        return jnp.concatenate([jnp.broadcast_to(pw_row[j0 + l][part], blk) for l in range(S5_CHUNK)], axis=0)

    cr, ci = per_pos(cr_ref), per_pos(ci_ref)
    cs0 = jnp.where(low, cr * spread(0, 0) - ci * spread(0, 1),
                    -(cr * spread(0, 1) + ci * spread(0, 0)))
    cs_ref[:, :S5_RI] = keep(cr * spread(1, 0) - ci * spread(1, 1)).astype(cs_ref.dtype)
    cs_ref[:, S5_RI:] = keep(-(cr * spread(1, 1) + ci * spread(1, 0))).astype(cs_ref.dtype)
    r = lax.dot_general(bf_t, cs0, (((1,), (1,)), ((), ())), preferred_element_type=F32,
                        precision=lax.Precision.HIGHEST)
    lane_lp = lax.broadcasted_iota(jnp.int32, (S5_GROUP, S5_LP), 1)
    for l in range(S5_CHUNK):
        sh = l * S5_GROUP
        blk_l = r if l == 0 else jnp.where(lane_lp >= sh, pltpu.roll(r, sh, axis=1), 0.0)
        tp_ref[l * S5_GROUP:(l + 1) * S5_GROUP, :] = blk_l.astype(tp_ref.dtype)


def _s5_prep(lam_re, lam_im, log_step, b_re, b_im, c_re, c_im):
    g = lam_re.shape[0]
    dup = lambda z: jnp.concatenate([z, z], axis=-1)
    lam_row = jnp.stack([dup(lam_re), dup(lam_im)], axis=1)
    lam_row = jnp.pad(lam_row, ((0, 0), (0, 6), (0, 0)))
    ls = log_step.reshape(g, 1, 1)
    btr, bti = dup(jnp.swapaxes(b_re, 1, 2)), dup(jnp.swapaxes(b_im, 1, 2))
    cr, ci = dup(c_re), dup(c_im)
    gspec = lambda *s: pl.BlockSpec((2,) + s, lambda i: (i,) + (0,) * len(s))
    table = jax.ShapeDtypeStruct((g, S5_LP, 2 * S5_RI), BF16)
    return pl.pallas_call(
        _s5_prep_kernel,
        grid=(g // 2,),
        in_specs=[gspec(8, S5_RI), gspec(1, 1)] + [gspec(S5_GROUP, S5_RI)] * 4,
        out_specs=[gspec(S5_LP, 2 * S5_RI)] * 2 + [gspec(S5_LP, S5_LP), gspec(8, S5_RI)],
        out_shape=[table] * 2 + [jax.ShapeDtypeStruct((g, S5_LP, S5_LP), BF16),
                                 jax.ShapeDtypeStruct((g, 8, S5_RI), F32)],
        compiler_params=_params(("arbitrary",)),
        name="s5_prep",
    )(lam_row, ls, btr, bti, cr, ci)


def _s5_scan_kernel(u_ref, us_ref, x0r_ref, x0i_ref, bs_ref, cs_ref, tp_ref, ap_ref,
                    y_ref, ys_ref, finr_ref, fini_ref, finsr_ref, finsi_ref, xs_ref, *, bsz, chunks):
    n_rows = bsz * chunks
    pair = range(2)
    us_in = [u_ref[g] for g in pair]
    x = sum(_dot(us_in[g], bs_ref[g]) for g in pair)
    xr, xi = x[:, :S5_RI], x[:, S5_RI:]
    row = lax.broadcasted_iota(jnp.int32, (n_rows, 1), 0) % chunks
    ap = ap_ref[0] + ap_ref[1]
    ar, ai = ap[0:1, :], ap[1:2, :]

    def prefix(xr, xi, ar, ai, pos, length):
        shift = 1
        while shift < length:
            sr = jnp.where(pos >= shift, pltpu.roll(xr, shift, axis=0), 0.0)
            si = jnp.where(pos >= shift, pltpu.roll(xi, shift, axis=0), 0.0)
            xr, xi = xr + (sr * ar - si * ai), xi + (sr * ai + si * ar)
            ar, ai = ar * ar - ai * ai, 2.0 * (ar * ai)
            shift *= 2
        return xr, xi, ar, ai

    n_runs, runs = n_rows // SUBLANES, chunks // SUBLANES
    xr, xi, br, bi = prefix(xr, xi, ar, ai, row % SUBLANES, SUBLANES)
    xs_ref[0], xs_ref[1] = xr, xi
    last = pl.ds(SUBLANES - 1, n_runs, stride=SUBLANES)
    run = lax.broadcasted_iota(jnp.int32, (n_runs, 1), 0) % runs
    er, ei, _, _ = prefix(xs_ref[0, last, :], xs_ref[1, last, :], br, bi, run, runs)
    cr = jnp.where(run >= 1, pltpu.roll(er, 1, axis=0), 0.0)
    ci = jnp.where(run >= 1, pltpu.roll(ei, 1, axis=0), 0.0)
    pr, pi = ar, ai
    for r in range(SUBLANES):
        dst = pl.ds(r, n_runs, stride=SUBLANES)
        xs_ref[0, dst, :] = cr * pr - ci * pi
        xs_ref[1, dst, :] = cr * pi + ci * pr
        pr, pi = pr * ar - pi * ai, pr * ai + pi * ar
    xr, xi = xr + xs_ref[0], xi + xs_ref[1]
    pr = jnp.where(row >= 1, pltpu.roll(xr, 1, axis=0), 0.0).astype(BF16)
    pi = jnp.where(row >= 1, pltpu.roll(xi, 1, axis=0), 0.0).astype(BF16)
    prev = jnp.concatenate([pr, pi], axis=1)
    for g in pair:
        y_ref[g] = (_dot(us_in[g], tp_ref[g]) + _dot_nt(prev, cs_ref[g])).astype(y_ref.dtype)
    finr_ref[...] = jnp.zeros_like(finr_ref)
    fini_ref[...] = jnp.zeros_like(fini_ref)
    for b in range(bsz):
        last = slice((b + 1) * chunks - 1, (b + 1) * chunks)
        finr_ref[b:b + 1, :] = xr[last]
        fini_ref[b:b + 1, :] = xi[last]

    half = S5_LP // 2
    x0r, x0i = x0r_ref[...], x0i_ref[...]
    hr, hi = ap[2:3, :], ap[3:4, :]
    xs = sum(_dot(us_ref[g], bs_ref[g, half:, :]) for g in pair)
    finsr_ref[...] = x0r * hr - x0i * hi + xs[:, :S5_RI]
    finsi_ref[...] = x0r * hi + x0i * hr + xs[:, S5_RI:]
    x0 = jnp.concatenate([x0r, x0i], axis=1).astype(BF16)
    for g in pair:
        ys_ref[g] = (_dot(us_ref[g], tp_ref[g, :half, :half]) + _dot_nt(x0, cs_ref[g, :half, :])).astype(ys_ref.dtype)


def _s5_scan(u_g, us_g, x0r, x0i, bs, cs, tp, ap, bsz):
    g, n_rows, _ = u_g.shape
    n_s = us_g.shape[1]
    half = S5_LP // 2
    pspec = lambda *s: pl.BlockSpec((2,) + s, lambda i: (i,) + (0,) * len(s))
    ospec = lambda *s: pl.BlockSpec((None,) + s, lambda i: (i,) + (0,) * len(s))
    packed = lambda n: jax.ShapeDtypeStruct((g // 2, n, S5_RI), F32)
    return pl.pallas_call(
        functools.partial(_s5_scan_kernel, bsz=bsz, chunks=n_rows // bsz),
        grid=(g // 2,),
        in_specs=[pspec(n_rows, S5_LP), pspec(n_s, half), ospec(n_s, S5_RI), ospec(n_s, S5_RI)]
                 + [pspec(S5_LP, 2 * S5_RI)] * 2 + [pspec(S5_LP, S5_LP), pspec(8, S5_RI)],
        out_specs=[pspec(n_rows, S5_LP), pspec(n_s, half), ospec(8, S5_RI), ospec(8, S5_RI),
                   ospec(n_s, S5_RI), ospec(n_s, S5_RI)],
        out_shape=[jax.ShapeDtypeStruct((g, n_rows, S5_LP), BF16), jax.ShapeDtypeStruct((g, n_s, half), BF16),
                   packed(8), packed(8), packed(n_s), packed(n_s)],
        scratch_shapes=[pltpu.VMEM((2, n_rows, S5_RI), F32)],
        compiler_params=_params(("arbitrary",)),
        name="s5_scan",
    )(u_g, us_g, x0r, x0i, bs, cs, tp, ap)


def kernel(x_prompt, x_sample, state_ret, state_swa_k, state_swa_v, state_ssm_re, state_ssm_im, norm_mix, norm_ffn, norm_final, w_in_ab, ret_gn, w_out_ab, ssm_lam_re, ssm_lam_im, ssm_log_step, ssm_b_re, ssm_b_im, ssm_c_re, ssm_c_im, ssm_d, w_glu, w_ffn_in, w_ffn_out):
    bsz, seq, d = x_prompt.shape
    dbsz, n_new, _ = x_sample.shape
    assert state_swa_k.shape[2] == SWA_BUF and n_new == S5_CHUNK // 2 and seq % SWA_ROWS == 0
    xp = x_prompt.reshape(bsz * seq, d)
    xs = x_sample.reshape(dbsz * n_new, d)
    tm_p, tm_s = TAIL_ROWS, dbsz * n_new
    row = lambda v: v.reshape(1, -1)

    w_in0 = w_in_ab[0].astype(BF16)
    g_mix0, g_ffn0, gn0 = row(norm_mix[0]), row(norm_ffn[0]), row(ret_gn[0])

    ra_p, qkv_p = _ab_proj(xp, g_mix0, w_in0, PROJ_ROWS)
    ra_s, qkv_s = _ab_proj(xs, g_mix0, w_in0, tm_s)

    zero_ret = jnp.zeros((bsz, H_A, DK_A, DV_A), F32)
    a_p, ret_p = _retention(ra_p, zero_ret, gn0, seq=seq, c_real=RET_CHUNK, n_chunks=RET_STEP_CHUNKS, n_seq=bsz,
                            out_dtype=BF16)
    a_s, ret_s = _retention(ra_s, state_ret[0], gn0, seq=n_new, c_real=n_new, n_chunks=1, n_seq=RET_SAMPLE_SEQS,
                            out_dtype=F32)

    width_b = H_B * DH_B
    rows_last = lambda w: w.transpose(0, 2, 3, 1).reshape(dbsz, width_b, SWA_BUF)
    rows_first = lambda w: w.reshape(dbsz, H_B, DH_B, SWA_BUF).transpose(0, 3, 1, 2)[None]
    later = (w_out_ab[0], w_ffn_in.reshape(-1, w_ffn_in.shape[-1]), w_ffn_out.reshape(-1, w_ffn_out.shape[-1]), w_glu[0])
    ob_p, ob_s, swk_s, swv_s, w_out0, w_f_in, w_f_out, w_glu0 = _swa(
        qkv_p, bsz, seq, qkv_s, rows_last(state_swa_k[0]), rows_last(state_swa_v[0]), n_new, later)
    w_f_in, w_f_out = w_f_in.reshape(w_ffn_in.shape), w_f_out.reshape(w_ffn_out.shape)
    kv_tail = qkv_p.reshape(bsz, seq, QKV_W)[:, seq - SWA_BUF:, width_b:]
    swk_p = kv_tail[..., :width_b].reshape(bsz, SWA_BUF, H_B, DH_B)
    swv_p = kv_tail[..., width_b:].reshape(bsz, SWA_BUF, H_B, DH_B)

    g_mix1, g_ffn1 = row(norm_mix[1]), row(norm_ffn[1])
    perm = _slab_perm()
    yp, u_p = _ab_tail(xp, a_p, ob_p, w_out0, g_ffn0, w_f_in, w_f_out, 0, g_mix1, perm, tm_p, S5_CHUNK)
    ys, u_s = _ab_tail(xs, a_s, ob_s, w_out0, g_ffn0, w_f_in, w_f_out, 0, g_mix1, perm, tm_s, n_new)

    operators = _s5_prep(ssm_lam_re[0], ssm_lam_im[0], ssm_log_step[0],
                         ssm_b_re[0], ssm_b_im[0], ssm_c_re[0], ssm_c_im[0])
    pack = lambda st: st.reshape(st.shape[0], -1, S5_RI).transpose(1, 0, 2)
    unpack = lambda st: st.transpose(1, 0, 2).reshape(st.shape[1], -1, S5_STATE)[None]
    y5_p, y5_s, finr_p, fini_p, finr_s, fini_s = _s5_scan(
        u_p, u_s, pack(state_ssm_re[0]), pack(state_ssm_im[0]), *operators, bsz)

    tail1 = (perm.T, g_mix1, row(ssm_d[0]), w_glu0, g_ffn1,
             w_f_in, w_f_out, 1, row(norm_final))
    yp = _s5_tail(yp, y5_p, *tail1, tm_p, S5_CHUNK)
    ys = _s5_tail(ys, y5_s, *tail1, tm_s, n_new)

    return (yp.reshape(bsz, seq, d), ys.reshape(dbsz, n_new, d),
            ret_p[None], ret_s[None],
            swk_p[None], swv_p[None],
            rows_first(swk_s), rows_first(swv_s),
            unpack(finr_p[:, :bsz]), unpack(fini_p[:, :bsz]), unpack(finr_s), unpack(fini_s))
```

```python
import functools

import jax
import jax.numpy as jnp
import numpy as np
from jax import lax
from jax.experimental import pallas as pl
from jax.experimental.pallas import tpu as pltpu

F32 = jnp.float32
BF16 = jnp.bfloat16

H_A, DK_A, DV_A = 4, 64, 128
H_B, DH_B = 8, 64
SWA_PAIRS = ((128, 1), (512, 4), (2048, 16))
SPAN = 128
SWA_BUF = 2048
PAST_LEN = 16384
RET_CHUNK = 128
S5_GROUP, S5_STATE = 16, 64
S5_CHUNK = 16
EPS = 1e-6
NEG_INF = -1e30
QA_W, KA_W, VA_W, GA_W = H_A * DK_A, H_A * DK_A, H_A * DV_A, H_A * DV_A
RA_W = QA_W + KA_W + VA_W + GA_W
QKV_W = 3 * H_B * DH_B

LANES = 128
SUBLANES = 8
VMEM_LIMIT = 58 * 1024 * 1024

PROJ_ROWS = 1024
PROJ_COLS = 512
TAIL_ROWS = 512
RET_STEP_CHUNKS = 4
RET_SAMPLE_SEQS = 4


def _params(sem):
    return pltpu.CompilerParams(dimension_semantics=sem, vmem_limit_bytes=VMEM_LIMIT)


def _const_spec(shape):
    nd = len(shape)
    return pl.BlockSpec(shape, lambda *_: (0,) * nd, pipeline_mode=pl.Buffered(1))


def _rms(x, g):
    return x * lax.rsqrt(jnp.mean(x * x, axis=-1, keepdims=True) + EPS) * g


def _dot(a, b):
    return jnp.dot(a, b, preferred_element_type=F32)


def _dot_nt(a, b):
    return lax.dot_general(a, b, (((1,), (1,)), ((), ())), preferred_element_type=F32)


def _dot_tn(a, b):
    return lax.dot_general(a, b, (((0,), (0,)), ((), ())), preferred_element_type=F32)


def _ab_proj_kernel(x_ref, g_ref, w_ref, ra_ref, qkv_ref):
    h = _rms(x_ref[...], g_ref[...]).astype(BF16)
    for n0 in range(0, RA_W, PROJ_COLS):
        ra_ref[:, n0:n0 + PROJ_COLS] = _dot(h, w_ref[:, n0:n0 + PROJ_COLS])
    for n0 in range(0, QKV_W, PROJ_COLS):
        qkv_ref[:, n0:n0 + PROJ_COLS] = _dot(h, w_ref[:, RA_W + n0:RA_W + n0 + PROJ_COLS])


def _ab_proj(x, g, w_bf16, tm):
    t, d = x.shape
    return pl.pallas_call(
        _ab_proj_kernel,
        grid=(t // tm,),
        in_specs=[pl.BlockSpec((tm, d), lambda i: (i, 0)),
                  _const_spec((1, d)),
                  _const_spec((d, RA_W + QKV_W))],
        out_specs=[pl.BlockSpec((tm, RA_W), lambda i: (i, 0)),
                   pl.BlockSpec((tm, QKV_W), lambda i: (i, 0))],
        out_shape=[jax.ShapeDtypeStruct((t, RA_W), F32), jax.ShapeDtypeStruct((t, QKV_W), F32)],
        compiler_params=_params(("arbitrary",)),
        name="ab_proj",
    )(x, g, w_bf16)


def _retention_tables(c_real):
    c = RET_CHUNK
    log_g = np.log1p(-np.exp2(-5.0 - np.arange(H_A, dtype=F32)))
    idx = np.arange(c, dtype=F32)
    rel = idx[:, None] - idx[None, :]
    dec = np.where(rel >= 0, np.exp(np.maximum(rel, 0.0)[None] * log_g[:, None, None]), 0.0)
    dec = dec.reshape(H_A * c, c)
    real = (idx < c_real)[:, None]
    qd = np.exp((idx + 1.0)[:, None] * log_g[None, :])
    qd = np.repeat(qd, DV_A, axis=1)
    kd = np.where(real, np.exp((c_real - 1.0 - idx)[:, None] * log_g[None, :]), 0.0)
    kd = np.repeat(kd, DK_A, axis=1)
    row_h = np.arange(H_A * DK_A)[:, None] // DK_A
    col_h = np.arange(H_A * DV_A)[None, :] // DV_A
    bd = (row_h == col_h).astype(F32)
    dm = bd * np.repeat(np.exp(c_real * log_g), DV_A)[None, :]
    return tuple(np.asarray(tb, F32) for tb in (dec, qd, kd, dm))


def _retention_kernel(ra_ref, s0_ref, gain_ref, dec_ref, qd_ref, kd_ref, dm_ref,
                      a_ref, sout_ref, sbd_ref, *, c_real, n_chunks, n_seq):
    c = RET_CHUNK
    j = pl.program_id(1)

    @pl.when(j == 0)
    def _():
        sbd_ref[...] = jnp.zeros_like(sbd_ref)
        for s in range(n_seq):
            for h in range(H_A):
                sbd_ref[s, h * DK_A:(h + 1) * DK_A, h * DV_A:(h + 1) * DV_A] = s0_ref[s, h]

    lane_q = lax.broadcasted_iota(jnp.int32, (c, QA_W), 1) // DK_A
    for ci in range(n_chunks):
        for s in range(n_seq):
            _retention_chunk(ra_ref.at[s], a_ref.at[s], sbd_ref.at[s], gain_ref, dec_ref, qd_ref, kd_ref, dm_ref,
                             lane_q, ci, c_real)

    @pl.when(j == pl.num_programs(1) - 1)
    def _():
        for s in range(n_seq):
            for h in range(H_A):
                sout_ref[s, h] = sbd_ref[s, h * DK_A:(h + 1) * DK_A, h * DV_A:(h + 1) * DV_A]


def _retention_chunk(ra_ref, a_ref, sbd_ref, gain_ref, dec_ref, qd_ref, kd_ref, dm_ref, lane_q, ci, c_real):
    c = RET_CHUNK
    rows = ra_ref[ci * c_real:(ci + 1) * c_real, :]
    gate = rows[:, QA_W + KA_W + VA_W:]
    if c_real < c:
        rows = jnp.concatenate([rows, jnp.zeros((c - c_real, RA_W), F32)], axis=0)
    q = rows[:, :QA_W]
    k = rows[:, QA_W:QA_W + KA_W] * (DK_A ** -0.5)
    v = rows[:, QA_W + KA_W:QA_W + KA_W + VA_W]
    vb = v.astype(BF16)
    qm = jnp.concatenate([jnp.where(lane_q == h, q, 0.0) for h in range(H_A)], axis=0).astype(BF16)
    s = _dot_nt(qm, k.astype(BF16)) * dec_ref[...]
    sb = s.astype(BF16)
    sbd = sbd_ref[...]
    cross = _dot(q.astype(BF16), sbd.astype(BF16)) * qd_ref[...]
    upd = _dot_tn((k * kd_ref[...]).astype(BF16), vb)
    for h in range(H_A):
        blk = (slice(h * DK_A, (h + 1) * DK_A), slice(h * DV_A, (h + 1) * DV_A))
        sbd_ref[blk] = sbd[blk] * dm_ref[blk] + upd[blk]
    for h in range(H_A):
        sl = slice(h * DV_A, (h + 1) * DV_A)
        o = _dot(sb[h * c:(h + 1) * c], vb[:, sl]) + cross[:, sl]
        o = o[:c_real]
        mu = jnp.mean(o, axis=-1, keepdims=True)
        var = jnp.mean(jnp.square(o - mu), axis=-1, keepdims=True)
        y = (o - mu) * lax.rsqrt(var + EPS) * gain_ref[:, sl]
        gh = gate[:, sl]
        a_ref[ci * c_real:(ci + 1) * c_real, sl] = (gh * jax.nn.sigmoid(gh) * y).astype(a_ref.dtype)


def _retention(ra, state0, gain, *, seq, c_real, n_chunks, n_seq, out_dtype):
    t = ra.shape[0]
    bsz = t // seq
    rows = c_real * n_chunks
    steps = seq // rows
    tables = _retention_tables(c_real)
    kern = functools.partial(_retention_kernel, c_real=c_real, n_chunks=n_chunks, n_seq=n_seq)
    a, s_out = pl.pallas_call(
        kern,
        grid=(bsz // n_seq, steps),
        in_specs=[pl.BlockSpec((n_seq, rows, RA_W), lambda b, j: (b, j, 0)),
                  pl.BlockSpec((n_seq, H_A, DK_A, DV_A), lambda b, j: (b, 0, 0, 0)),
                  _const_spec((1, VA_W))] + [_const_spec(tb.shape) for tb in tables],
        out_specs=[pl.BlockSpec((n_seq, rows, VA_W), lambda b, j: (b, j, 0)),
                   pl.BlockSpec((n_seq, H_A, DK_A, DV_A), lambda b, j: (b, 0, 0, 0))],
        out_shape=[jax.ShapeDtypeStruct((bsz, seq, VA_W), out_dtype),
                   jax.ShapeDtypeStruct((bsz, H_A, DK_A, DV_A), F32)],
        scratch_shapes=[pltpu.VMEM((n_seq, H_A * DK_A, H_A * DV_A), F32)],
        compiler_params=_params(("arbitrary", "arbitrary")),
        name="retention",
    )(ra.reshape(bsz, seq, RA_W), state0, gain, *tables)
    return a.reshape(t, VA_W), s_out


SWA_ROWS = 2048
N_BRANCH = len(SWA_PAIRS)


def _alibi_slopes():
    return np.exp2(-8.0 * np.arange(1, H_B + 1, dtype=F32) / H_B)


def _swa_prompt_bias():
    qi = np.arange(SPAN)[:, None]
    kj = np.arange(2 * SPAN)[None, :]
    dist = SPAN + qi - kj
    band = (dist >= 0) & (dist <= SPAN)
    slopes = _alibi_slopes()
    out = []
    for (_, dil) in SWA_PAIRS:
        pen = -slopes[:, None, None] * (dil * dist).astype(F32)[None]
        normal = np.where(band[None], pen, NEG_INF)
        first = np.where((band & (kj >= SPAN))[None], pen, NEG_INF)
        out.append(np.stack([normal, first], axis=1))
    tab = np.stack(out, axis=1)
    tab = tab.reshape(H_B // 2, 2, N_BRANCH, 2, SPAN, 2 * SPAN).transpose(0, 2, 3, 1, 4, 5)
    return np.asarray(tab.reshape(H_B // 2, N_BRANCH, 2, 2 * SPAN, 2 * SPAN), F32)


COARSE = 4


def _swa_unit(q, kk, vv, bias, head0):
    q = q * (DH_B ** -0.5)
    qm = jnp.concatenate([jnp.where(head0, q, 0.0), jnp.where(head0, 0.0, q)], axis=0).astype(BF16)
    s = _dot_nt(qm, kk.astype(BF16)) + bias
    m = jnp.max(s, axis=-1, keepdims=True)
    p = jnp.exp(s - m).astype(BF16)
    ones = jnp.ones((2 * SPAN, LANES), BF16)
    res = _dot(p, jnp.concatenate([vv.astype(BF16), ones], axis=1))
    acc = jnp.where(head0, res[:SPAN, :LANES], res[SPAN:, :LANES])
    den = jnp.where(head0, res[:SPAN, LANES:], res[SPAN:, LANES:])
    mm = jnp.where(head0, jnp.broadcast_to(m[:SPAN], (SPAN, LANES)), jnp.broadcast_to(m[SPAN:], (SPAN, LANES)))
    return acc, den, mm


SWA_PREV = SPAN * COARSE


def _swa_prompt_half(half, q_ref, k_ref, v_ref, bias_ref, o_ref, kbuf, vbuf, acc_ref, l_ref, m_ref,
                     q4, k4, v4, acc4, l4, m4):
    j = pl.program_id(2)
    rows = SWA_ROWS
    sub = rows // COARSE
    slot = j % 2

    @pl.when((half == 0) & (j > 0))
    def _():
        kbuf[0:SWA_PREV, :] = kbuf[rows:rows + SWA_PREV, :]
        vbuf[0:SWA_PREV, :] = vbuf[rows:rows + SWA_PREV, :]

    @pl.when((half == 0) & (j == 0))
    def _():
        k4[1] = jnp.zeros(k4.shape[1:], F32)
        v4[1] = jnp.zeros(v4.shape[1:], F32)
        kbuf[0:SWA_PREV, :] = jnp.zeros((SWA_PREV, LANES), F32)
        vbuf[0:SWA_PREV, :] = jnp.zeros((SWA_PREV, LANES), F32)

    @pl.when(half == 0)
    def _():
        kbuf[SWA_PREV:, :] = k_ref[...]
        vbuf[SWA_PREV:, :] = v_ref[...]
        for c in range(COARSE):
            q4[c] = q_ref[pl.ds(c, sub, stride=COARSE), :]
            k4[slot, c] = k_ref[pl.ds(c, sub, stride=COARSE), :]
            v4[slot, c] = v_ref[pl.ds(c, sub, stride=COARSE), :]

    lane = lax.broadcasted_iota(jnp.int32, (SPAN, LANES), 1)
    head0 = lane < DH_B
    first_step = (j == 0).astype(jnp.int32)

    for g, (_, dil) in enumerate(SWA_PAIRS):
        blocks = rows // (SPAN * dil)
        per_half = blocks * dil // 2

        def unit(u, g=g, dil=dil):
            wb = u // dil
            r = u % dil
            q_start = wb * (SPAN * dil) + r
            k_start = SWA_PREV + (wb - 1) * (SPAN * dil) + r
            variant = jnp.where(wb == 0, first_step, 0)
            acc, den, mm = _swa_unit(q_ref[pl.ds(q_start, SPAN, stride=dil), :],
                                     kbuf[pl.ds(k_start, 2 * SPAN, stride=dil), :],
                                     vbuf[pl.ds(k_start, 2 * SPAN, stride=dil), :],
                                     bias_ref[0, g, variant], head0)
            acc_ref[g, pl.ds(q_start, SPAN, stride=dil), :] = acc
            l_ref[g, pl.ds(q_start, SPAN, stride=dil), :] = den
            m_ref[g, pl.ds(q_start, SPAN, stride=dil), :] = mm

        def unit_two_level(u, g=g, fine=dil // COARSE):
            c = u % COARSE
            f = u // COARSE
            pick = pl.ds(f, SPAN, stride=fine)
            acc, den, mm = _swa_unit(q4[c, pick, :],
                                     jnp.concatenate([k4[1 - slot, c, pick, :], k4[slot, c, pick, :]], axis=0),
                                     jnp.concatenate([v4[1 - slot, c, pick, :], v4[slot, c, pick, :]], axis=0),
                                     bias_ref[0, g, first_step], head0)
            dst = pl.ds(c * sub + f, SPAN, stride=fine)
            acc4[dst, :] = acc
            l4[dst, :] = den
            m4[dst, :] = mm

        two_level = dil % (COARSE * COARSE) == 0
        assert not two_level or blocks == 1
        body = unit_two_level if two_level else unit

        @pl.when(half >= 0)
        def _(body=body, per_half=per_half):
            for i in range(per_half):
                body(half * per_half + i)

        if two_level:
            @pl.when(half == 1)
            def _(g=g):
                for c in range(COARSE):
                    src, dst = slice(c * sub, (c + 1) * sub), pl.ds(c, sub, stride=COARSE)
                    acc_ref[g, dst, :] = acc4[src, :]
                    l_ref[g, dst, :] = l4[src, :]
                    m_ref[g, dst, :] = m4[src, :]

    @pl.when(half == 1)
    def _():
        tile = 256
        for r0 in range(0, rows, tile):
            sl = slice(r0, r0 + tile)
            ms = [m_ref[g, sl, :] for g in range(N_BRANCH)]
            mx = functools.reduce(jnp.maximum, ms)
            ws = [jnp.exp(mg - mx) for mg in ms]
            num = sum(w * acc_ref[g, sl, :] for g, w in enumerate(ws))
            den = sum(w * l_ref[g, sl, :] for g, w in enumerate(ws))
            o_ref[sl, :] = (num / den).astype(o_ref.dtype)


def _swa_sample_bias(n_new):
    t = np.arange(n_new)[:, None]
    slopes = _alibi_slopes()

    def table(j):
        dist = SWA_BUF + t - j
        out = []
        for (window, dil) in SWA_PAIRS:
            valid = (dist >= 0) & (dist <= window) & (dist % dil == 0) & (PAST_LEN + t - dist >= 0)
            pen = -slopes[:, None, None] * dist.astype(F32)[None]
            out.append(np.where(valid[None], pen, NEG_INF).reshape(H_B * n_new, -1))
        return np.stack(out)

    bias_a = table(np.arange(SWA_BUF)[None, :])
    jb = np.arange(LANES)[None, :]
    bias_b = np.where(jb < n_new, table(SWA_BUF + jb), NEG_INF)
    return np.asarray(bias_a, F32), np.asarray(bias_b, F32)


def _pad_new_rows(new_ref, n_new):
    return jnp.concatenate([new_ref[...], jnp.zeros((LANES - n_new, new_ref.shape[1]), F32)], axis=0)


def _swa_sample_attend(q_ref, k_ref, v_ref, kp_ref, vp_ref, ba_ref, bb_ref, o_ref, n_new):
    width = q_ref.shape[1]
    n_heads = width // DH_B
    lane_h = lax.broadcasted_iota(jnp.int32, (n_new, width), 1) // DH_B
    q = q_ref[...] * (DH_B ** -0.5)
    qm = jnp.concatenate([jnp.where(lane_h == h, q, 0.0) for h in range(n_heads)], axis=0).astype(BF16)
    k_new, v_new = _pad_new_rows(k_ref, n_new), _pad_new_rows(v_ref, n_new)
    s_a = _dot(qm, kp_ref[0].astype(BF16))
    s_b = _dot_nt(qm, k_new.astype(BF16))
    sa = [s_a + ba_ref[g] for g in range(N_BRANCH)]
    sb = [s_b + bb_ref[g] for g in range(N_BRANCH)]
    mx = functools.reduce(jnp.maximum, [jnp.max(x, axis=-1, keepdims=True) for x in sa + sb])
    p_a = sum(jnp.exp(x - mx) for x in sa).astype(BF16)
    p_b = sum(jnp.exp(x - mx) for x in sb).astype(BF16)
    den = (jnp.sum(p_a.astype(F32), axis=-1, keepdims=True)
           + jnp.sum(p_b.astype(F32), axis=-1, keepdims=True))
    o = (_dot_nt(p_a, vp_ref[0].astype(BF16)) + _dot(p_b, v_new.astype(BF16))) / den
    o_ref[...] = sum(jnp.where(lane_h == h, o[h * n_new:(h + 1) * n_new], 0.0) for h in range(n_heads))

def _swa_window_update(src_ref, new_ref, dst_ref, n_new):
    lane = lax.broadcasted_iota(jnp.int32, (DH_B, LANES), 1)
    new_t = pltpu.roll(_pad_new_rows(new_ref, n_new).T, LANES - n_new, axis=1)
    for h in range(new_ref.shape[1] // DH_B):
        rows = slice(h * DH_B, (h + 1) * DH_B)
        shifted = pltpu.roll(src_ref[0, rows, :], SWA_BUF - n_new, axis=1)
        dst_ref[0, rows, 0:SWA_BUF - LANES] = shifted[:, 0:SWA_BUF - LANES]
        dst_ref[0, rows, SWA_BUF - LANES:] = jnp.where(lane >= LANES - n_new, new_t[rows],
                                                       shifted[:, SWA_BUF - LANES:])


N_SWA_IN, N_SWA_OUT = 11, 4


def _swa_kernel(*refs, n_new, n_cast):
    (q_ref, k_ref, v_ref, bias_ref, qs_ref, ks_ref, vs_ref, kp_ref, vp_ref, ba_ref, bb_ref), refs = (
        refs[:N_SWA_IN], refs[N_SWA_IN:])
    cast_src, refs = refs[:n_cast], refs[n_cast:]
    (o_ref, os_ref, ko_ref, vo_ref), refs = refs[:N_SWA_OUT], refs[N_SWA_OUT:]
    cast_dst, scratch = refs[:n_cast], refs[n_cast:]
    half = pl.program_id(3)
    _swa_prompt_half(half, q_ref, k_ref, v_ref, bias_ref, o_ref, *scratch)
    _swa_sample_attend(qs_ref, ks_ref, vs_ref, kp_ref, vp_ref, ba_ref, bb_ref, os_ref, n_new)
    _swa_window_update(kp_ref, ks_ref, ko_ref, n_new)
    _swa_window_update(vp_ref, vs_ref, vo_ref, n_new)

    @pl.when(half == 0)
    def _():
        for src, dst in zip(cast_src, cast_dst):
            dst[...] = src[...].astype(dst.dtype)


def _swa(qkv, bsz, seq, qkv_s, k_past, v_past, n_new, weights):
    t = qkv.shape[0]
    dbsz = k_past.shape[0]
    steps = seq // SWA_ROWS
    npair = H_B // 2
    width = H_B * DH_B
    half_w = width // 2
    assert dbsz == bsz * npair * steps, "one sample sequence per prompt (sequence, head pair, row block) step"
    bias = _swa_prompt_bias()
    bias_a, bias_b = _swa_sample_bias(n_new)
    blk = (SWA_ROWS, LANES)
    rows_of = lambda b, j: b * steps + j
    seq_of = lambda b, hp, j: (b * npair + hp) * steps + j
    new_blk = lambda col: pl.BlockSpec((n_new, half_w), lambda b, hp, j, h, col=col: (seq_of(b, hp, j), 2 * col + h))
    state_blk = pl.BlockSpec((1, half_w, SWA_BUF), lambda b, hp, j, h: (seq_of(b, hp, j), h, 0))
    half_rows = H_B // 2 * n_new
    sub = SWA_ROWS // COARSE
    assert all(w.shape[0] % (dbsz * 2 * SUBLANES) == 0 for w in weights), "bf16 row blocks are 16-row tiles"
    cast_blk = [pl.BlockSpec((w.shape[0] // dbsz, w.shape[1]), lambda b, hp, j, h: (seq_of(b, hp, j), 0))
                for w in weights]
    in_specs = [pl.BlockSpec(blk, lambda b, hp, j, h: (rows_of(b, j), hp)),
                pl.BlockSpec(blk, lambda b, hp, j, h: (rows_of(b, j), npair + hp)),
                pl.BlockSpec(blk, lambda b, hp, j, h: (rows_of(b, j), 2 * npair + hp)),
                pl.BlockSpec((1, N_BRANCH, 2, 2 * SPAN, 2 * SPAN), lambda b, hp, j, h: (hp, 0, 0, 0, 0)),
                new_blk(0), new_blk(1), new_blk(2), state_blk, state_blk,
                pl.BlockSpec((N_BRANCH, half_rows, SWA_BUF), lambda b, hp, j, h: (0, h, 0)),
                pl.BlockSpec((N_BRANCH, half_rows, LANES), lambda b, hp, j, h: (0, h, 0))]
    assert len(in_specs) == N_SWA_IN
    return pl.pallas_call(
        functools.partial(_swa_kernel, n_new=n_new, n_cast=len(weights)),
        grid=(bsz, npair, steps, 2),
        in_specs=in_specs + cast_blk,
        out_specs=[pl.BlockSpec(blk, lambda b, hp, j, h: (rows_of(b, j), hp)),
                   pl.BlockSpec((n_new, half_w), lambda b, hp, j, h: (seq_of(b, hp, j), h)),
                   state_blk, state_blk] + cast_blk,
        out_shape=[jax.ShapeDtypeStruct((t, width), BF16),
                   jax.ShapeDtypeStruct((dbsz * n_new, width), F32),
                   jax.ShapeDtypeStruct(k_past.shape, F32), jax.ShapeDtypeStruct(v_past.shape, F32)]
                  + [jax.ShapeDtypeStruct(w.shape, BF16) for w in weights],
        scratch_shapes=[pltpu.VMEM((SWA_PREV + SWA_ROWS, LANES), F32)] * 2
                       + [pltpu.VMEM((N_BRANCH, SWA_ROWS, LANES), F32)] * 3
                       + [pltpu.VMEM((COARSE, sub, LANES), F32)]
                       + [pltpu.VMEM((2, COARSE, sub, LANES), F32)] * 2
                       + [pltpu.VMEM((SWA_ROWS, LANES), F32)] * 3,
        compiler_params=_params(("arbitrary",) * 4),
        name="swa",
    )(qkv, qkv, qkv, bias, qkv_s, qkv_s, qkv_s, k_past, v_past, bias_a, bias_b, *weights)


FFN_CHUNK = 256


def _ffn(y1, g_ffn_ref, w_in_ref, w_o_ref, acc_ref):
    d_ff = w_o_ref.shape[0]
    h = _rms(y1, g_ffn_ref[...]).astype(BF16)
    for ci, c0 in enumerate(range(0, d_ff, FFN_CHUNK)):
        gate = _dot(h, w_in_ref[:, c0:c0 + FFN_CHUNK])
        up = _dot(h, w_in_ref[:, d_ff + c0:d_ff + c0 + FFN_CHUNK])
        act = (gate * jax.nn.sigmoid(gate) * up).astype(BF16)
        part = _dot(act, w_o_ref[c0:c0 + FFN_CHUNK, :])
        if ci == 0:
            acc_ref[...] = part
        else:
            acc_ref[...] += part
    return y1 + acc_ref[...]


def _ab_tail_kernel(x_ref, a_ref, ob_ref, w_out_ref, g_ffn_ref, w_in_ref, w_o_ref, g_next_ref, perm_ref,
                    o_ref, u_ref, acc_ref, h_ref, z_ref, *, chunk):
    mix = (_dot(a_ref[...].astype(BF16), w_out_ref[0:VA_W, :])
           + _dot(ob_ref[...].astype(BF16), w_out_ref[VA_W:, :]))
    y = _ffn(x_ref[...] + mix, g_ffn_ref, w_in_ref, w_o_ref, acc_ref)
    o_ref[...] = y
    _group_rows(_rms(y, g_next_ref[...]), perm_ref, u_ref, h_ref, z_ref, chunk)


def _s5_tail_kernel(x_ref, yg_ref, perm_ref, g_mix_ref, dskip_ref, w_glu_ref, g_ffn_ref, w_in_ref, w_o_ref,
                    g_fin_ref, o_ref, acc_ref, z_ref, t_ref, *, chunk):
    x = x_ref[...]
    d = x.shape[-1]
    u = _rms(x, g_mix_ref[...])
    _ungroup_rows(yg_ref, perm_ref, z_ref, t_ref, chunk)
    ys = jnp.concatenate([t_ref[v] for v in range(d // LANES)], axis=1)
    z = jax.nn.gelu(ys + dskip_ref[...] * u, approximate=True).astype(BF16)
    val = _dot(z, w_glu_ref[:, 0:d])
    gate = _dot(z, w_glu_ref[:, d:2 * d])
    y2 = _ffn(x + val * jax.nn.sigmoid(gate), g_ffn_ref, w_in_ref, w_o_ref, acc_ref)
    o_ref[...] = _rms(y2, g_fin_ref[...])


def _row_spec(tm, width):
    return pl.BlockSpec((tm, width), lambda i: (i, 0))


def _group_scratch(tm, d, chunk):
    n_slab, nc, n_col = d // LANES, tm // chunk, chunk // SLAB_GROUPS
    assert min(S5_MM_ROWS, n_slab * n_col * nc) % nc == 0
    return (pltpu.VMEM((n_slab, tm, LANES), F32), pltpu.VMEM((n_slab * n_col * nc, SLAB_GROUPS * LANES), BF16))


def _group_spec(tm, d, chunk):
    return pl.BlockSpec((d // S5_GROUP, tm // chunk, chunk * S5_GROUP), lambda i: (0, i, 0))


def _layer_spec(stacked, layer):
    nd = stacked.ndim - 1
    return pl.BlockSpec((None,) + stacked.shape[1:], lambda *_: (layer,) + (0,) * nd, pipeline_mode=pl.Buffered(1))


def _ab_tail(x, a, ob, w_out, g_ffn, w_in, w_o, layer, g_next, perm, tm, chunk):
    t, d = x.shape
    h_scr, z_scr = _group_scratch(tm, d, chunk)
    return pl.pallas_call(
        functools.partial(_ab_tail_kernel, chunk=chunk),
        grid=(t // tm,),
        in_specs=[_row_spec(tm, d), _row_spec(tm, a.shape[1]), _row_spec(tm, ob.shape[1]),
                  _const_spec(w_out.shape), _const_spec((1, d)), _layer_spec(w_in, layer), _layer_spec(w_o, layer),
                  _const_spec((1, d)), _const_spec(perm.shape)],
        out_specs=[_row_spec(tm, d), _group_spec(tm, d, chunk)],
        out_shape=[jax.ShapeDtypeStruct((t, d), F32),
                   jax.ShapeDtypeStruct((d // S5_GROUP, t // chunk, chunk * S5_GROUP), BF16)],
        scratch_shapes=[pltpu.VMEM((tm, d), F32), h_scr, z_scr],
        compiler_params=_params(("arbitrary",)),
        name="ab_tail",
    )(x, a, ob, w_out, g_ffn, w_in, w_o, g_next, perm)


def _s5_tail(x, yg, perm_t, g_mix, dskip, w_glu, g_ffn, w_in, w_o, layer, g_fin, tm, chunk):
    t, d = x.shape
    t_scr, z_scr = _group_scratch(tm, d, chunk)
    return pl.pallas_call(
        functools.partial(_s5_tail_kernel, chunk=chunk),
        grid=(t // tm,),
        in_specs=[_row_spec(tm, d), _group_spec(tm, d, chunk), _const_spec(perm_t.shape),
                  _const_spec((1, d)), _const_spec((1, d)),
                  _const_spec(w_glu.shape), _const_spec((1, d)), _layer_spec(w_in, layer), _layer_spec(w_o, layer),
                  _const_spec((1, d))],
        out_specs=_row_spec(tm, d),
        out_shape=jax.ShapeDtypeStruct((t, d), F32),
        scratch_shapes=[pltpu.VMEM((tm, d), F32), z_scr, t_scr],
        compiler_params=_params(("arbitrary",)),
        name="s5_tail",
    )(x, yg, perm_t, g_mix, dskip, w_glu, g_ffn, w_in, w_o, g_fin)


SLAB_GROUPS = LANES // S5_GROUP
S5_MM_ROWS = 256


def _slab_perm():
    idx = np.arange(SLAB_GROUPS * LANES)
    l8, g8, p = idx // LANES, (idx % LANES) // S5_GROUP, idx % S5_GROUP
    dst = g8 * LANES + l8 * S5_GROUP + p
    return (dst[:, None] == idx[None, :]).astype(BF16)


def _permute_rows(z_ref, perm_ref, emit):
    step = min(S5_MM_ROWS, z_ref.shape[0])
    for r0 in range(0, z_ref.shape[0], step):
        emit(r0, _dot(z_ref[r0:r0 + step, :], perm_ref[...]))


def _group_rows(h, perm_ref, o_ref, h_ref, z_ref, chunk):
    rows = h.shape[0]
    nc = rows // chunk
    n_slab = h.shape[1] // LANES
    n_col = chunk // SLAB_GROUPS
    for v in range(n_slab):
        h_ref[v] = h[:, v * LANES:(v + 1) * LANES]
    for v in range(n_slab):
        for l in range(chunk):
            j, l8 = l // SLAB_GROUPS, l % SLAB_GROUPS
            r0 = (v * n_col + j) * nc
            z_ref[r0:r0 + nc, l8 * LANES:(l8 + 1) * LANES] = h_ref[v, pl.ds(l, nc, stride=chunk), :].astype(BF16)

    def emit(r0, blk):
        for q in range(blk.shape[0] // nc):
            v, j = divmod(r0 // nc + q, n_col)
            for g8 in range(SLAB_GROUPS):
                o_ref[v * SLAB_GROUPS + g8, :, j * LANES:(j + 1) * LANES] = (
                    blk[q * nc:(q + 1) * nc, g8 * LANES:(g8 + 1) * LANES].astype(o_ref.dtype))

    _permute_rows(z_ref, perm_ref, emit)


def _ungroup_rows(y_ref, perm_ref, z_ref, t_ref, chunk):
    n_slab, rows, _ = t_ref.shape
    nc = rows // chunk
    n_col = chunk // SLAB_GROUPS
    for v in range(n_slab):
        for j in range(n_col):
            r0 = (v * n_col + j) * nc
            for g8 in range(SLAB_GROUPS):
                z_ref[r0:r0 + nc, g8 * LANES:(g8 + 1) * LANES] = y_ref[v * SLAB_GROUPS + g8, :, j * LANES:(j + 1) * LANES]

    def emit(r0, blk):
        for q in range(blk.shape[0] // nc):
            v, j = divmod(r0 // nc + q, n_col)
            for l8 in range(SLAB_GROUPS):
                t_ref[v, pl.ds(j * SLAB_GROUPS + l8, nc, stride=chunk), :] = (
                    blk[q * nc:(q + 1) * nc, l8 * LANES:(l8 + 1) * LANES])

    _permute_rows(z_ref, perm_ref, emit)


S5_LP = S5_CHUNK * S5_GROUP
S5_RI = 2 * S5_STATE


def _s5_prep_kernel(*refs):
    for parity in range(2):
        _s5_prep_group(parity, *[r.at[parity] for r in refs])


def _s5_prep_group(parity, lam_row_ref, ls_ref, btr_ref, bti_ref, cr_ref, ci_ref,
                   bs_ref, cs_ref, tp_ref, ap_ref):
    n_pow = S5_CHUNK + 1
    dt = jnp.exp(ls_ref[...])

    def powers(lr, li):
        mag = jnp.exp(lr * dt)
        a_re, a_im = mag * jnp.cos(li * dt), mag * jnp.sin(li * dt)
        pw = [(jnp.ones_like(a_re), jnp.zeros_like(a_im))]
        for _ in range(n_pow - 1):
            pr, pi = pw[-1]
            pw.append((pr * a_re - pi * a_im, pr * a_im + pi * a_re))
        return pw

    lr, li = lam_row_ref[0:1, :], lam_row_ref[1:2, :]
    pw_row = powers(lr, li)
    a_re, a_im = pw_row[1]
    den = lr * lr + li * li
    f_re = ((a_re - 1.0) * lr + a_im * li) / den
    f_im = (a_im * lr - (a_re - 1.0) * li) / den
    blk = (S5_GROUP, S5_RI)
    w_re = jnp.concatenate([jnp.broadcast_to(f_re * pw_row[S5_CHUNK - 1 - l][0] - f_im * pw_row[S5_CHUNK - 1 - l][1], blk)
                            for l in range(S5_CHUNK)], axis=0)
    w_im = jnp.concatenate([jnp.broadcast_to(f_re * pw_row[S5_CHUNK - 1 - l][1] + f_im * pw_row[S5_CHUNK - 1 - l][0], blk)
                            for l in range(S5_CHUNK)], axis=0)

    lane = lax.broadcasted_iota(jnp.int32, (1, S5_RI), 1)
    own = (lane // S5_STATE) == parity
    low = lane < S5_STATE
    keep = lambda z: jnp.where(own, z, 0.0)
    per_pos = lambda ref: jnp.concatenate([ref[...]] * S5_CHUNK, axis=0)
    btr, bti = per_pos(btr_ref), per_pos(bti_ref)
    bs_ref[:, :S5_RI] = keep(w_re * btr - w_im * bti).astype(bs_ref.dtype)
    bs_ref[:, S5_RI:] = keep(w_re * bti + w_im * btr).astype(bs_ref.dtype)
    bf_t = jnp.where(low, f_re * btr[0:S5_GROUP] - f_im * bti[0:S5_GROUP],
                     f_re * bti[0:S5_GROUP] + f_im * btr[0:S5_GROUP])
    ap_ref[...] = jnp.zeros_like(ap_ref)
    for i, (j, part) in enumerate(((S5_CHUNK, 0), (S5_CHUNK, 1), (S5_CHUNK // 2, 0), (S5_CHUNK // 2, 1))):
        ap_ref[i:i + 1, :] = keep(pw_row[j][part])

    def spread(j0, part):
        return jnp.concatenate([jnp.broadcast_to(pw_row[j0 + l][part], blk) for l in range(S5_CHUNK)], axis=0)

    cr, ci = per_pos(cr_ref), per_pos(ci_ref)
    cs0 = jnp.where(low, cr * spread(0, 0) - ci * spread(0, 1),
                    -(cr * spread(0, 1) + ci * spread(0, 0)))
    cs_ref[:, :S5_RI] = keep(cr * spread(1, 0) - ci * spread(1, 1)).astype(cs_ref.dtype)
    cs_ref[:, S5_RI:] = keep(-(cr * spread(1, 1) + ci * spread(1, 0))).astype(cs_ref.dtype)
    r = lax.dot_general(bf_t, cs0, (((1,), (1,)), ((), ())), preferred_element_type=F32,
                        precision=lax.Precision.HIGHEST)
    lane_lp = lax.broadcasted_iota(jnp.int32, (S5_GROUP, S5_LP), 1)
    for l in range(S5_CHUNK):
        sh = l * S5_GROUP
        blk_l = r if l == 0 else jnp.where(lane_lp >= sh, pltpu.roll(r, sh, axis=1), 0.0)
        tp_ref[l * S5_GROUP:(l + 1) * S5_GROUP, :] = blk_l.astype(tp_ref.dtype)


def _s5_prep(lam_re, lam_im, log_step, b_re, b_im, c_re, c_im):
    g = lam_re.shape[0]
    dup = lambda z: jnp.concatenate([z, z], axis=-1)
    lam_row = jnp.stack([dup(lam_re), dup(lam_im)], axis=1)
    lam_row = jnp.pad(lam_row, ((0, 0), (0, 6), (0, 0)))
    ls = log_step.reshape(g, 1, 1)
    btr, bti = dup(jnp.swapaxes(b_re, 1, 2)), dup(jnp.swapaxes(b_im, 1, 2))
    cr, ci = dup(c_re), dup(c_im)
    gspec = lambda *s: pl.BlockSpec((2,) + s, lambda i: (i,) + (0,) * len(s))
    table = jax.ShapeDtypeStruct((g, S5_LP, 2 * S5_RI), BF16)
    return pl.pallas_call(
        _s5_prep_kernel,
        grid=(g // 2,),
        in_specs=[gspec(8, S5_RI), gspec(1, 1)] + [gspec(S5_GROUP, S5_RI)] * 4,
        out_specs=[gspec(S5_LP, 2 * S5_RI)] * 2 + [gspec(S5_LP, S5_LP), gspec(8, S5_RI)],
        out_shape=[table] * 2 + [jax.ShapeDtypeStruct((g, S5_LP, S5_LP), BF16),
                                 jax.ShapeDtypeStruct((g, 8, S5_RI), F32)],
        compiler_params=_params(("arbitrary",)),
        name="s5_prep",
    )(lam_row, ls, btr, bti, cr, ci)


def _s5_scan_kernel(u_ref, us_ref, x0r_ref, x0i_ref, bs_ref, cs_ref, tp_ref, ap_ref,
                    y_ref, ys_ref, finr_ref, fini_ref, finsr_ref, finsi_ref, xs_ref, *, bsz, chunks):
    n_rows = bsz * chunks
    pair = range(2)
    us_in = [u_ref[g] for g in pair]
    x = sum(_dot(us_in[g], bs_ref[g]) for g in pair)
    xr, xi = x[:, :S5_RI], x[:, S5_RI:]
    row = lax.broadcasted_iota(jnp.int32, (n_rows, 1), 0) % chunks
    ap = ap_ref[0] + ap_ref[1]
    ar, ai = ap[0:1, :], ap[1:2, :]

    def prefix(xr, xi, ar, ai, pos, length):
        shift = 1
        while shift < length:
            sr = jnp.where(pos >= shift, pltpu.roll(xr, shift, axis=0), 0.0)
            si = jnp.where(pos >= shift, pltpu.roll(xi, shift, axis=0), 0.0)
            xr, xi = xr + (sr * ar - si * ai), xi + (sr * ai + si * ar)
            ar, ai = ar * ar - ai * ai, 2.0 * (ar * ai)
            shift *= 2
        return xr, xi, ar, ai

    n_runs, runs = n_rows // SUBLANES, chunks // SUBLANES
    xr, xi, br, bi = prefix(xr, xi, ar, ai, row % SUBLANES, SUBLANES)
    xs_ref[0], xs_ref[1] = xr, xi
    last = pl.ds(SUBLANES - 1, n_runs, stride=SUBLANES)
    run = lax.broadcasted_iota(jnp.int32, (n_runs, 1), 0) % runs
    er, ei, _, _ = prefix(xs_ref[0, last, :], xs_ref[1, last, :], br, bi, run, runs)
    cr = jnp.where(run >= 1, pltpu.roll(er, 1, axis=0), 0.0)
    ci = jnp.where(run >= 1, pltpu.roll(ei, 1, axis=0), 0.0)
    pr, pi = ar, ai
    for r in range(SUBLANES):
        dst = pl.ds(r, n_runs, stride=SUBLANES)
        xs_ref[0, dst, :] = cr * pr - ci * pi
        xs_ref[1, dst, :] = cr * pi + ci * pr
        pr, pi = pr * ar - pi * ai, pr * ai + pi * ar
    xr, xi = xr + xs_ref[0], xi + xs_ref[1]
    pr = jnp.where(row >= 1, pltpu.roll(xr, 1, axis=0), 0.0).astype(BF16)
    pi = jnp.where(row >= 1, pltpu.roll(xi, 1, axis=0), 0.0).astype(BF16)
    prev = jnp.concatenate([pr, pi], axis=1)
    for g in pair:
        y_ref[g] = (_dot(us_in[g], tp_ref[g]) + _dot_nt(prev, cs_ref[g])).astype(y_ref.dtype)
    finr_ref[...] = jnp.zeros_like(finr_ref)
    fini_ref[...] = jnp.zeros_like(fini_ref)
    for b in range(bsz):
        last = slice((b + 1) * chunks - 1, (b + 1) * chunks)
        finr_ref[b:b + 1, :] = xr[last]
        fini_ref[b:b + 1, :] = xi[last]

    half = S5_LP // 2
    x0r, x0i = x0r_ref[...], x0i_ref[...]
    hr, hi = ap[2:3, :], ap[3:4, :]
    xs = sum(_dot(us_ref[g], bs_ref[g, half:, :]) for g in pair)
    finsr_ref[...] = x0r * hr - x0i * hi + xs[:, :S5_RI]
    finsi_ref[...] = x0r * hi + x0i * hr + xs[:, S5_RI:]
    x0 = jnp.concatenate([x0r, x0i], axis=1).astype(BF16)
    for g in pair:
        ys_ref[g] = (_dot(us_ref[g], tp_ref[g, :half, :half]) + _dot_nt(x0, cs_ref[g, :half, :])).astype(ys_ref.dtype)


def _s5_scan(u_g, us_g, x0r, x0i, bs, cs, tp, ap, bsz):
    g, n_rows, _ = u_g.shape
    n_s = us_g.shape[1]
    half = S5_LP // 2
    pspec = lambda *s: pl.BlockSpec((2,) + s, lambda i: (i,) + (0,) * len(s))
    ospec = lambda *s: pl.BlockSpec((None,) + s, lambda i: (i,) + (0,) * len(s))
    packed = lambda n: jax.ShapeDtypeStruct((g // 2, n, S5_RI), F32)
    return pl.pallas_call(
        functools.partial(_s5_scan_kernel, bsz=bsz, chunks=n_rows // bsz),
        grid=(g // 2,),
        in_specs=[pspec(n_rows, S5_LP), pspec(n_s, half), ospec(n_s, S5_RI), ospec(n_s, S5_RI)]
                 + [pspec(S5_LP, 2 * S5_RI)] * 2 + [pspec(S5_LP, S5_LP), pspec(8, S5_RI)],
        out_specs=[pspec(n_rows, S5_LP), pspec(n_s, half), ospec(8, S5_RI), ospec(8, S5_RI),
                   ospec(n_s, S5_RI), ospec(n_s, S5_RI)],
        out_shape=[jax.ShapeDtypeStruct((g, n_rows, S5_LP), BF16), jax.ShapeDtypeStruct((g, n_s, half), BF16),
                   packed(8), packed(8), packed(n_s), packed(n_s)],
        scratch_shapes=[pltpu.VMEM((2, n_rows, S5_RI), F32)],
        compiler_params=_params(("arbitrary",)),
        name="s5_scan",
    )(u_g, us_g, x0r, x0i, bs, cs, tp, ap)


def kernel(x_prompt, x_sample, state_ret, state_swa_k, state_swa_v, state_ssm_re, state_ssm_im, norm_mix, norm_ffn, norm_final, w_in_ab, ret_gn, w_out_ab, ssm_lam_re, ssm_lam_im, ssm_log_step, ssm_b_re, ssm_b_im, ssm_c_re, ssm_c_im, ssm_d, w_glu, w_ffn_in, w_ffn_out):
    bsz, seq, d = x_prompt.shape
    dbsz, n_new, _ = x_sample.shape
    assert state_swa_k.shape[2] == SWA_BUF and n_new == S5_CHUNK // 2 and seq % SWA_ROWS == 0
    xp = x_prompt.reshape(bsz * seq, d)
    xs = x_sample.reshape(dbsz * n_new, d)
    tm_p, tm_s = TAIL_ROWS, dbsz * n_new
    row = lambda v: v.reshape(1, -1)

    w_in0 = w_in_ab[0].astype(BF16)
    g_mix0, g_ffn0, gn0 = row(norm_mix[0]), row(norm_ffn[0]), row(ret_gn[0])

    ra_p, qkv_p = _ab_proj(xp, g_mix0, w_in0, PROJ_ROWS)
    ra_s, qkv_s = _ab_proj(xs, g_mix0, w_in0, tm_s)

    zero_ret = jnp.zeros((bsz, H_A, DK_A, DV_A), F32)
    a_p, ret_p = _retention(ra_p, zero_ret, gn0, seq=seq, c_real=RET_CHUNK, n_chunks=RET_STEP_CHUNKS, n_seq=bsz,
                            out_dtype=BF16)
    a_s, ret_s = _retention(ra_s, state_ret[0], gn0, seq=n_new, c_real=n_new, n_chunks=1, n_seq=RET_SAMPLE_SEQS,
                            out_dtype=F32)

    width_b = H_B * DH_B
    rows_last = lambda w: w.transpose(0, 2, 3, 1).reshape(dbsz, width_b, SWA_BUF)
    rows_first = lambda w: w.reshape(dbsz, H_B, DH_B, SWA_BUF).transpose(0, 3, 1, 2)[None]
    later = (w_out_ab[0], w_ffn_in.reshape(-1, w_ffn_in.shape[-1]), w_ffn_out.reshape(-1, w_ffn_out.shape[-1]), w_glu[0])
    ob_p, ob_s, swk_s, swv_s, w_out0, w_f_in, w_f_out, w_glu0 = _swa(
        qkv_p, bsz, seq, qkv_s, rows_last(state_swa_k[0]), rows_last(state_swa_v[0]), n_new, later)
    w_f_in, w_f_out = w_f_in.reshape(w_ffn_in.shape), w_f_out.reshape(w_ffn_out.shape)
    kv_tail = qkv_p.reshape(bsz, seq, QKV_W)[:, seq - SWA_BUF:, width_b:]
    swk_p = kv_tail[..., :width_b].reshape(bsz, SWA_BUF, H_B, DH_B)
    swv_p = kv_tail[..., width_b:].reshape(bsz, SWA_BUF, H_B, DH_B)

    g_mix1, g_ffn1 = row(norm_mix[1]), row(norm_ffn[1])
    perm = _slab_perm()
    yp, u_p = _ab_tail(xp, a_p, ob_p, w_out0, g_ffn0, w_f_in, w_f_out, 0, g_mix1, perm, tm_p, S5_CHUNK)
    ys, u_s = _ab_tail(xs, a_s, ob_s, w_out0, g_ffn0, w_f_in, w_f_out, 0, g_mix1, perm, tm_s, n_new)

    operators = _s5_prep(ssm_lam_re[0], ssm_lam_im[0], ssm_log_step[0],
                         ssm_b_re[0], ssm_b_im[0], ssm_c_re[0], ssm_c_im[0])
    pack = lambda st: st.reshape(st.shape[0], -1, S5_RI).transpose(1, 0, 2)
    unpack = lambda st: st.transpose(1, 0, 2).reshape(st.shape[1], -1, S5_STATE)[None]
    y5_p, y5_s, finr_p, fini_p, finr_s, fini_s = _s5_scan(
        u_p, u_s, pack(state_ssm_re[0]), pack(state_ssm_im[0]), *operators, bsz)

    tail1 = (perm.T, g_mix1, row(ssm_d[0]), w_glu0, g_ffn1,
             w_f_in, w_f_out, 1, row(norm_final))
    yp = _s5_tail(yp, y5_p, *tail1, tm_p, S5_CHUNK)
    ys = _s5_tail(ys, y5_s, *tail1, tm_s, n_new)

    return (yp.reshape(bsz, seq, d), ys.reshape(dbsz, n_new, d),
            ret_p[None], ret_s[None],
            swk_p[None], swv_p[None],
            rows_first(swk_s), rows_first(swv_s),
            unpack(finr_p[:, :bsz]), unpack(fini_p[:, :bsz]), unpack(finr_s), unpack(fini_s))
```

```python
import functools

import jax
import jax.numpy as jnp
import numpy as np
from jax import lax
from jax.experimental import pallas as pl
from jax.experimental.pallas import tpu as pltpu

F32 = jnp.float32
BF16 = jnp.bfloat16

H_A, DK_A, DV_A = 4, 64, 128
H_B, DH_B = 8, 64
SWA_PAIRS = ((128, 1), (512, 4), (2048, 16))
SPAN = 128
SWA_BUF = 2048
PAST_LEN = 16384
RET_CHUNK = 128
S5_GROUP, S5_STATE = 16, 64
S5_CHUNK = 16
EPS = 1e-6
NEG_INF = -1e30
QA_W, KA_W, VA_W, GA_W = H_A * DK_A, H_A * DK_A, H_A * DV_A, H_A * DV_A
RA_W = QA_W + KA_W + VA_W + GA_W
QKV_W = 3 * H_B * DH_B

LANES = 128
SUBLANES = 8
VMEM_LIMIT = 58 * 1024 * 1024

PROJ_ROWS = 1024
PROJ_COLS = 512
TAIL_ROWS = 512
RET_STEP_CHUNKS = 4
RET_SAMPLE_SEQS = 4


def _params(sem):
    return pltpu.CompilerParams(dimension_semantics=sem, vmem_limit_bytes=VMEM_LIMIT)


def _const_spec(shape):
    nd = len(shape)
    return pl.BlockSpec(shape, lambda *_: (0,) * nd, pipeline_mode=pl.Buffered(1))


def _rms(x, g):
    return x * lax.rsqrt(jnp.mean(x * x, axis=-1, keepdims=True) + EPS) * g


def _dot(a, b):
    return jnp.dot(a, b, preferred_element_type=F32)


def _dot_nt(a, b):
    return lax.dot_general(a, b, (((1,), (1,)), ((), ())), preferred_element_type=F32)


def _dot_tn(a, b):
    return lax.dot_general(a, b, (((0,), (0,)), ((), ())), preferred_element_type=F32)


def _ab_proj_kernel(x_ref, g_ref, w_ref, ra_ref, qkv_ref):
    h = _rms(x_ref[...], g_ref[...]).astype(BF16)
    for n0 in range(0, RA_W, PROJ_COLS):
        ra_ref[:, n0:n0 + PROJ_COLS] = _dot(h, w_ref[:, n0:n0 + PROJ_COLS])
    for n0 in range(0, QKV_W, PROJ_COLS):
        qkv_ref[:, n0:n0 + PROJ_COLS] = _dot(h, w_ref[:, RA_W + n0:RA_W + n0 + PROJ_COLS])


def _ab_proj(x, g, w_bf16, tm):
    t, d = x.shape
    return pl.pallas_call(
        _ab_proj_kernel,
        grid=(t // tm,),
        in_specs=[pl.BlockSpec((tm, d), lambda i: (i, 0)),
                  _const_spec((1, d)),
                  _const_spec((d, RA_W + QKV_W))],
        out_specs=[pl.BlockSpec((tm, RA_W), lambda i: (i, 0)),
                   pl.BlockSpec((tm, QKV_W), lambda i: (i, 0))],
        out_shape=[jax.ShapeDtypeStruct((t, RA_W), F32), jax.ShapeDtypeStruct((t, QKV_W), F32)],
        compiler_params=_params(("arbitrary",)),
        name="ab_proj",
    )(x, g, w_bf16)


def _retention_tables(c_real):
    c = RET_CHUNK
    log_g = np.log1p(-np.exp2(-5.0 - np.arange(H_A, dtype=F32)))
    idx = np.arange(c, dtype=F32)
    rel = idx[:, None] - idx[None, :]
    dec = np.where(rel >= 0, np.exp(np.maximum(rel, 0.0)[None] * log_g[:, None, None]), 0.0)
    dec = dec.reshape(H_A * c, c)
    real = (idx < c_real)[:, None]
    qd = np.exp((idx + 1.0)[:, None] * log_g[None, :])
    qd = np.repeat(qd, DV_A, axis=1)
    kd = np.where(real, np.exp((c_real - 1.0 - idx)[:, None] * log_g[None, :]), 0.0)
    kd = np.repeat(kd, DK_A, axis=1)
    row_h = np.arange(H_A * DK_A)[:, None] // DK_A
    col_h = np.arange(H_A * DV_A)[None, :] // DV_A
    bd = (row_h == col_h).astype(F32)
    dm = bd * np.repeat(np.exp(c_real * log_g), DV_A)[None, :]
    return tuple(np.asarray(tb, F32) for tb in (dec, qd, kd, dm))


def _retention_kernel(ra_ref, s0_ref, gain_ref, dec_ref, qd_ref, kd_ref, dm_ref,
                      a_ref, sout_ref, sbd_ref, *, c_real, n_chunks, n_seq):
    c = RET_CHUNK
    j = pl.program_id(1)

    @pl.when(j == 0)
    def _():
        sbd_ref[...] = jnp.zeros_like(sbd_ref)
        for s in range(n_seq):
            for h in range(H_A):
                sbd_ref[s, h * DK_A:(h + 1) * DK_A, h * DV_A:(h + 1) * DV_A] = s0_ref[s, h]

    lane_q = lax.broadcasted_iota(jnp.int32, (c, QA_W), 1) // DK_A
    for ci in range(n_chunks):
        for s in range(n_seq):
            _retention_chunk(ra_ref.at[s], a_ref.at[s], sbd_ref.at[s], gain_ref, dec_ref, qd_ref, kd_ref, dm_ref,
                             lane_q, ci, c_real)

    @pl.when(j == pl.num_programs(1) - 1)
    def _():
        for s in range(n_seq):
            for h in range(H_A):
                sout_ref[s, h] = sbd_ref[s, h * DK_A:(h + 1) * DK_A, h * DV_A:(h + 1) * DV_A]


def _retention_chunk(ra_ref, a_ref, sbd_ref, gain_ref, dec_ref, qd_ref, kd_ref, dm_ref, lane_q, ci, c_real):
    c = RET_CHUNK
    rows = ra_ref[ci * c_real:(ci + 1) * c_real, :]
    gate = rows[:, QA_W + KA_W + VA_W:]
    if c_real < c:
        rows = jnp.concatenate([rows, jnp.zeros((c - c_real, RA_W), F32)], axis=0)
    q = rows[:, :QA_W]
    k = rows[:, QA_W:QA_W + KA_W] * (DK_A ** -0.5)
    v = rows[:, QA_W + KA_W:QA_W + KA_W + VA_W]
    vb = v.astype(BF16)
    qm = jnp.concatenate([jnp.where(lane_q == h, q, 0.0) for h in range(H_A)], axis=0).astype(BF16)
    s = _dot_nt(qm, k.astype(BF16)) * dec_ref[...]
    sb = s.astype(BF16)
    sbd = sbd_ref[...]
    cross = _dot(q.astype(BF16), sbd.astype(BF16)) * qd_ref[...]
    upd = _dot_tn((k * kd_ref[...]).astype(BF16), vb)
    for h in range(H_A):
        blk = (slice(h * DK_A, (h + 1) * DK_A), slice(h * DV_A, (h + 1) * DV_A))
        sbd_ref[blk] = sbd[blk] * dm_ref[blk] + upd[blk]
    for h in range(H_A):
        sl = slice(h * DV_A, (h + 1) * DV_A)
        o = _dot(sb[h * c:(h + 1) * c], vb[:, sl]) + cross[:, sl]
        o = o[:c_real]
        mu = jnp.mean(o, axis=-1, keepdims=True)
        var = jnp.mean(jnp.square(o - mu), axis=-1, keepdims=True)
        y = (o - mu) * lax.rsqrt(var + EPS) * gain_ref[:, sl]
        gh = gate[:, sl]
        a_ref[ci * c_real:(ci + 1) * c_real, sl] = (gh * jax.nn.sigmoid(gh) * y).astype(a_ref.dtype)


def _retention(ra, state0, gain, *, seq, c_real, n_chunks, n_seq, out_dtype):
    t = ra.shape[0]
    bsz = t // seq
    rows = c_real * n_chunks
    steps = seq // rows
    tables = _retention_tables(c_real)
    kern = functools.partial(_retention_kernel, c_real=c_real, n_chunks=n_chunks, n_seq=n_seq)
    a, s_out = pl.pallas_call(
        kern,
        grid=(bsz // n_seq, steps),
        in_specs=[pl.BlockSpec((n_seq, rows, RA_W), lambda b, j: (b, j, 0)),
                  pl.BlockSpec((n_seq, H_A, DK_A, DV_A), lambda b, j: (b, 0, 0, 0)),
                  _const_spec((1, VA_W))] + [_const_spec(tb.shape) for tb in tables],
        out_specs=[pl.BlockSpec((n_seq, rows, VA_W), lambda b, j: (b, j, 0)),
                   pl.BlockSpec((n_seq, H_A, DK_A, DV_A), lambda b, j: (b, 0, 0, 0))],
        out_shape=[jax.ShapeDtypeStruct((bsz, seq, VA_W), out_dtype),
                   jax.ShapeDtypeStruct((bsz, H_A, DK_A, DV_A), F32)],
        scratch_shapes=[pltpu.VMEM((n_seq, H_A * DK_A, H_A * DV_A), F32)],
        compiler_params=_params(("arbitrary", "arbitrary")),
        name="retention",
    )(ra.reshape(bsz, seq, RA_W), state0, gain, *tables)
    return a.reshape(t, VA_W), s_out


SWA_ROWS = 2048
N_BRANCH = len(SWA_PAIRS)


def _alibi_slopes():
    return np.exp2(-8.0 * np.arange(1, H_B + 1, dtype=F32) / H_B)


def _swa_prompt_bias():
    qi = np.arange(SPAN)[:, None]
    kj = np.arange(2 * SPAN)[None, :]
    dist = SPAN + qi - kj
    band = (dist >= 0) & (dist <= SPAN)
    slopes = _alibi_slopes()
    out = []
    for (_, dil) in SWA_PAIRS:
        pen = -slopes[:, None, None] * (dil * dist).astype(F32)[None]
        normal = np.where(band[None], pen, NEG_INF)
        first = np.where((band & (kj >= SPAN))[None], pen, NEG_INF)
        out.append(np.stack([normal, first], axis=1))
    tab = np.stack(out, axis=1)
    tab = tab.reshape(H_B // 2, 2, N_BRANCH, 2, SPAN, 2 * SPAN).transpose(0, 2, 3, 1, 4, 5)
    return np.asarray(tab.reshape(H_B // 2, N_BRANCH, 2, 2 * SPAN, 2 * SPAN), F32)


COARSE = 4


def _swa_unit(q, kk, vv, bias, head0):
    q = q * (DH_B ** -0.5)
    qm = jnp.concatenate([jnp.where(head0, q, 0.0), jnp.where(head0, 0.0, q)], axis=0).astype(BF16)
    s = _dot_nt(qm, kk.astype(BF16)) + bias
    m = jnp.max(s, axis=-1, keepdims=True)
    p = jnp.exp(s - m).astype(BF16)
    ones = jnp.ones((2 * SPAN, LANES), BF16)
    res = _dot(p, jnp.concatenate([vv.astype(BF16), ones], axis=1))
    acc = jnp.where(head0, res[:SPAN, :LANES], res[SPAN:, :LANES])
    den = jnp.where(head0, res[:SPAN, LANES:], res[SPAN:, LANES:])
    mm = jnp.where(head0, jnp.broadcast_to(m[:SPAN], (SPAN, LANES)), jnp.broadcast_to(m[SPAN:], (SPAN, LANES)))
    return acc, den, mm


SWA_PREV = SPAN * COARSE


def _swa_prompt_half(half, q_ref, k_ref, v_ref, bias_ref, o_ref, kbuf, vbuf, acc_ref, l_ref, m_ref,
                     q4, k4, v4, acc4, l4, m4):
    j = pl.program_id(2)
    rows = SWA_ROWS
    sub = rows // COARSE
    slot = j % 2

    @pl.when((half == 0) & (j > 0))
    def _():
        kbuf[0:SWA_PREV, :] = kbuf[rows:rows + SWA_PREV, :]
        vbuf[0:SWA_PREV, :] = vbuf[rows:rows + SWA_PREV, :]

    @pl.when((half == 0) & (j == 0))
    def _():
        k4[1] = jnp.zeros(k4.shape[1:], F32)
        v4[1] = jnp.zeros(v4.shape[1:], F32)
        kbuf[0:SWA_PREV, :] = jnp.zeros((SWA_PREV, LANES), F32)
        vbuf[0:SWA_PREV, :] = jnp.zeros((SWA_PREV, LANES), F32)

    @pl.when(half == 0)
    def _():
        kbuf[SWA_PREV:, :] = k_ref[...]
        vbuf[SWA_PREV:, :] = v_ref[...]
        for c in range(COARSE):
            q4[c] = q_ref[pl.ds(c, sub, stride=COARSE), :]
            k4[slot, c] = k_ref[pl.ds(c, sub, stride=COARSE), :]
            v4[slot, c] = v_ref[pl.ds(c, sub, stride=COARSE), :]

    lane = lax.broadcasted_iota(jnp.int32, (SPAN, LANES), 1)
    head0 = lane < DH_B
    first_step = (j == 0).astype(jnp.int32)

    for g, (_, dil) in enumerate(SWA_PAIRS):
        blocks = rows // (SPAN * dil)
        per_half = blocks * dil // 2

        def unit(u, g=g, dil=dil):
            wb = u // dil
            r = u % dil
            q_start = wb * (SPAN * dil) + r
            k_start = SWA_PREV + (wb - 1) * (SPAN * dil) + r
            variant = jnp.where(wb == 0, first_step, 0)
            acc, den, mm = _swa_unit(q_ref[pl.ds(q_start, SPAN, stride=dil), :],
                                     kbuf[pl.ds(k_start, 2 * SPAN, stride=dil), :],
                                     vbuf[pl.ds(k_start, 2 * SPAN, stride=dil), :],
                                     bias_ref[0, g, variant], head0)
            acc_ref[g, pl.ds(q_start, SPAN, stride=dil), :] = acc
            l_ref[g, pl.ds(q_start, SPAN, stride=dil), :] = den
            m_ref[g, pl.ds(q_start, SPAN, stride=dil), :] = mm

        def unit_two_level(u, g=g, fine=dil // COARSE):
            c = u % COARSE
            f = u // COARSE
            pick = pl.ds(f, SPAN, stride=fine)
            acc, den, mm = _swa_unit(q4[c, pick, :],
                                     jnp.concatenate([k4[1 - slot, c, pick, :], k4[slot, c, pick, :]], axis=0),
                                     jnp.concatenate([v4[1 - slot, c, pick, :], v4[slot, c, pick, :]], axis=0),
                                     bias_ref[0, g, first_step], head0)
            dst = pl.ds(c * sub + f, SPAN, stride=fine)
            acc4[dst, :] = acc
            l4[dst, :] = den
            m4[dst, :] = mm

        two_level = dil % (COARSE * COARSE) == 0
        assert not two_level or blocks == 1
        body = unit_two_level if two_level else unit

        @pl.when(half >= 0)
        def _(body=body, per_half=per_half):
            for i in range(per_half):
                body(half * per_half + i)

        if two_level:
            @pl.when(half == 1)
            def _(g=g):
                for c in range(COARSE):
                    src, dst = slice(c * sub, (c + 1) * sub), pl.ds(c, sub, stride=COARSE)
                    acc_ref[g, dst, :] = acc4[src, :]
                    l_ref[g, dst, :] = l4[src, :]
                    m_ref[g, dst, :] = m4[src, :]

    @pl.when(half == 1)
    def _():
        tile = 256
        for r0 in range(0, rows, tile):
            sl = slice(r0, r0 + tile)
            ms = [m_ref[g, sl, :] for g in range(N_BRANCH)]
            mx = functools.reduce(jnp.maximum, ms)
            ws = [jnp.exp(mg - mx) for mg in ms]
            num = sum(w * acc_ref[g, sl, :] for g, w in enumerate(ws))
            den = sum(w * l_ref[g, sl, :] for g, w in enumerate(ws))
            o_ref[sl, :] = (num / den).astype(o_ref.dtype)


def _swa_sample_bias(n_new):
    t = np.arange(n_new)[:, None]
    slopes = _alibi_slopes()

    def table(j):
        dist = SWA_BUF + t - j
        out = []
        for (window, dil) in SWA_PAIRS:
            valid = (dist >= 0) & (dist <= window) & (dist % dil == 0) & (PAST_LEN + t - dist >= 0)
            pen = -slopes[:, None, None] * dist.astype(F32)[None]
            out.append(np.where(valid[None], pen, NEG_INF).reshape(H_B * n_new, -1))
        return np.stack(out)

    bias_a = table(np.arange(SWA_BUF)[None, :])
    jb = np.arange(LANES)[None, :]
    bias_b = np.where(jb < n_new, table(SWA_BUF + jb), NEG_INF)
    return np.asarray(bias_a, F32), np.asarray(bias_b, F32)


def _pad_new_rows(new_ref, n_new):
    return jnp.concatenate([new_ref[...], jnp.zeros((LANES - n_new, new_ref.shape[1]), F32)], axis=0)


def _swa_sample_attend(q_ref, k_ref, v_ref, kp_ref, vp_ref, ba_ref, bb_ref, o_ref, n_new):
    width = q_ref.shape[1]
    n_heads = width // DH_B
    lane_h = lax.broadcasted_iota(jnp.int32, (n_new, width), 1) // DH_B
    q = q_ref[...] * (DH_B ** -0.5)
    qm = jnp.concatenate([jnp.where(lane_h == h, q, 0.0) for h in range(n_heads)], axis=0).astype(BF16)
    k_new, v_new = _pad_new_rows(k_ref, n_new), _pad_new_rows(v_ref, n_new)
    s_a = _dot(qm, kp_ref[0].astype(BF16))
    s_b = _dot_nt(qm, k_new.astype(BF16))
    sa = [s_a + ba_ref[g] for g in range(N_BRANCH)]
    sb = [s_b + bb_ref[g] for g in range(N_BRANCH)]
    mx = functools.reduce(jnp.maximum, [jnp.max(x, axis=-1, keepdims=True) for x in sa + sb])
    p_a = sum(jnp.exp(x - mx) for x in sa).astype(BF16)
    p_b = sum(jnp.exp(x - mx) for x in sb).astype(BF16)
    den = (jnp.sum(p_a.astype(F32), axis=-1, keepdims=True)
           + jnp.sum(p_b.astype(F32), axis=-1, keepdims=True))
    o = (_dot_nt(p_a, vp_ref[0].astype(BF16)) + _dot(p_b, v_new.astype(BF16))) / den
    o_ref[...] = sum(jnp.where(lane_h == h, o[h * n_new:(h + 1) * n_new], 0.0) for h in range(n_heads))

def _swa_window_update(src_ref, new_ref, dst_ref, n_new):
    lane = lax.broadcasted_iota(jnp.int32, (DH_B, LANES), 1)
    new_t = pltpu.roll(_pad_new_rows(new_ref, n_new).T, LANES - n_new, axis=1)
    for h in range(new_ref.shape[1] // DH_B):
        rows = slice(h * DH_B, (h + 1) * DH_B)
        shifted = pltpu.roll(src_ref[0, rows, :], SWA_BUF - n_new, axis=1)
        dst_ref[0, rows, 0:SWA_BUF - LANES] = shifted[:, 0:SWA_BUF - LANES]
        dst_ref[0, rows, SWA_BUF - LANES:] = jnp.where(lane >= LANES - n_new, new_t[rows],
                                                       shifted[:, SWA_BUF - LANES:])


N_SWA_IN, N_SWA_OUT = 11, 4


def _swa_kernel(*refs, n_new, n_cast):
    (q_ref, k_ref, v_ref, bias_ref, qs_ref, ks_ref, vs_ref, kp_ref, vp_ref, ba_ref, bb_ref), refs = (
        refs[:N_SWA_IN], refs[N_SWA_IN:])
    cast_src, refs = refs[:n_cast], refs[n_cast:]
    (o_ref, os_ref, ko_ref, vo_ref), refs = refs[:N_SWA_OUT], refs[N_SWA_OUT:]
    cast_dst, scratch = refs[:n_cast], refs[n_cast:]
    half = pl.program_id(3)
    _swa_prompt_half(half, q_ref, k_ref, v_ref, bias_ref, o_ref, *scratch)
    _swa_sample_attend(qs_ref, ks_ref, vs_ref, kp_ref, vp_ref, ba_ref, bb_ref, os_ref, n_new)
    _swa_window_update(kp_ref, ks_ref, ko_ref, n_new)
    _swa_window_update(vp_ref, vs_ref, vo_ref, n_new)

    @pl.when(half == 0)
    def _():
        for src, dst in zip(cast_src, cast_dst):
            dst[...] = src[...].astype(dst.dtype)


def _swa(qkv, bsz, seq, qkv_s, k_past, v_past, n_new, weights):
    t = qkv.shape[0]
    dbsz = k_past.shape[0]
    steps = seq // SWA_ROWS
    npair = H_B // 2
    width = H_B * DH_B
    half_w = width // 2
    assert dbsz == bsz * npair * steps, "one sample sequence per prompt (sequence, head pair, row block) step"
    bias = _swa_prompt_bias()
    bias_a, bias_b = _swa_sample_bias(n_new)
    blk = (SWA_ROWS, LANES)
    rows_of = lambda b, j: b * steps + j
    seq_of = lambda b, hp, j: (b * npair + hp) * steps + j
    new_blk = lambda col: pl.BlockSpec((n_new, half_w), lambda b, hp, j, h, col=col: (seq_of(b, hp, j), 2 * col + h))
    state_blk = pl.BlockSpec((1, half_w, SWA_BUF), lambda b, hp, j, h: (seq_of(b, hp, j), h, 0))
    half_rows = H_B // 2 * n_new
    sub = SWA_ROWS // COARSE
    assert all(w.shape[0] % (dbsz * 2 * SUBLANES) == 0 for w in weights), "bf16 row blocks are 16-row tiles"
    cast_blk = [pl.BlockSpec((w.shape[0] // dbsz, w.shape[1]), lambda b, hp, j, h: (seq_of(b, hp, j), 0))
                for w in weights]
    in_specs = [pl.BlockSpec(blk, lambda b, hp, j, h: (rows_of(b, j), hp)),
                pl.BlockSpec(blk, lambda b, hp, j, h: (rows_of(b, j), npair + hp)),
                pl.BlockSpec(blk, lambda b, hp, j, h: (rows_of(b, j), 2 * npair + hp)),
                pl.BlockSpec((1, N_BRANCH, 2, 2 * SPAN, 2 * SPAN), lambda b, hp, j, h: (hp, 0, 0, 0, 0)),
                new_blk(0), new_blk(1), new_blk(2), state_blk, state_blk,
                pl.BlockSpec((N_BRANCH, half_rows, SWA_BUF), lambda b, hp, j, h: (0, h, 0)),
                pl.BlockSpec((N_BRANCH, half_rows, LANES), lambda b, hp, j, h: (0, h, 0))]
    assert len(in_specs) == N_SWA_IN
    return pl.pallas_call(
        functools.partial(_swa_kernel, n_new=n_new, n_cast=len(weights)),
        grid=(bsz, npair, steps, 2),
        in_specs=in_specs + cast_blk,
        out_specs=[pl.BlockSpec(blk, lambda b, hp, j, h: (rows_of(b, j), hp)),
                   pl.BlockSpec((n_new, half_w), lambda b, hp, j, h: (seq_of(b, hp, j), h)),
                   state_blk, state_blk] + cast_blk,
        out_shape=[jax.ShapeDtypeStruct((t, width), BF16),
                   jax.ShapeDtypeStruct((dbsz * n_new, width), F32),
                   jax.ShapeDtypeStruct(k_past.shape, F32), jax.ShapeDtypeStruct(v_past.shape, F32)]
                  + [jax.ShapeDtypeStruct(w.shape, BF16) for w in weights],
        scratch_shapes=[pltpu.VMEM((SWA_PREV + SWA_ROWS, LANES), F32)] * 2
                       + [pltpu.VMEM((N_BRANCH, SWA_ROWS, LANES), F32)] * 3
                       + [pltpu.VMEM((COARSE, sub, LANES), F32)]
                       + [pltpu.VMEM((2, COARSE, sub, LANES), F32)] * 2
                       + [pltpu.VMEM((SWA_ROWS, LANES), F32)] * 3,
        compiler_params=_params(("arbitrary",) * 4),
        name="swa",
    )(qkv, qkv, qkv, bias, qkv_s, qkv_s, qkv_s, k_past, v_past, bias_a, bias_b, *weights)


FFN_CHUNK = 256


def _ffn(y1, g_ffn_ref, w_in_ref, w_o_ref, acc_ref):
    d_ff = w_o_ref.shape[0]
    h = _rms(y1, g_ffn_ref[...]).astype(BF16)
    for ci, c0 in enumerate(range(0, d_ff, FFN_CHUNK)):
        gate = _dot(h, w_in_ref[:, c0:c0 + FFN_CHUNK])
        up = _dot(h, w_in_ref[:, d_ff + c0:d_ff + c0 + FFN_CHUNK])
        act = (gate * jax.nn.sigmoid(gate) * up).astype(BF16)
        part = _dot(act, w_o_ref[c0:c0 + FFN_CHUNK, :])
        if ci == 0:
            acc_ref[...] = part
        else:
            acc_ref[...] += part
    return y1 + acc_ref[...]


def _ab_tail_kernel(x_ref, a_ref, ob_ref, w_out_ref, g_ffn_ref, w_in_ref, w_o_ref, g_next_ref, perm_ref,
                    o_ref, u_ref, acc_ref, h_ref, z_ref, *, chunk):
    mix = (_dot(a_ref[...].astype(BF16), w_out_ref[0:VA_W, :])
           + _dot(ob_ref[...].astype(BF16), w_out_ref[VA_W:, :]))
    y = _ffn(x_ref[...] + mix, g_ffn_ref, w_in_ref, w_o_ref, acc_ref)
    o_ref[...] = y
    _group_rows(_rms(y, g_next_ref[...]), perm_ref, u_ref, h_ref, z_ref, chunk)


def _s5_tail_kernel(x_ref, yg_ref, perm_ref, g_mix_ref, dskip_ref, w_glu_ref, g_ffn_ref, w_in_ref, w_o_ref,
                    g_fin_ref, o_ref, acc_ref, z_ref, t_ref, *, chunk):
    x = x_ref[...]
    d = x.shape[-1]
    u = _rms(x, g_mix_ref[...])
    _ungroup_rows(yg_ref, perm_ref, z_ref, t_ref, chunk)
    ys = jnp.concatenate([t_ref[v] for v in range(d // LANES)], axis=1)
    z = jax.nn.gelu(ys + dskip_ref[...] * u, approximate=True).astype(BF16)
    val = _dot(z, w_glu_ref[:, 0:d])
    gate = _dot(z, w_glu_ref[:, d:2 * d])
    y2 = _ffn(x + val * jax.nn.sigmoid(gate), g_ffn_ref, w_in_ref, w_o_ref, acc_ref)
    o_ref[...] = _rms(y2, g_fin_ref[...])


def _row_spec(tm, width):
    return pl.BlockSpec((tm, width), lambda i: (i, 0))


def _group_scratch(tm, d, chunk):
    n_slab, nc, n_col = d // LANES, tm // chunk, chunk // SLAB_GROUPS
    assert min(S5_MM_ROWS, n_slab * n_col * nc) % nc == 0
    return (pltpu.VMEM((n_slab, tm, LANES), F32), pltpu.VMEM((n_slab * n_col * nc, SLAB_GROUPS * LANES), BF16))


def _group_spec(tm, d, chunk):
    return pl.BlockSpec((d // S5_GROUP, tm // chunk, chunk * S5_GROUP), lambda i: (0, i, 0))


def _layer_spec(stacked, layer):
    nd = stacked.ndim - 1
    return pl.BlockSpec((None,) + stacked.shape[1:], lambda *_: (layer,) + (0,) * nd, pipeline_mode=pl.Buffered(1))


def _ab_tail(x, a, ob, w_out, g_ffn, w_in, w_o, layer, g_next, perm, tm, chunk):
    t, d = x.shape
    h_scr, z_scr = _group_scratch(tm, d, chunk)
    return pl.pallas_call(
        functools.partial(_ab_tail_kernel, chunk=chunk),
        grid=(t // tm,),
        in_specs=[_row_spec(tm, d), _row_spec(tm, a.shape[1]), _row_spec(tm, ob.shape[1]),
                  _const_spec(w_out.shape), _const_spec((1, d)), _layer_spec(w_in, layer), _layer_spec(w_o, layer),
                  _const_spec((1, d)), _const_spec(perm.shape)],
        out_specs=[_row_spec(tm, d), _group_spec(tm, d, chunk)],
        out_shape=[jax.ShapeDtypeStruct((t, d), F32),
                   jax.ShapeDtypeStruct((d // S5_GROUP, t // chunk, chunk * S5_GROUP), BF16)],
        scratch_shapes=[pltpu.VMEM((tm, d), F32), h_scr, z_scr],
        compiler_params=_params(("arbitrary",)),
        name="ab_tail",
    )(x, a, ob, w_out, g_ffn, w_in, w_o, g_next, perm)


def _s5_tail(x, yg, perm_t, g_mix, dskip, w_glu, g_ffn, w_in, w_o, layer, g_fin, tm, chunk):
    t, d = x.shape
    t_scr, z_scr = _group_scratch(tm, d, chunk)
    return pl.pallas_call(
        functools.partial(_s5_tail_kernel, chunk=chunk),
        grid=(t // tm,),
        in_specs=[_row_spec(tm, d), _group_spec(tm, d, chunk), _const_spec(perm_t.shape),
                  _const_spec((1, d)), _const_spec((1, d)),
                  _const_spec(w_glu.shape), _const_spec((1, d)), _layer_spec(w_in, layer), _layer_spec(w_o, layer),
                  _const_spec((1, d))],
        out_specs=_row_spec(tm, d),
        out_shape=jax.ShapeDtypeStruct((t, d), F32),
        scratch_shapes=[pltpu.VMEM((tm, d), F32), z_scr, t_scr],
        compiler_params=_params(("arbitrary",)),
        name="s5_tail",
    )(x, yg, perm_t, g_mix, dskip, w_glu, g_ffn, w_in, w_o, g_fin)


SLAB_GROUPS = LANES // S5_GROUP
S5_MM_ROWS = 256


def _slab_perm():
    idx = np.arange(SLAB_GROUPS * LANES)
    l8, g8, p = idx // LANES, (idx % LANES) // S5_GROUP, idx % S5_GROUP
    dst = g8 * LANES + l8 * S5_GROUP + p
    return (dst[:, None] == idx[None, :]).astype(BF16)


def _permute_rows(z_ref, perm_ref, emit):
    step = min(S5_MM_ROWS, z_ref.shape[0])
    for r0 in range(0, z_ref.shape[0], step):
        emit(r0, _dot(z_ref[r0:r0 + step, :], perm_ref[...]))


def _group_rows(h, perm_ref, o_ref, h_ref, z_ref, chunk):
    rows = h.shape[0]
    nc = rows // chunk
    n_slab = h.shape[1] // LANES
    n_col = chunk // SLAB_GROUPS
    for v in range(n_slab):
        h_ref[v] = h[:, v * LANES:(v + 1) * LANES]
    for v in range(n_slab):
        for l in range(chunk):
            j, l8 = l // SLAB_GROUPS, l % SLAB_GROUPS
            r0 = (v * n_col + j) * nc
            z_ref[r0:r0 + nc, l8 * LANES:(l8 + 1) * LANES] = h_ref[v, pl.ds(l, nc, stride=chunk), :].astype(BF16)

    def emit(r0, blk):
        for q in range(blk.shape[0] // nc):
            v, j = divmod(r0 // nc + q, n_col)
            for g8 in range(SLAB_GROUPS):
                o_ref[v * SLAB_GROUPS + g8, :, j * LANES:(j + 1) * LANES] = (
                    blk[q * nc:(q + 1) * nc, g8 * LANES:(g8 + 1) * LANES].astype(o_ref.dtype))

    _permute_rows(z_ref, perm_ref, emit)


def _ungroup_rows(y_ref, perm_ref, z_ref, t_ref, chunk):
    n_slab, rows, _ = t_ref.shape
    nc = rows // chunk
    n_col = chunk // SLAB_GROUPS
    for v in range(n_slab):
        for j in range(n_col):
            r0 = (v * n_col + j) * nc
            for g8 in range(SLAB_GROUPS):
                z_ref[r0:r0 + nc, g8 * LANES:(g8 + 1) * LANES] = y_ref[v * SLAB_GROUPS + g8, :, j * LANES:(j + 1) * LANES]

    def emit(r0, blk):
        for q in range(blk.shape[0] // nc):
            v, j = divmod(r0 // nc + q, n_col)
            for l8 in range(SLAB_GROUPS):
                t_ref[v, pl.ds(j * SLAB_GROUPS + l8, nc, stride=chunk), :] = (
                    blk[q * nc:(q + 1) * nc, l8 * LANES:(l8 + 1) * LANES])

    _permute_rows(z_ref, perm_ref, emit)


S5_LP = S5_CHUNK * S5_GROUP
S5_RI = 2 * S5_STATE


def _s5_prep_kernel(*refs):
    for parity in range(2):
        _s5_prep_group(parity, *[r.at[parity] for r in refs])


def _s5_prep_group(parity, lam_row_ref, ls_ref, btr_ref, bti_ref, cr_ref, ci_ref,
                   bs_ref, cs_ref, tp_ref, ap_ref):
    n_pow = S5_CHUNK + 1
    dt = jnp.exp(ls_ref[...])

    def powers(lr, li):
        mag = jnp.exp(lr * dt)
        a_re, a_im = mag * jnp.cos(li * dt), mag * jnp.sin(li * dt)
        pw = [(jnp.ones_like(a_re), jnp.zeros_like(a_im))]
        for _ in range(n_pow - 1):
            pr, pi = pw[-1]
            pw.append((pr * a_re - pi * a_im, pr * a_im + pi * a_re))
        return pw

    lr, li = lam_row_ref[0:1, :], lam_row_ref[1:2, :]
    pw_row = powers(lr, li)
    a_re, a_im = pw_row[1]
    den = lr * lr + li * li
    f_re = ((a_re - 1.0) * lr + a_im * li) / den
    f_im = (a_im * lr - (a_re - 1.0) * li) / den
    blk = (S5_GROUP, S5_RI)
    w_re = jnp.concatenate([jnp.broadcast_to(f_re * pw_row[S5_CHUNK - 1 - l][0] - f_im * pw_row[S5_CHUNK - 1 - l][1], blk)
                            for l in range(S5_CHUNK)], axis=0)
    w_im = jnp.concatenate([jnp.broadcast_to(f_re * pw_row[S5_CHUNK - 1 - l][1] + f_im * pw_row[S5_CHUNK - 1 - l][0], blk)
                            for l in range(S5_CHUNK)], axis=0)

    lane = lax.broadcasted_iota(jnp.int32, (1, S5_RI), 1)
    own = (lane // S5_STATE) == parity
    low = lane < S5_STATE
    keep = lambda z: jnp.where(own, z, 0.0)
    per_pos = lambda ref: jnp.concatenate([ref[...]] * S5_CHUNK, axis=0)
    btr, bti = per_pos(btr_ref), per_pos(bti_ref)
    bs_ref[:, :S5_RI] = keep(w_re * btr - w_im * bti).astype(bs_ref.dtype)
    bs_ref[:, S5_RI:] = keep(w_re * bti + w_im * btr).astype(bs_ref.dtype)
    bf_t = jnp.where(low, f_re * btr[0:S5_GROUP] - f_im * bti[0:S5_GROUP],
                     f_re * bti[0:S5_GROUP] + f_im * btr[0:S5_GROUP])
    ap_ref[...] = jnp.zeros_like(ap_ref)
    for i, (j, part) in enumerate(((S5_CHUNK, 0), (S5_CHUNK, 1), (S5_CHUNK // 2, 0), (S5_CHUNK // 2, 1))):
        ap_ref[i:i + 1, :] = keep(pw_row[j][part])

    def spread(j0, part):
        return jnp.concatenate([jnp.broadcast_to(pw_row[j0 + l][part], blk) for l in range(S5_CHUNK)], axis=0)

    cr, ci = per_pos(cr_ref), per_pos(ci_ref)
    cs0 = jnp.where(low, cr * spread(0, 0) - ci * spread(0, 1),
                    -(cr * spread(0, 1) + ci * spread(0, 0)))
    cs_ref[:, :S5_RI] = keep(cr * spread(1, 0) - ci * spread(1, 1)).astype(cs_ref.dtype)
    cs_ref[:, S5_RI:] = keep(-(cr * spread(1, 1) + ci * spread(1, 0))).astype(cs_ref.dtype)
    r = lax.dot_general(bf_t, cs0, (((1,), (1,)), ((), ())), preferred_element_type=F32,
                        precision=lax.Precision.HIGHEST)
    lane_lp = lax.broadcasted_iota(jnp.int32, (S5_GROUP, S5_LP), 1)
    for l in range(S5_CHUNK):
        sh = l * S5_GROUP
        blk_l = r if l == 0 else jnp.where(lane_lp >= sh, pltpu.roll(r, sh, axis=1), 0.0)
        tp_ref[l * S5_GROUP:(l + 1) * S5_GROUP, :] = blk_l.astype(tp_ref.dtype)


def _s5_prep(lam_re, lam_im, log_step, b_re, b_im, c_re, c_im):
    g = lam_re.shape[0]
    dup = lambda z: jnp.concatenate([z, z], axis=-1)
    lam_row = jnp.stack([dup(lam_re), dup(lam_im)], axis=1)
    lam_row = jnp.pad(lam_row, ((0, 0), (0, 6), (0, 0)))
    ls = log_step.reshape(g, 1, 1)
    btr, bti = dup(jnp.swapaxes(b_re, 1, 2)), dup(jnp.swapaxes(b_im, 1, 2))
    cr, ci = dup(c_re), dup(c_im)
    gspec = lambda *s: pl.BlockSpec((2,) + s, lambda i: (i,) + (0,) * len(s))
    table = jax.ShapeDtypeStruct((g, S5_LP, 2 * S5_RI), BF16)
    return pl.pallas_call(
        _s5_prep_kernel,
        grid=(g // 2,),
        in_specs=[gspec(8, S5_RI), gspec(1, 1)] + [gspec(S5_GROUP, S5_RI)] * 4,
        out_specs=[gspec(S5_LP, 2 * S5_RI)] * 2 + [gspec(S5_LP, S5_LP), gspec(8, S5_RI)],
        out_shape=[table] * 2 + [jax.ShapeDtypeStruct((g, S5_LP, S5_LP), BF16),
                                 jax.ShapeDtypeStruct((g, 8, S5_RI), F32)],
        compiler_params=_params(("arbitrary",)),
        name="s5_prep",
    )(lam_row, ls, btr, bti, cr, ci)


def _s5_scan_kernel(u_ref, us_ref, x0r_ref, x0i_ref, bs_ref, cs_ref, tp_ref, ap_ref,
                    y_ref, ys_ref, finr_ref, fini_ref, finsr_ref, finsi_ref, xs_ref, *, bsz, chunks):
    n_rows = bsz * chunks
    pair = range(2)
    us_in = [u_ref[g] for g in pair]
    x = sum(_dot(us_in[g], bs_ref[g]) for g in pair)
    xr, xi = x[:, :S5_RI], x[:, S5_RI:]
    row = lax.broadcasted_iota(jnp.int32, (n_rows, 1), 0) % chunks
    ap = ap_ref[0] + ap_ref[1]
    ar, ai = ap[0:1, :], ap[1:2, :]

    def prefix(xr, xi, ar, ai, pos, length, axis=0):
        shift = 1
        while shift < length:
            sr = jnp.where(pos >= shift, pltpu.roll(xr, shift, axis=axis), 0.0)
            si = jnp.where(pos >= shift, pltpu.roll(xi, shift, axis=axis), 0.0)
            xr, xi = xr + (sr * ar - si * ai), xi + (sr * ai + si * ar)
            ar, ai = ar * ar - ai * ai, 2.0 * (ar * ai)
            shift *= 2
        return xr, xi, ar, ai

    n_runs, runs = n_rows // SUBLANES, chunks // SUBLANES
    in_run = lambda z: z.reshape(n_rows // SUBLANES, SUBLANES, S5_RI)
    pos3 = lax.broadcasted_iota(jnp.int32, (1, SUBLANES, 1), 1)
    x3r, x3i, br, bi = prefix(in_run(xr), in_run(xi), ar, ai, pos3, SUBLANES, axis=1)
    xr, xi = x3r.reshape(n_rows, S5_RI), x3i.reshape(n_rows, S5_RI)
    xs_ref[0], xs_ref[1] = xr, xi
    last = pl.ds(SUBLANES - 1, n_runs, stride=SUBLANES)
    run = lax.broadcasted_iota(jnp.int32, (n_runs, 1), 0) % runs
    er, ei, _, _ = prefix(xs_ref[0, last, :], xs_ref[1, last, :], br, bi, run, runs)
    cr = jnp.where(run >= 1, pltpu.roll(er, 1, axis=0), 0.0)
    ci = jnp.where(run >= 1, pltpu.roll(ei, 1, axis=0), 0.0)
    pr, pi = ar, ai
    for r in range(SUBLANES):
        dst = pl.ds(r, n_runs, stride=SUBLANES)
        xs_ref[0, dst, :] = cr * pr - ci * pi
        xs_ref[1, dst, :] = cr * pi + ci * pr
        pr, pi = pr * ar - pi * ai, pr * ai + pi * ar
    xr, xi = xr + xs_ref[0], xi + xs_ref[1]
    pr = jnp.where(row >= 1, pltpu.roll(xr, 1, axis=0), 0.0).astype(BF16)
    pi = jnp.where(row >= 1, pltpu.roll(xi, 1, axis=0), 0.0).astype(BF16)
    prev = jnp.concatenate([pr, pi], axis=1)
    for g in pair:
        y_ref[g] = (_dot(us_in[g], tp_ref[g]) + _dot_nt(prev, cs_ref[g])).astype(y_ref.dtype)
    finr_ref[...] = jnp.zeros_like(finr_ref)
    fini_ref[...] = jnp.zeros_like(fini_ref)
    for b in range(bsz):
        last = slice((b + 1) * chunks - 1, (b + 1) * chunks)
        finr_ref[b:b + 1, :] = xr[last]
        fini_ref[b:b + 1, :] = xi[last]

    half = S5_LP // 2
    x0r, x0i = x0r_ref[...], x0i_ref[...]
    hr, hi = ap[2:3, :], ap[3:4, :]
    xs = sum(_dot(us_ref[g], bs_ref[g, half:, :]) for g in pair)
    finsr_ref[...] = x0r * hr - x0i * hi + xs[:, :S5_RI]
    finsi_ref[...] = x0r * hi + x0i * hr + xs[:, S5_RI:]
    x0 = jnp.concatenate([x0r, x0i], axis=1).astype(BF16)
    for g in pair:
        ys_ref[g] = (_dot(us_ref[g], tp_ref[g, :half, :half]) + _dot_nt(x0, cs_ref[g, :half, :])).astype(ys_ref.dtype)


def _s5_scan(u_g, us_g, x0r, x0i, bs, cs, tp, ap, bsz):
    g, n_rows, _ = u_g.shape
    n_s = us_g.shape[1]
    half = S5_LP // 2
    pspec = lambda *s: pl.BlockSpec((2,) + s, lambda i: (i,) + (0,) * len(s))
    ospec = lambda *s: pl.BlockSpec((None,) + s, lambda i: (i,) + (0,) * len(s))
    packed = lambda n: jax.ShapeDtypeStruct((g // 2, n, S5_RI), F32)
    return pl.pallas_call(
        functools.partial(_s5_scan_kernel, bsz=bsz, chunks=n_rows // bsz),
        grid=(g // 2,),
        in_specs=[pspec(n_rows, S5_LP), pspec(n_s, half), ospec(n_s, S5_RI), ospec(n_s, S5_RI)]
                 + [pspec(S5_LP, 2 * S5_RI)] * 2 + [pspec(S5_LP, S5_LP), pspec(8, S5_RI)],
        out_specs=[pspec(n_rows, S5_LP), pspec(n_s, half), ospec(8, S5_RI), ospec(8, S5_RI),
                   ospec(n_s, S5_RI), ospec(n_s, S5_RI)],
        out_shape=[jax.ShapeDtypeStruct((g, n_rows, S5_LP), BF16), jax.ShapeDtypeStruct((g, n_s, half), BF16),
                   packed(8), packed(8), packed(n_s), packed(n_s)],
        scratch_shapes=[pltpu.VMEM((2, n_rows, S5_RI), F32)],
        compiler_params=_params(("arbitrary",)),
        name="s5_scan",
    )(u_g, us_g, x0r, x0i, bs, cs, tp, ap)


def kernel(x_prompt, x_sample, state_ret, state_swa_k, state_swa_v, state_ssm_re, state_ssm_im, norm_mix, norm_ffn, norm_final, w_in_ab, ret_gn, w_out_ab, ssm_lam_re, ssm_lam_im, ssm_log_step, ssm_b_re, ssm_b_im, ssm_c_re, ssm_c_im, ssm_d, w_glu, w_ffn_in, w_ffn_out):
    bsz, seq, d = x_prompt.shape
    dbsz, n_new, _ = x_sample.shape
    assert state_swa_k.shape[2] == SWA_BUF and n_new == S5_CHUNK // 2 and seq % SWA_ROWS == 0
    xp = x_prompt.reshape(bsz * seq, d)
    xs = x_sample.reshape(dbsz * n_new, d)
    tm_p, tm_s = TAIL_ROWS, dbsz * n_new
    row = lambda v: v.reshape(1, -1)

    w_in0 = w_in_ab[0].astype(BF16)
    g_mix0, g_ffn0, gn0 = row(norm_mix[0]), row(norm_ffn[0]), row(ret_gn[0])

    ra_p, qkv_p = _ab_proj(xp, g_mix0, w_in0, PROJ_ROWS)
    ra_s, qkv_s = _ab_proj(xs, g_mix0, w_in0, tm_s)

    zero_ret = jnp.zeros((bsz, H_A, DK_A, DV_A), F32)
    a_p, ret_p = _retention(ra_p, zero_ret, gn0, seq=seq, c_real=RET_CHUNK, n_chunks=RET_STEP_CHUNKS, n_seq=bsz,
                            out_dtype=BF16)
    a_s, ret_s = _retention(ra_s, state_ret[0], gn0, seq=n_new, c_real=n_new, n_chunks=1, n_seq=RET_SAMPLE_SEQS,
                            out_dtype=F32)

    width_b = H_B * DH_B
    rows_last = lambda w: w.transpose(0, 2, 3, 1).reshape(dbsz, width_b, SWA_BUF)
    rows_first = lambda w: w.reshape(dbsz, H_B, DH_B, SWA_BUF).transpose(0, 3, 1, 2)[None]
    later = (w_out_ab[0], w_ffn_in.reshape(-1, w_ffn_in.shape[-1]), w_ffn_out.reshape(-1, w_ffn_out.shape[-1]), w_glu[0])
    ob_p, ob_s, swk_s, swv_s, w_out0, w_f_in, w_f_out, w_glu0 = _swa(
        qkv_p, bsz, seq, qkv_s, rows_last(state_swa_k[0]), rows_last(state_swa_v[0]), n_new, later)
    w_f_in, w_f_out = w_f_in.reshape(w_ffn_in.shape), w_f_out.reshape(w_ffn_out.shape)
    kv_tail = qkv_p.reshape(bsz, seq, QKV_W)[:, seq - SWA_BUF:, width_b:]
    swk_p = kv_tail[..., :width_b].reshape(bsz, SWA_BUF, H_B, DH_B)
    swv_p = kv_tail[..., width_b:].reshape(bsz, SWA_BUF, H_B, DH_B)

    g_mix1, g_ffn1 = row(norm_mix[1]), row(norm_ffn[1])
    perm = _slab_perm()
    yp, u_p = _ab_tail(xp, a_p, ob_p, w_out0, g_ffn0, w_f_in, w_f_out, 0, g_mix1, perm, tm_p, S5_CHUNK)
    ys, u_s = _ab_tail(xs, a_s, ob_s, w_out0, g_ffn0, w_f_in, w_f_out, 0, g_mix1, perm, tm_s, n_new)

    operators = _s5_prep(ssm_lam_re[0], ssm_lam_im[0], ssm_log_step[0],
                         ssm_b_re[0], ssm_b_im[0], ssm_c_re[0], ssm_c_im[0])
    pack = lambda st: st.reshape(st.shape[0], -1, S5_RI).transpose(1, 0, 2)
    unpack = lambda st: st.transpose(1, 0, 2).reshape(st.shape[1], -1, S5_STATE)[None]
    y5_p, y5_s, finr_p, fini_p, finr_s, fini_s = _s5_scan(
        u_p, u_s, pack(state_ssm_re[0]), pack(state_ssm_im[0]), *operators, bsz)

    tail1 = (perm.T, g_mix1, row(ssm_d[0]), w_glu0, g_ffn1,
             w_f_in, w_f_out, 1, row(norm_final))
    yp = _s5_tail(yp, y5_p, *tail1, tm_p, S5_CHUNK)
    ys = _s5_tail(ys, y5_s, *tail1, tm_s, n_new)

    return (yp.reshape(bsz, seq, d), ys.reshape(dbsz, n_new, d),
            ret_p[None], ret_s[None],
            swk_p[None], swv_p[None],
            rows_first(swk_s), rows_first(swv_s),
            unpack(finr_p[:, :bsz]), unpack(fini_p[:, :bsz]), unpack(finr_s), unpack(fini_s))
```

```python
import functools

import jax
import jax.numpy as jnp
import numpy as np
from jax import lax
from jax.experimental import pallas as pl
from jax.experimental.pallas import tpu as pltpu

F32 = jnp.float32
BF16 = jnp.bfloat16

H_A, DK_A, DV_A = 4, 64, 128
H_B, DH_B = 8, 64
SWA_PAIRS = ((128, 1), (512, 4), (2048, 16))
SPAN = 128
SWA_BUF = 2048
PAST_LEN = 16384
RET_CHUNK = 128
S5_GROUP, S5_STATE = 16, 64
S5_CHUNK = 16
EPS = 1e-6
NEG_INF = -1e30
QA_W, KA_W, VA_W, GA_W = H_A * DK_A, H_A * DK_A, H_A * DV_A, H_A * DV_A
RA_W = QA_W + KA_W + VA_W + GA_W
QKV_W = 3 * H_B * DH_B

LANES = 128
SUBLANES = 8
VMEM_LIMIT = 58 * 1024 * 1024

PROJ_ROWS = 1024
PROJ_COLS = 512
TAIL_ROWS = 512
RET_STEP_CHUNKS = 8
RET_SAMPLE_SEQS = 8


def _params(sem):
    return pltpu.CompilerParams(dimension_semantics=sem, vmem_limit_bytes=VMEM_LIMIT)


def _const_spec(shape):
    nd = len(shape)
    return pl.BlockSpec(shape, lambda *_: (0,) * nd, pipeline_mode=pl.Buffered(1))


def _rms(x, g):
    return x * lax.rsqrt(jnp.mean(x * x, axis=-1, keepdims=True) + EPS) * g


def _dot(a, b):
    return jnp.dot(a, b, preferred_element_type=F32)


def _dot_nt(a, b):
    return lax.dot_general(a, b, (((1,), (1,)), ((), ())), preferred_element_type=F32)


def _dot_tn(a, b):
    return lax.dot_general(a, b, (((0,), (0,)), ((), ())), preferred_element_type=F32)


def _ab_proj_kernel(x_ref, g_ref, w_ref, ra_ref, qkv_ref):
    h = _rms(x_ref[...], g_ref[...]).astype(BF16)
    for n0 in range(0, RA_W, PROJ_COLS):
        ra_ref[:, n0:n0 + PROJ_COLS] = _dot(h, w_ref[:, n0:n0 + PROJ_COLS])
    for n0 in range(0, QKV_W, PROJ_COLS):
        qkv_ref[:, n0:n0 + PROJ_COLS] = _dot(h, w_ref[:, RA_W + n0:RA_W + n0 + PROJ_COLS])


def _ab_proj(x, g, w_bf16, tm):
    t, d = x.shape
    return pl.pallas_call(
        _ab_proj_kernel,
        grid=(t // tm,),
        in_specs=[pl.BlockSpec((tm, d), lambda i: (i, 0)),
                  _const_spec((1, d)),
                  _const_spec((d, RA_W + QKV_W))],
        out_specs=[pl.BlockSpec((tm, RA_W), lambda i: (i, 0)),
                   pl.BlockSpec((tm, QKV_W), lambda i: (i, 0))],
        out_shape=[jax.ShapeDtypeStruct((t, RA_W), F32), jax.ShapeDtypeStruct((t, QKV_W), F32)],
        compiler_params=_params(("arbitrary",)),
        name="ab_proj",
    )(x, g, w_bf16)


def _retention_tables(c_real):
    c = RET_CHUNK
    log_g = np.log1p(-np.exp2(-5.0 - np.arange(H_A, dtype=F32)))
    idx = np.arange(c, dtype=F32)
    rel = idx[:, None] - idx[None, :]
    dec = np.where(rel >= 0, np.exp(np.maximum(rel, 0.0)[None] * log_g[:, None, None]), 0.0)
    dec = dec.reshape(H_A * c, c)
    real = (idx < c_real)[:, None]
    qd = np.exp((idx + 1.0)[:, None] * log_g[None, :])
    qd = np.repeat(qd, DV_A, axis=1)
    kd = np.where(real, np.exp((c_real - 1.0 - idx)[:, None] * log_g[None, :]), 0.0)
    kd = np.repeat(kd, DK_A, axis=1)
    row_h = np.arange(H_A * DK_A)[:, None] // DK_A
    col_h = np.arange(H_A * DV_A)[None, :] // DV_A
    bd = (row_h == col_h).astype(F32)
    dm = bd * np.repeat(np.exp(c_real * log_g), DV_A)[None, :]
    return tuple(np.asarray(tb, F32) for tb in (dec, qd, kd, dm))


def _retention_kernel(ra_ref, s0_ref, gain_ref, dec_ref, qd_ref, kd_ref, dm_ref,
                      a_ref, sout_ref, sbd_ref, *, c_real, n_chunks, n_seq):
    c = RET_CHUNK
    j = pl.program_id(1)

    @pl.when(j == 0)
    def _():
        sbd_ref[...] = jnp.zeros_like(sbd_ref)
        for s in range(n_seq):
            for h in range(H_A):
                sbd_ref[s, h * DK_A:(h + 1) * DK_A, h * DV_A:(h + 1) * DV_A] = s0_ref[s, h]

    lane_q = lax.broadcasted_iota(jnp.int32, (c, QA_W), 1) // DK_A
    for ci in range(n_chunks):
        for s in range(n_seq):
            _retention_chunk(ra_ref.at[s], a_ref.at[s], sbd_ref.at[s], gain_ref, dec_ref, qd_ref, kd_ref, dm_ref,
                             lane_q, ci, c_real)

    @pl.when(j == pl.num_programs(1) - 1)
    def _():
        for s in range(n_seq):
            for h in range(H_A):
                sout_ref[s, h] = sbd_ref[s, h * DK_A:(h + 1) * DK_A, h * DV_A:(h + 1) * DV_A]


def _retention_chunk(ra_ref, a_ref, sbd_ref, gain_ref, dec_ref, qd_ref, kd_ref, dm_ref, lane_q, ci, c_real):
    c = RET_CHUNK
    rows = ra_ref[ci * c_real:(ci + 1) * c_real, :]
    gate = rows[:, QA_W + KA_W + VA_W:]
    if c_real < c:
        rows = jnp.concatenate([rows, jnp.zeros((c - c_real, RA_W), F32)], axis=0)
    q = rows[:, :QA_W]
    k = rows[:, QA_W:QA_W + KA_W] * (DK_A ** -0.5)
    v = rows[:, QA_W + KA_W:QA_W + KA_W + VA_W]
    vb = v.astype(BF16)
    qm = jnp.concatenate([jnp.where(lane_q == h, q, 0.0) for h in range(H_A)], axis=0).astype(BF16)
    s = _dot_nt(qm, k.astype(BF16)) * dec_ref[...]
    sb = s.astype(BF16)
    sbd = sbd_ref[...]
    cross = _dot(q.astype(BF16), sbd.astype(BF16)) * qd_ref[...]
    upd = _dot_tn((k * kd_ref[...]).astype(BF16), vb)
    for h in range(H_A):
        blk = (slice(h * DK_A, (h + 1) * DK_A), slice(h * DV_A, (h + 1) * DV_A))
        sbd_ref[blk] = sbd[blk] * dm_ref[blk] + upd[blk]
    for h in range(H_A):
        sl = slice(h * DV_A, (h + 1) * DV_A)
        o = _dot(sb[h * c:(h + 1) * c], vb[:, sl]) + cross[:, sl]
        o = o[:c_real]
        mu = jnp.mean(o, axis=-1, keepdims=True)
        var = jnp.mean(jnp.square(o - mu), axis=-1, keepdims=True)
        y = (o - mu) * lax.rsqrt(var + EPS) * gain_ref[:, sl]
        gh = gate[:, sl]
        a_ref[ci * c_real:(ci + 1) * c_real, sl] = (gh * jax.nn.sigmoid(gh) * y).astype(a_ref.dtype)


def _retention(ra, state0, gain, *, seq, c_real, n_chunks, n_seq, out_dtype):
    t = ra.shape[0]
    bsz = t // seq
    rows = c_real * n_chunks
    steps = seq // rows
    tables = _retention_tables(c_real)
    kern = functools.partial(_retention_kernel, c_real=c_real, n_chunks=n_chunks, n_seq=n_seq)
    a, s_out = pl.pallas_call(
        kern,
        grid=(bsz // n_seq, steps),
        in_specs=[pl.BlockSpec((n_seq, rows, RA_W), lambda b, j: (b, j, 0)),
                  pl.BlockSpec((n_seq, H_A, DK_A, DV_A), lambda b, j: (b, 0, 0, 0)),
                  _const_spec((1, VA_W))] + [_const_spec(tb.shape) for tb in tables],
        out_specs=[pl.BlockSpec((n_seq, rows, VA_W), lambda b, j: (b, j, 0)),
                   pl.BlockSpec((n_seq, H_A, DK_A, DV_A), lambda b, j: (b, 0, 0, 0))],
        out_shape=[jax.ShapeDtypeStruct((bsz, seq, VA_W), out_dtype),
                   jax.ShapeDtypeStruct((bsz, H_A, DK_A, DV_A), F32)],
        scratch_shapes=[pltpu.VMEM((n_seq, H_A * DK_A, H_A * DV_A), F32)],
        compiler_params=_params(("arbitrary", "arbitrary")),
        name="retention",
    )(ra.reshape(bsz, seq, RA_W), state0, gain, *tables)
    return a.reshape(t, VA_W), s_out


SWA_ROWS = 2048
N_BRANCH = len(SWA_PAIRS)


def _alibi_slopes():
    return np.exp2(-8.0 * np.arange(1, H_B + 1, dtype=F32) / H_B)


def _swa_prompt_bias():
    qi = np.arange(SPAN)[:, None]
    kj = np.arange(2 * SPAN)[None, :]
    dist = SPAN + qi - kj
    band = (dist >= 0) & (dist <= SPAN)
    slopes = _alibi_slopes()
    out = []
    for (_, dil) in SWA_PAIRS:
        pen = -slopes[:, None, None] * (dil * dist).astype(F32)[None]
        normal = np.where(band[None], pen, NEG_INF)
        first = np.where((band & (kj >= SPAN))[None], pen, NEG_INF)
        out.append(np.stack([normal, first], axis=1))
    tab = np.stack(out, axis=1)
    tab = tab.reshape(H_B // 2, 2, N_BRANCH, 2, SPAN, 2 * SPAN).transpose(0, 2, 3, 1, 4, 5)
    return np.asarray(tab.reshape(H_B // 2, N_BRANCH, 2, 2 * SPAN, 2 * SPAN), F32)


COARSE = 4


def _swa_unit(q, kk, vv, bias, head0):
    q = q * (DH_B ** -0.5)
    qm = jnp.concatenate([jnp.where(head0, q, 0.0), jnp.where(head0, 0.0, q)], axis=0).astype(BF16)
    s = _dot_nt(qm, kk.astype(BF16)) + bias
    m = jnp.max(s, axis=-1, keepdims=True)
    p = jnp.exp(s - m).astype(BF16)
    ones = jnp.ones((2 * SPAN, LANES), BF16)
    res = _dot(p, jnp.concatenate([vv.astype(BF16), ones], axis=1))
    acc = jnp.where(head0, res[:SPAN, :LANES], res[SPAN:, :LANES])
    den = jnp.where(head0, res[:SPAN, LANES:], res[SPAN:, LANES:])
    mm = jnp.where(head0, jnp.broadcast_to(m[:SPAN], (SPAN, LANES)), jnp.broadcast_to(m[SPAN:], (SPAN, LANES)))
    return acc, den, mm


SWA_PREV = SPAN * COARSE


def _swa_prompt_half(half, q_ref, k_ref, v_ref, bias_ref, o_ref, kbuf, vbuf, acc_ref, l_ref, m_ref,
                     q4, k4, v4, acc4, l4, m4):
    j = pl.program_id(2)
    rows = SWA_ROWS
    sub = rows // COARSE
    slot = j % 2

    @pl.when((half == 0) & (j > 0))
    def _():
        kbuf[0:SWA_PREV, :] = kbuf[rows:rows + SWA_PREV, :]
        vbuf[0:SWA_PREV, :] = vbuf[rows:rows + SWA_PREV, :]

    @pl.when((half == 0) & (j == 0))
    def _():
        k4[1] = jnp.zeros(k4.shape[1:], F32)
        v4[1] = jnp.zeros(v4.shape[1:], F32)
        kbuf[0:SWA_PREV, :] = jnp.zeros((SWA_PREV, LANES), F32)
        vbuf[0:SWA_PREV, :] = jnp.zeros((SWA_PREV, LANES), F32)

    @pl.when(half == 0)
    def _():
        kbuf[SWA_PREV:, :] = k_ref[...]
        vbuf[SWA_PREV:, :] = v_ref[...]
        for c in range(COARSE):
            q4[c] = q_ref[pl.ds(c, sub, stride=COARSE), :]
            k4[slot, c] = k_ref[pl.ds(c, sub, stride=COARSE), :]
            v4[slot, c] = v_ref[pl.ds(c, sub, stride=COARSE), :]

    lane = lax.broadcasted_iota(jnp.int32, (SPAN, LANES), 1)
    head0 = lane < DH_B
    first_step = (j == 0).astype(jnp.int32)

    for g, (_, dil) in enumerate(SWA_PAIRS):
        blocks = rows // (SPAN * dil)
        per_half = blocks * dil // 2

        def unit(u, g=g, dil=dil):
            wb = u // dil
            r = u % dil
            q_start = wb * (SPAN * dil) + r
            k_start = SWA_PREV + (wb - 1) * (SPAN * dil) + r
            variant = jnp.where(wb == 0, first_step, 0)
            acc, den, mm = _swa_unit(q_ref[pl.ds(q_start, SPAN, stride=dil), :],
                                     kbuf[pl.ds(k_start, 2 * SPAN, stride=dil), :],
                                     vbuf[pl.ds(k_start, 2 * SPAN, stride=dil), :],
                                     bias_ref[0, g, variant], head0)
            acc_ref[g, pl.ds(q_start, SPAN, stride=dil), :] = acc
            l_ref[g, pl.ds(q_start, SPAN, stride=dil), :] = den
            m_ref[g, pl.ds(q_start, SPAN, stride=dil), :] = mm

        def unit_two_level(u, g=g, fine=dil // COARSE):
            c = u % COARSE
            f = u // COARSE
            pick = pl.ds(f, SPAN, stride=fine)
            acc, den, mm = _swa_unit(q4[c, pick, :],
                                     jnp.concatenate([k4[1 - slot, c, pick, :], k4[slot, c, pick, :]], axis=0),
                                     jnp.concatenate([v4[1 - slot, c, pick, :], v4[slot, c, pick, :]], axis=0),
                                     bias_ref[0, g, first_step], head0)
            dst = pl.ds(c * sub + f, SPAN, stride=fine)
            acc4[dst, :] = acc
            l4[dst, :] = den
            m4[dst, :] = mm

        two_level = dil % (COARSE * COARSE) == 0
        assert not two_level or blocks == 1
        body = unit_two_level if two_level else unit

        @pl.when(half >= 0)
        def _(body=body, per_half=per_half):
            for i in range(per_half):
                body(half * per_half + i)

        if two_level:
            @pl.when(half == 1)
            def _(g=g):
                for c in range(COARSE):
                    src, dst = slice(c * sub, (c + 1) * sub), pl.ds(c, sub, stride=COARSE)
                    acc_ref[g, dst, :] = acc4[src, :]
                    l_ref[g, dst, :] = l4[src, :]
                    m_ref[g, dst, :] = m4[src, :]

    @pl.when(half == 1)
    def _():
        tile = 256
        for r0 in range(0, rows, tile):
            sl = slice(r0, r0 + tile)
            ms = [m_ref[g, sl, :] for g in range(N_BRANCH)]
            mx = functools.reduce(jnp.maximum, ms)
            ws = [jnp.exp(mg - mx) for mg in ms]
            num = sum(w * acc_ref[g, sl, :] for g, w in enumerate(ws))
            den = sum(w * l_ref[g, sl, :] for g, w in enumerate(ws))
            o_ref[sl, :] = (num / den).astype(o_ref.dtype)


def _swa_sample_bias(n_new):
    t = np.arange(n_new)[:, None]
    slopes = _alibi_slopes()

    def table(j, exists):
        dist = SWA_BUF + t - j
        reach = [(dist >= 0) & (dist <= window) & (dist % dil == 0) & (PAST_LEN + t - dist >= 0) & exists
                 for (window, dil) in SWA_PAIRS]
        count = sum(r.astype(F32) for r in reach)
        pen = np.where((count > 0)[None], -slopes[:, None, None] * dist.astype(F32)[None], NEG_INF)
        rows = lambda z: z.reshape(H_B * n_new, -1)
        return np.asarray(np.stack([rows(pen), rows(np.broadcast_to(count[None], pen.shape))]), F32)

    jb = np.arange(LANES)[None, :]
    return table(np.arange(SWA_BUF)[None, :], True), table(SWA_BUF + jb, jb < n_new)


def _pad_new_rows(new_ref, n_new):
    return jnp.concatenate([new_ref[...], jnp.zeros((LANES - n_new, new_ref.shape[1]), F32)], axis=0)


def _swa_sample_attend(q_ref, k_ref, v_ref, kp_ref, vp_ref, ba_ref, bb_ref, o_ref, n_new):
    width = q_ref.shape[1]
    n_heads = width // DH_B
    lane_h = lax.broadcasted_iota(jnp.int32, (n_new, width), 1) // DH_B
    q = q_ref[...] * (DH_B ** -0.5)
    qm = jnp.concatenate([jnp.where(lane_h == h, q, 0.0) for h in range(n_heads)], axis=0).astype(BF16)
    k_new, v_new = _pad_new_rows(k_ref, n_new), _pad_new_rows(v_ref, n_new)
    s_a = _dot(qm, kp_ref[0].astype(BF16))
    s_b = _dot_nt(qm, k_new.astype(BF16))
    s_a, s_b = s_a + ba_ref[0], s_b + bb_ref[0]
    mx = jnp.maximum(jnp.max(s_a, axis=-1, keepdims=True), jnp.max(s_b, axis=-1, keepdims=True))
    p_a = (jnp.exp(s_a - mx) * ba_ref[1]).astype(BF16)
    p_b = (jnp.exp(s_b - mx) * bb_ref[1]).astype(BF16)
    den = (jnp.sum(p_a.astype(F32), axis=-1, keepdims=True)
           + jnp.sum(p_b.astype(F32), axis=-1, keepdims=True))
    o = (_dot_nt(p_a, vp_ref[0].astype(BF16)) + _dot(p_b, v_new.astype(BF16))) / den
    o_ref[...] = sum(jnp.where(lane_h == h, o[h * n_new:(h + 1) * n_new], 0.0) for h in range(n_heads))

def _swa_window_update(src_ref, new_ref, dst_ref, n_new):
    lane = lax.broadcasted_iota(jnp.int32, (DH_B, LANES), 1)
    new_t = pltpu.roll(_pad_new_rows(new_ref, n_new).T, LANES - n_new, axis=1)
    for h in range(new_ref.shape[1] // DH_B):
        rows = slice(h * DH_B, (h + 1) * DH_B)
        shifted = pltpu.roll(src_ref[0, rows, :], SWA_BUF - n_new, axis=1)
        dst_ref[0, rows, 0:SWA_BUF - LANES] = shifted[:, 0:SWA_BUF - LANES]
        dst_ref[0, rows, SWA_BUF - LANES:] = jnp.where(lane >= LANES - n_new, new_t[rows],
                                                       shifted[:, SWA_BUF - LANES:])


N_SWA_IN, N_SWA_OUT = 11, 4


def _swa_kernel(*refs, n_new, n_cast):
    (q_ref, k_ref, v_ref, bias_ref, qs_ref, ks_ref, vs_ref, kp_ref, vp_ref, ba_ref, bb_ref), refs = (
        refs[:N_SWA_IN], refs[N_SWA_IN:])
    cast_src, refs = refs[:n_cast], refs[n_cast:]
    (o_ref, os_ref, ko_ref, vo_ref), refs = refs[:N_SWA_OUT], refs[N_SWA_OUT:]
    cast_dst, scratch = refs[:n_cast], refs[n_cast:]
    half = pl.program_id(3)
    _swa_prompt_half(half, q_ref, k_ref, v_ref, bias_ref, o_ref, *scratch)
    _swa_sample_attend(qs_ref, ks_ref, vs_ref, kp_ref, vp_ref, ba_ref, bb_ref, os_ref, n_new)
    _swa_window_update(kp_ref, ks_ref, ko_ref, n_new)
    _swa_window_update(vp_ref, vs_ref, vo_ref, n_new)

    @pl.when(half == 0)
    def _():
        for src, dst in zip(cast_src, cast_dst):
            dst[...] = src[...].astype(dst.dtype)


def _swa(qkv, bsz, seq, qkv_s, k_past, v_past, n_new, weights):
    t = qkv.shape[0]
    dbsz = k_past.shape[0]
    steps = seq // SWA_ROWS
    npair = H_B // 2
    width = H_B * DH_B
    half_w = width // 2
    assert dbsz == bsz * npair * steps, "one sample sequence per prompt (sequence, head pair, row block) step"
    bias = _swa_prompt_bias()
    bias_a, bias_b = _swa_sample_bias(n_new)
    blk = (SWA_ROWS, LANES)
    rows_of = lambda b, j: b * steps + j
    seq_of = lambda b, hp, j: (b * npair + hp) * steps + j
    new_blk = lambda col: pl.BlockSpec((n_new, half_w), lambda b, hp, j, h, col=col: (seq_of(b, hp, j), 2 * col + h))
    state_blk = pl.BlockSpec((1, half_w, SWA_BUF), lambda b, hp, j, h: (seq_of(b, hp, j), h, 0))
    half_rows = H_B // 2 * n_new
    sub = SWA_ROWS // COARSE
    assert all(w.shape[0] % (dbsz * 2 * SUBLANES) == 0 for w in weights), "bf16 row blocks are 16-row tiles"
    cast_blk = [pl.BlockSpec((w.shape[0] // dbsz, w.shape[1]), lambda b, hp, j, h: (seq_of(b, hp, j), 0))
                for w in weights]
    in_specs = [pl.BlockSpec(blk, lambda b, hp, j, h: (rows_of(b, j), hp)),
                pl.BlockSpec(blk, lambda b, hp, j, h: (rows_of(b, j), npair + hp)),
                pl.BlockSpec(blk, lambda b, hp, j, h: (rows_of(b, j), 2 * npair + hp)),
                pl.BlockSpec((1, N_BRANCH, 2, 2 * SPAN, 2 * SPAN), lambda b, hp, j, h: (hp, 0, 0, 0, 0)),
                new_blk(0), new_blk(1), new_blk(2), state_blk, state_blk,
                pl.BlockSpec((2, half_rows, SWA_BUF), lambda b, hp, j, h: (0, h, 0)),
                pl.BlockSpec((2, half_rows, LANES), lambda b, hp, j, h: (0, h, 0))]
    assert len(in_specs) == N_SWA_IN
    return pl.pallas_call(
        functools.partial(_swa_kernel, n_new=n_new, n_cast=len(weights)),
        grid=(bsz, npair, steps, 2),
        in_specs=in_specs + cast_blk,
        out_specs=[pl.BlockSpec(blk, lambda b, hp, j, h: (rows_of(b, j), hp)),
                   pl.BlockSpec((n_new, half_w), lambda b, hp, j, h: (seq_of(b, hp, j), h)),
                   state_blk, state_blk] + cast_blk,
        out_shape=[jax.ShapeDtypeStruct((t, width), BF16),
                   jax.ShapeDtypeStruct((dbsz * n_new, width), F32),
                   jax.ShapeDtypeStruct(k_past.shape, F32), jax.ShapeDtypeStruct(v_past.shape, F32)]
                  + [jax.ShapeDtypeStruct(w.shape, BF16) for w in weights],
        scratch_shapes=[pltpu.VMEM((SWA_PREV + SWA_ROWS, LANES), F32)] * 2
                       + [pltpu.VMEM((N_BRANCH, SWA_ROWS, LANES), F32)] * 3
                       + [pltpu.VMEM((COARSE, sub, LANES), F32)]
                       + [pltpu.VMEM((2, COARSE, sub, LANES), F32)] * 2
                       + [pltpu.VMEM((SWA_ROWS, LANES), F32)] * 3,
        compiler_params=_params(("arbitrary",) * 4),
        name="swa",
    )(qkv, qkv, qkv, bias, qkv_s, qkv_s, qkv_s, k_past, v_past, bias_a, bias_b, *weights)


FFN_CHUNK = 256


def _ffn(y1, g_ffn_ref, w_in_ref, w_o_ref, acc_ref):
    d_ff = w_o_ref.shape[0]
    h = _rms(y1, g_ffn_ref[...]).astype(BF16)
    for ci, c0 in enumerate(range(0, d_ff, FFN_CHUNK)):
        gate = _dot(h, w_in_ref[:, c0:c0 + FFN_CHUNK])
        up = _dot(h, w_in_ref[:, d_ff + c0:d_ff + c0 + FFN_CHUNK])
        act = (gate * jax.nn.sigmoid(gate) * up).astype(BF16)
        part = _dot(act, w_o_ref[c0:c0 + FFN_CHUNK, :])
        if ci == 0:
            acc_ref[...] = part
        else:
            acc_ref[...] += part
    return y1 + acc_ref[...]


def _ab_tail_kernel(x_ref, a_ref, ob_ref, w_out_ref, g_ffn_ref, w_in_ref, w_o_ref, g_next_ref, perm_ref,
                    o_ref, u_ref, acc_ref, h_ref, z_ref, *, chunk):
    mix = (_dot(a_ref[...].astype(BF16), w_out_ref[0:VA_W, :])
           + _dot(ob_ref[...].astype(BF16), w_out_ref[VA_W:, :]))
    y = _ffn(x_ref[...] + mix, g_ffn_ref, w_in_ref, w_o_ref, acc_ref)
    o_ref[...] = y
    _group_rows(_rms(y, g_next_ref[...]), perm_ref, u_ref, h_ref, z_ref, chunk)


def _s5_tail_kernel(x_ref, yg_ref, perm_ref, g_mix_ref, dskip_ref, w_glu_ref, g_ffn_ref, w_in_ref, w_o_ref,
                    g_fin_ref, o_ref, acc_ref, z_ref, t_ref, *, chunk):
    x = x_ref[...]
    d = x.shape[-1]
    u = _rms(x, g_mix_ref[...])
    _ungroup_rows(yg_ref, perm_ref, z_ref, t_ref, chunk)
    ys = jnp.concatenate([t_ref[v] for v in range(d // LANES)], axis=1)
    z = jax.nn.gelu(ys + dskip_ref[...] * u, approximate=True).astype(BF16)
    val = _dot(z, w_glu_ref[:, 0:d])
    gate = _dot(z, w_glu_ref[:, d:2 * d])
    y2 = _ffn(x + val * jax.nn.sigmoid(gate), g_ffn_ref, w_in_ref, w_o_ref, acc_ref)
    o_ref[...] = _rms(y2, g_fin_ref[...])


def _row_spec(tm, width):
    return pl.BlockSpec((tm, width), lambda i: (i, 0))


def _group_scratch(tm, d, chunk):
    n_slab, nc, n_col = d // LANES, tm // chunk, chunk // SLAB_GROUPS
    assert min(S5_MM_ROWS, n_slab * n_col * nc) % nc == 0
    return (pltpu.VMEM((n_slab, tm, LANES), F32), pltpu.VMEM((n_slab * n_col * nc, SLAB_GROUPS * LANES), BF16))


def _group_spec(tm, d, chunk):
    return pl.BlockSpec((d // S5_GROUP, tm // chunk, chunk * S5_GROUP), lambda i: (0, i, 0))


def _layer_spec(stacked, layer):
    nd = stacked.ndim - 1
    return pl.BlockSpec((None,) + stacked.shape[1:], lambda *_: (layer,) + (0,) * nd, pipeline_mode=pl.Buffered(1))


def _ab_tail(x, a, ob, w_out, g_ffn, w_in, w_o, layer, g_next, perm, tm, chunk):
    t, d = x.shape
    h_scr, z_scr = _group_scratch(tm, d, chunk)
    return pl.pallas_call(
        functools.partial(_ab_tail_kernel, chunk=chunk),
        grid=(t // tm,),
        in_specs=[_row_spec(tm, d), _row_spec(tm, a.shape[1]), _row_spec(tm, ob.shape[1]),
                  _const_spec(w_out.shape), _const_spec((1, d)), _layer_spec(w_in, layer), _layer_spec(w_o, layer),
                  _const_spec((1, d)), _const_spec(perm.shape)],
        out_specs=[_row_spec(tm, d), _group_spec(tm, d, chunk)],
        out_shape=[jax.ShapeDtypeStruct((t, d), F32),
                   jax.ShapeDtypeStruct((d // S5_GROUP, t // chunk, chunk * S5_GROUP), BF16)],
        scratch_shapes=[pltpu.VMEM((tm, d), F32), h_scr, z_scr],
        compiler_params=_params(("arbitrary",)),
        name="ab_tail",
    )(x, a, ob, w_out, g_ffn, w_in, w_o, g_next, perm)


def _s5_tail(x, yg, perm_t, g_mix, dskip, w_glu, g_ffn, w_in, w_o, layer, g_fin, tm, chunk):
    t, d = x.shape
    t_scr, z_scr = _group_scratch(tm, d, chunk)
    return pl.pallas_call(
        functools.partial(_s5_tail_kernel, chunk=chunk),
        grid=(t // tm,),
        in_specs=[_row_spec(tm, d), _group_spec(tm, d, chunk), _const_spec(perm_t.shape),
                  _const_spec((1, d)), _const_spec((1, d)),
                  _const_spec(w_glu.shape), _const_spec((1, d)), _layer_spec(w_in, layer), _layer_spec(w_o, layer),
                  _const_spec((1, d))],
        out_specs=_row_spec(tm, d),
        out_shape=jax.ShapeDtypeStruct((t, d), F32),
        scratch_shapes=[pltpu.VMEM((tm, d), F32), z_scr, t_scr],
        compiler_params=_params(("arbitrary",)),
        name="s5_tail",
    )(x, yg, perm_t, g_mix, dskip, w_glu, g_ffn, w_in, w_o, g_fin)


SLAB_GROUPS = LANES // S5_GROUP
S5_MM_ROWS = 256


def _slab_perm():
    idx = np.arange(SLAB_GROUPS * LANES)
    l8, g8, p = idx // LANES, (idx % LANES) // S5_GROUP, idx % S5_GROUP
    dst = g8 * LANES + l8 * S5_GROUP + p
    return (dst[:, None] == idx[None, :]).astype(BF16)


def _permute_rows(z_ref, perm_ref, emit):
    step = min(S5_MM_ROWS, z_ref.shape[0])
    for r0 in range(0, z_ref.shape[0], step):
        emit(r0, _dot(z_ref[r0:r0 + step, :], perm_ref[...]))


def _group_rows(h, perm_ref, o_ref, h_ref, z_ref, chunk):
    rows = h.shape[0]
    nc = rows // chunk
    n_slab = h.shape[1] // LANES
    n_col = chunk // SLAB_GROUPS
    for v in range(n_slab):
        h_ref[v] = h[:, v * LANES:(v + 1) * LANES]
    for v in range(n_slab):
        for l in range(chunk):
            j, l8 = l // SLAB_GROUPS, l % SLAB_GROUPS
            r0 = (v * n_col + j) * nc
            z_ref[r0:r0 + nc, l8 * LANES:(l8 + 1) * LANES] = h_ref[v, pl.ds(l, nc, stride=chunk), :].astype(BF16)

    def emit(r0, blk):
        for q in range(blk.shape[0] // nc):
            v, j = divmod(r0 // nc + q, n_col)
            for g8 in range(SLAB_GROUPS):
                o_ref[v * SLAB_GROUPS + g8, :, j * LANES:(j + 1) * LANES] = (
                    blk[q * nc:(q + 1) * nc, g8 * LANES:(g8 + 1) * LANES].astype(o_ref.dtype))

    _permute_rows(z_ref, perm_ref, emit)


def _ungroup_rows(y_ref, perm_ref, z_ref, t_ref, chunk):
    n_slab, rows, _ = t_ref.shape
    nc = rows // chunk
    n_col = chunk // SLAB_GROUPS
    for v in range(n_slab):
        for j in range(n_col):
            r0 = (v * n_col + j) * nc
            for g8 in range(SLAB_GROUPS):
                z_ref[r0:r0 + nc, g8 * LANES:(g8 + 1) * LANES] = y_ref[v * SLAB_GROUPS + g8, :, j * LANES:(j + 1) * LANES]

    def emit(r0, blk):
        for q in range(blk.shape[0] // nc):
            v, j = divmod(r0 // nc + q, n_col)
            for l8 in range(SLAB_GROUPS):
                t_ref[v, pl.ds(j * SLAB_GROUPS + l8, nc, stride=chunk), :] = (
                    blk[q * nc:(q + 1) * nc, l8 * LANES:(l8 + 1) * LANES])

    _permute_rows(z_ref, perm_ref, emit)


S5_LP = S5_CHUNK * S5_GROUP
S5_RI = 2 * S5_STATE


def _s5_prep_kernel(*refs):
    for parity in range(2):
        _s5_prep_group(parity, *[r.at[parity] for r in refs])


def _s5_prep_group(parity, lam_row_ref, ls_ref, btr_ref, bti_ref, cr_ref, ci_ref,
                   bs_ref, cs_ref, tp_ref, ap_ref):
    n_pow = S5_CHUNK + 1
    dt = jnp.exp(ls_ref[...])

    def powers(lr, li):
        mag = jnp.exp(lr * dt)
        a_re, a_im = mag * jnp.cos(li * dt), mag * jnp.sin(li * dt)
        pw = [(jnp.ones_like(a_re), jnp.zeros_like(a_im))]
        for _ in range(n_pow - 1):
            pr, pi = pw[-1]
            pw.append((pr * a_re - pi * a_im, pr * a_im + pi * a_re))
        return pw

    lr, li = lam_row_ref[0:1, :], lam_row_ref[1:2, :]
    pw_row = powers(lr, li)
    a_re, a_im = pw_row[1]
    den = lr * lr + li * li
    f_re = ((a_re - 1.0) * lr + a_im * li) / den
    f_im = (a_im * lr - (a_re - 1.0) * li) / den
    blk = (S5_GROUP, S5_RI)
    w_re = jnp.concatenate([jnp.broadcast_to(f_re * pw_row[S5_CHUNK - 1 - l][0] - f_im * pw_row[S5_CHUNK - 1 - l][1], blk)
                            for l in range(S5_CHUNK)], axis=0)
    w_im = jnp.concatenate([jnp.broadcast_to(f_re * pw_row[S5_CHUNK - 1 - l][1] + f_im * pw_row[S5_CHUNK - 1 - l][0], blk)
                            for l in range(S5_CHUNK)], axis=0)

    lane = lax.broadcasted_iota(jnp.int32, (1, S5_RI), 1)
    own = (lane // S5_STATE) == parity
    low = lane < S5_STATE
    keep = lambda z: jnp.where(own, z, 0.0)
    per_pos = lambda ref: jnp.concatenate([ref[...]] * S5_CHUNK, axis=0)
    btr, bti = per_pos(btr_ref), per_pos(bti_ref)
    bs_ref[:, :S5_RI] = keep(w_re * btr - w_im * bti).astype(bs_ref.dtype)
    bs_ref[:, S5_RI:] = keep(w_re * bti + w_im * btr).astype(bs_ref.dtype)
    bf_t = jnp.where(low, f_re * btr[0:S5_GROUP] - f_im * bti[0:S5_GROUP],
                     f_re * bti[0:S5_GROUP] + f_im * btr[0:S5_GROUP])
    ap_ref[...] = jnp.zeros_like(ap_ref)
    for i, (j, part) in enumerate(((S5_CHUNK, 0), (S5_CHUNK, 1), (S5_CHUNK // 2, 0), (S5_CHUNK // 2, 1))):
        ap_ref[i:i + 1, :] = keep(pw_row[j][part])

    def spread(j0, part):
        return jnp.concatenate([jnp.broadcast_to(pw_row[j0 + l][part], blk) for l in range(S5_CHUNK)], axis=0)

    cr, ci = per_pos(cr_ref), per_pos(ci_ref)
    cs0 = jnp.where(low, cr * spread(0, 0) - ci * spread(0, 1),
                    -(cr * spread(0, 1) + ci * spread(0, 0)))
    cs_ref[:, :S5_RI] = keep(cr * spread(1, 0) - ci * spread(1, 1)).astype(cs_ref.dtype)
    cs_ref[:, S5_RI:] = keep(-(cr * spread(1, 1) + ci * spread(1, 0))).astype(cs_ref.dtype)
    r = lax.dot_general(bf_t, cs0, (((1,), (1,)), ((), ())), preferred_element_type=F32,
                        precision=lax.Precision.HIGHEST)
    lane_lp = lax.broadcasted_iota(jnp.int32, (S5_GROUP, S5_LP), 1)
    for l in range(S5_CHUNK):
        sh = l * S5_GROUP
        blk_l = r if l == 0 else jnp.where(lane_lp >= sh, pltpu.roll(r, sh, axis=1), 0.0)
        tp_ref[l * S5_GROUP:(l + 1) * S5_GROUP, :] = blk_l.astype(tp_ref.dtype)


def _s5_prep(lam_re, lam_im, log_step, b_re, b_im, c_re, c_im):
    g = lam_re.shape[0]
    dup = lambda z: jnp.concatenate([z, z], axis=-1)
    lam_row = jnp.stack([dup(lam_re), dup(lam_im)], axis=1)
    lam_row = jnp.pad(lam_row, ((0, 0), (0, 6), (0, 0)))
    ls = log_step.reshape(g, 1, 1)
    btr, bti = dup(jnp.swapaxes(b_re, 1, 2)), dup(jnp.swapaxes(b_im, 1, 2))
    cr, ci = dup(c_re), dup(c_im)
    gspec = lambda *s: pl.BlockSpec((2,) + s, lambda i: (i,) + (0,) * len(s))
    table = jax.ShapeDtypeStruct((g, S5_LP, 2 * S5_RI), BF16)
    return pl.pallas_call(
        _s5_prep_kernel,
        grid=(g // 2,),
        in_specs=[gspec(8, S5_RI), gspec(1, 1)] + [gspec(S5_GROUP, S5_RI)] * 4,
        out_specs=[gspec(S5_LP, 2 * S5_RI)] * 2 + [gspec(S5_LP, S5_LP), gspec(8, S5_RI)],
        out_shape=[table] * 2 + [jax.ShapeDtypeStruct((g, S5_LP, S5_LP), BF16),
                                 jax.ShapeDtypeStruct((g, 8, S5_RI), F32)],
        compiler_params=_params(("arbitrary",)),
        name="s5_prep",
    )(lam_row, ls, btr, bti, cr, ci)


def _s5_scan_kernel(u_ref, us_ref, x0r_ref, x0i_ref, bs_ref, cs_ref, tp_ref, ap_ref,
                    y_ref, ys_ref, finr_ref, fini_ref, finsr_ref, finsi_ref, xs_ref, *, bsz, chunks):
    n_rows = bsz * chunks
    pair = range(2)
    us_in = [u_ref[g] for g in pair]
    x = sum(_dot(us_in[g], bs_ref[g]) for g in pair)
    xr, xi = x[:, :S5_RI], x[:, S5_RI:]
    row = lax.broadcasted_iota(jnp.int32, (n_rows, 1), 0) % chunks
    ap = ap_ref[0] + ap_ref[1]
    ar, ai = ap[0:1, :], ap[1:2, :]

    def prefix(xr, xi, ar, ai, pos, length, axis=0):
        shift = 1
        while shift < length:
            sr = jnp.where(pos >= shift, pltpu.roll(xr, shift, axis=axis), 0.0)
            si = jnp.where(pos >= shift, pltpu.roll(xi, shift, axis=axis), 0.0)
            xr, xi = xr + (sr * ar - si * ai), xi + (sr * ai + si * ar)
            ar, ai = ar * ar - ai * ai, 2.0 * (ar * ai)
            shift *= 2
        return xr, xi, ar, ai

    n_runs, runs = n_rows // SUBLANES, chunks // SUBLANES
    in_run = lambda z: z.reshape(n_rows // SUBLANES, SUBLANES, S5_RI)
    pos3 = lax.broadcasted_iota(jnp.int32, (1, SUBLANES, 1), 1)
    x3r, x3i, br, bi = prefix(in_run(xr), in_run(xi), ar, ai, pos3, SUBLANES, axis=1)
    xr, xi = x3r.reshape(n_rows, S5_RI), x3i.reshape(n_rows, S5_RI)
    xs_ref[0], xs_ref[1] = xr, xi
    last = pl.ds(SUBLANES - 1, n_runs, stride=SUBLANES)
    run = lax.broadcasted_iota(jnp.int32, (n_runs, 1), 0) % runs
    er, ei, _, _ = prefix(xs_ref[0, last, :], xs_ref[1, last, :], br, bi, run, runs)
    cr = jnp.where(run >= 1, pltpu.roll(er, 1, axis=0), 0.0)
    ci = jnp.where(run >= 1, pltpu.roll(ei, 1, axis=0), 0.0)
    pr, pi = ar, ai
    for r in range(SUBLANES):
        dst = pl.ds(r, n_runs, stride=SUBLANES)
        xs_ref[0, dst, :] = cr * pr - ci * pi
        xs_ref[1, dst, :] = cr * pi + ci * pr
        pr, pi = pr * ar - pi * ai, pr * ai + pi * ar
    xr, xi = xr + xs_ref[0], xi + xs_ref[1]
    pr = jnp.where(row >= 1, pltpu.roll(xr, 1, axis=0), 0.0).astype(BF16)
    pi = jnp.where(row >= 1, pltpu.roll(xi, 1, axis=0), 0.0).astype(BF16)
    prev = jnp.concatenate([pr, pi], axis=1)
    for g in pair:
        y_ref[g] = (_dot(us_in[g], tp_ref[g]) + _dot_nt(prev, cs_ref[g])).astype(y_ref.dtype)
    finr_ref[...] = jnp.zeros_like(finr_ref)
    fini_ref[...] = jnp.zeros_like(fini_ref)
    for b in range(bsz):
        last = slice((b + 1) * chunks - 1, (b + 1) * chunks)
        finr_ref[b:b + 1, :] = xr[last]
        fini_ref[b:b + 1, :] = xi[last]

    half = S5_LP // 2
    x0r, x0i = x0r_ref[...], x0i_ref[...]
    hr, hi = ap[2:3, :], ap[3:4, :]
    xs = sum(_dot(us_ref[g], bs_ref[g, half:, :]) for g in pair)
    finsr_ref[...] = x0r * hr - x0i * hi + xs[:, :S5_RI]
    finsi_ref[...] = x0r * hi + x0i * hr + xs[:, S5_RI:]
    x0 = jnp.concatenate([x0r, x0i], axis=1).astype(BF16)
    for g in pair:
        ys_ref[g] = (_dot(us_ref[g], tp_ref[g, :half, :half]) + _dot_nt(x0, cs_ref[g, :half, :])).astype(ys_ref.dtype)


def _s5_scan(u_g, us_g, x0r, x0i, bs, cs, tp, ap, bsz):
    g, n_rows, _ = u_g.shape
    n_s = us_g.shape[1]
    half = S5_LP // 2
    pspec = lambda *s: pl.BlockSpec((2,) + s, lambda i: (i,) + (0,) * len(s))
    ospec = lambda *s: pl.BlockSpec((None,) + s, lambda i: (i,) + (0,) * len(s))
    packed = lambda n: jax.ShapeDtypeStruct((g // 2, n, S5_RI), F32)
    return pl.pallas_call(
        functools.partial(_s5_scan_kernel, bsz=bsz, chunks=n_rows // bsz),
        grid=(g // 2,),
        in_specs=[pspec(n_rows, S5_LP), pspec(n_s, half), ospec(n_s, S5_RI), ospec(n_s, S5_RI)]
                 + [pspec(S5_LP, 2 * S5_RI)] * 2 + [pspec(S5_LP, S5_LP), pspec(8, S5_RI)],
        out_specs=[pspec(n_rows, S5_LP), pspec(n_s, half), ospec(8, S5_RI), ospec(8, S5_RI),
                   ospec(n_s, S5_RI), ospec(n_s, S5_RI)],
        out_shape=[jax.ShapeDtypeStruct((g, n_rows, S5_LP), BF16), jax.ShapeDtypeStruct((g, n_s, half), BF16),
                   packed(8), packed(8), packed(n_s), packed(n_s)],
        scratch_shapes=[pltpu.VMEM((2, n_rows, S5_RI), F32)],
        compiler_params=_params(("arbitrary",)),
        name="s5_scan",
    )(u_g, us_g, x0r, x0i, bs, cs, tp, ap)


def kernel(x_prompt, x_sample, state_ret, state_swa_k, state_swa_v, state_ssm_re, state_ssm_im, norm_mix, norm_ffn, norm_final, w_in_ab, ret_gn, w_out_ab, ssm_lam_re, ssm_lam_im, ssm_log_step, ssm_b_re, ssm_b_im, ssm_c_re, ssm_c_im, ssm_d, w_glu, w_ffn_in, w_ffn_out):
    bsz, seq, d = x_prompt.shape
    dbsz, n_new, _ = x_sample.shape
    assert state_swa_k.shape[2] == SWA_BUF and n_new == S5_CHUNK // 2 and seq % SWA_ROWS == 0
    xp = x_prompt.reshape(bsz * seq, d)
    xs = x_sample.reshape(dbsz * n_new, d)
    tm_p, tm_s = TAIL_ROWS, dbsz * n_new
    row = lambda v: v.reshape(1, -1)

    w_in0 = w_in_ab[0].astype(BF16)
    g_mix0, g_ffn0, gn0 = row(norm_mix[0]), row(norm_ffn[0]), row(ret_gn[0])

    ra_p, qkv_p = _ab_proj(xp, g_mix0, w_in0, PROJ_ROWS)
    ra_s, qkv_s = _ab_proj(xs, g_mix0, w_in0, tm_s)

    zero_ret = jnp.zeros((bsz, H_A, DK_A, DV_A), F32)
    a_p, ret_p = _retention(ra_p, zero_ret, gn0, seq=seq, c_real=RET_CHUNK, n_chunks=RET_STEP_CHUNKS, n_seq=bsz,
                            out_dtype=BF16)
    a_s, ret_s = _retention(ra_s, state_ret[0], gn0, seq=n_new, c_real=n_new, n_chunks=1, n_seq=RET_SAMPLE_SEQS,
                            out_dtype=F32)

    width_b = H_B * DH_B
    rows_last = lambda w: w.transpose(0, 2, 3, 1).reshape(dbsz, width_b, SWA_BUF)
    rows_first = lambda w: w.reshape(dbsz, H_B, DH_B, SWA_BUF).transpose(0, 3, 1, 2)[None]
    later = (w_out_ab[0], w_ffn_in.reshape(-1, w_ffn_in.shape[-1]), w_ffn_out.reshape(-1, w_ffn_out.shape[-1]), w_glu[0])
    ob_p, ob_s, swk_s, swv_s, w_out0, w_f_in, w_f_out, w_glu0 = _swa(
        qkv_p, bsz, seq, qkv_s, rows_last(state_swa_k[0]), rows_last(state_swa_v[0]), n_new, later)
    w_f_in, w_f_out = w_f_in.reshape(w_ffn_in.shape), w_f_out.reshape(w_ffn_out.shape)
    kv_tail = qkv_p.reshape(bsz, seq, QKV_W)[:, seq - SWA_BUF:, width_b:]
    swk_p = kv_tail[..., :width_b].reshape(bsz, SWA_BUF, H_B, DH_B)
    swv_p = kv_tail[..., width_b:].reshape(bsz, SWA_BUF, H_B, DH_B)

    g_mix1, g_ffn1 = row(norm_mix[1]), row(norm_ffn[1])
    perm = _slab_perm()
    yp, u_p = _ab_tail(xp, a_p, ob_p, w_out0, g_ffn0, w_f_in, w_f_out, 0, g_mix1, perm, tm_p, S5_CHUNK)
    ys, u_s = _ab_tail(xs, a_s, ob_s, w_out0, g_ffn0, w_f_in, w_f_out, 0, g_mix1, perm, tm_s, n_new)

    operators = _s5_prep(ssm_lam_re[0], ssm_lam_im[0], ssm_log_step[0],
                         ssm_b_re[0], ssm_b_im[0], ssm_c_re[0], ssm_c_im[0])
    pack = lambda st: st.reshape(st.shape[0], -1, S5_RI).transpose(1, 0, 2)
    unpack = lambda st: st.transpose(1, 0, 2).reshape(st.shape[1], -1, S5_STATE)[None]
    y5_p, y5_s, finr_p, fini_p, finr_s, fini_s = _s5_scan(
        u_p, u_s, pack(state_ssm_re[0]), pack(state_ssm_im[0]), *operators, bsz)

    tail1 = (perm.T, g_mix1, row(ssm_d[0]), w_glu0, g_ffn1,
             w_f_in, w_f_out, 1, row(norm_final))
    yp = _s5_tail(yp, y5_p, *tail1, tm_p, S5_CHUNK)
    ys = _s5_tail(ys, y5_s, *tail1, tm_s, n_new)

    return (yp.reshape(bsz, seq, d), ys.reshape(dbsz, n_new, d),
            ret_p[None], ret_s[None],
            swk_p[None], swv_p[None],
            rows_first(swk_s), rows_first(swv_s),
            unpack(finr_p[:, :bsz]), unpack(fini_p[:, :bsz]), unpack(finr_s), unpack(fini_s))
```

```python
import functools

import jax
import jax.numpy as jnp
import numpy as np
from jax import lax
from jax.experimental import pallas as pl
from jax.experimental.pallas import tpu as pltpu

F32 = jnp.float32
BF16 = jnp.bfloat16

H_A, DK_A, DV_A = 4, 64, 128
H_B, DH_B = 8, 64
SWA_PAIRS = ((128, 1), (512, 4), (2048, 16))
SPAN = 128
SWA_BUF = 2048
PAST_LEN = 16384
RET_CHUNK = 128
S5_GROUP, S5_STATE = 16, 64
S5_CHUNK = 16
EPS = 1e-6
NEG_INF = -1e30
QA_W, KA_W, VA_W, GA_W = H_A * DK_A, H_A * DK_A, H_A * DV_A, H_A * DV_A
RA_W = QA_W + KA_W + VA_W + GA_W
QKV_W = 3 * H_B * DH_B

LANES = 128
SUBLANES = 8
VMEM_LIMIT = 58 * 1024 * 1024

PROJ_ROWS = 1024
PROJ_COLS = 512
TAIL_ROWS = 512
RET_STEP_CHUNKS = 4
RET_SAMPLE_SEQS = 4


def _params(sem):
    return pltpu.CompilerParams(dimension_semantics=sem, vmem_limit_bytes=VMEM_LIMIT)


def _const_spec(shape):
    nd = len(shape)
    return pl.BlockSpec(shape, lambda *_: (0,) * nd, pipeline_mode=pl.Buffered(1))


def _rms(x, g):
    return x * lax.rsqrt(jnp.mean(x * x, axis=-1, keepdims=True) + EPS) * g


def _dot(a, b):
    return jnp.dot(a, b, preferred_element_type=F32)


def _dot_nt(a, b):
    return lax.dot_general(a, b, (((1,), (1,)), ((), ())), preferred_element_type=F32)


def _dot_tn(a, b):
    return lax.dot_general(a, b, (((0,), (0,)), ((), ())), preferred_element_type=F32)


def _ab_proj_kernel(x_ref, g_ref, w_ref, ra_ref, qkv_ref):
    h = _rms(x_ref[...], g_ref[...]).astype(BF16)
    for n0 in range(0, RA_W, PROJ_COLS):
        ra_ref[:, n0:n0 + PROJ_COLS] = _dot(h, w_ref[:, n0:n0 + PROJ_COLS])
    for n0 in range(0, QKV_W, PROJ_COLS):
        qkv_ref[:, n0:n0 + PROJ_COLS] = _dot(h, w_ref[:, RA_W + n0:RA_W + n0 + PROJ_COLS])


def _ab_proj(x, g, w_bf16, tm):
    t, d = x.shape
    return pl.pallas_call(
        _ab_proj_kernel,
        grid=(t // tm,),
        in_specs=[pl.BlockSpec((tm, d), lambda i: (i, 0)),
                  _const_spec((1, d)),
                  _const_spec((d, RA_W + QKV_W))],
        out_specs=[pl.BlockSpec((tm, RA_W), lambda i: (i, 0)),
                   pl.BlockSpec((tm, QKV_W), lambda i: (i, 0))],
        out_shape=[jax.ShapeDtypeStruct((t, RA_W), F32), jax.ShapeDtypeStruct((t, QKV_W), F32)],
        compiler_params=_params(("arbitrary",)),
        name="ab_proj",
    )(x, g, w_bf16)


def _retention_tables(c_real):
    c = RET_CHUNK
    log_g = np.log1p(-np.exp2(-5.0 - np.arange(H_A, dtype=F32)))
    idx = np.arange(c, dtype=F32)
    rel = idx[:, None] - idx[None, :]
    dec = np.where(rel >= 0, np.exp(np.maximum(rel, 0.0)[None] * log_g[:, None, None]), 0.0)
    dec = dec.reshape(H_A * c, c)
    real = (idx < c_real)[:, None]
    qd = np.exp((idx + 1.0)[:, None] * log_g[None, :])
    qd = np.repeat(qd, DV_A, axis=1)
    kd = np.where(real, np.exp((c_real - 1.0 - idx)[:, None] * log_g[None, :]), 0.0)
    kd = np.repeat(kd, DK_A, axis=1)
    row_h = np.arange(H_A * DK_A)[:, None] // DK_A
    col_h = np.arange(H_A * DV_A)[None, :] // DV_A
    bd = (row_h == col_h).astype(F32)
    dm = bd * np.repeat(np.exp(c_real * log_g), DV_A)[None, :]
    return tuple(np.asarray(tb, F32) for tb in (dec, qd, kd, dm))


def _retention_kernel(ra_ref, s0_ref, gain_ref, dec_ref, qd_ref, kd_ref, dm_ref,
                      a_ref, sout_ref, sbd_ref, *, c_real, n_chunks, n_seq):
    c = RET_CHUNK
    j = pl.program_id(1)

    @pl.when(j == 0)
    def _():
        sbd_ref[...] = jnp.zeros_like(sbd_ref)
        for s in range(n_seq):
            for h in range(H_A):
                sbd_ref[s, h * DK_A:(h + 1) * DK_A, h * DV_A:(h + 1) * DV_A] = s0_ref[s, h]

    lane_q = lax.broadcasted_iota(jnp.int32, (c, QA_W), 1) // DK_A
    for ci in range(n_chunks):
        for s in range(n_seq):
            _retention_chunk(ra_ref.at[s], a_ref.at[s], sbd_ref.at[s], gain_ref, dec_ref, qd_ref, kd_ref, dm_ref,
                             lane_q, ci, c_real)

    @pl.when(j == pl.num_programs(1) - 1)
    def _():
        for s in range(n_seq):
            for h in range(H_A):
                sout_ref[s, h] = sbd_ref[s, h * DK_A:(h + 1) * DK_A, h * DV_A:(h + 1) * DV_A]


def _retention_chunk(ra_ref, a_ref, sbd_ref, gain_ref, dec_ref, qd_ref, kd_ref, dm_ref, lane_q, ci, c_real):
    c = RET_CHUNK
    rows = ra_ref[ci * c_real:(ci + 1) * c_real, :]
    gate = rows[:, QA_W + KA_W + VA_W:]
    if c_real < c:
        rows = jnp.concatenate([rows, jnp.zeros((c - c_real, RA_W), F32)], axis=0)
    q = rows[:, :QA_W]
    k = rows[:, QA_W:QA_W + KA_W] * (DK_A ** -0.5)
    v = rows[:, QA_W + KA_W:QA_W + KA_W + VA_W]
    vb = v.astype(BF16)
    qm = jnp.concatenate([jnp.where(lane_q == h, q, 0.0) for h in range(H_A)], axis=0).astype(BF16)
    s = _dot_nt(qm, k.astype(BF16)) * dec_ref[...]
    sb = s.astype(BF16)
    sbd = sbd_ref[...]
    cross = _dot(q.astype(BF16), sbd.astype(BF16)) * qd_ref[...]
    upd = _dot_tn((k * kd_ref[...]).astype(BF16), vb)
    for h in range(H_A):
        blk = (slice(h * DK_A, (h + 1) * DK_A), slice(h * DV_A, (h + 1) * DV_A))
        sbd_ref[blk] = sbd[blk] * dm_ref[blk] + upd[blk]
    for h in range(H_A):
        sl = slice(h * DV_A, (h + 1) * DV_A)
        o = _dot(sb[h * c:(h + 1) * c], vb[:, sl]) + cross[:, sl]
        o = o[:c_real]
        mu = jnp.mean(o, axis=-1, keepdims=True)
        var = jnp.mean(jnp.square(o - mu), axis=-1, keepdims=True)
        y = (o - mu) * lax.rsqrt(var + EPS) * gain_ref[:, sl]
        gh = gate[:, sl]
        a_ref[ci * c_real:(ci + 1) * c_real, sl] = (gh * jax.nn.sigmoid(gh) * y).astype(a_ref.dtype)


def _retention(ra, state0, gain, *, seq, c_real, n_chunks, n_seq, out_dtype):
    t = ra.shape[0]
    bsz = t // seq
    rows = c_real * n_chunks
    steps = seq // rows
    tables = _retention_tables(c_real)
    kern = functools.partial(_retention_kernel, c_real=c_real, n_chunks=n_chunks, n_seq=n_seq)
    a, s_out = pl.pallas_call(
        kern,
        grid=(bsz // n_seq, steps),
        in_specs=[pl.BlockSpec((n_seq, rows, RA_W), lambda b, j: (b, j, 0)),
                  pl.BlockSpec((n_seq, H_A, DK_A, DV_A), lambda b, j: (b, 0, 0, 0)),
                  _const_spec((1, VA_W))] + [_const_spec(tb.shape) for tb in tables],
        out_specs=[pl.BlockSpec((n_seq, rows, VA_W), lambda b, j: (b, j, 0)),
                   pl.BlockSpec((n_seq, H_A, DK_A, DV_A), lambda b, j: (b, 0, 0, 0))],
        out_shape=[jax.ShapeDtypeStruct((bsz, seq, VA_W), out_dtype),
                   jax.ShapeDtypeStruct((bsz, H_A, DK_A, DV_A), F32)],
        scratch_shapes=[pltpu.VMEM((n_seq, H_A * DK_A, H_A * DV_A), F32)],
        compiler_params=_params(("arbitrary", "arbitrary")),
        name="retention",
    )(ra.reshape(bsz, seq, RA_W), state0, gain, *tables)
    return a.reshape(t, VA_W), s_out


SWA_ROWS = 2048
N_BRANCH = len(SWA_PAIRS)


def _alibi_slopes():
    return np.exp2(-8.0 * np.arange(1, H_B + 1, dtype=F32) / H_B)


def _swa_prompt_bias():
    qi = np.arange(SPAN)[:, None]
    kj = np.arange(2 * SPAN)[None, :]
    dist = SPAN + qi - kj
    band = (dist >= 0) & (dist <= SPAN)
    slopes = _alibi_slopes()
    out = []
    for (_, dil) in SWA_PAIRS:
        pen = -slopes[:, None, None] * (dil * dist).astype(F32)[None]
        normal = np.where(band[None], pen, NEG_INF)
        first = np.where((band & (kj >= SPAN))[None], pen, NEG_INF)
        out.append(np.stack([normal, first], axis=1))
    tab = np.stack(out, axis=1)
    tab = tab.reshape(H_B // 2, 2, N_BRANCH, 2, SPAN, 2 * SPAN).transpose(0, 2, 3, 1, 4, 5)
    return np.asarray(tab.reshape(H_B // 2, N_BRANCH, 2, 2 * SPAN, 2 * SPAN), F32)


COARSE = 4


def _swa_unit(q, kk, vv, bias, head0):
    q = q * (DH_B ** -0.5)
    qm = jnp.concatenate([jnp.where(head0, q, 0.0), jnp.where(head0, 0.0, q)], axis=0).astype(BF16)
    s = _dot_nt(qm, kk.astype(BF16)) + bias
    m = jnp.max(s, axis=-1, keepdims=True)
    p = jnp.exp(s - m).astype(BF16)
    ones = jnp.ones((2 * SPAN, LANES), BF16)
    res = _dot(p, jnp.concatenate([vv.astype(BF16), ones], axis=1))
    acc = jnp.where(head0, res[:SPAN, :LANES], res[SPAN:, :LANES])
    den = jnp.where(head0, res[:SPAN, LANES:], res[SPAN:, LANES:])
    mm = jnp.where(head0, jnp.broadcast_to(m[:SPAN], (SPAN, LANES)), jnp.broadcast_to(m[SPAN:], (SPAN, LANES)))
    return acc, den, mm


SWA_PREV = SPAN * COARSE


def _swa_prompt_half(half, q_ref, k_ref, v_ref, bias_ref, o_ref, kbuf, vbuf, acc_ref, l_ref, m_ref,
                     q4, k4, v4, acc4, l4, m4):
    j = pl.program_id(2)
    rows = SWA_ROWS
    sub = rows // COARSE
    slot = j % 2

    @pl.when((half == 0) & (j > 0))
    def _():
        kbuf[0:SWA_PREV, :] = kbuf[rows:rows + SWA_PREV, :]
        vbuf[0:SWA_PREV, :] = vbuf[rows:rows + SWA_PREV, :]

    @pl.when((half == 0) & (j == 0))
    def _():
        k4[1] = jnp.zeros(k4.shape[1:], F32)
        v4[1] = jnp.zeros(v4.shape[1:], F32)
        kbuf[0:SWA_PREV, :] = jnp.zeros((SWA_PREV, LANES), F32)
        vbuf[0:SWA_PREV, :] = jnp.zeros((SWA_PREV, LANES), F32)

    @pl.when(half == 0)
    def _():
        kbuf[SWA_PREV:, :] = k_ref[...]
        vbuf[SWA_PREV:, :] = v_ref[...]
        for c in range(COARSE):
            q4[c] = q_ref[pl.ds(c, sub, stride=COARSE), :]
            k4[slot, c] = k_ref[pl.ds(c, sub, stride=COARSE), :]
            v4[slot, c] = v_ref[pl.ds(c, sub, stride=COARSE), :]

    lane = lax.broadcasted_iota(jnp.int32, (SPAN, LANES), 1)
    head0 = lane < DH_B
    first_step = (j == 0).astype(jnp.int32)

    for g, (_, dil) in enumerate(SWA_PAIRS):
        blocks = rows // (SPAN * dil)
        per_half = blocks * dil // 2

        def unit(u, g=g, dil=dil):
            wb = u // dil
            r = u % dil
            q_start = wb * (SPAN * dil) + r
            k_start = SWA_PREV + (wb - 1) * (SPAN * dil) + r
            variant = jnp.where(wb == 0, first_step, 0)
            acc, den, mm = _swa_unit(q_ref[pl.ds(q_start, SPAN, stride=dil), :],
                                     kbuf[pl.ds(k_start, 2 * SPAN, stride=dil), :],
                                     vbuf[pl.ds(k_start, 2 * SPAN, stride=dil), :],
                                     bias_ref[0, g, variant], head0)
            acc_ref[g, pl.ds(q_start, SPAN, stride=dil), :] = acc
            l_ref[g, pl.ds(q_start, SPAN, stride=dil), :] = den
            m_ref[g, pl.ds(q_start, SPAN, stride=dil), :] = mm

        def unit_two_level(u, g=g, fine=dil // COARSE):
            c = u % COARSE
            f = u // COARSE
            pick = pl.ds(f, SPAN, stride=fine)
            acc, den, mm = _swa_unit(q4[c, pick, :],
                                     jnp.concatenate([k4[1 - slot, c, pick, :], k4[slot, c, pick, :]], axis=0),
                                     jnp.concatenate([v4[1 - slot, c, pick, :], v4[slot, c, pick, :]], axis=0),
                                     bias_ref[0, g, first_step], head0)
            dst = pl.ds(c * sub + f, SPAN, stride=fine)
            acc4[dst, :] = acc
            l4[dst, :] = den
            m4[dst, :] = mm

        two_level = dil % (COARSE * COARSE) == 0
        assert not two_level or blocks == 1
        body = unit_two_level if two_level else unit

        @pl.when(half >= 0)
        def _(body=body, per_half=per_half):
            for i in range(per_half):
                body(half * per_half + i)

        if two_level:
            @pl.when(half == 1)
            def _(g=g):
                for c in range(COARSE):
                    src, dst = slice(c * sub, (c + 1) * sub), pl.ds(c, sub, stride=COARSE)
                    acc_ref[g, dst, :] = acc4[src, :]
                    l_ref[g, dst, :] = l4[src, :]
                    m_ref[g, dst, :] = m4[src, :]

    @pl.when(half == 1)
    def _():
        tile = 256
        for r0 in range(0, rows, tile):
            sl = slice(r0, r0 + tile)
            ms = [m_ref[g, sl, :] for g in range(N_BRANCH)]
            mx = functools.reduce(jnp.maximum, ms)
            ws = [jnp.exp(mg - mx) for mg in ms]
            num = sum(w * acc_ref[g, sl, :] for g, w in enumerate(ws))
            den = sum(w * l_ref[g, sl, :] for g, w in enumerate(ws))
            o_ref[sl, :] = (num / den).astype(o_ref.dtype)


def _swa_sample_bias(n_new):
    t = np.arange(n_new)[:, None]
    slopes = _alibi_slopes()

    def table(j):
        dist = SWA_BUF + t - j
        out = []
        for (window, dil) in SWA_PAIRS:
            valid = (dist >= 0) & (dist <= window) & (dist % dil == 0) & (PAST_LEN + t - dist >= 0)
            pen = -slopes[:, None, None] * dist.astype(F32)[None]
            out.append(np.where(valid[None], pen, NEG_INF).reshape(H_B * n_new, -1))
        return np.stack(out)

    bias_a = table(np.arange(SWA_BUF)[None, :])
    jb = np.arange(LANES)[None, :]
    bias_b = np.where(jb < n_new, table(SWA_BUF + jb), NEG_INF)
    return np.asarray(bias_a, F32), np.asarray(bias_b, F32)


def _pad_new_rows(new_ref, n_new):
    return jnp.concatenate([new_ref[...], jnp.zeros((LANES - n_new, new_ref.shape[1]), F32)], axis=0)


def _swa_sample_attend(q_ref, k_ref, v_ref, kp_ref, vp_ref, ba_ref, bb_ref, o_ref, n_new):
    width = q_ref.shape[1]
    n_heads = width // DH_B
    lane_h = lax.broadcasted_iota(jnp.int32, (n_new, width), 1) // DH_B
    q = q_ref[...] * (DH_B ** -0.5)
    qm = jnp.concatenate([jnp.where(lane_h == h, q, 0.0) for h in range(n_heads)], axis=0).astype(BF16)
    k_new, v_new = _pad_new_rows(k_ref, n_new), _pad_new_rows(v_ref, n_new)
    s_a = _dot(qm, kp_ref[0].astype(BF16))
    s_b = _dot_nt(qm, k_new.astype(BF16))
    sa = [s_a + ba_ref[g] for g in range(N_BRANCH)]
    sb = [s_b + bb_ref[g] for g in range(N_BRANCH)]
    mx = functools.reduce(jnp.maximum, [jnp.max(x, axis=-1, keepdims=True) for x in sa + sb])
    p_a = sum(jnp.exp(x - mx) for x in sa).astype(BF16)
    p_b = sum(jnp.exp(x - mx) for x in sb).astype(BF16)
    den = (jnp.sum(p_a.astype(F32), axis=-1, keepdims=True)
           + jnp.sum(p_b.astype(F32), axis=-1, keepdims=True))
    o = (_dot_nt(p_a, vp_ref[0].astype(BF16)) + _dot(p_b, v_new.astype(BF16))) / den
    o_ref[...] = sum(jnp.where(lane_h == h, o[h * n_new:(h + 1) * n_new], 0.0) for h in range(n_heads))

def _swa_window_update(src_ref, new_ref, dst_ref, n_new):
    lane = lax.broadcasted_iota(jnp.int32, (DH_B, LANES), 1)
    new_t = pltpu.roll(_pad_new_rows(new_ref, n_new).T, LANES - n_new, axis=1)
    for h in range(new_ref.shape[1] // DH_B):
        rows = slice(h * DH_B, (h + 1) * DH_B)
        shifted = pltpu.roll(src_ref[0, rows, :], SWA_BUF - n_new, axis=1)
        dst_ref[0, rows, 0:SWA_BUF - LANES] = shifted[:, 0:SWA_BUF - LANES]
        dst_ref[0, rows, SWA_BUF - LANES:] = jnp.where(lane >= LANES - n_new, new_t[rows],
                                                       shifted[:, SWA_BUF - LANES:])


N_SWA_IN, N_SWA_OUT = 11, 4


def _swa_kernel(*refs, n_new, n_cast):
    (q_ref, k_ref, v_ref, bias_ref, qs_ref, ks_ref, vs_ref, kp_ref, vp_ref, ba_ref, bb_ref), refs = (
        refs[:N_SWA_IN], refs[N_SWA_IN:])
    cast_src, refs = refs[:n_cast], refs[n_cast:]
    (o_ref, os_ref, ko_ref, vo_ref), refs = refs[:N_SWA_OUT], refs[N_SWA_OUT:]
    cast_dst, scratch = refs[:n_cast], refs[n_cast:]
    half = pl.program_id(3)
    _swa_prompt_half(half, q_ref, k_ref, v_ref, bias_ref, o_ref, *scratch)
    _swa_sample_attend(qs_ref, ks_ref, vs_ref, kp_ref, vp_ref, ba_ref, bb_ref, os_ref, n_new)
    _swa_window_update(kp_ref, ks_ref, ko_ref, n_new)
    _swa_window_update(vp_ref, vs_ref, vo_ref, n_new)

    @pl.when(half == 0)
    def _():
        for src, dst in zip(cast_src, cast_dst):
            dst[...] = src[...].astype(dst.dtype)


def _swa(qkv, bsz, seq, qkv_s, k_past, v_past, n_new, weights):
    t = qkv.shape[0]
    dbsz = k_past.shape[0]
    steps = seq // SWA_ROWS
    npair = H_B // 2
    width = H_B * DH_B
    half_w = width // 2
    assert dbsz == bsz * npair * steps, "one sample sequence per prompt (sequence, head pair, row block) step"
    bias = _swa_prompt_bias()
    bias_a, bias_b = _swa_sample_bias(n_new)
    blk = (SWA_ROWS, LANES)
    rows_of = lambda b, j: b * steps + j
    seq_of = lambda b, hp, j: (b * npair + hp) * steps + j
    new_blk = lambda col: pl.BlockSpec((n_new, half_w), lambda b, hp, j, h, col=col: (seq_of(b, hp, j), 2 * col + h))
    state_blk = pl.BlockSpec((1, half_w, SWA_BUF), lambda b, hp, j, h: (seq_of(b, hp, j), h, 0))
    half_rows = H_B // 2 * n_new
    sub = SWA_ROWS // COARSE
    assert all(w.shape[0] % (dbsz * 2 * SUBLANES) == 0 for w in weights), "bf16 row blocks are 16-row tiles"
    cast_blk = [pl.BlockSpec((w.shape[0] // dbsz, w.shape[1]), lambda b, hp, j, h: (seq_of(b, hp, j), 0))
                for w in weights]
    in_specs = [pl.BlockSpec(blk, lambda b, hp, j, h: (rows_of(b, j), hp)),
                pl.BlockSpec(blk, lambda b, hp, j, h: (rows_of(b, j), npair + hp)),
                pl.BlockSpec(blk, lambda b, hp, j, h: (rows_of(b, j), 2 * npair + hp)),
                pl.BlockSpec((1, N_BRANCH, 2, 2 * SPAN, 2 * SPAN), lambda b, hp, j, h: (hp, 0, 0, 0, 0)),
                new_blk(0), new_blk(1), new_blk(2), state_blk, state_blk,
                pl.BlockSpec((N_BRANCH, half_rows, SWA_BUF), lambda b, hp, j, h: (0, h, 0)),
                pl.BlockSpec((N_BRANCH, half_rows, LANES), lambda b, hp, j, h: (0, h, 0))]
    assert len(in_specs) == N_SWA_IN
    return pl.pallas_call(
        functools.partial(_swa_kernel, n_new=n_new, n_cast=len(weights)),
        grid=(bsz, npair, steps, 2),
        in_specs=in_specs + cast_blk,
        out_specs=[pl.BlockSpec(blk, lambda b, hp, j, h: (rows_of(b, j), hp)),
                   pl.BlockSpec((n_new, half_w), lambda b, hp, j, h: (seq_of(b, hp, j), h)),
                   state_blk, state_blk] + cast_blk,
        out_shape=[jax.ShapeDtypeStruct((t, width), BF16),
                   jax.ShapeDtypeStruct((dbsz * n_new, width), F32),
                   jax.ShapeDtypeStruct(k_past.shape, F32), jax.ShapeDtypeStruct(v_past.shape, F32)]
                  + [jax.ShapeDtypeStruct(w.shape, BF16) for w in weights],
        scratch_shapes=[pltpu.VMEM((SWA_PREV + SWA_ROWS, LANES), F32)] * 2
                       + [pltpu.VMEM((N_BRANCH, SWA_ROWS, LANES), F32)] * 3
                       + [pltpu.VMEM((COARSE, sub, LANES), F32)]
                       + [pltpu.VMEM((2, COARSE, sub, LANES), F32)] * 2
                       + [pltpu.VMEM((SWA_ROWS, LANES), F32)] * 3,
        compiler_params=_params(("arbitrary",) * 4),
        name="swa",
    )(qkv, qkv, qkv, bias, qkv_s, qkv_s, qkv_s, k_past, v_past, bias_a, bias_b, *weights)


FFN_CHUNK = 256


def _ffn(y1, g_ffn_ref, w_in_ref, w_o_ref, acc_ref):
    d_ff = w_o_ref.shape[0]
    h = _rms(y1, g_ffn_ref[...]).astype(BF16)
    for ci, c0 in enumerate(range(0, d_ff, FFN_CHUNK)):
        gate = _dot(h, w_in_ref[:, c0:c0 + FFN_CHUNK])
        up = _dot(h, w_in_ref[:, d_ff + c0:d_ff + c0 + FFN_CHUNK])
        act = (gate * jax.nn.sigmoid(gate) * up).astype(BF16)
        part = _dot(act, w_o_ref[c0:c0 + FFN_CHUNK, :])
        if ci == 0:
            acc_ref[...] = part
        else:
            acc_ref[...] += part
    return y1 + acc_ref[...]


def _ab_tail_kernel(x_ref, a_ref, ob_ref, w_out_ref, g_ffn_ref, w_in_ref, w_o_ref, g_next_ref, perm_ref,
                    o_ref, u_ref, acc_ref, h_ref, z_ref, *, chunk):
    mix = (_dot(a_ref[...].astype(BF16), w_out_ref[0:VA_W, :])
           + _dot(ob_ref[...].astype(BF16), w_out_ref[VA_W:, :]))
    y = _ffn(x_ref[...] + mix, g_ffn_ref, w_in_ref, w_o_ref, acc_ref)
    o_ref[...] = y
    _group_rows(_rms(y, g_next_ref[...]), perm_ref, u_ref, h_ref, z_ref, chunk)


def _s5_tail_kernel(x_ref, yg_ref, perm_ref, g_mix_ref, dskip_ref, w_glu_ref, g_ffn_ref, w_in_ref, w_o_ref,
                    g_fin_ref, o_ref, acc_ref, z_ref, t_ref, *, chunk):
    x = x_ref[...]
    d = x.shape[-1]
    u = _rms(x, g_mix_ref[...])
    _ungroup_rows(yg_ref, perm_ref, z_ref, t_ref, chunk)
    ys = jnp.concatenate([t_ref[v] for v in range(d // LANES)], axis=1)
    z = jax.nn.gelu(ys + dskip_ref[...] * u, approximate=True).astype(BF16)
    val = _dot(z, w_glu_ref[:, 0:d])
    gate = _dot(z, w_glu_ref[:, d:2 * d])
    y2 = _ffn(x + val * jax.nn.sigmoid(gate), g_ffn_ref, w_in_ref, w_o_ref, acc_ref)
    o_ref[...] = _rms(y2, g_fin_ref[...])


def _row_spec(tm, width):
    return pl.BlockSpec((tm, width), lambda i: (i, 0))


def _group_scratch(tm, d, chunk):
    n_slab, nc, n_col = d // LANES, tm // chunk, chunk // SLAB_GROUPS
    assert min(S5_MM_ROWS, n_slab * n_col * nc) % nc == 0
    return (pltpu.VMEM((n_slab, tm, LANES), F32), pltpu.VMEM((n_slab * n_col * nc, SLAB_GROUPS * LANES), BF16))


def _group_spec(tm, d, chunk):
    return pl.BlockSpec((d // S5_GROUP, tm // chunk, chunk * S5_GROUP), lambda i: (0, i, 0))


def _layer_spec(stacked, layer):
    nd = stacked.ndim - 1
    return pl.BlockSpec((None,) + stacked.shape[1:], lambda *_: (layer,) + (0,) * nd, pipeline_mode=pl.Buffered(1))


def _ab_tail(x, a, ob, w_out, g_ffn, w_in, w_o, layer, g_next, perm, tm, chunk):
    t, d = x.shape
    h_scr, z_scr = _group_scratch(tm, d, chunk)
    return pl.pallas_call(
        functools.partial(_ab_tail_kernel, chunk=chunk),
        grid=(t // tm,),
        in_specs=[_row_spec(tm, d), _row_spec(tm, a.shape[1]), _row_spec(tm, ob.shape[1]),
                  _const_spec(w_out.shape), _const_spec((1, d)), _layer_spec(w_in, layer), _layer_spec(w_o, layer),
                  _const_spec((1, d)), _const_spec(perm.shape)],
        out_specs=[_row_spec(tm, d), _group_spec(tm, d, chunk)],
        out_shape=[jax.ShapeDtypeStruct((t, d), F32),
                   jax.ShapeDtypeStruct((d // S5_GROUP, t // chunk, chunk * S5_GROUP), BF16)],
        scratch_shapes=[pltpu.VMEM((tm, d), F32), h_scr, z_scr],
        compiler_params=_params(("arbitrary",)),
        name="ab_tail",
    )(x, a, ob, w_out, g_ffn, w_in, w_o, g_next, perm)


def _s5_tail(x, yg, perm_t, g_mix, dskip, w_glu, g_ffn, w_in, w_o, layer, g_fin, tm, chunk):
    t, d = x.shape
    t_scr, z_scr = _group_scratch(tm, d, chunk)
    return pl.pallas_call(
        functools.partial(_s5_tail_kernel, chunk=chunk),
        grid=(t // tm,),
        in_specs=[_row_spec(tm, d), _group_spec(tm, d, chunk), _const_spec(perm_t.shape),
                  _const_spec((1, d)), _const_spec((1, d)),
                  _const_spec(w_glu.shape), _const_spec((1, d)), _layer_spec(w_in, layer), _layer_spec(w_o, layer),
                  _const_spec((1, d))],
        out_specs=_row_spec(tm, d),
        out_shape=jax.ShapeDtypeStruct((t, d), F32),
        scratch_shapes=[pltpu.VMEM((tm, d), F32), z_scr, t_scr],
        compiler_params=_params(("arbitrary",)),
        name="s5_tail",
    )(x, yg, perm_t, g_mix, dskip, w_glu, g_ffn, w_in, w_o, g_fin)


SLAB_GROUPS = LANES // S5_GROUP
S5_MM_ROWS = 256


def _slab_perm():
    idx = np.arange(SLAB_GROUPS * LANES)
    l8, g8, p = idx // LANES, (idx % LANES) // S5_GROUP, idx % S5_GROUP
    dst = g8 * LANES + l8 * S5_GROUP + p
    return (dst[:, None] == idx[None, :]).astype(BF16)


def _permute_rows(z_ref, perm_ref, emit):
    step = min(S5_MM_ROWS, z_ref.shape[0])
    for r0 in range(0, z_ref.shape[0], step):
        emit(r0, _dot(z_ref[r0:r0 + step, :], perm_ref[...]))


def _group_rows(h, perm_ref, o_ref, h_ref, z_ref, chunk):
    rows = h.shape[0]
    nc = rows // chunk
    n_slab = h.shape[1] // LANES
    n_col = chunk // SLAB_GROUPS
    for v in range(n_slab):
        h_ref[v] = h[:, v * LANES:(v + 1) * LANES]
    for v in range(n_slab):
        for l in range(chunk):
            j, l8 = l // SLAB_GROUPS, l % SLAB_GROUPS
            r0 = (v * n_col + j) * nc
            z_ref[r0:r0 + nc, l8 * LANES:(l8 + 1) * LANES] = h_ref[v, pl.ds(l, nc, stride=chunk), :].astype(BF16)

    def emit(r0, blk):
        for q in range(blk.shape[0] // nc):
            v, j = divmod(r0 // nc + q, n_col)
            for g8 in range(SLAB_GROUPS):
                o_ref[v * SLAB_GROUPS + g8, :, j * LANES:(j + 1) * LANES] = (
                    blk[q * nc:(q + 1) * nc, g8 * LANES:(g8 + 1) * LANES].astype(o_ref.dtype))

    _permute_rows(z_ref, perm_ref, emit)


def _ungroup_rows(y_ref, perm_ref, z_ref, t_ref, chunk):
    n_slab, rows, _ = t_ref.shape
    nc = rows // chunk
    n_col = chunk // SLAB_GROUPS
    for v in range(n_slab):
        for j in range(n_col):
            r0 = (v * n_col + j) * nc
            for g8 in range(SLAB_GROUPS):
                z_ref[r0:r0 + nc, g8 * LANES:(g8 + 1) * LANES] = y_ref[v * SLAB_GROUPS + g8, :, j * LANES:(j + 1) * LANES]

    def emit(r0, blk):
        for q in range(blk.shape[0] // nc):
            v, j = divmod(r0 // nc + q, n_col)
            for l8 in range(SLAB_GROUPS):
                t_ref[v, pl.ds(j * SLAB_GROUPS + l8, nc, stride=chunk), :] = (
                    blk[q * nc:(q + 1) * nc, l8 * LANES:(l8 + 1) * LANES])

    _permute_rows(z_ref, perm_ref, emit)


S5_LP = S5_CHUNK * S5_GROUP
S5_RI = 2 * S5_STATE


S5_PREP_GROUPS = 8


def _s5_prep_kernel(*refs):
    for k in range(refs[0].shape[0]):
        _s5_prep_group(k % 2, *[r.at[k] for r in refs])


def _s5_prep_group(parity, lam_row_ref, ls_ref, btr_ref, bti_ref, cr_ref, ci_ref,
                   bs_ref, cs_ref, tp_ref, ap_ref):
    n_pow = S5_CHUNK + 1
    dt = jnp.exp(ls_ref[...])

    def powers(lr, li):
        mag = jnp.exp(lr * dt)
        a_re, a_im = mag * jnp.cos(li * dt), mag * jnp.sin(li * dt)
        pw = [(jnp.ones_like(a_re), jnp.zeros_like(a_im))]
        for _ in range(n_pow - 1):
            pr, pi = pw[-1]
            pw.append((pr * a_re - pi * a_im, pr * a_im + pi * a_re))
        return pw

    lr, li = lam_row_ref[0:1, :], lam_row_ref[1:2, :]
    pw_row = powers(lr, li)
    a_re, a_im = pw_row[1]
    den = lr * lr + li * li
    f_re = ((a_re - 1.0) * lr + a_im * li) / den
    f_im = (a_im * lr - (a_re - 1.0) * li) / den
    blk = (S5_GROUP, S5_RI)
    w_re = jnp.concatenate([jnp.broadcast_to(f_re * pw_row[S5_CHUNK - 1 - l][0] - f_im * pw_row[S5_CHUNK - 1 - l][1], blk)
                            for l in range(S5_CHUNK)], axis=0)
    w_im = jnp.concatenate([jnp.broadcast_to(f_re * pw_row[S5_CHUNK - 1 - l][1] + f_im * pw_row[S5_CHUNK - 1 - l][0], blk)
                            for l in range(S5_CHUNK)], axis=0)

    lane = lax.broadcasted_iota(jnp.int32, (1, S5_RI), 1)
    own = (lane // S5_STATE) == parity
    low = lane < S5_STATE
    keep = lambda z: jnp.where(own, z, 0.0)
    per_pos = lambda ref: jnp.concatenate([ref[...]] * S5_CHUNK, axis=0)
    btr, bti = per_pos(btr_ref), per_pos(bti_ref)
    bs_ref[:, :S5_RI] = keep(w_re * btr - w_im * bti).astype(bs_ref.dtype)
    bs_ref[:, S5_RI:] = keep(w_re * bti + w_im * btr).astype(bs_ref.dtype)
    bf_t = jnp.where(low, f_re * btr[0:S5_GROUP] - f_im * bti[0:S5_GROUP],
                     f_re * bti[0:S5_GROUP] + f_im * btr[0:S5_GROUP])
    ap_ref[...] = jnp.zeros_like(ap_ref)
    for i, (j, part) in enumerate(((S5_CHUNK, 0), (S5_CHUNK, 1), (S5_CHUNK // 2, 0), (S5_CHUNK // 2, 1))):
        ap_ref[i:i + 1, :] = keep(pw_row[j][part])

    def spread(j0, part):
        return jnp.concatenate([jnp.broadcast_to(pw_row[j0 + l][part], blk) for l in range(S5_CHUNK)], axis=0)

    cr, ci = per_pos(cr_ref), per_pos(ci_ref)
    cs0 = jnp.where(low, cr * spread(0, 0) - ci * spread(0, 1),
                    -(cr * spread(0, 1) + ci * spread(0, 0)))
    cs_ref[:, :S5_RI] = keep(cr * spread(1, 0) - ci * spread(1, 1)).astype(cs_ref.dtype)
    cs_ref[:, S5_RI:] = keep(-(cr * spread(1, 1) + ci * spread(1, 0))).astype(cs_ref.dtype)
    r = lax.dot_general(bf_t, cs0, (((1,), (1,)), ((), ())), preferred_element_type=F32,
                        precision=lax.Precision.HIGHEST)
    lane_lp = lax.broadcasted_iota(jnp.int32, (S5_GROUP, S5_LP), 1)
    for l in range(S5_CHUNK):
        sh = l * S5_GROUP
        blk_l = r if l == 0 else jnp.where(lane_lp >= sh, pltpu.roll(r, sh, axis=1), 0.0)
        tp_ref[l * S5_GROUP:(l + 1) * S5_GROUP, :] = blk_l.astype(tp_ref.dtype)


def _s5_prep(lam_re, lam_im, log_step, b_re, b_im, c_re, c_im):
    g = lam_re.shape[0]
    dup = lambda z: jnp.concatenate([z, z], axis=-1)
    lam_row = jnp.stack([dup(lam_re), dup(lam_im)], axis=1)
    lam_row = jnp.pad(lam_row, ((0, 0), (0, 6), (0, 0)))
    ls = log_step.reshape(g, 1, 1)
    btr, bti = dup(jnp.swapaxes(b_re, 1, 2)), dup(jnp.swapaxes(b_im, 1, 2))
    cr, ci = dup(c_re), dup(c_im)
    gspec = lambda *s: pl.BlockSpec((S5_PREP_GROUPS,) + s, lambda i: (i,) + (0,) * len(s))
    table = jax.ShapeDtypeStruct((g, S5_LP, 2 * S5_RI), BF16)
    return pl.pallas_call(
        _s5_prep_kernel,
        grid=(g // S5_PREP_GROUPS,),
        in_specs=[gspec(8, S5_RI), gspec(1, 1)] + [gspec(S5_GROUP, S5_RI)] * 4,
        out_specs=[gspec(S5_LP, 2 * S5_RI)] * 2 + [gspec(S5_LP, S5_LP), gspec(8, S5_RI)],
        out_shape=[table] * 2 + [jax.ShapeDtypeStruct((g, S5_LP, S5_LP), BF16),
                                 jax.ShapeDtypeStruct((g, 8, S5_RI), F32)],
        compiler_params=_params(("arbitrary",)),
        name="s5_prep",
    )(lam_row, ls, btr, bti, cr, ci)


def _s5_scan_kernel(u_ref, us_ref, x0r_ref, x0i_ref, bs_ref, cs_ref, tp_ref, ap_ref,
                    y_ref, ys_ref, finr_ref, fini_ref, finsr_ref, finsi_ref, xs_ref, *, bsz, chunks):
    n_rows = bsz * chunks
    pair = range(2)
    us_in = [u_ref[g] for g in pair]
    x = sum(_dot(us_in[g], bs_ref[g]) for g in pair)
    xr, xi = x[:, :S5_RI], x[:, S5_RI:]
    row = lax.broadcasted_iota(jnp.int32, (n_rows, 1), 0) % chunks
    ap = ap_ref[0] + ap_ref[1]
    ar, ai = ap[0:1, :], ap[1:2, :]

    def prefix(xr, xi, ar, ai, pos, length, axis=0):
        shift = 1
        while shift < length:
            sr = jnp.where(pos >= shift, pltpu.roll(xr, shift, axis=axis), 0.0)
            si = jnp.where(pos >= shift, pltpu.roll(xi, shift, axis=axis), 0.0)
            xr, xi = xr + (sr * ar - si * ai), xi + (sr * ai + si * ar)
            ar, ai = ar * ar - ai * ai, 2.0 * (ar * ai)
            shift *= 2
        return xr, xi, ar, ai

    n_runs, runs = n_rows // SUBLANES, chunks // SUBLANES
    in_run = lambda z: z.reshape(n_rows // SUBLANES, SUBLANES, S5_RI)
    pos3 = lax.broadcasted_iota(jnp.int32, (1, SUBLANES, 1), 1)
    x3r, x3i, br, bi = prefix(in_run(xr), in_run(xi), ar, ai, pos3, SUBLANES, axis=1)
    xr, xi = x3r.reshape(n_rows, S5_RI), x3i.reshape(n_rows, S5_RI)
    xs_ref[0], xs_ref[1] = xr, xi
    last = pl.ds(SUBLANES - 1, n_runs, stride=SUBLANES)
    run = lax.broadcasted_iota(jnp.int32, (n_runs, 1), 0) % runs
    er, ei, _, _ = prefix(xs_ref[0, last, :], xs_ref[1, last, :], br, bi, run, runs)
    cr = jnp.where(run >= 1, pltpu.roll(er, 1, axis=0), 0.0)
    ci = jnp.where(run >= 1, pltpu.roll(ei, 1, axis=0), 0.0)
    pr, pi = ar, ai
    for r in range(SUBLANES):
        dst = pl.ds(r, n_runs, stride=SUBLANES)
        xs_ref[0, dst, :] = cr * pr - ci * pi
        xs_ref[1, dst, :] = cr * pi + ci * pr
        pr, pi = pr * ar - pi * ai, pr * ai + pi * ar
    xr, xi = xr + xs_ref[0], xi + xs_ref[1]
    pr = jnp.where(row >= 1, pltpu.roll(xr, 1, axis=0), 0.0).astype(BF16)
    pi = jnp.where(row >= 1, pltpu.roll(xi, 1, axis=0), 0.0).astype(BF16)
    prev = jnp.concatenate([pr, pi], axis=1)
    for g in pair:
        y_ref[g] = (_dot(us_in[g], tp_ref[g]) + _dot_nt(prev, cs_ref[g])).astype(y_ref.dtype)
    finr_ref[...] = jnp.zeros_like(finr_ref)
    fini_ref[...] = jnp.zeros_like(fini_ref)
    for b in range(bsz):
        last = slice((b + 1) * chunks - 1, (b + 1) * chunks)
        finr_ref[b:b + 1, :] = xr[last]
        fini_ref[b:b + 1, :] = xi[last]

    half = S5_LP // 2
    x0r, x0i = x0r_ref[...], x0i_ref[...]
    hr, hi = ap[2:3, :], ap[3:4, :]
    xs = sum(_dot(us_ref[g], bs_ref[g, half:, :]) for g in pair)
    finsr_ref[...] = x0r * hr - x0i * hi + xs[:, :S5_RI]
    finsi_ref[...] = x0r * hi + x0i * hr + xs[:, S5_RI:]
    x0 = jnp.concatenate([x0r, x0i], axis=1).astype(BF16)
    for g in pair:
        ys_ref[g] = (_dot(us_ref[g], tp_ref[g, :half, :half]) + _dot_nt(x0, cs_ref[g, :half, :])).astype(ys_ref.dtype)


def _s5_scan(u_g, us_g, x0r, x0i, bs, cs, tp, ap, bsz):
    g, n_rows, _ = u_g.shape
    n_s = us_g.shape[1]
    half = S5_LP // 2
    pspec = lambda *s: pl.BlockSpec((2,) + s, lambda i: (i,) + (0,) * len(s))
    ospec = lambda *s: pl.BlockSpec((None,) + s, lambda i: (i,) + (0,) * len(s))
    packed = lambda n: jax.ShapeDtypeStruct((g // 2, n, S5_RI), F32)
    return pl.pallas_call(
        functools.partial(_s5_scan_kernel, bsz=bsz, chunks=n_rows // bsz),
        grid=(g // 2,),
        in_specs=[pspec(n_rows, S5_LP), pspec(n_s, half), ospec(n_s, S5_RI), ospec(n_s, S5_RI)]
                 + [pspec(S5_LP, 2 * S5_RI)] * 2 + [pspec(S5_LP, S5_LP), pspec(8, S5_RI)],
        out_specs=[pspec(n_rows, S5_LP), pspec(n_s, half), ospec(8, S5_RI), ospec(8, S5_RI),
                   ospec(n_s, S5_RI), ospec(n_s, S5_RI)],
        out_shape=[jax.ShapeDtypeStruct((g, n_rows, S5_LP), BF16), jax.ShapeDtypeStruct((g, n_s, half), BF16),
                   packed(8), packed(8), packed(n_s), packed(n_s)],
        scratch_shapes=[pltpu.VMEM((2, n_rows, S5_RI), F32)],
        compiler_params=_params(("arbitrary",)),
        name="s5_scan",
    )(u_g, us_g, x0r, x0i, bs, cs, tp, ap)


def kernel(x_prompt, x_sample, state_ret, state_swa_k, state_swa_v, state_ssm_re, state_ssm_im, norm_mix, norm_ffn, norm_final, w_in_ab, ret_gn, w_out_ab, ssm_lam_re, ssm_lam_im, ssm_log_step, ssm_b_re, ssm_b_im, ssm_c_re, ssm_c_im, ssm_d, w_glu, w_ffn_in, w_ffn_out):
    bsz, seq, d = x_prompt.shape
    dbsz, n_new, _ = x_sample.shape
    assert state_swa_k.shape[2] == SWA_BUF and n_new == S5_CHUNK // 2 and seq % SWA_ROWS == 0
    xp = x_prompt.reshape(bsz * seq, d)
    xs = x_sample.reshape(dbsz * n_new, d)
    tm_p, tm_s = TAIL_ROWS, dbsz * n_new
    row = lambda v: v.reshape(1, -1)

    w_in0 = w_in_ab[0].astype(BF16)
    g_mix0, g_ffn0, gn0 = row(norm_mix[0]), row(norm_ffn[0]), row(ret_gn[0])

    ra_p, qkv_p = _ab_proj(xp, g_mix0, w_in0, PROJ_ROWS)
    ra_s, qkv_s = _ab_proj(xs, g_mix0, w_in0, tm_s)

    zero_ret = jnp.zeros((bsz, H_A, DK_A, DV_A), F32)
    a_p, ret_p = _retention(ra_p, zero_ret, gn0, seq=seq, c_real=RET_CHUNK, n_chunks=RET_STEP_CHUNKS, n_seq=bsz,
                            out_dtype=BF16)
    a_s, ret_s = _retention(ra_s, state_ret[0], gn0, seq=n_new, c_real=n_new, n_chunks=1, n_seq=RET_SAMPLE_SEQS,
                            out_dtype=F32)

    width_b = H_B * DH_B
    rows_last = lambda w: w.transpose(0, 2, 3, 1).reshape(dbsz, width_b, SWA_BUF)
    rows_first = lambda w: w.reshape(dbsz, H_B, DH_B, SWA_BUF).transpose(0, 3, 1, 2)[None]
    later = (w_out_ab[0], w_ffn_in.reshape(-1, w_ffn_in.shape[-1]), w_ffn_out.reshape(-1, w_ffn_out.shape[-1]), w_glu[0])
    ob_p, ob_s, swk_s, swv_s, w_out0, w_f_in, w_f_out, w_glu0 = _swa(
        qkv_p, bsz, seq, qkv_s, rows_last(state_swa_k[0]), rows_last(state_swa_v[0]), n_new, later)
    w_f_in, w_f_out = w_f_in.reshape(w_ffn_in.shape), w_f_out.reshape(w_ffn_out.shape)
    kv_tail = qkv_p.reshape(bsz, seq, QKV_W)[:, seq - SWA_BUF:, width_b:]
    swk_p = kv_tail[..., :width_b].reshape(bsz, SWA_BUF, H_B, DH_B)
    swv_p = kv_tail[..., width_b:].reshape(bsz, SWA_BUF, H_B, DH_B)

    g_mix1, g_ffn1 = row(norm_mix[1]), row(norm_ffn[1])
    perm = _slab_perm()
    yp, u_p = _ab_tail(xp, a_p, ob_p, w_out0, g_ffn0, w_f_in, w_f_out, 0, g_mix1, perm, tm_p, S5_CHUNK)
    ys, u_s = _ab_tail(xs, a_s, ob_s, w_out0, g_ffn0, w_f_in, w_f_out, 0, g_mix1, perm, tm_s, n_new)

    operators = _s5_prep(ssm_lam_re[0], ssm_lam_im[0], ssm_log_step[0],
                         ssm_b_re[0], ssm_b_im[0], ssm_c_re[0], ssm_c_im[0])
    pack = lambda st: st.reshape(st.shape[0], -1, S5_RI).transpose(1, 0, 2)
    unpack = lambda st: st.transpose(1, 0, 2).reshape(st.shape[1], -1, S5_STATE)[None]
    y5_p, y5_s, finr_p, fini_p, finr_s, fini_s = _s5_scan(
        u_p, u_s, pack(state_ssm_re[0]), pack(state_ssm_im[0]), *operators, bsz)

    tail1 = (perm.T, g_mix1, row(ssm_d[0]), w_glu0, g_ffn1,
             w_f_in, w_f_out, 1, row(norm_final))
    yp = _s5_tail(yp, y5_p, *tail1, tm_p, S5_CHUNK)
    ys = _s5_tail(ys, y5_s, *tail1, tm_s, n_new)

    return (yp.reshape(bsz, seq, d), ys.reshape(dbsz, n_new, d),
            ret_p[None], ret_s[None],
            swk_p[None], swv_p[None],
            rows_first(swk_s), rows_first(swv_s),
            unpack(finr_p[:, :bsz]), unpack(fini_p[:, :bsz]), unpack(finr_s), unpack(fini_s))
```

```python
import functools

import jax
import jax.numpy as jnp
import numpy as np
from jax import lax
from jax.experimental import pallas as pl
from jax.experimental.pallas import tpu as pltpu

F32 = jnp.float32
BF16 = jnp.bfloat16

H_A, DK_A, DV_A = 4, 64, 128
H_B, DH_B = 8, 64
SWA_PAIRS = ((128, 1), (512, 4), (2048, 16))
SPAN = 128
SWA_BUF = 2048
PAST_LEN = 16384
RET_CHUNK = 128
S5_GROUP, S5_STATE = 16, 64
S5_CHUNK = 16
EPS = 1e-6
NEG_INF = -1e30
QA_W, KA_W, VA_W, GA_W = H_A * DK_A, H_A * DK_A, H_A * DV_A, H_A * DV_A
RA_W = QA_W + KA_W + VA_W + GA_W
QKV_W = 3 * H_B * DH_B

LANES = 128
SUBLANES = 8
VMEM_LIMIT = 58 * 1024 * 1024

PROJ_ROWS = 1024
PROJ_COLS = 512
TAIL_ROWS = 512
RET_STEP_CHUNKS = 4
RET_SAMPLE_SEQS = 4


def _params(sem):
    return pltpu.CompilerParams(dimension_semantics=sem, vmem_limit_bytes=VMEM_LIMIT)


def _const_spec(shape):
    nd = len(shape)
    return pl.BlockSpec(shape, lambda *_: (0,) * nd, pipeline_mode=pl.Buffered(1))


def _rms(x, g):
    return x * lax.rsqrt(jnp.mean(x * x, axis=-1, keepdims=True) + EPS) * g


def _dot(a, b):
    return jnp.dot(a, b, preferred_element_type=F32)


def _dot_nt(a, b):
    return lax.dot_general(a, b, (((1,), (1,)), ((), ())), preferred_element_type=F32)


def _dot_tn(a, b):
    return lax.dot_general(a, b, (((0,), (0,)), ((), ())), preferred_element_type=F32)


def _ab_proj_kernel(x_ref, g_ref, w_ref, ra_ref, qkv_ref):
    h = _rms(x_ref[...], g_ref[...]).astype(BF16)
    for n0 in range(0, RA_W, PROJ_COLS):
        ra_ref[:, n0:n0 + PROJ_COLS] = _dot(h, w_ref[:, n0:n0 + PROJ_COLS])
    for n0 in range(0, QKV_W, PROJ_COLS):
        qkv_ref[:, n0:n0 + PROJ_COLS] = _dot(h, w_ref[:, RA_W + n0:RA_W + n0 + PROJ_COLS])


def _ab_proj(x, g, w_bf16, tm):
    t, d = x.shape
    return pl.pallas_call(
        _ab_proj_kernel,
        grid=(t // tm,),
        in_specs=[pl.BlockSpec((tm, d), lambda i: (i, 0)),
                  _const_spec((1, d)),
                  _const_spec((d, RA_W + QKV_W))],
        out_specs=[pl.BlockSpec((tm, RA_W), lambda i: (i, 0)),
                   pl.BlockSpec((tm, QKV_W), lambda i: (i, 0))],
        out_shape=[jax.ShapeDtypeStruct((t, RA_W), F32), jax.ShapeDtypeStruct((t, QKV_W), F32)],
        compiler_params=_params(("arbitrary",)),
        name="ab_proj",
    )(x, g, w_bf16)


def _retention_tables(c_real):
    c = RET_CHUNK
    log_g = np.log1p(-np.exp2(-5.0 - np.arange(H_A, dtype=F32)))
    idx = np.arange(c, dtype=F32)
    rel = idx[:, None] - idx[None, :]
    dec = np.where(rel >= 0, np.exp(np.maximum(rel, 0.0)[None] * log_g[:, None, None]), 0.0)
    dec = dec.reshape(H_A * c, c)
    real = (idx < c_real)[:, None]
    qd = np.exp((idx + 1.0)[:, None] * log_g[None, :])
    qd = np.repeat(qd, DV_A, axis=1)
    kd = np.where(real, np.exp((c_real - 1.0 - idx)[:, None] * log_g[None, :]), 0.0)
    kd = np.repeat(kd, DK_A, axis=1)
    row_h = np.arange(H_A * DK_A)[:, None] // DK_A
    col_h = np.arange(H_A * DV_A)[None, :] // DV_A
    bd = (row_h == col_h).astype(F32)
    dm = bd * np.repeat(np.exp(c_real * log_g), DV_A)[None, :]
    return tuple(np.asarray(tb, F32) for tb in (dec, qd, kd, dm))


def _retention_kernel(ra_ref, s0_ref, gain_ref, dec_ref, qd_ref, kd_ref, dm_ref,
                      a_ref, sout_ref, sbd_ref, *, c_real, n_chunks, n_seq):
    c = RET_CHUNK
    j = pl.program_id(1)

    @pl.when(j == 0)
    def _():
        sbd_ref[...] = jnp.zeros_like(sbd_ref)
        for s in range(n_seq):
            for h in range(H_A):
                sbd_ref[s, h * DK_A:(h + 1) * DK_A, h * DV_A:(h + 1) * DV_A] = s0_ref[s, h]

    lane_q = lax.broadcasted_iota(jnp.int32, (c, QA_W), 1) // DK_A
    for ci in range(n_chunks):
        for s in range(n_seq):
            _retention_chunk(ra_ref.at[s], a_ref.at[s], sbd_ref.at[s], gain_ref, dec_ref, qd_ref, kd_ref, dm_ref,
                             lane_q, ci, c_real)

    @pl.when(j == pl.num_programs(1) - 1)
    def _():
        for s in range(n_seq):
            for h in range(H_A):
                sout_ref[s, h] = sbd_ref[s, h * DK_A:(h + 1) * DK_A, h * DV_A:(h + 1) * DV_A]


def _retention_chunk(ra_ref, a_ref, sbd_ref, gain_ref, dec_ref, qd_ref, kd_ref, dm_ref, lane_q, ci, c_real):
    c = RET_CHUNK
    tok = slice(ci * c_real, (ci + 1) * c_real)

    def slab(c0, width):
        x = ra_ref[tok, c0:c0 + width]
        return x if c_real == c else jnp.concatenate([x, jnp.zeros((c - c_real, width), F32)], axis=0)

    q = slab(0, QA_W)
    k = slab(QA_W, KA_W) * (DK_A ** -0.5)
    vb = slab(QA_W + KA_W, VA_W).astype(BF16)
    qm = jnp.concatenate([jnp.where(lane_q == h, q, 0.0) for h in range(H_A)], axis=0).astype(BF16)
    s = _dot_nt(qm, k.astype(BF16)) * dec_ref[...]
    sb = s.astype(BF16)
    sbd = sbd_ref[...]
    cross = _dot(q.astype(BF16), sbd.astype(BF16)) * qd_ref[...]
    upd = _dot_tn((k * kd_ref[...]).astype(BF16), vb)
    for h in range(H_A):
        blk = (slice(h * DK_A, (h + 1) * DK_A), slice(h * DV_A, (h + 1) * DV_A))
        sbd_ref[blk] = sbd[blk] * dm_ref[blk] + upd[blk]
    for h in range(H_A):
        sl = slice(h * DV_A, (h + 1) * DV_A)
        o = _dot(sb[h * c:(h + 1) * c], vb[:, sl]) + cross[:, sl]
        o = o[:c_real]
        mu = jnp.mean(o, axis=-1, keepdims=True)
        var = jnp.mean(jnp.square(o - mu), axis=-1, keepdims=True)
        y = (o - mu) * lax.rsqrt(var + EPS) * gain_ref[:, sl]
        gh = ra_ref[tok, QA_W + KA_W + VA_W + h * DV_A:QA_W + KA_W + VA_W + (h + 1) * DV_A]
        a_ref[tok, sl] = (gh * jax.nn.sigmoid(gh) * y).astype(a_ref.dtype)


def _retention(ra, state0, gain, *, seq, c_real, n_chunks, n_seq, out_dtype):
    t = ra.shape[0]
    bsz = t // seq
    rows = c_real * n_chunks
    steps = seq // rows
    tables = _retention_tables(c_real)
    kern = functools.partial(_retention_kernel, c_real=c_real, n_chunks=n_chunks, n_seq=n_seq)
    a, s_out = pl.pallas_call(
        kern,
        grid=(bsz // n_seq, steps),
        in_specs=[pl.BlockSpec((n_seq, rows, RA_W), lambda b, j: (b, j, 0)),
                  pl.BlockSpec((n_seq, H_A, DK_A, DV_A), lambda b, j: (b, 0, 0, 0)),
                  _const_spec((1, VA_W))] + [_const_spec(tb.shape) for tb in tables],
        out_specs=[pl.BlockSpec((n_seq, rows, VA_W), lambda b, j: (b, j, 0)),
                   pl.BlockSpec((n_seq, H_A, DK_A, DV_A), lambda b, j: (b, 0, 0, 0))],
        out_shape=[jax.ShapeDtypeStruct((bsz, seq, VA_W), out_dtype),
                   jax.ShapeDtypeStruct((bsz, H_A, DK_A, DV_A), F32)],
        scratch_shapes=[pltpu.VMEM((n_seq, H_A * DK_A, H_A * DV_A), F32)],
        compiler_params=_params(("arbitrary", "arbitrary")),
        name="retention",
    )(ra.reshape(bsz, seq, RA_W), state0, gain, *tables)
    return a.reshape(t, VA_W), s_out


SWA_ROWS = 2048
N_BRANCH = len(SWA_PAIRS)


def _alibi_slopes():
    return np.exp2(-8.0 * np.arange(1, H_B + 1, dtype=F32) / H_B)


def _swa_prompt_bias():
    qi = np.arange(SPAN)[:, None]
    kj = np.arange(2 * SPAN)[None, :]
    dist = SPAN + qi - kj
    band = (dist >= 0) & (dist <= SPAN)
    slopes = _alibi_slopes()
    out = []
    for (_, dil) in SWA_PAIRS:
        pen = -slopes[:, None, None] * (dil * dist).astype(F32)[None]
        normal = np.where(band[None], pen, NEG_INF)
        first = np.where((band & (kj >= SPAN))[None], pen, NEG_INF)
        out.append(np.stack([normal, first], axis=1))
    tab = np.stack(out, axis=1)
    tab = tab.reshape(H_B // 2, 2, N_BRANCH, 2, SPAN, 2 * SPAN).transpose(0, 2, 3, 1, 4, 5)
    return np.asarray(tab.reshape(H_B // 2, N_BRANCH, 2, 2 * SPAN, 2 * SPAN), F32)


COARSE = 4


def _swa_unit(q, kk, vv, bias, head0):
    q = q * (DH_B ** -0.5)
    qm = jnp.concatenate([jnp.where(head0, q, 0.0), jnp.where(head0, 0.0, q)], axis=0).astype(BF16)
    s = _dot_nt(qm, kk.astype(BF16)) + bias
    m = jnp.max(s, axis=-1, keepdims=True)
    p = jnp.exp(s - m).astype(BF16)
    ones = jnp.ones((2 * SPAN, LANES), BF16)
    res = _dot(p, jnp.concatenate([vv.astype(BF16), ones], axis=1))
    acc = jnp.where(head0, res[:SPAN, :LANES], res[SPAN:, :LANES])
    den = jnp.where(head0, res[:SPAN, LANES:], res[SPAN:, LANES:])
    mm = jnp.where(head0, jnp.broadcast_to(m[:SPAN], (SPAN, LANES)), jnp.broadcast_to(m[SPAN:], (SPAN, LANES)))
    return acc, den, mm


SWA_PREV = SPAN * COARSE


def _swa_prompt_half(half, q_ref, k_ref, v_ref, bias_ref, o_ref, kbuf, vbuf, acc_ref, l_ref, m_ref,
                     q4, k4, v4, acc4, l4, m4):
    j = pl.program_id(2)
    rows = SWA_ROWS
    sub = rows // COARSE
    slot = j % 2

    @pl.when((half == 0) & (j > 0))
    def _():
        kbuf[0:SWA_PREV, :] = kbuf[rows:rows + SWA_PREV, :]
        vbuf[0:SWA_PREV, :] = vbuf[rows:rows + SWA_PREV, :]

    @pl.when((half == 0) & (j == 0))
    def _():
        k4[1] = jnp.zeros(k4.shape[1:], F32)
        v4[1] = jnp.zeros(v4.shape[1:], F32)
        kbuf[0:SWA_PREV, :] = jnp.zeros((SWA_PREV, LANES), F32)
        vbuf[0:SWA_PREV, :] = jnp.zeros((SWA_PREV, LANES), F32)

    @pl.when(half == 0)
    def _():
        kbuf[SWA_PREV:, :] = k_ref[...]
        vbuf[SWA_PREV:, :] = v_ref[...]
        for c in range(COARSE):
            q4[c] = q_ref[pl.ds(c, sub, stride=COARSE), :]
            k4[slot, c] = k_ref[pl.ds(c, sub, stride=COARSE), :]
            v4[slot, c] = v_ref[pl.ds(c, sub, stride=COARSE), :]

    lane = lax.broadcasted_iota(jnp.int32, (SPAN, LANES), 1)
    head0 = lane < DH_B
    first_step = (j == 0).astype(jnp.int32)

    for g, (_, dil) in enumerate(SWA_PAIRS):
        blocks = rows // (SPAN * dil)
        per_half = blocks * dil // 2

        def unit(u, g=g, dil=dil):
            wb = u // dil
            r = u % dil
            q_start = wb * (SPAN * dil) + r
            k_start = SWA_PREV + (wb - 1) * (SPAN * dil) + r
            variant = jnp.where(wb == 0, first_step, 0)
            acc, den, mm = _swa_unit(q_ref[pl.ds(q_start, SPAN, stride=dil), :],
                                     kbuf[pl.ds(k_start, 2 * SPAN, stride=dil), :],
                                     vbuf[pl.ds(k_start, 2 * SPAN, stride=dil), :],
                                     bias_ref[0, g, variant], head0)
            acc_ref[g, pl.ds(q_start, SPAN, stride=dil), :] = acc
            l_ref[g, pl.ds(q_start, SPAN, stride=dil), :] = den
            m_ref[g, pl.ds(q_start, SPAN, stride=dil), :] = mm

        def unit_two_level(u, g=g, fine=dil // COARSE):
            c = u % COARSE
            f = u // COARSE
            pick = pl.ds(f, SPAN, stride=fine)
            acc, den, mm = _swa_unit(q4[c, pick, :],
                                     jnp.concatenate([k4[1 - slot, c, pick, :], k4[slot, c, pick, :]], axis=0),
                                     jnp.concatenate([v4[1 - slot, c, pick, :], v4[slot, c, pick, :]], axis=0),
                                     bias_ref[0, g, first_step], head0)
            dst = pl.ds(c * sub + f, SPAN, stride=fine)
            acc4[dst, :] = acc
            l4[dst, :] = den
            m4[dst, :] = mm

        two_level = dil % (COARSE * COARSE) == 0
        assert not two_level or blocks == 1
        body = unit_two_level if two_level else unit

        @pl.when(half >= 0)
        def _(body=body, per_half=per_half):
            for i in range(per_half):
                body(half * per_half + i)

        if two_level:
            @pl.when(half == 1)
            def _(g=g):
                for c in range(COARSE):
                    src, dst = slice(c * sub, (c + 1) * sub), pl.ds(c, sub, stride=COARSE)
                    acc_ref[g, dst, :] = acc4[src, :]
                    l_ref[g, dst, :] = l4[src, :]
                    m_ref[g, dst, :] = m4[src, :]

    @pl.when(half == 1)
    def _():
        tile = 256
        for r0 in range(0, rows, tile):
            sl = slice(r0, r0 + tile)
            ms = [m_ref[g, sl, :] for g in range(N_BRANCH)]
            mx = functools.reduce(jnp.maximum, ms)
            ws = [jnp.exp(mg - mx) for mg in ms]
            num = sum(w * acc_ref[g, sl, :] for g, w in enumerate(ws))
            den = sum(w * l_ref[g, sl, :] for g, w in enumerate(ws))
            o_ref[sl, :] = (num / den).astype(o_ref.dtype)


def _swa_sample_bias(n_new):
    t = np.arange(n_new)[:, None]
    slopes = _alibi_slopes()

    def table(j):
        dist = SWA_BUF + t - j
        out = []
        for (window, dil) in SWA_PAIRS:
            valid = (dist >= 0) & (dist <= window) & (dist % dil == 0) & (PAST_LEN + t - dist >= 0)
            pen = -slopes[:, None, None] * dist.astype(F32)[None]
            out.append(np.where(valid[None], pen, NEG_INF).reshape(H_B * n_new, -1))
        return np.stack(out)

    bias_a = table(np.arange(SWA_BUF)[None, :])
    jb = np.arange(LANES)[None, :]
    bias_b = np.where(jb < n_new, table(SWA_BUF + jb), NEG_INF)
    return np.asarray(bias_a, F32), np.asarray(bias_b, F32)


def _pad_new_rows(new_ref, n_new):
    return jnp.concatenate([new_ref[...], jnp.zeros((LANES - n_new, new_ref.shape[1]), F32)], axis=0)


def _swa_sample_attend(q_ref, k_ref, v_ref, kp_ref, vp_ref, ba_ref, bb_ref, o_ref, n_new):
    width = q_ref.shape[1]
    n_heads = width // DH_B
    lane_h = lax.broadcasted_iota(jnp.int32, (n_new, width), 1) // DH_B
    q = q_ref[...] * (DH_B ** -0.5)
    qm = jnp.concatenate([jnp.where(lane_h == h, q, 0.0) for h in range(n_heads)], axis=0).astype(BF16)
    k_new, v_new = _pad_new_rows(k_ref, n_new), _pad_new_rows(v_ref, n_new)
    s_a = _dot(qm, kp_ref[0].astype(BF16))
    s_b = _dot_nt(qm, k_new.astype(BF16))
    sa = [s_a + ba_ref[g] for g in range(N_BRANCH)]
    sb = [s_b + bb_ref[g] for g in range(N_BRANCH)]
    mx = functools.reduce(jnp.maximum, [jnp.max(x, axis=-1, keepdims=True) for x in sa + sb])
    p_a = sum(jnp.exp(x - mx) for x in sa).astype(BF16)
    p_b = sum(jnp.exp(x - mx) for x in sb).astype(BF16)
    den = (jnp.sum(p_a.astype(F32), axis=-1, keepdims=True)
           + jnp.sum(p_b.astype(F32), axis=-1, keepdims=True))
    o = (_dot_nt(p_a, vp_ref[0].astype(BF16)) + _dot(p_b, v_new.astype(BF16))) / den
    o_ref[...] = sum(jnp.where(lane_h == h, o[h * n_new:(h + 1) * n_new], 0.0) for h in range(n_heads))

def _swa_window_update(src_ref, new_ref, dst_ref, n_new):
    lane = lax.broadcasted_iota(jnp.int32, (DH_B, LANES), 1)
    new_t = pltpu.roll(_pad_new_rows(new_ref, n_new).T, LANES - n_new, axis=1)
    for h in range(new_ref.shape[1] // DH_B):
        rows = slice(h * DH_B, (h + 1) * DH_B)
        shifted = pltpu.roll(src_ref[0, rows, :], SWA_BUF - n_new, axis=1)
        dst_ref[0, rows, 0:SWA_BUF - LANES] = shifted[:, 0:SWA_BUF - LANES]
        dst_ref[0, rows, SWA_BUF - LANES:] = jnp.where(lane >= LANES - n_new, new_t[rows],
                                                       shifted[:, SWA_BUF - LANES:])


N_SWA_IN, N_SWA_OUT = 11, 4


def _swa_kernel(*refs, n_new, n_cast):
    (q_ref, k_ref, v_ref, bias_ref, qs_ref, ks_ref, vs_ref, kp_ref, vp_ref, ba_ref, bb_ref), refs = (
        refs[:N_SWA_IN], refs[N_SWA_IN:])
    cast_src, refs = refs[:n_cast], refs[n_cast:]
    (o_ref, os_ref, ko_ref, vo_ref), refs = refs[:N_SWA_OUT], refs[N_SWA_OUT:]
    cast_dst, scratch = refs[:n_cast], refs[n_cast:]
    half = pl.program_id(3)
    _swa_prompt_half(half, q_ref, k_ref, v_ref, bias_ref, o_ref, *scratch)
    _swa_sample_attend(qs_ref, ks_ref, vs_ref, kp_ref, vp_ref, ba_ref, bb_ref, os_ref, n_new)
    _swa_window_update(kp_ref, ks_ref, ko_ref, n_new)
    _swa_window_update(vp_ref, vs_ref, vo_ref, n_new)

    @pl.when(half == 0)
    def _():
        for src, dst in zip(cast_src, cast_dst):
            dst[...] = src[...].astype(dst.dtype)


def _swa(qkv, bsz, seq, qkv_s, k_past, v_past, n_new, weights):
    t = qkv.shape[0]
    dbsz = k_past.shape[0]
    steps = seq // SWA_ROWS
    npair = H_B // 2
    width = H_B * DH_B
    half_w = width // 2
    assert dbsz == bsz * npair * steps, "one sample sequence per prompt (sequence, head pair, row block) step"
    bias = _swa_prompt_bias()
    bias_a, bias_b = _swa_sample_bias(n_new)
    blk = (SWA_ROWS, LANES)
    rows_of = lambda b, j: b * steps + j
    seq_of = lambda b, hp, j: (b * npair + hp) * steps + j
    new_blk = lambda col: pl.BlockSpec((n_new, half_w), lambda b, hp, j, h, col=col: (seq_of(b, hp, j), 2 * col + h))
    state_blk = pl.BlockSpec((1, half_w, SWA_BUF), lambda b, hp, j, h: (seq_of(b, hp, j), h, 0))
    half_rows = H_B // 2 * n_new
    sub = SWA_ROWS // COARSE
    assert all(w.shape[0] % (dbsz * 2 * SUBLANES) == 0 for w in weights), "bf16 row blocks are 16-row tiles"
    cast_blk = [pl.BlockSpec((w.shape[0] // dbsz, w.shape[1]), lambda b, hp, j, h: (seq_of(b, hp, j), 0))
                for w in weights]
    in_specs = [pl.BlockSpec(blk, lambda b, hp, j, h: (rows_of(b, j), hp)),
                pl.BlockSpec(blk, lambda b, hp, j, h: (rows_of(b, j), npair + hp)),
                pl.BlockSpec(blk, lambda b, hp, j, h: (rows_of(b, j), 2 * npair + hp)),
                pl.BlockSpec((1, N_BRANCH, 2, 2 * SPAN, 2 * SPAN), lambda b, hp, j, h: (hp, 0, 0, 0, 0)),
                new_blk(0), new_blk(1), new_blk(2), state_blk, state_blk,
                pl.BlockSpec((N_BRANCH, half_rows, SWA_BUF), lambda b, hp, j, h: (0, h, 0)),
                pl.BlockSpec((N_BRANCH, half_rows, LANES), lambda b, hp, j, h: (0, h, 0))]
    assert len(in_specs) == N_SWA_IN
    return pl.pallas_call(
        functools.partial(_swa_kernel, n_new=n_new, n_cast=len(weights)),
        grid=(bsz, npair, steps, 2),
        in_specs=in_specs + cast_blk,
        out_specs=[pl.BlockSpec(blk, lambda b, hp, j, h: (rows_of(b, j), hp)),
                   pl.BlockSpec((n_new, half_w), lambda b, hp, j, h: (seq_of(b, hp, j), h)),
                   state_blk, state_blk] + cast_blk,
        out_shape=[jax.ShapeDtypeStruct((t, width), BF16),
                   jax.ShapeDtypeStruct((dbsz * n_new, width), F32),
                   jax.ShapeDtypeStruct(k_past.shape, F32), jax.ShapeDtypeStruct(v_past.shape, F32)]
                  + [jax.ShapeDtypeStruct(w.shape, BF16) for w in weights],
        scratch_shapes=[pltpu.VMEM((SWA_PREV + SWA_ROWS, LANES), F32)] * 2
                       + [pltpu.VMEM((N_BRANCH, SWA_ROWS, LANES), F32)] * 3
                       + [pltpu.VMEM((COARSE, sub, LANES), F32)]
                       + [pltpu.VMEM((2, COARSE, sub, LANES), F32)] * 2
                       + [pltpu.VMEM((SWA_ROWS, LANES), F32)] * 3,
        compiler_params=_params(("arbitrary",) * 4),
        name="swa",
    )(qkv, qkv, qkv, bias, qkv_s, qkv_s, qkv_s, k_past, v_past, bias_a, bias_b, *weights)


FFN_CHUNK = 256


def _ffn(y1, g_ffn_ref, w_in_ref, w_o_ref, acc_ref):
    d_ff = w_o_ref.shape[0]
    h = _rms(y1, g_ffn_ref[...]).astype(BF16)
    for ci, c0 in enumerate(range(0, d_ff, FFN_CHUNK)):
        gate = _dot(h, w_in_ref[:, c0:c0 + FFN_CHUNK])
        up = _dot(h, w_in_ref[:, d_ff + c0:d_ff + c0 + FFN_CHUNK])
        act = (gate * jax.nn.sigmoid(gate) * up).astype(BF16)
        part = _dot(act, w_o_ref[c0:c0 + FFN_CHUNK, :])
        if ci == 0:
            acc_ref[...] = part
        else:
            acc_ref[...] += part
    return y1 + acc_ref[...]


def _ab_tail_kernel(x_ref, a_ref, ob_ref, w_out_ref, g_ffn_ref, w_in_ref, w_o_ref, g_next_ref, perm_ref,
                    o_ref, u_ref, acc_ref, h_ref, z_ref, *, chunk):
    mix = (_dot(a_ref[...].astype(BF16), w_out_ref[0:VA_W, :])
           + _dot(ob_ref[...].astype(BF16), w_out_ref[VA_W:, :]))
    y = _ffn(x_ref[...] + mix, g_ffn_ref, w_in_ref, w_o_ref, acc_ref)
    o_ref[...] = y
    _group_rows(_rms(y, g_next_ref[...]), perm_ref, u_ref, h_ref, z_ref, chunk)


def _s5_tail_kernel(x_ref, yg_ref, perm_ref, g_mix_ref, dskip_ref, w_glu_ref, g_ffn_ref, w_in_ref, w_o_ref,
                    g_fin_ref, o_ref, acc_ref, z_ref, t_ref, *, chunk):
    x = x_ref[...]
    d = x.shape[-1]
    u = _rms(x, g_mix_ref[...])
    _ungroup_rows(yg_ref, perm_ref, z_ref, t_ref, chunk)
    ys = jnp.concatenate([t_ref[v] for v in range(d // LANES)], axis=1)
    z = jax.nn.gelu(ys + dskip_ref[...] * u, approximate=True).astype(BF16)
    val = _dot(z, w_glu_ref[:, 0:d])
    gate = _dot(z, w_glu_ref[:, d:2 * d])
    y2 = _ffn(x + val * jax.nn.sigmoid(gate), g_ffn_ref, w_in_ref, w_o_ref, acc_ref)
    o_ref[...] = _rms(y2, g_fin_ref[...])


def _row_spec(tm, width):
    return pl.BlockSpec((tm, width), lambda i: (i, 0))


def _group_scratch(tm, d, chunk):
    n_slab, nc, n_col = d // LANES, tm // chunk, chunk // SLAB_GROUPS
    assert min(S5_MM_ROWS, n_slab * n_col * nc) % nc == 0
    return (pltpu.VMEM((n_slab, tm, LANES), F32), pltpu.VMEM((n_slab * n_col * nc, SLAB_GROUPS * LANES), BF16))


def _group_spec(tm, d, chunk):
    return pl.BlockSpec((d // S5_GROUP, tm // chunk, chunk * S5_GROUP), lambda i: (0, i, 0))


def _layer_spec(stacked, layer):
    nd = stacked.ndim - 1
    return pl.BlockSpec((None,) + stacked.shape[1:], lambda *_: (layer,) + (0,) * nd, pipeline_mode=pl.Buffered(1))


def _ab_tail(x, a, ob, w_out, g_ffn, w_in, w_o, layer, g_next, perm, tm, chunk):
    t, d = x.shape
    h_scr, z_scr = _group_scratch(tm, d, chunk)
    return pl.pallas_call(
        functools.partial(_ab_tail_kernel, chunk=chunk),
        grid=(t // tm,),
        in_specs=[_row_spec(tm, d), _row_spec(tm, a.shape[1]), _row_spec(tm, ob.shape[1]),
                  _const_spec(w_out.shape), _const_spec((1, d)), _layer_spec(w_in, layer), _layer_spec(w_o, layer),
                  _const_spec((1, d)), _const_spec(perm.shape)],
        out_specs=[_row_spec(tm, d), _group_spec(tm, d, chunk)],
        out_shape=[jax.ShapeDtypeStruct((t, d), F32),
                   jax.ShapeDtypeStruct((d // S5_GROUP, t // chunk, chunk * S5_GROUP), BF16)],
        scratch_shapes=[pltpu.VMEM((tm, d), F32), h_scr, z_scr],
        compiler_params=_params(("arbitrary",)),
        name="ab_tail",
    )(x, a, ob, w_out, g_ffn, w_in, w_o, g_next, perm)


def _s5_tail(x, yg, perm_t, g_mix, dskip, w_glu, g_ffn, w_in, w_o, layer, g_fin, tm, chunk):
    t, d = x.shape
    t_scr, z_scr = _group_scratch(tm, d, chunk)
    return pl.pallas_call(
        functools.partial(_s5_tail_kernel, chunk=chunk),
        grid=(t // tm,),
        in_specs=[_row_spec(tm, d), _group_spec(tm, d, chunk), _const_spec(perm_t.shape),
                  _const_spec((1, d)), _const_spec((1, d)),
                  _const_spec(w_glu.shape), _const_spec((1, d)), _layer_spec(w_in, layer), _layer_spec(w_o, layer),
                  _const_spec((1, d))],
        out_specs=_row_spec(tm, d),
        out_shape=jax.ShapeDtypeStruct((t, d), F32),
        scratch_shapes=[pltpu.VMEM((tm, d), F32), z_scr, t_scr],
        compiler_params=_params(("arbitrary",)),
        name="s5_tail",
    )(x, yg, perm_t, g_mix, dskip, w_glu, g_ffn, w_in, w_o, g_fin)


SLAB_GROUPS = LANES // S5_GROUP
S5_MM_ROWS = 256


def _slab_perm():
    idx = np.arange(SLAB_GROUPS * LANES)
    l8, g8, p = idx // LANES, (idx % LANES) // S5_GROUP, idx % S5_GROUP
    dst = g8 * LANES + l8 * S5_GROUP + p
    return (dst[:, None] == idx[None, :]).astype(BF16)


def _permute_rows(z_ref, perm_ref, emit):
    step = min(S5_MM_ROWS, z_ref.shape[0])
    for r0 in range(0, z_ref.shape[0], step):
        emit(r0, _dot(z_ref[r0:r0 + step, :], perm_ref[...]))


def _group_rows(h, perm_ref, o_ref, h_ref, z_ref, chunk):
    rows = h.shape[0]
    nc = rows // chunk
    n_slab = h.shape[1] // LANES
    n_col = chunk // SLAB_GROUPS
    for v in range(n_slab):
        h_ref[v] = h[:, v * LANES:(v + 1) * LANES]
    for v in range(n_slab):
        for l in range(chunk):
            j, l8 = l // SLAB_GROUPS, l % SLAB_GROUPS
            r0 = (v * n_col + j) * nc
            z_ref[r0:r0 + nc, l8 * LANES:(l8 + 1) * LANES] = h_ref[v, pl.ds(l, nc, stride=chunk), :].astype(BF16)

    def emit(r0, blk):
        for q in range(blk.shape[0] // nc):
            v, j = divmod(r0 // nc + q, n_col)
            for g8 in range(SLAB_GROUPS):
                o_ref[v * SLAB_GROUPS + g8, :, j * LANES:(j + 1) * LANES] = (
                    blk[q * nc:(q + 1) * nc, g8 * LANES:(g8 + 1) * LANES].astype(o_ref.dtype))

    _permute_rows(z_ref, perm_ref, emit)


def _ungroup_rows(y_ref, perm_ref, z_ref, t_ref, chunk):
    n_slab, rows, _ = t_ref.shape
    nc = rows // chunk
    n_col = chunk // SLAB_GROUPS
    for v in range(n_slab):
        for j in range(n_col):
            r0 = (v * n_col + j) * nc
            for g8 in range(SLAB_GROUPS):
                z_ref[r0:r0 + nc, g8 * LANES:(g8 + 1) * LANES] = y_ref[v * SLAB_GROUPS + g8, :, j * LANES:(j + 1) * LANES]

    def emit(r0, blk):
        for q in range(blk.shape[0] // nc):
            v, j = divmod(r0 // nc + q, n_col)
            for l8 in range(SLAB_GROUPS):
                t_ref[v, pl.ds(j * SLAB_GROUPS + l8, nc, stride=chunk), :] = (
                    blk[q * nc:(q + 1) * nc, l8 * LANES:(l8 + 1) * LANES])

    _permute_rows(z_ref, perm_ref, emit)


S5_LP = S5_CHUNK * S5_GROUP
S5_RI = 2 * S5_STATE


S5_PREP_GROUPS = 8


def _s5_prep_kernel(*refs):
    for k in range(refs[0].shape[0]):
        _s5_prep_group(k % 2, *[r.at[k] for r in refs])


def _s5_prep_group(parity, lam_row_ref, ls_ref, btr_ref, bti_ref, cr_ref, ci_ref,
                   bs_ref, cs_ref, tp_ref, ap_ref):
    n_pow = S5_CHUNK + 1
    dt = jnp.exp(ls_ref[...])

    def powers(lr, li):
        mag = jnp.exp(lr * dt)
        a_re, a_im = mag * jnp.cos(li * dt), mag * jnp.sin(li * dt)
        pw = [(jnp.ones_like(a_re), jnp.zeros_like(a_im))]
        for _ in range(n_pow - 1):
            pr, pi = pw[-1]
            pw.append((pr * a_re - pi * a_im, pr * a_im + pi * a_re))
        return pw

    lr, li = lam_row_ref[0:1, :], lam_row_ref[1:2, :]
    pw_row = powers(lr, li)
    a_re, a_im = pw_row[1]
    den = lr * lr + li * li
    f_re = ((a_re - 1.0) * lr + a_im * li) / den
    f_im = (a_im * lr - (a_re - 1.0) * li) / den
    blk = (S5_GROUP, S5_RI)
    w_re = jnp.concatenate([jnp.broadcast_to(f_re * pw_row[S5_CHUNK - 1 - l][0] - f_im * pw_row[S5_CHUNK - 1 - l][1], blk)
                            for l in range(S5_CHUNK)], axis=0)
    w_im = jnp.concatenate([jnp.broadcast_to(f_re * pw_row[S5_CHUNK - 1 - l][1] + f_im * pw_row[S5_CHUNK - 1 - l][0], blk)
                            for l in range(S5_CHUNK)], axis=0)

    lane = lax.broadcasted_iota(jnp.int32, (1, S5_RI), 1)
    own = (lane // S5_STATE) == parity
    low = lane < S5_STATE
    keep = lambda z: jnp.where(own, z, 0.0)
    per_pos = lambda ref: jnp.concatenate([ref[...]] * S5_CHUNK, axis=0)
    btr, bti = per_pos(btr_ref), per_pos(bti_ref)
    bs_ref[:, :S5_RI] = keep(w_re * btr - w_im * bti).astype(bs_ref.dtype)
    bs_ref[:, S5_RI:] = keep(w_re * bti + w_im * btr).astype(bs_ref.dtype)
    bf_t = jnp.where(low, f_re * btr[0:S5_GROUP] - f_im * bti[0:S5_GROUP],
                     f_re * bti[0:S5_GROUP] + f_im * btr[0:S5_GROUP])
    ap_ref[...] = jnp.zeros_like(ap_ref)
    for i, (j, part) in enumerate(((S5_CHUNK, 0), (S5_CHUNK, 1), (S5_CHUNK // 2, 0), (S5_CHUNK // 2, 1))):
        ap_ref[i:i + 1, :] = keep(pw_row[j][part])

    def spread(j0, part):
        return jnp.concatenate([jnp.broadcast_to(pw_row[j0 + l][part], blk) for l in range(S5_CHUNK)], axis=0)

    cr, ci = per_pos(cr_ref), per_pos(ci_ref)
    cs0 = jnp.where(low, cr * spread(0, 0) - ci * spread(0, 1),
                    -(cr * spread(0, 1) + ci * spread(0, 0)))
    cs_ref[:, :S5_RI] = keep(cr * spread(1, 0) - ci * spread(1, 1)).astype(cs_ref.dtype)
    cs_ref[:, S5_RI:] = keep(-(cr * spread(1, 1) + ci * spread(1, 0))).astype(cs_ref.dtype)
    r = lax.dot_general(bf_t, cs0, (((1,), (1,)), ((), ())), preferred_element_type=F32,
                        precision=lax.Precision.HIGHEST)
    lane_lp = lax.broadcasted_iota(jnp.int32, (S5_GROUP, S5_LP), 1)
    for l in range(S5_CHUNK):
        sh = l * S5_GROUP
        blk_l = r if l == 0 else jnp.where(lane_lp >= sh, pltpu.roll(r, sh, axis=1), 0.0)
        tp_ref[l * S5_GROUP:(l + 1) * S5_GROUP, :] = blk_l.astype(tp_ref.dtype)


def _s5_prep(lam_re, lam_im, log_step, b_re, b_im, c_re, c_im):
    g = lam_re.shape[0]
    dup = lambda z: jnp.concatenate([z, z], axis=-1)
    lam_row = jnp.stack([dup(lam_re), dup(lam_im)], axis=1)
    lam_row = jnp.pad(lam_row, ((0, 0), (0, 6), (0, 0)))
    ls = log_step.reshape(g, 1, 1)
    btr, bti = dup(jnp.swapaxes(b_re, 1, 2)), dup(jnp.swapaxes(b_im, 1, 2))
    cr, ci = dup(c_re), dup(c_im)
    gspec = lambda *s: pl.BlockSpec((S5_PREP_GROUPS,) + s, lambda i: (i,) + (0,) * len(s))
    table = jax.ShapeDtypeStruct((g, S5_LP, 2 * S5_RI), BF16)
    return pl.pallas_call(
        _s5_prep_kernel,
        grid=(g // S5_PREP_GROUPS,),
        in_specs=[gspec(8, S5_RI), gspec(1, 1)] + [gspec(S5_GROUP, S5_RI)] * 4,
        out_specs=[gspec(S5_LP, 2 * S5_RI)] * 2 + [gspec(S5_LP, S5_LP), gspec(8, S5_RI)],
        out_shape=[table] * 2 + [jax.ShapeDtypeStruct((g, S5_LP, S5_LP), BF16),
                                 jax.ShapeDtypeStruct((g, 8, S5_RI), F32)],
        compiler_params=_params(("arbitrary",)),
        name="s5_prep",
    )(lam_row, ls, btr, bti, cr, ci)


def _s5_scan_kernel(u_ref, us_ref, x0r_ref, x0i_ref, bs_ref, cs_ref, tp_ref, ap_ref,
                    y_ref, ys_ref, finr_ref, fini_ref, finsr_ref, finsi_ref, xs_ref, *, bsz, chunks):
    n_rows = bsz * chunks
    pair = range(2)
    us_in = [u_ref[g] for g in pair]
    x = sum(_dot(us_in[g], bs_ref[g]) for g in pair)
    xr, xi = x[:, :S5_RI], x[:, S5_RI:]
    row = lax.broadcasted_iota(jnp.int32, (n_rows, 1), 0) % chunks
    ap = ap_ref[0] + ap_ref[1]
    ar, ai = ap[0:1, :], ap[1:2, :]

    def prefix(xr, xi, ar, ai, pos, length, axis=0):
        shift = 1
        while shift < length:
            sr = jnp.where(pos >= shift, pltpu.roll(xr, shift, axis=axis), 0.0)
            si = jnp.where(pos >= shift, pltpu.roll(xi, shift, axis=axis), 0.0)
            xr, xi = xr + (sr * ar - si * ai), xi + (sr * ai + si * ar)
            ar, ai = ar * ar - ai * ai, 2.0 * (ar * ai)
            shift *= 2
        return xr, xi, ar, ai

    n_runs, runs = n_rows // SUBLANES, chunks // SUBLANES
    in_run = lambda z: z.reshape(n_rows // SUBLANES, SUBLANES, S5_RI)
    pos3 = lax.broadcasted_iota(jnp.int32, (1, SUBLANES, 1), 1)
    x3r, x3i, br, bi = prefix(in_run(xr), in_run(xi), ar, ai, pos3, SUBLANES, axis=1)
    xr, xi = x3r.reshape(n_rows, S5_RI), x3i.reshape(n_rows, S5_RI)
    xs_ref[0], xs_ref[1] = xr, xi
    last = pl.ds(SUBLANES - 1, n_runs, stride=SUBLANES)
    run = lax.broadcasted_iota(jnp.int32, (n_runs, 1), 0) % runs
    er, ei, _, _ = prefix(xs_ref[0, last, :], xs_ref[1, last, :], br, bi, run, runs)
    cr = jnp.where(run >= 1, pltpu.roll(er, 1, axis=0), 0.0)
    ci = jnp.where(run >= 1, pltpu.roll(ei, 1, axis=0), 0.0)
    pr, pi = ar, ai
    for r in range(SUBLANES):
        dst = pl.ds(r, n_runs, stride=SUBLANES)
        xs_ref[0, dst, :] = cr * pr - ci * pi
        xs_ref[1, dst, :] = cr * pi + ci * pr
        pr, pi = pr * ar - pi * ai, pr * ai + pi * ar
    xr, xi = xr + xs_ref[0], xi + xs_ref[1]
    pr = jnp.where(row >= 1, pltpu.roll(xr, 1, axis=0), 0.0).astype(BF16)
    pi = jnp.where(row >= 1, pltpu.roll(xi, 1, axis=0), 0.0).astype(BF16)
    prev = jnp.concatenate([pr, pi], axis=1)
    for g in pair:
        y_ref[g] = (_dot(us_in[g], tp_ref[g]) + _dot_nt(prev, cs_ref[g])).astype(y_ref.dtype)
    finr_ref[...] = jnp.zeros_like(finr_ref)
    fini_ref[...] = jnp.zeros_like(fini_ref)
    for b in range(bsz):
        last = slice((b + 1) * chunks - 1, (b + 1) * chunks)
        finr_ref[b:b + 1, :] = xr[last]
        fini_ref[b:b + 1, :] = xi[last]

    half = S5_LP // 2
    x0r, x0i = x0r_ref[...], x0i_ref[...]
    hr, hi = ap[2:3, :], ap[3:4, :]
    xs = sum(_dot(us_ref[g], bs_ref[g, half:, :]) for g in pair)
    finsr_ref[...] = x0r * hr - x0i * hi + xs[:, :S5_RI]
    finsi_ref[...] = x0r * hi + x0i * hr + xs[:, S5_RI:]
    x0 = jnp.concatenate([x0r, x0i], axis=1).astype(BF16)
    for g in pair:
        ys_ref[g] = (_dot(us_ref[g], tp_ref[g, :half, :half]) + _dot_nt(x0, cs_ref[g, :half, :])).astype(ys_ref.dtype)


def _s5_scan(u_g, us_g, x0r, x0i, bs, cs, tp, ap, bsz):
    g, n_rows, _ = u_g.shape
    n_s = us_g.shape[1]
    half = S5_LP // 2
    pspec = lambda *s: pl.BlockSpec((2,) + s, lambda i: (i,) + (0,) * len(s))
    ospec = lambda *s: pl.BlockSpec((None,) + s, lambda i: (i,) + (0,) * len(s))
    packed = lambda n: jax.ShapeDtypeStruct((g // 2, n, S5_RI), F32)
    return pl.pallas_call(
        functools.partial(_s5_scan_kernel, bsz=bsz, chunks=n_rows // bsz),
        grid=(g // 2,),
        in_specs=[pspec(n_rows, S5_LP), pspec(n_s, half), ospec(n_s, S5_RI), ospec(n_s, S5_RI)]
                 + [pspec(S5_LP, 2 * S5_RI)] * 2 + [pspec(S5_LP, S5_LP), pspec(8, S5_RI)],
        out_specs=[pspec(n_rows, S5_LP), pspec(n_s, half), ospec(8, S5_RI), ospec(8, S5_RI),
                   ospec(n_s, S5_RI), ospec(n_s, S5_RI)],
        out_shape=[jax.ShapeDtypeStruct((g, n_rows, S5_LP), BF16), jax.ShapeDtypeStruct((g, n_s, half), BF16),
                   packed(8), packed(8), packed(n_s), packed(n_s)],
        scratch_shapes=[pltpu.VMEM((2, n_rows, S5_RI), F32)],
        compiler_params=_params(("arbitrary",)),
        name="s5_scan",
    )(u_g, us_g, x0r, x0i, bs, cs, tp, ap)


def kernel(x_prompt, x_sample, state_ret, state_swa_k, state_swa_v, state_ssm_re, state_ssm_im, norm_mix, norm_ffn, norm_final, w_in_ab, ret_gn, w_out_ab, ssm_lam_re, ssm_lam_im, ssm_log_step, ssm_b_re, ssm_b_im, ssm_c_re, ssm_c_im, ssm_d, w_glu, w_ffn_in, w_ffn_out):
    bsz, seq, d = x_prompt.shape
    dbsz, n_new, _ = x_sample.shape
    assert state_swa_k.shape[2] == SWA_BUF and n_new == S5_CHUNK // 2 and seq % SWA_ROWS == 0
    xp = x_prompt.reshape(bsz * seq, d)
    xs = x_sample.reshape(dbsz * n_new, d)
    tm_p, tm_s = TAIL_ROWS, dbsz * n_new
    row = lambda v: v.reshape(1, -1)

    w_in0 = w_in_ab[0].astype(BF16)
    g_mix0, g_ffn0, gn0 = row(norm_mix[0]), row(norm_ffn[0]), row(ret_gn[0])

    ra_p, qkv_p = _ab_proj(xp, g_mix0, w_in0, PROJ_ROWS)
    ra_s, qkv_s = _ab_proj(xs, g_mix0, w_in0, tm_s)

    zero_ret = jnp.zeros((bsz, H_A, DK_A, DV_A), F32)
    a_p, ret_p = _retention(ra_p, zero_ret, gn0, seq=seq, c_real=RET_CHUNK, n_chunks=RET_STEP_CHUNKS, n_seq=bsz,
                            out_dtype=BF16)
    a_s, ret_s = _retention(ra_s, state_ret[0], gn0, seq=n_new, c_real=n_new, n_chunks=1, n_seq=RET_SAMPLE_SEQS,
                            out_dtype=F32)

    width_b = H_B * DH_B
    rows_last = lambda w: w.transpose(0, 2, 3, 1).reshape(dbsz, width_b, SWA_BUF)
    rows_first = lambda w: w.reshape(dbsz, H_B, DH_B, SWA_BUF).transpose(0, 3, 1, 2)[None]
    later = (w_out_ab[0], w_ffn_in.reshape(-1, w_ffn_in.shape[-1]), w_ffn_out.reshape(-1, w_ffn_out.shape[-1]), w_glu[0])
    ob_p, ob_s, swk_s, swv_s, w_out0, w_f_in, w_f_out, w_glu0 = _swa(
        qkv_p, bsz, seq, qkv_s, rows_last(state_swa_k[0]), rows_last(state_swa_v[0]), n_new, later)
    w_f_in, w_f_out = w_f_in.reshape(w_ffn_in.shape), w_f_out.reshape(w_ffn_out.shape)
    kv_tail = qkv_p.reshape(bsz, seq, QKV_W)[:, seq - SWA_BUF:, width_b:]
    swk_p = kv_tail[..., :width_b].reshape(bsz, SWA_BUF, H_B, DH_B)
    swv_p = kv_tail[..., width_b:].reshape(bsz, SWA_BUF, H_B, DH_B)

    g_mix1, g_ffn1 = row(norm_mix[1]), row(norm_ffn[1])
    perm = _slab_perm()
    yp, u_p = _ab_tail(xp, a_p, ob_p, w_out0, g_ffn0, w_f_in, w_f_out, 0, g_mix1, perm, tm_p, S5_CHUNK)
    ys, u_s = _ab_tail(xs, a_s, ob_s, w_out0, g_ffn0, w_f_in, w_f_out, 0, g_mix1, perm, tm_s, n_new)

    operators = _s5_prep(ssm_lam_re[0], ssm_lam_im[0], ssm_log_step[0],
                         ssm_b_re[0], ssm_b_im[0], ssm_c_re[0], ssm_c_im[0])
    pack = lambda st: st.reshape(st.shape[0], -1, S5_RI).transpose(1, 0, 2)
    unpack = lambda st: st.transpose(1, 0, 2).reshape(st.shape[1], -1, S5_STATE)[None]
    y5_p, y5_s, finr_p, fini_p, finr_s, fini_s = _s5_scan(
        u_p, u_s, pack(state_ssm_re[0]), pack(state_ssm_im[0]), *operators, bsz)

    tail1 = (perm.T, g_mix1, row(ssm_d[0]), w_glu0, g_ffn1,
             w_f_in, w_f_out, 1, row(norm_final))
    yp = _s5_tail(yp, y5_p, *tail1, tm_p, S5_CHUNK)
    ys = _s5_tail(ys, y5_s, *tail1, tm_s, n_new)

    return (yp.reshape(bsz, seq, d), ys.reshape(dbsz, n_new, d),
            ret_p[None], ret_s[None],
            swk_p[None], swv_p[None],
            rows_first(swk_s), rows_first(swv_s),
            unpack(finr_p[:, :bsz]), unpack(fini_p[:, :bsz]), unpack(finr_s), unpack(fini_s))
```
